```python
import jax
import jax.numpy as jnp
from jax import lax
import numpy as np

D_MODEL = 1024
BATCH = 16
SEQ = 2048
DEPTH = 1

FOX_HEADS = 8
FOX_HEAD_DIM = 64
FOX_WIDTH = FOX_HEADS * FOX_HEAD_DIM
Q_BLOCK = 128
FORGET_BIAS_INIT = 5.0
GMLP_GROUPS = 8
GMLP_GROUP_DIM = 64
GMLP_WIDTH = GMLP_GROUPS * GMLP_GROUP_DIM
GMLP_CHUNK = 128
N_EXPERTS = 32
TOP_K = 4
D_FF = D_MODEL
SWIGLU_LIMIT = 7.0
SWIGLU_ALPHA = 1.702
MOE_BLOCK = 128
PLE_DIM = 256
LN_EPS = 1e-5
DEEPNORM_ALPHA = (2.0 * DEPTH) ** 0.25
DEEPNORM_BETA = (8.0 * DEPTH) ** -0.25

OFF_Q = 0
OFF_K = OFF_Q + FOX_WIDTH
OFF_V = OFF_K + FOX_WIDTH
OFF_F = OFF_V + FOX_WIDTH
OFF_U = OFF_F + FOX_HEADS
OFF_GV = OFF_U + GMLP_WIDTH
OFF_GA = OFF_GV + GMLP_WIDTH
OFF_GB = OFF_GA + D_MODEL
D_IN_PROJ = OFF_GB + D_MODEL
SPLIT_POINTS = (OFF_K, OFF_V, OFF_F, OFF_U, OFF_GV, OFF_GA, OFF_GB)

kernel_name = 'hybrid_fox_gmlp_moe_block'


def layer_norm(x, g, b):
    xf = x.astype(jnp.float32)
    mu = jnp.mean(xf, axis=-1, keepdims=True)
    var = jnp.mean(jnp.square(xf - mu), axis=-1, keepdims=True)
    return ((xf - mu) * lax.rsqrt(var + LN_EPS) * g + b).astype(x.dtype)


def forgetting_attention(q, k, v, f_logit):
    B, S, H, Dh = q.shape
    c = jnp.cumsum(jax.nn.log_sigmoid(f_logit.astype(jnp.float32)), axis=1)
    c = jnp.transpose(c, (0, 2, 1))
    q = jnp.transpose(q, (0, 2, 1, 3))
    k = jnp.transpose(k, (0, 2, 1, 3))
    v = jnp.transpose(v, (0, 2, 1, 3))
    scale = Dh ** -0.5
    outs = []
    for blk in range(S // Q_BLOCK):
        q0 = blk * Q_BLOCK
        kv_len = q0 + Q_BLOCK
        qb = q[:, :, q0:kv_len]
        kb = k[:, :, :kv_len]
        vb = v[:, :, :kv_len]
        logits = jnp.einsum('bhtd,bhsd->bhts', qb, kb).astype(jnp.float32) * scale
        logits = logits + c[:, :, q0:kv_len, None] - c[:, :, None, :kv_len]
        causal = jnp.arange(kv_len)[None, :] <= (q0 + jnp.arange(Q_BLOCK))[:, None]
        logits = jnp.where(causal, logits, -jnp.inf)
        probs = jax.nn.softmax(logits, axis=-1).astype(v.dtype)
        outs.append(jnp.einsum('bhts,bhsd->bthd', probs, vb))
    return jnp.concatenate(outs, axis=1).reshape(B, S, H * Dh)


def spatial_gating(u, v, ln_g, ln_b, w_s, b_s):
    B, S, _ = u.shape
    u = jax.nn.gelu(u)
    v = layer_norm(jax.nn.gelu(v), ln_g, ln_b)
    v = v.reshape(B, S // GMLP_CHUNK, GMLP_CHUNK, GMLP_GROUPS, GMLP_GROUP_DIM)
    w = w_s * jnp.tril(jnp.ones((GMLP_CHUNK, GMLP_CHUNK), w_s.dtype))
    s = jnp.einsum('gts,bnsgd->bntgd', w, v) + b_s.T[None, None, :, :, None]
    return u * s.reshape(B, S, GMLP_WIDTH)


def expert_ffn(xb, w_gu, b_gu, w_dn, b_dn):
    gu = xb @ w_gu + b_gu
    gate = jnp.minimum(gu[..., :D_FF], SWIGLU_LIMIT)
    up = jnp.clip(gu[..., D_FF:], -SWIGLU_LIMIT, SWIGLU_LIMIT)
    h = (up + 1.0) * (gate * jax.nn.sigmoid(SWIGLU_ALPHA * gate))
    return h @ w_dn + b_dn


def moe_ffn(x, w_router, b_router, w_gu, b_gu, w_dn, b_dn):
    B, S, D = x.shape
    T = B * S
    n_assign = T * TOP_K
    xf = x.reshape(T, D)
    logits = xf.astype(jnp.float32) @ w_router.astype(jnp.float32) + b_router.astype(jnp.float32)
    top_val, top_idx = lax.top_k(logits, TOP_K)
    gates = jax.nn.softmax(top_val, axis=-1).astype(x.dtype)
    e_flat = top_idx.reshape(-1).astype(jnp.int32)
    tok_flat = jnp.arange(n_assign, dtype=jnp.int32) // TOP_K
    order = jnp.argsort(e_flat)
    e_sorted = e_flat[order]
    tok_sorted = tok_flat[order]
    g_sorted = gates.reshape(-1)[order]
    counts = jnp.bincount(e_flat, length=N_EXPERTS).astype(jnp.int32)
    starts = jnp.cumsum(counts) - counts
    padded = (counts + MOE_BLOCK - 1) // MOE_BLOCK * MOE_BLOCK
    pad_end = jnp.cumsum(padded)
    pad_start = pad_end - padded
    rank = jnp.arange(n_assign, dtype=jnp.int32) - starts[e_sorted]
    dest = pad_start[e_sorted] + rank
    n_blocks = -(-n_assign // MOE_BLOCK) + N_EXPERTS
    x_pad = jnp.zeros((n_blocks * MOE_BLOCK, D), x.dtype).at[dest].set(xf[tok_sorted])
    block_start = jnp.arange(n_blocks, dtype=jnp.int32) * MOE_BLOCK
    block_expert = jnp.minimum(jnp.searchsorted(pad_end, block_start, side='right'), N_EXPERTS - 1)

    def run_block(args):
        xb, e = args
        return expert_ffn(xb, w_gu[e], b_gu[e], w_dn[e], b_dn[e])

    y_pad = lax.map(run_block, (x_pad.reshape(n_blocks, MOE_BLOCK, D), block_expert))
    y_sorted = y_pad.reshape(-1, D)[dest] * g_sorted[:, None]
    return jax.ops.segment_sum(y_sorted, tok_sorted, num_segments=T).reshape(B, S, D)


def setup_inputs(seed: int = 0) -> dict:
    key = jax.random.key(seed)
    ks = jax.random.split(key, 32)
    f32 = jnp.float32
    L = DEPTH

    def nrm(k, shape, scale):
        return scale * jax.random.normal(k, shape, f32)

    col_scale = jnp.ones((D_IN_PROJ,), f32).at[OFF_V:OFF_F].set(DEEPNORM_BETA)
    return {
        'x': nrm(ks[0], (BATCH, SEQ, D_MODEL), 1.0),
        'p': nrm(ks[1], (DEPTH, BATCH, SEQ, PLE_DIM), 1.0),
        'w_in': nrm(ks[2], (L, D_MODEL, D_IN_PROJ), D_MODEL ** -0.5) * col_scale,
        'b_in': nrm(ks[3], (L, D_IN_PROJ), 0.02).at[:, OFF_F:OFF_U].add(FORGET_BIAS_INIT),
        'gmlp_ln_g': 1.0 + nrm(ks[4], (L, GMLP_WIDTH), 0.02),
        'gmlp_ln_b': nrm(ks[5], (L, GMLP_WIDTH), 0.02),
        'w_spatial': nrm(ks[6], (L, GMLP_GROUPS, GMLP_CHUNK, GMLP_CHUNK), GMLP_CHUNK ** -0.5),
        'b_spatial': 1.0 + nrm(ks[7], (L, GMLP_GROUPS, GMLP_CHUNK), 0.02),
        'w_branch_a': nrm(ks[8], (L, FOX_WIDTH, D_MODEL), FOX_WIDTH ** -0.5 * DEEPNORM_BETA),
        'w_branch_b': nrm(ks[9], (L, GMLP_WIDTH, D_MODEL), GMLP_WIDTH ** -0.5 * DEEPNORM_BETA),
        'w_out': nrm(ks[10], (L, D_MODEL, D_MODEL), D_MODEL ** -0.5 * DEEPNORM_BETA),
        'b_out': nrm(ks[11], (L, D_MODEL), 0.02),
        'ln1_g': 1.0 + nrm(ks[12], (L, D_MODEL), 0.02),
        'ln1_b': nrm(ks[13], (L, D_MODEL), 0.02),
        'w_router': nrm(ks[14], (L, D_MODEL, N_EXPERTS), D_MODEL ** -0.5),
        'b_router': nrm(ks[15], (L, N_EXPERTS), 0.01),
        'w_gate_up': nrm(ks[16], (L, N_EXPERTS, D_MODEL, 2 * D_FF), D_MODEL ** -0.5),
        'b_gate_up': nrm(ks[17], (L, N_EXPERTS, 2 * D_FF), 0.02),
        'w_down': nrm(ks[18], (L, N_EXPERTS, D_FF, D_MODEL), D_FF ** -0.5 * DEEPNORM_BETA),
        'b_down': nrm(ks[19], (L, N_EXPERTS, D_MODEL), 0.02),
        'ln2_g': 1.0 + nrm(ks[20], (L, D_MODEL), 0.02),
        'ln2_b': nrm(ks[21], (L, D_MODEL), 0.02),
        'w_ple': nrm(ks[22], (L, PLE_DIM, D_MODEL), PLE_DIM ** -0.5 * DEEPNORM_BETA),
        'w_ple_gate': nrm(ks[23], (L, D_MODEL, D_MODEL), D_MODEL ** -0.5),
        'b_ple_gate': nrm(ks[24], (L, D_MODEL), 0.02),
        'ln3_g': 1.0 + nrm(ks[25], (L, D_MODEL), 0.02),
        'ln3_b': nrm(ks[26], (L, D_MODEL), 0.02),
    }


def reference(x, p, w_in, b_in, gmlp_ln_g, gmlp_ln_b, w_spatial, b_spatial,
              w_branch_a, w_branch_b, w_out, b_out, ln1_g, ln1_b,
              w_router, b_router, w_gate_up, b_gate_up, w_down, b_down,
              ln2_g, ln2_b, w_ple, w_ple_gate, b_ple_gate, ln3_g, ln3_b):
    h = x
    B, S, _ = h.shape
    for i in range(DEPTH):
        proj = h @ w_in[i] + b_in[i]
        q, k, v, f_logit, u, gv, gate_a, gate_b = jnp.split(proj, SPLIT_POINTS, axis=-1)
        attn = forgetting_attention(q.reshape(B, S, FOX_HEADS, FOX_HEAD_DIM),
                                    k.reshape(B, S, FOX_HEADS, FOX_HEAD_DIM),
                                    v.reshape(B, S, FOX_HEADS, FOX_HEAD_DIM),
                                    f_logit)
        sgu = spatial_gating(u, gv, gmlp_ln_g[i], gmlp_ln_b[i], w_spatial[i], b_spatial[i])
        merged = (jax.nn.sigmoid(gate_a) * (attn @ w_branch_a[i])
                  + jax.nn.sigmoid(gate_b) * (sgu @ w_branch_b[i]))
        mix = merged @ w_out[i] + b_out[i]
        h = layer_norm(DEEPNORM_ALPHA * h + mix, ln1_g[i], ln1_b[i])
        ffn = moe_ffn(h, w_router[i], b_router[i], w_gate_up[i], b_gate_up[i], w_down[i], b_down[i])
        h = layer_norm(DEEPNORM_ALPHA * h + ffn, ln2_g[i], ln2_b[i])
        ple = (p[i] @ w_ple[i]) * jax.nn.sigmoid(h @ w_ple_gate[i] + b_ple_gate[i])
        h = layer_norm(DEEPNORM_ALPHA * h + ple, ln3_g[i], ln3_b[i])
    return h
```

```python
import functools
import math

import jax
import jax.numpy as jnp
from jax import lax
from jax.experimental import pallas as pl
from jax.experimental.pallas import tpu as pltpu

F32 = jnp.float32
BF16 = jnp.bfloat16
I32 = jnp.int32

D_MODEL = 1024
N_HEADS = 8
HEAD_DIM = 64
FOX_WIDTH = N_HEADS * HEAD_DIM
GMLP_WIDTH = 512
GMLP_CHUNK = 128
N_EXPERTS = 32
TOP_K = 4
D_FF = 1024
PLE_DIM = 256
SWIGLU_LIMIT = 7.0
SWIGLU_ALPHA = 1.702
LN_EPS = 1e-5
LANES = 128
ROW_TILE = 8
VMEM_LIMIT = 56 * 1024 * 1024

MOE_BLOCK = 256
ATTN_TILE = 256
ROW_TILE_A = 512
COPY_CHUNK = 8192
COPY_SUB = 512


def _cparams(sem):
    return pltpu.CompilerParams(dimension_semantics=sem, vmem_limit_bytes=VMEM_LIMIT)


def _gelu(x):
    c = math.sqrt(2.0 / math.pi)
    return 0.5 * x * (1.0 + jnp.tanh(c * (x + 0.044715 * (x * x * x))))


def _layer_norm(x, g, b):
    mu = jnp.mean(x, axis=-1, keepdims=True)
    xc = x - mu
    var = jnp.mean(xc * xc, axis=-1, keepdims=True)
    return xc * lax.rsqrt(var + LN_EPS) * g + b


def _split3(x):
    hi = x.astype(BF16)
    r = x - hi.astype(F32)
    mid = r.astype(BF16)
    lo = (r - mid.astype(F32)).astype(BF16)
    return hi, mid, lo


def _dot(a, b):
    return jnp.dot(a, b, preferred_element_type=F32)


def _qkvf_kernel(x_ref, w_ref, b_ref, q_ref, k_ref, v_ref, f_ref):
    x = x_ref[...].astype(BF16)
    proj = _dot(x, w_ref[...]) + b_ref[...]
    q_ref[...] = (proj[:, :FOX_WIDTH] * (HEAD_DIM ** -0.5)).astype(BF16)
    k_ref[...] = proj[:, FOX_WIDTH:2 * FOX_WIDTH].astype(BF16)
    v_ref[...] = proj[:, 2 * FOX_WIDTH:3 * FOX_WIDTH].astype(BF16)
    f_ref[...] = proj[:, 3 * FOX_WIDTH:]


def _qkvf(x2, w, b, tm):
    T = x2.shape[0]
    n_out = w.shape[1]
    return pl.pallas_call(
        _qkvf_kernel,
        grid=(T // tm,),
        in_specs=[
            pl.BlockSpec((tm, D_MODEL), lambda i: (i, 0)),
            pl.BlockSpec((D_MODEL, n_out), lambda i: (0, 0)),
            pl.BlockSpec((1, n_out), lambda i: (0, 0)),
        ],
        out_specs=[
            pl.BlockSpec((tm, FOX_WIDTH), lambda i: (i, 0)),
            pl.BlockSpec((tm, FOX_WIDTH), lambda i: (i, 0)),
            pl.BlockSpec((tm, FOX_WIDTH), lambda i: (i, 0)),
            pl.BlockSpec((tm, LANES), lambda i: (i, 0)),
        ],
        out_shape=[
            jax.ShapeDtypeStruct((T, FOX_WIDTH), BF16),
            jax.ShapeDtypeStruct((T, FOX_WIDTH), BF16),
            jax.ShapeDtypeStruct((T, FOX_WIDTH), BF16),
            jax.ShapeDtypeStruct((T, LANES), F32),
        ],
        compiler_params=_cparams(("arbitrary",)),
        name="qkvf",
    )(x2, w, b)


def _decay_kernel(f_ref, ccol_ref, crow_ref, *, seq, blk):
    r = lax.broadcasted_iota(I32, (blk, blk), 0)
    c = lax.broadcasted_iota(I32, (blk, blk), 1)
    tri = jnp.where(c <= r, 1.0, 0.0).astype(BF16)
    carry = jnp.zeros((1, LANES), F32)
    for i in range(seq // blk):
        f = f_ref[i * blk:(i + 1) * blk, :]
        ls = jnp.minimum(f, 0.0) - jnp.log1p(jnp.exp(-jnp.abs(f)))
        hi, mid, lo = _split3(ls)
        cs = _dot(tri, hi) + _dot(tri, mid) + _dot(tri, lo) + carry
        ccol_ref[i * blk:(i + 1) * blk, :] = cs
        carry = cs[blk - 1:blk, :]
        for j in range(blk // LANES):
            t = cs[j * LANES:(j + 1) * LANES, :].T
            off = i * blk + j * LANES
            crow_ref[:, off:off + LANES] = t[:N_HEADS, :]


def _decay(f_pad, batch, seq):
    blk = 256 if seq % 256 == 0 else LANES
    return pl.pallas_call(
        functools.partial(_decay_kernel, seq=seq, blk=blk),
        grid=(batch,),
        in_specs=[pl.BlockSpec((seq, LANES), lambda b: (b, 0))],
        out_specs=[
            pl.BlockSpec((seq, LANES), lambda b: (b, 0)),
            pl.BlockSpec((None, N_HEADS, seq), lambda b: (b, 0, 0)),
        ],
        out_shape=[
            jax.ShapeDtypeStruct((batch * seq, LANES), F32),
            jax.ShapeDtypeStruct((batch, N_HEADS, seq), F32),
        ],
        compiler_params=_cparams(("arbitrary",)),
        name="decay",
    )(f_pad)


def _attn_kernel(q_ref, k_ref, v_ref, ccol_ref, crow_ref, o_ref, *, tq):
    hp = pl.program_id(1)
    qi = pl.program_id(2)
    lane = lax.broadcasted_iota(I32, (tq, LANES), 1)
    row = lax.broadcasted_iota(I32, (tq, tq), 0)
    col = lax.broadcasted_iota(I32, (tq, tq), 1)
    q = q_ref[...]
    ccol = ccol_ref[...]
    outs = []
    for j in range(2):
        h = hp * 2 + j
        in_half = (lane >= HEAD_DIM * j) & (lane < HEAD_DIM * (j + 1))
        qj = jnp.where(in_half, q, jnp.zeros_like(q))
        cq = jnp.sum(jnp.where(lane == h, ccol, 0.0), axis=1, keepdims=True)

        def step(kt, carry, masked, qj=qj, cq=cq, h=h):
            m, l, acc = carry
            start = pl.multiple_of(kt * tq, tq)
            kb = k_ref[pl.ds(start, tq), :]
            vb = v_ref[pl.ds(start, tq), :]
            ck = crow_ref[pl.ds(h, 1), pl.ds(start, tq)]
            s = lax.dot_general(qj, kb, (((1,), (1,)), ((), ())), preferred_element_type=F32)
            s = s + (cq - ck)
            if masked:
                s = jnp.where(col <= row, s, -jnp.inf)
            m_new = jnp.maximum(m, jnp.max(s, axis=1, keepdims=True))
            a = jnp.exp(m - m_new)
            p = jnp.exp(s - m_new)
            l = a * l + jnp.sum(p, axis=1, keepdims=True)
            acc = a * acc + _dot(p.astype(BF16), vb)
            return m_new, l, acc

        init = (jnp.full((tq, 1), -jnp.inf, F32), jnp.zeros((tq, 1), F32), jnp.zeros((tq, LANES), F32))
        carry = lax.fori_loop(0, qi, functools.partial(step, masked=False), init)
        m, l, acc = step(qi, carry, True)
        outs.append(acc / l)
    o_ref[...] = jnp.where(lane < HEAD_DIM, outs[0], outs[1]).astype(BF16)


def _attention(q, k, v, ccol, crow, batch, seq):
    tq = min(ATTN_TILE, seq)
    nq = seq // tq
    T = batch * seq
    return pl.pallas_call(
        functools.partial(_attn_kernel, tq=tq),
        grid=(batch, N_HEADS // 2, nq),
        in_specs=[
            pl.BlockSpec((tq, LANES), lambda b, hp, qi: (b * nq + qi, hp)),
            pl.BlockSpec((seq, LANES), lambda b, hp, qi: (b, hp)),
            pl.BlockSpec((seq, LANES), lambda b, hp, qi: (b, hp)),
            pl.BlockSpec((tq, LANES), lambda b, hp, qi: (b * nq + qi, 0)),
            pl.BlockSpec((None, N_HEADS, seq), lambda b, hp, qi: (b, 0, 0)),
        ],
        out_specs=pl.BlockSpec((tq, LANES), lambda b, hp, qi: (b * nq + qi, hp)),
        out_shape=jax.ShapeDtypeStruct((T, FOX_WIDTH), BF16),
        compiler_params=_cparams(("arbitrary", "arbitrary", "arbitrary")),
        name="attn",
    )(q, k, v, ccol, crow)


def _mix_kernel(x_ref, attn_ref, w2_ref, b2_ref, lng_ref, lnb_ref, ws_ref, bs_ref,
                wa_ref, wb_ref, wo_ref, bo_ref, g1_ref, b1_ref, wr_ref, br_ref,
                h_ref, idx_ref, gate_ref, rank_ref, cnt_ref, carry_ref, *, tm, alpha):
    i = pl.program_id(0)

    @pl.when(i == 0)
    def _():
        carry_ref[...] = jnp.zeros_like(carry_ref)

    x = x_ref[...]
    proj = _dot(x.astype(BF16), w2_ref[...]) + b2_ref[...]
    u = _gelu(proj[:, :GMLP_WIDTH])
    gv = _gelu(proj[:, GMLP_WIDTH:2 * GMLP_WIDTH])
    vln = _layer_norm(gv, lng_ref[...], lnb_ref[...]).astype(BF16)

    cr = lax.broadcasted_iota(I32, (GMLP_CHUNK, GMLP_CHUNK), 0)
    cc = lax.broadcasted_iota(I32, (GMLP_CHUNK, GMLP_CHUNK), 1)
    tril = cc <= cr
    lo_half = cc < HEAD_DIM
    zero_w = jnp.zeros((GMLP_CHUNK, GMLP_CHUNK), BF16)
    n_slab = GMLP_WIDTH // LANES
    lhs = []
    for s in range(n_slab):
        w0 = jnp.where(tril, ws_ref[2 * s], zero_w)
        w1 = jnp.where(tril, ws_ref[2 * s + 1], zero_w)
        lhs.append(jnp.concatenate([w0, w1], axis=1))
    bs = bs_ref[...]
    rows = []
    for c in range(tm // GMLP_CHUNK):
        cols = []
        for s in range(n_slab):
            vs = vln[c * GMLP_CHUNK:(c + 1) * GMLP_CHUNK, s * LANES:(s + 1) * LANES]
            rhs = jnp.concatenate([jnp.where(lo_half, vs, zero_w), jnp.where(lo_half, zero_w, vs)], axis=0)
            cols.append(_dot(lhs[s], rhs))
        rows.append(jnp.concatenate(cols, axis=1) + bs)
    sp = jnp.concatenate(rows, axis=0) if len(rows) > 1 else rows[0]
    sgu = (u * sp).astype(BF16)

    ga = jax.nn.sigmoid(proj[:, 2 * GMLP_WIDTH:2 * GMLP_WIDTH + D_MODEL])
    gb = jax.nn.sigmoid(proj[:, 2 * GMLP_WIDTH + D_MODEL:])
    merged = ga * _dot(attn_ref[...], wa_ref[...]) + gb * _dot(sgu, wb_ref[...])
    mix = _dot(merged.astype(BF16), wo_ref[...]) + bo_ref[...]
    h = _layer_norm(alpha * x + mix, g1_ref[...], b1_ref[...])
    h_ref[...] = h

    a_hi = h.astype(BF16)
    a_lo = (h - a_hi.astype(F32)).astype(BF16)
    wr = wr_ref[...]
    w_hi = wr.astype(BF16)
    w_lo = (wr - w_hi.astype(F32)).astype(BF16)
    logits = _dot(a_hi, w_hi) + _dot(a_lo, w_hi) + _dot(a_hi, w_lo) + br_ref[...]

    lane_i = lax.broadcasted_iota(I32, (tm, LANES), 1)
    lane_f = lane_i.astype(F32)
    vals, idxs = [], []
    l = logits
    for _ in range(TOP_K):
        m = jnp.max(l, axis=1, keepdims=True)
        ix = jnp.min(jnp.where(l == m, lane_f, float(LANES)), axis=1, keepdims=True)
        vals.append(m)
        idxs.append(ix)
        l = jnp.where(lane_f == ix, -jnp.inf, l)
    es = [jnp.exp(v - vals[0]) for v in vals]
    den = es[0] + es[1] + es[2] + es[3]

    onehot = jnp.zeros((tm, LANES), F32)
    idx_out = jnp.zeros((tm, LANES), F32)
    gate_out = jnp.zeros((tm, LANES), F32)
    for k in range(TOP_K):
        onehot = onehot + jnp.where(lane_f == idxs[k], 1.0, 0.0)
        idx_out = jnp.where(lane_i == k, idxs[k], idx_out)
        gate_out = jnp.where(lane_i == k, es[k] / den, gate_out)

    tr = lax.broadcasted_iota(I32, (tm, tm), 0)
    tc = lax.broadcasted_iota(I32, (tm, tm), 1)
    strict = jnp.where(tc < tr, 1.0, 0.0).astype(BF16)
    carry = carry_ref[0:1, :]
    before = _dot(strict, onehot.astype(BF16)) + carry
    rank_out = jnp.zeros((tm, LANES), F32)
    for k in range(TOP_K):
        rk = jnp.sum(jnp.where(lane_f == idxs[k], before, 0.0), axis=1, keepdims=True)
        rank_out = jnp.where(lane_i == k, rk, rank_out)
    new_carry = carry + jnp.sum(onehot, axis=0, keepdims=True)
    carry_ref[...] = jnp.broadcast_to(new_carry, carry_ref.shape)
    cnt_ref[...] = jnp.broadcast_to(new_carry, cnt_ref.shape).astype(I32)
    idx_ref[...] = idx_out.astype(I32)
    gate_ref[...] = gate_out
    rank_ref[...] = rank_out.astype(I32)


def _mix(x2, attn, w2, b2, lng, lnb, ws, bs_tile, wa, wb, wo, bo, g1, b1, wr, br, tm, alpha):
    T = x2.shape[0]
    n2 = w2.shape[1]
    const = lambda *shape: pl.BlockSpec(shape, lambda i: (0,) * len(shape))
    return pl.pallas_call(
        functools.partial(_mix_kernel, tm=tm, alpha=alpha),
        grid=(T // tm,),
        in_specs=[
            pl.BlockSpec((tm, D_MODEL), lambda i: (i, 0)),
            pl.BlockSpec((tm, FOX_WIDTH), lambda i: (i, 0)),
            const(D_MODEL, n2), const(1, n2),
            const(1, GMLP_WIDTH), const(1, GMLP_WIDTH),
            const(GMLP_WIDTH // HEAD_DIM, GMLP_CHUNK, GMLP_CHUNK), const(GMLP_CHUNK, GMLP_WIDTH),
            const(FOX_WIDTH, D_MODEL), const(GMLP_WIDTH, D_MODEL),
            const(D_MODEL, D_MODEL), const(1, D_MODEL),
            const(1, D_MODEL), const(1, D_MODEL),
            const(D_MODEL, LANES), const(1, LANES),
        ],
        out_specs=[
            pl.BlockSpec((tm, D_MODEL), lambda i: (i, 0)),
            pl.BlockSpec((tm, LANES), lambda i: (i, 0)),
            pl.BlockSpec((tm, LANES), lambda i: (i, 0)),
            pl.BlockSpec((tm, LANES), lambda i: (i, 0)),
            pl.BlockSpec((ROW_TILE, LANES), lambda i: (0, 0)),
        ],
        out_shape=[
            jax.ShapeDtypeStruct((T, D_MODEL), F32),
            jax.ShapeDtypeStruct((T, LANES), I32),
            jax.ShapeDtypeStruct((T, LANES), F32),
            jax.ShapeDtypeStruct((T, LANES), I32),
            jax.ShapeDtypeStruct((ROW_TILE, LANES), I32),
        ],
        scratch_shapes=[pltpu.VMEM((ROW_TILE, LANES), F32)],
        compiler_params=_cparams(("arbitrary",)),
        name="mix",
    )(x2, attn, w2, b2, lng, lnb, ws, bs_tile, wa, wb, wo, bo, g1, b1, wr, br)


def _rowcopy_kernel(sidx_ref, didx_ref, src_ref, dst_ref, sem, *, chunk, sub, unroll):
    n_sub = chunk // sub

    def issue(g):
        base = g * sub

        def body(j, carry):
            for u in range(unroll):
                jj = base + j * unroll + u
                pltpu.make_async_copy(src_ref.at[sidx_ref[jj]], dst_ref.at[didx_ref[jj]], sem).start()
            return carry

        lax.fori_loop(0, sub // unroll, body, 0)

    def drain():
        def body(j, carry):
            for u in range(unroll):
                pltpu.make_async_copy(src_ref.at[0], dst_ref.at[0], sem).wait()
            return carry

        lax.fori_loop(0, sub // unroll, body, 0)

    issue(0)

    def outer(g, carry):
        @pl.when(g + 1 < n_sub)
        def _():
            issue(g + 1)

        drain()
        return carry

    lax.fori_loop(0, n_sub, outer, 0)


def _rowcopy(sidx, didx, src3, n_dst):
    n = sidx.shape[0]
    chunk = math.gcd(n, COPY_CHUNK)
    sub = math.gcd(chunk, COPY_SUB)
    unroll = math.gcd(sub, 8)
    return pl.pallas_call(
        functools.partial(_rowcopy_kernel, chunk=chunk, sub=sub, unroll=unroll),
        grid=(n // chunk,),
        in_specs=[
            pl.BlockSpec((chunk,), lambda i: (i,), memory_space=pltpu.SMEM),
            pl.BlockSpec((chunk,), lambda i: (i,), memory_space=pltpu.SMEM),
            pl.BlockSpec(memory_space=pl.ANY),
        ],
        out_specs=pl.BlockSpec(memory_space=pl.ANY),
        out_shape=jax.ShapeDtypeStruct((n_dst,) + src3.shape[1:], src3.dtype),
        scratch_shapes=[pltpu.SemaphoreType.DMA(())],
        compiler_params=_cparams(("arbitrary",)),
        name="rowcopy",
    )(sidx, didx, src3)


def _moe_kernel(be_ref, bsrc_ref, nu_ref, x_ref, wgu_ref, bgu_ref, wdn_ref, bdn_ref, y_ref,
                wgu_bf, wdn_bf):
    i = pl.program_id(0)
    active = i < nu_ref[0]
    prev = jnp.maximum(i - 1, 0)
    fresh = (i == 0) | (be_ref[i] != be_ref[prev])

    @pl.when(active & fresh)
    def _():
        wgu_bf[...] = wgu_ref[...].astype(BF16)
        wdn_bf[...] = wdn_ref[...].astype(BF16)

    @pl.when(active)
    def _():
        x = x_ref[...].astype(BF16)
        gu = _dot(x, wgu_bf[...]) + bgu_ref[...]
        gate = jnp.minimum(gu[:, :D_FF], SWIGLU_LIMIT)
        up = jnp.clip(gu[:, D_FF:], -SWIGLU_LIMIT, SWIGLU_LIMIT)
        hid = (up + 1.0) * (gate * jax.nn.sigmoid(SWIGLU_ALPHA * gate))
        y_ref[...] = _dot(hid.astype(BF16), wdn_bf[...]) + bdn_ref[...]


def _moe(blk_e, blk_src, n_used, xpad, wgu, bgu, wdn, bdn, n_blk, bm):
    grid_spec = pltpu.PrefetchScalarGridSpec(
        num_scalar_prefetch=3,
        grid=(n_blk,),
        in_specs=[
            pl.BlockSpec((bm, D_MODEL), lambda i, be, bs, nu: (bs[i], 0)),
            pl.BlockSpec((None, D_MODEL, 2 * D_FF), lambda i, be, bs, nu: (be[i], 0, 0)),
            pl.BlockSpec((None, 1, 2 * D_FF), lambda i, be, bs, nu: (be[i], 0, 0)),
            pl.BlockSpec((None, D_FF, D_MODEL), lambda i, be, bs, nu: (be[i], 0, 0)),
            pl.BlockSpec((None, 1, D_MODEL), lambda i, be, bs, nu: (be[i], 0, 0)),
        ],
        out_specs=pl.BlockSpec((bm, D_MODEL), lambda i, be, bs, nu: (bs[i], 0)),
        scratch_shapes=[pltpu.VMEM((D_MODEL, 2 * D_FF), BF16), pltpu.VMEM((D_FF, D_MODEL), BF16)],
    )
    return pl.pallas_call(
        _moe_kernel,
        grid_spec=grid_spec,
        out_shape=jax.ShapeDtypeStruct((n_blk * bm, D_MODEL), F32),
        compiler_params=_cparams(("arbitrary",)),
        name="moe",
    )(blk_e, blk_src, n_used, xpad, wgu, bgu, wdn, bdn)


def _final_kernel(h_ref, yg_ref, gate_ref, p_ref, wple_ref, wpg_ref, bpg_ref,
                  g2_ref, b2_ref, g3_ref, b3_ref, o_ref, *, alpha):
    h = h_ref[...]
    gates = gate_ref[...]
    ffn = gates[:, 0:1] * yg_ref[:, 0:D_MODEL]
    for k in range(1, TOP_K):
        ffn = ffn + gates[:, k:k + 1] * yg_ref[:, k * D_MODEL:(k + 1) * D_MODEL]
    h2 = _layer_norm(alpha * h + ffn, g2_ref[...], b2_ref[...])
    emb = _dot(p_ref[...].astype(BF16), wple_ref[...])
    pg = jax.nn.sigmoid(_dot(h2.astype(BF16), wpg_ref[...]) + bpg_ref[...])
    o_ref[...] = _layer_norm(alpha * h2 + emb * pg, g3_ref[...], b3_ref[...])


def _final(h1, yg, gates, p2, wple, wpg, bpg, g2, b2, g3, b3, tm, alpha):
    T = h1.shape[0]
    const = lambda *shape: pl.BlockSpec(shape, lambda i: (0,) * len(shape))
    return pl.pallas_call(
        functools.partial(_final_kernel, alpha=alpha),
        grid=(T // tm,),
        in_specs=[
            pl.BlockSpec((tm, D_MODEL), lambda i: (i, 0)),
            pl.BlockSpec((tm, TOP_K * D_MODEL), lambda i: (i, 0)),
            pl.BlockSpec((tm, LANES), lambda i: (i, 0)),
            pl.BlockSpec((tm, PLE_DIM), lambda i: (i, 0)),
            const(PLE_DIM, D_MODEL), const(D_MODEL, D_MODEL), const(1, D_MODEL),
            const(1, D_MODEL), const(1, D_MODEL), const(1, D_MODEL), const(1, D_MODEL),
        ],
        out_specs=pl.BlockSpec((tm, D_MODEL), lambda i: (i, 0)),
        out_shape=jax.ShapeDtypeStruct((T, D_MODEL), F32),
        compiler_params=_cparams(("arbitrary",)),
        name="final",
    )(h1, yg, gates, p2, wple, wpg, bpg, g2, b2, g3, b3)


def _layer(h2d, p2d, batch, seq, alpha, w_in, b_in, gmlp_ln_g, gmlp_ln_b, w_spatial, b_spatial,
           w_branch_a, w_branch_b, w_out, b_out, ln1_g, ln1_b, w_router, b_router,
           w_gate_up, b_gate_up, w_down, b_down, ln2_g, ln2_b, w_ple, w_ple_gate, b_ple_gate,
           ln3_g, ln3_b):
    T = batch * seq
    tm = math.gcd(T, ROW_TILE_A)
    bm = MOE_BLOCK
    off_f = 3 * FOX_WIDTH
    off_u = off_f + N_HEADS
    row = lambda v: v.reshape(1, -1).astype(F32)

    w1 = jnp.concatenate([w_in[:, :off_u], jnp.zeros((D_MODEL, LANES - N_HEADS), F32)], axis=1).astype(BF16)
    b1 = jnp.concatenate([b_in[:off_u], jnp.zeros((LANES - N_HEADS,), F32)]).reshape(1, -1)
    w2 = w_in[:, off_u:].astype(BF16)
    b2 = row(b_in[off_u:])
    bs_tile = jnp.repeat(b_spatial.T, HEAD_DIM, axis=1)
    wr = jnp.concatenate([w_router, jnp.zeros((D_MODEL, LANES - N_EXPERTS), F32)], axis=1)
    br = jnp.concatenate([b_router, jnp.full((LANES - N_EXPERTS,), -1e30, F32)]).reshape(1, -1)

    q, k, v, f_pad = _qkvf(h2d, w1, b1, tm)
    ccol, crow = _decay(f_pad, batch, seq)
    attn = _attention(q, k, v, ccol, crow, batch, seq)
    h1, idx_o, gate_o, rank_o, cnt_o = _mix(
        h2d, attn, w2, b2, row(gmlp_ln_g), row(gmlp_ln_b), w_spatial.astype(BF16), bs_tile,
        w_branch_a.astype(BF16), w_branch_b.astype(BF16), w_out.astype(BF16), row(b_out),
        row(ln1_g), row(ln1_b), wr, br, tm, alpha)

    counts = cnt_o[0, :N_EXPERTS]
    padded = (counts + bm - 1) // bm * bm
    pad_end = jnp.cumsum(padded)
    pad_start = pad_end - padded
    dest = (jnp.take(pad_start, idx_o[:, :TOP_K]) + rank_o[:, :TOP_K]).reshape(-1)
    n_assign = T * TOP_K
    n_blk = -(-n_assign // bm) + N_EXPERTS
    n_rows = n_blk * bm
    n_used = pad_end[-1] // bm
    blk_src = jnp.minimum(jnp.arange(n_blk, dtype=I32), n_used - 1).astype(I32)
    blk_e = jnp.minimum(jnp.searchsorted(pad_end, blk_src * bm, side="right"), N_EXPERTS - 1).astype(I32)
    r = jnp.arange(bm, dtype=I32)[None, :]
    fill_pos = (pad_start + counts)[:, None] + r
    fill_ok = (counts[:, None] + r) < padded[:, None]
    spare = n_rows + jnp.arange(N_EXPERTS * bm, dtype=I32).reshape(N_EXPERTS, bm)
    fill_dst = jnp.where(fill_ok, fill_pos, spare).reshape(-1).astype(I32)
    tok = jnp.arange(n_assign, dtype=I32) // TOP_K
    sidx = jnp.concatenate([tok, jnp.zeros((N_EXPERTS * bm,), I32)])
    didx = jnp.concatenate([dest.astype(I32), fill_dst])

    h1_rows = h1.reshape(T, ROW_TILE, LANES)
    xpad = _rowcopy(sidx, didx, h1_rows, n_rows + N_EXPERTS * bm)
    ypad = _moe(blk_e, blk_src, n_used.reshape(1).astype(I32), xpad.reshape(-1, D_MODEL),
                w_gate_up, b_gate_up.reshape(N_EXPERTS, 1, -1), w_down, b_down.reshape(N_EXPERTS, 1, -1),
                n_blk, bm)
    yg = _rowcopy(dest.astype(I32), jnp.arange(n_assign, dtype=I32),
                  ypad.reshape(n_rows, ROW_TILE, LANES), n_assign)
    return _final(h1, yg.reshape(T, TOP_K * D_MODEL), gate_o, p2d, w_ple.astype(BF16),
                  w_ple_gate.astype(BF16), row(b_ple_gate), row(ln2_g), row(ln2_b),
                  row(ln3_g), row(ln3_b), tm, alpha)


def kernel(x, p, w_in, b_in, gmlp_ln_g, gmlp_ln_b, w_spatial, b_spatial, w_branch_a, w_branch_b, w_out, b_out, ln1_g, ln1_b, w_router, b_router, w_gate_up, b_gate_up, w_down, b_down, ln2_g, ln2_b, w_ple, w_ple_gate, b_ple_gate, ln3_g, ln3_b):
    batch, seq, d = x.shape
    depth = w_in.shape[0]
    assert d == D_MODEL and seq % GMLP_CHUNK == 0
    alpha = (2.0 * depth) ** 0.25
    h = x.reshape(batch * seq, d)
    for i in range(depth):
        h = _layer(h, p[i].reshape(batch * seq, PLE_DIM), batch, seq, alpha,
                   w_in[i], b_in[i], gmlp_ln_g[i], gmlp_ln_b[i], w_spatial[i], b_spatial[i],
                   w_branch_a[i], w_branch_b[i], w_out[i], b_out[i], ln1_g[i], ln1_b[i],
                   w_router[i], b_router[i], w_gate_up[i], b_gate_up[i], w_down[i], b_down[i],
                   ln2_g[i], ln2_b[i], w_ple[i], w_ple_gate[i], b_ple_gate[i], ln3_g[i], ln3_b[i])
    return h.reshape(batch, seq, d)
```

```python
import functools
import math

import jax
import jax.numpy as jnp
from jax import lax
from jax.experimental import pallas as pl
from jax.experimental.pallas import tpu as pltpu

F32 = jnp.float32
BF16 = jnp.bfloat16
I32 = jnp.int32

D_MODEL = 1024
N_HEADS = 8
HEAD_DIM = 64
FOX_WIDTH = N_HEADS * HEAD_DIM
GMLP_WIDTH = 512
GMLP_CHUNK = 128
N_EXPERTS = 32
TOP_K = 4
D_FF = 1024
PLE_DIM = 256
SWIGLU_LIMIT = 7.0
SWIGLU_ALPHA = 1.702
LN_EPS = 1e-5
LANES = 128
ROW_TILE = 8
VMEM_LIMIT = 56 * 1024 * 1024

MOE_BLOCK = 256
ATTN_TILE = 256
ROW_TILE_A = 512
SCATTER_TOKENS = 1024
GATHER_ROWS = 2048
COPY_SUB = 256


def _cparams(sem):
    return pltpu.CompilerParams(dimension_semantics=sem, vmem_limit_bytes=VMEM_LIMIT)


def _gelu(x):
    c = math.sqrt(2.0 / math.pi)
    return 0.5 * x * (1.0 + jnp.tanh(c * (x + 0.044715 * (x * x * x))))


def _layer_norm(x, g, b):
    mu = jnp.mean(x, axis=-1, keepdims=True)
    xc = x - mu
    var = jnp.mean(xc * xc, axis=-1, keepdims=True)
    return xc * lax.rsqrt(var + LN_EPS) * g + b


def _split3(x):
    hi = x.astype(BF16)
    r = x - hi.astype(F32)
    mid = r.astype(BF16)
    lo = (r - mid.astype(F32)).astype(BF16)
    return hi, mid, lo


def _dot(a, b):
    return jnp.dot(a, b, preferred_element_type=F32)


def _qkvf_kernel(x_ref, w_ref, b_ref, q_ref, k_ref, v_ref, f_ref):
    x = x_ref[...].astype(BF16)
    proj = _dot(x, w_ref[...]) + b_ref[...]
    q_ref[...] = (proj[:, :FOX_WIDTH] * (HEAD_DIM ** -0.5)).astype(BF16)
    k_ref[...] = proj[:, FOX_WIDTH:2 * FOX_WIDTH].astype(BF16)
    v_ref[...] = proj[:, 2 * FOX_WIDTH:3 * FOX_WIDTH].astype(BF16)
    f_ref[...] = proj[:, 3 * FOX_WIDTH:]


def _qkvf(x2, w, b, tm):
    T = x2.shape[0]
    n_out = w.shape[1]
    return pl.pallas_call(
        _qkvf_kernel,
        grid=(T // tm,),
        in_specs=[
            pl.BlockSpec((tm, D_MODEL), lambda i: (i, 0)),
            pl.BlockSpec((D_MODEL, n_out), lambda i: (0, 0)),
            pl.BlockSpec((1, n_out), lambda i: (0, 0)),
        ],
        out_specs=[
            pl.BlockSpec((tm, FOX_WIDTH), lambda i: (i, 0)),
            pl.BlockSpec((tm, FOX_WIDTH), lambda i: (i, 0)),
            pl.BlockSpec((tm, FOX_WIDTH), lambda i: (i, 0)),
            pl.BlockSpec((tm, LANES), lambda i: (i, 0)),
        ],
        out_shape=[
            jax.ShapeDtypeStruct((T, FOX_WIDTH), BF16),
            jax.ShapeDtypeStruct((T, FOX_WIDTH), BF16),
            jax.ShapeDtypeStruct((T, FOX_WIDTH), BF16),
            jax.ShapeDtypeStruct((T, LANES), F32),
        ],
        compiler_params=_cparams(("arbitrary",)),
        name="qkvf",
    )(x2, w, b)


def _decay_kernel(f_ref, ccol_ref, crow_ref, *, seq, blk):
    r = lax.broadcasted_iota(I32, (blk, blk), 0)
    c = lax.broadcasted_iota(I32, (blk, blk), 1)
    tri = jnp.where(c <= r, 1.0, 0.0).astype(BF16)
    carry = jnp.zeros((1, LANES), F32)
    for i in range(seq // blk):
        f = f_ref[i * blk:(i + 1) * blk, :]
        ls = jnp.minimum(f, 0.0) - jnp.log1p(jnp.exp(-jnp.abs(f)))
        hi, mid, lo = _split3(ls)
        cs = _dot(tri, hi) + _dot(tri, mid) + _dot(tri, lo) + carry
        ccol_ref[i * blk:(i + 1) * blk, :] = cs
        carry = cs[blk - 1:blk, :]
        for j in range(blk // LANES):
            t = cs[j * LANES:(j + 1) * LANES, :].T
            off = i * blk + j * LANES
            crow_ref[:, off:off + LANES] = t[:N_HEADS, :]


def _decay(f_pad, batch, seq):
    blk = 256 if seq % 256 == 0 else LANES
    return pl.pallas_call(
        functools.partial(_decay_kernel, seq=seq, blk=blk),
        grid=(batch,),
        in_specs=[pl.BlockSpec((seq, LANES), lambda b: (b, 0))],
        out_specs=[
            pl.BlockSpec((seq, LANES), lambda b: (b, 0)),
            pl.BlockSpec((None, N_HEADS, seq), lambda b: (b, 0, 0)),
        ],
        out_shape=[
            jax.ShapeDtypeStruct((batch * seq, LANES), F32),
            jax.ShapeDtypeStruct((batch, N_HEADS, seq), F32),
        ],
        compiler_params=_cparams(("arbitrary",)),
        name="decay",
    )(f_pad)


def _attn_kernel(q_ref, k_ref, v_ref, ccol_ref, crow_ref, o_ref, *, tq):
    hp = pl.program_id(1)
    qi = pl.program_id(2)
    lane = lax.broadcasted_iota(I32, (tq, LANES), 1)
    row = lax.broadcasted_iota(I32, (tq, tq), 0)
    col = lax.broadcasted_iota(I32, (tq, tq), 1)
    q = q_ref[...]
    ccol = ccol_ref[...]
    outs = []
    for j in range(2):
        h = hp * 2 + j
        in_half = (lane >= HEAD_DIM * j) & (lane < HEAD_DIM * (j + 1))
        qj = jnp.where(in_half, q, jnp.zeros_like(q))
        cq = jnp.sum(jnp.where(lane == h, ccol, 0.0), axis=1, keepdims=True)

        def step(kt, carry, masked, qj=qj, cq=cq, h=h):
            m, l, acc = carry
            start = pl.multiple_of(kt * tq, tq)
            kb = k_ref[pl.ds(start, tq), :]
            vb = v_ref[pl.ds(start, tq), :]
            ck = crow_ref[pl.ds(h, 1), pl.ds(start, tq)]
            s = lax.dot_general(qj, kb, (((1,), (1,)), ((), ())), preferred_element_type=F32)
            s = s + (cq - ck)
            if masked:
                s = jnp.where(col <= row, s, -jnp.inf)
            m_new = jnp.maximum(m, jnp.max(s, axis=1, keepdims=True))
            a = jnp.exp(m - m_new)
            p = jnp.exp(s - m_new)
            l = a * l + jnp.sum(p, axis=1, keepdims=True)
            acc = a * acc + _dot(p.astype(BF16), vb)
            return m_new, l, acc

        init = (jnp.full((tq, 1), -jnp.inf, F32), jnp.zeros((tq, 1), F32), jnp.zeros((tq, LANES), F32))
        carry = lax.fori_loop(0, qi, functools.partial(step, masked=False), init)
        m, l, acc = step(qi, carry, True)
        outs.append(acc / l)
    o_ref[...] = jnp.where(lane < HEAD_DIM, outs[0], outs[1]).astype(BF16)


def _attention(q, k, v, ccol, crow, batch, seq):
    tq = min(ATTN_TILE, seq)
    nq = seq // tq
    T = batch * seq
    return pl.pallas_call(
        functools.partial(_attn_kernel, tq=tq),
        grid=(batch, N_HEADS // 2, nq),
        in_specs=[
            pl.BlockSpec((tq, LANES), lambda b, hp, qi: (b * nq + qi, hp)),
            pl.BlockSpec((seq, LANES), lambda b, hp, qi: (b, hp)),
            pl.BlockSpec((seq, LANES), lambda b, hp, qi: (b, hp)),
            pl.BlockSpec((tq, LANES), lambda b, hp, qi: (b * nq + qi, 0)),
            pl.BlockSpec((None, N_HEADS, seq), lambda b, hp, qi: (b, 0, 0)),
        ],
        out_specs=pl.BlockSpec((tq, LANES), lambda b, hp, qi: (b * nq + qi, hp)),
        out_shape=jax.ShapeDtypeStruct((T, FOX_WIDTH), BF16),
        compiler_params=_cparams(("arbitrary", "arbitrary", "arbitrary")),
        name="attn",
    )(q, k, v, ccol, crow)


def _mix_kernel(x_ref, attn_ref, w2_ref, b2_ref, lng_ref, lnb_ref, ws_ref, bs_ref,
                wa_ref, wb_ref, wo_ref, bo_ref, g1_ref, b1_ref, wr_ref, br_ref,
                h_ref, idx_ref, gate_ref, rank_ref, cnt_ref, carry_ref, *, tm, alpha):
    i = pl.program_id(0)

    @pl.when(i == 0)
    def _():
        carry_ref[...] = jnp.zeros_like(carry_ref)

    x = x_ref[...]
    proj = _dot(x.astype(BF16), w2_ref[...]) + b2_ref[...]
    u = _gelu(proj[:, :GMLP_WIDTH])
    gv = _gelu(proj[:, GMLP_WIDTH:2 * GMLP_WIDTH])
    vln = _layer_norm(gv, lng_ref[...], lnb_ref[...]).astype(BF16)

    cr = lax.broadcasted_iota(I32, (GMLP_CHUNK, GMLP_CHUNK), 0)
    cc = lax.broadcasted_iota(I32, (GMLP_CHUNK, GMLP_CHUNK), 1)
    tril = cc <= cr
    lo_half = cc < HEAD_DIM
    zero_w = jnp.zeros((GMLP_CHUNK, GMLP_CHUNK), BF16)
    n_slab = GMLP_WIDTH // LANES
    lhs = []
    for s in range(n_slab):
        w0 = jnp.where(tril, ws_ref[2 * s], zero_w)
        w1 = jnp.where(tril, ws_ref[2 * s + 1], zero_w)
        lhs.append(jnp.concatenate([w0, w1], axis=1))
    bs = bs_ref[...]
    rows = []
    for c in range(tm // GMLP_CHUNK):
        cols = []
        for s in range(n_slab):
            vs = vln[c * GMLP_CHUNK:(c + 1) * GMLP_CHUNK, s * LANES:(s + 1) * LANES]
            rhs = jnp.concatenate([jnp.where(lo_half, vs, zero_w), jnp.where(lo_half, zero_w, vs)], axis=0)
            cols.append(_dot(lhs[s], rhs))
        rows.append(jnp.concatenate(cols, axis=1) + bs)
    sp = jnp.concatenate(rows, axis=0) if len(rows) > 1 else rows[0]
    sgu = (u * sp).astype(BF16)

    ga = jax.nn.sigmoid(proj[:, 2 * GMLP_WIDTH:2 * GMLP_WIDTH + D_MODEL])
    gb = jax.nn.sigmoid(proj[:, 2 * GMLP_WIDTH + D_MODEL:])
    merged = ga * _dot(attn_ref[...], wa_ref[...]) + gb * _dot(sgu, wb_ref[...])
    mix = _dot(merged.astype(BF16), wo_ref[...]) + bo_ref[...]
    h = _layer_norm(alpha * x + mix, g1_ref[...], b1_ref[...])
    h_ref[...] = h

    a_hi = h.astype(BF16)
    a_lo = (h - a_hi.astype(F32)).astype(BF16)
    wr = wr_ref[...]
    w_hi = wr.astype(BF16)
    w_lo = (wr - w_hi.astype(F32)).astype(BF16)
    logits = _dot(a_hi, w_hi) + _dot(a_lo, w_hi) + _dot(a_hi, w_lo) + br_ref[...]

    lane_i = lax.broadcasted_iota(I32, (tm, LANES), 1)
    lane_f = lane_i.astype(F32)
    vals, idxs = [], []
    l = logits
    for _ in range(TOP_K):
        m = jnp.max(l, axis=1, keepdims=True)
        ix = jnp.min(jnp.where(l == m, lane_f, float(LANES)), axis=1, keepdims=True)
        vals.append(m)
        idxs.append(ix)
        l = jnp.where(lane_f == ix, -jnp.inf, l)
    es = [jnp.exp(v - vals[0]) for v in vals]
    den = es[0] + es[1] + es[2] + es[3]

    onehot = jnp.zeros((tm, LANES), F32)
    idx_out = jnp.zeros((tm, LANES), F32)
    gate_out = jnp.zeros((tm, LANES), F32)
    for k in range(TOP_K):
        onehot = onehot + jnp.where(lane_f == idxs[k], 1.0, 0.0)
        idx_out = jnp.where(lane_i == k, idxs[k], idx_out)
        gate_out = jnp.where(lane_i == k, es[k] / den, gate_out)

    tr = lax.broadcasted_iota(I32, (tm, tm), 0)
    tc = lax.broadcasted_iota(I32, (tm, tm), 1)
    strict = jnp.where(tc < tr, 1.0, 0.0).astype(BF16)
    carry = carry_ref[0:1, :]
    before = _dot(strict, onehot.astype(BF16)) + carry
    rank_out = jnp.zeros((tm, LANES), F32)
    for k in range(TOP_K):
        rk = jnp.sum(jnp.where(lane_f == idxs[k], before, 0.0), axis=1, keepdims=True)
        rank_out = jnp.where(lane_i == k, rk, rank_out)
    new_carry = carry + jnp.sum(onehot, axis=0, keepdims=True)
    carry_ref[...] = jnp.broadcast_to(new_carry, carry_ref.shape)
    cnt_ref[...] = jnp.broadcast_to(new_carry, cnt_ref.shape).astype(I32)
    idx_ref[...] = idx_out.astype(I32)
    gate_ref[...] = gate_out
    rank_ref[...] = rank_out.astype(I32)


def _mix(x2, attn, w2, b2, lng, lnb, ws, bs_tile, wa, wb, wo, bo, g1, b1, wr, br, tm, alpha):
    T = x2.shape[0]
    n2 = w2.shape[1]
    const = lambda *shape: pl.BlockSpec(shape, lambda i: (0,) * len(shape))
    return pl.pallas_call(
        functools.partial(_mix_kernel, tm=tm, alpha=alpha),
        grid=(T // tm,),
        in_specs=[
            pl.BlockSpec((tm, D_MODEL), lambda i: (i, 0)),
            pl.BlockSpec((tm, FOX_WIDTH), lambda i: (i, 0)),
            const(D_MODEL, n2), const(1, n2),
            const(1, GMLP_WIDTH), const(1, GMLP_WIDTH),
            const(GMLP_WIDTH // HEAD_DIM, GMLP_CHUNK, GMLP_CHUNK), const(GMLP_CHUNK, GMLP_WIDTH),
            const(FOX_WIDTH, D_MODEL), const(GMLP_WIDTH, D_MODEL),
            const(D_MODEL, D_MODEL), const(1, D_MODEL),
            const(1, D_MODEL), const(1, D_MODEL),
            const(D_MODEL, LANES), const(1, LANES),
        ],
        out_specs=[
            pl.BlockSpec((tm, D_MODEL), lambda i: (i, 0)),
            pl.BlockSpec((tm, LANES), lambda i: (i, 0)),
            pl.BlockSpec((tm, LANES), lambda i: (i, 0)),
            pl.BlockSpec((tm, LANES), lambda i: (i, 0)),
            pl.BlockSpec((ROW_TILE, LANES), lambda i: (0, 0)),
        ],
        out_shape=[
            jax.ShapeDtypeStruct((T, D_MODEL), F32),
            jax.ShapeDtypeStruct((T, LANES), I32),
            jax.ShapeDtypeStruct((T, LANES), F32),
            jax.ShapeDtypeStruct((T, LANES), I32),
            jax.ShapeDtypeStruct((ROW_TILE, LANES), I32),
        ],
        scratch_shapes=[pltpu.VMEM((ROW_TILE, LANES), F32)],
        compiler_params=_cparams(("arbitrary",)),
        name="mix",
    )(x2, attn, w2, b2, lng, lnb, ws, bs_tile, wa, wb, wo, bo, g1, b1, wr, br)


def _pipelined_row_dmas(n_sub, issue, drain):
    issue(0)

    def outer(g, carry):
        @pl.when(g + 1 < n_sub)
        def _():
            issue(g + 1)

        drain()
        return carry

    lax.fori_loop(0, n_sub, outer, 0)


def _scatter_kernel(fbase_ref, fcnt_ref, didx_ref, src_ref, dst_ref, sem, *, tn, sub, unroll):
    def row_wait():
        pltpu.make_async_copy(src_ref.at[0], dst_ref.at[0], sem).wait()

    def issue(g):
        def body(j, carry):
            for u in range(unroll):
                t = g * sub + j * unroll + u
                for k in range(TOP_K):
                    pltpu.make_async_copy(src_ref.at[t], dst_ref.at[didx_ref[t * TOP_K + k]], sem).start()
            return carry

        lax.fori_loop(0, sub // unroll, body, 0)

    def drain():
        def body(j, carry):
            for _ in range(unroll * TOP_K):
                row_wait()
            return carry

        lax.fori_loop(0, sub // unroll, body, 0)

    _pipelined_row_dmas(tn // sub, issue, drain)

    @pl.when(pl.program_id(0) == 0)
    def _():
        def per_expert(e, carry):
            base = fbase_ref[e]

            def start_one(r, c):
                pltpu.make_async_copy(src_ref.at[0], dst_ref.at[base + r], sem).start()
                return c

            def wait_one(r, c):
                row_wait()
                return c

            lax.fori_loop(0, fcnt_ref[e], start_one, 0)
            lax.fori_loop(0, fcnt_ref[e], wait_one, 0)
            return carry

        lax.fori_loop(0, N_EXPERTS, per_expert, 0)


def _scatter_rows(fill_base, fill_cnt, didx, src3, n_dst):
    T = src3.shape[0]
    tn = math.gcd(T, SCATTER_TOKENS)
    sub = math.gcd(tn, COPY_SUB)
    unroll = math.gcd(sub, 4)
    grid_spec = pltpu.PrefetchScalarGridSpec(
        num_scalar_prefetch=2,
        grid=(T // tn,),
        in_specs=[
            pl.BlockSpec((tn * TOP_K,), lambda i, fb, fc: (i,), memory_space=pltpu.SMEM),
            pl.BlockSpec((tn, ROW_TILE, LANES), lambda i, fb, fc: (i, 0, 0)),
        ],
        out_specs=pl.BlockSpec(memory_space=pl.ANY),
        scratch_shapes=[pltpu.SemaphoreType.DMA(())],
    )
    return pl.pallas_call(
        functools.partial(_scatter_kernel, tn=tn, sub=sub, unroll=unroll),
        grid_spec=grid_spec,
        out_shape=jax.ShapeDtypeStruct((n_dst, ROW_TILE, LANES), src3.dtype),
        compiler_params=_cparams(("arbitrary",)),
        name="scatter_rows",
    )(fill_base, fill_cnt, didx, src3)


def _gather_kernel(sidx_ref, src_ref, out_ref, sem, *, n, sub, unroll):
    def issue(g):
        def body(j, carry):
            for u in range(unroll):
                r = g * sub + j * unroll + u
                pltpu.make_async_copy(src_ref.at[sidx_ref[r]], out_ref.at[r], sem).start()
            return carry

        lax.fori_loop(0, sub // unroll, body, 0)

    def drain():
        def body(j, carry):
            for _ in range(unroll):
                pltpu.make_async_copy(src_ref.at[0], out_ref.at[0], sem).wait()
            return carry

        lax.fori_loop(0, sub // unroll, body, 0)

    _pipelined_row_dmas(n // sub, issue, drain)


def _gather_rows(sidx, src3):
    n_all = sidx.shape[0]
    n = math.gcd(n_all, GATHER_ROWS)
    sub = math.gcd(n, COPY_SUB * TOP_K)
    unroll = math.gcd(sub, 16)
    return pl.pallas_call(
        functools.partial(_gather_kernel, n=n, sub=sub, unroll=unroll),
        grid=(n_all // n,),
        in_specs=[
            pl.BlockSpec((n,), lambda i: (i,), memory_space=pltpu.SMEM),
            pl.BlockSpec(memory_space=pl.ANY),
        ],
        out_specs=pl.BlockSpec((n, ROW_TILE, LANES), lambda i: (i, 0, 0)),
        out_shape=jax.ShapeDtypeStruct((n_all, ROW_TILE, LANES), src3.dtype),
        scratch_shapes=[pltpu.SemaphoreType.DMA(())],
        compiler_params=_cparams(("arbitrary",)),
        name="gather_rows",
    )(sidx, src3)


def _moe_kernel(be_ref, bsrc_ref, nu_ref, x_ref, wgu_ref, bgu_ref, wdn_ref, bdn_ref, y_ref,
                wgu_bf, wdn_bf):
    i = pl.program_id(0)
    active = i < nu_ref[0]
    prev = jnp.maximum(i - 1, 0)
    fresh = (i == 0) | (be_ref[i] != be_ref[prev])

    @pl.when(active & fresh)
    def _():
        wgu_bf[...] = wgu_ref[...].astype(BF16)
        wdn_bf[...] = wdn_ref[...].astype(BF16)

    @pl.when(active)
    def _():
        x = x_ref[...].astype(BF16)
        gu = _dot(x, wgu_bf[...]) + bgu_ref[...]
        gate = jnp.minimum(gu[:, :D_FF], SWIGLU_LIMIT)
        up = jnp.clip(gu[:, D_FF:], -SWIGLU_LIMIT, SWIGLU_LIMIT)
        hid = (up + 1.0) * (gate * jax.nn.sigmoid(SWIGLU_ALPHA * gate))
        y_ref[...] = _dot(hid.astype(BF16), wdn_bf[...]) + bdn_ref[...]


def _moe(blk_e, blk_src, n_used, xpad, wgu, bgu, wdn, bdn, n_blk, bm):
    grid_spec = pltpu.PrefetchScalarGridSpec(
        num_scalar_prefetch=3,
        grid=(n_blk,),
        in_specs=[
            pl.BlockSpec((bm, D_MODEL), lambda i, be, bs, nu: (bs[i], 0)),
            pl.BlockSpec((None, D_MODEL, 2 * D_FF), lambda i, be, bs, nu: (be[i], 0, 0)),
            pl.BlockSpec((None, 1, 2 * D_FF), lambda i, be, bs, nu: (be[i], 0, 0)),
            pl.BlockSpec((None, D_FF, D_MODEL), lambda i, be, bs, nu: (be[i], 0, 0)),
            pl.BlockSpec((None, 1, D_MODEL), lambda i, be, bs, nu: (be[i], 0, 0)),
        ],
        out_specs=pl.BlockSpec((bm, D_MODEL), lambda i, be, bs, nu: (bs[i], 0)),
        scratch_shapes=[pltpu.VMEM((D_MODEL, 2 * D_FF), BF16), pltpu.VMEM((D_FF, D_MODEL), BF16)],
    )
    return pl.pallas_call(
        _moe_kernel,
        grid_spec=grid_spec,
        out_shape=jax.ShapeDtypeStruct((n_blk * bm, D_MODEL), F32),
        compiler_params=_cparams(("arbitrary",)),
        name="moe",
    )(blk_e, blk_src, n_used, xpad, wgu, bgu, wdn, bdn)


def _final_kernel(h_ref, yg_ref, gate_ref, p_ref, wple_ref, wpg_ref, bpg_ref,
                  g2_ref, b2_ref, g3_ref, b3_ref, o_ref, *, alpha):
    h = h_ref[...]
    gates = gate_ref[...]
    ffn = gates[:, 0:1] * yg_ref[:, 0:D_MODEL]
    for k in range(1, TOP_K):
        ffn = ffn + gates[:, k:k + 1] * yg_ref[:, k * D_MODEL:(k + 1) * D_MODEL]
    h2 = _layer_norm(alpha * h + ffn, g2_ref[...], b2_ref[...])
    emb = _dot(p_ref[...].astype(BF16), wple_ref[...])
    pg = jax.nn.sigmoid(_dot(h2.astype(BF16), wpg_ref[...]) + bpg_ref[...])
    o_ref[...] = _layer_norm(alpha * h2 + emb * pg, g3_ref[...], b3_ref[...])


def _final(h1, yg, gates, p2, wple, wpg, bpg, g2, b2, g3, b3, tm, alpha):
    T = h1.shape[0]
    const = lambda *shape: pl.BlockSpec(shape, lambda i: (0,) * len(shape))
    return pl.pallas_call(
        functools.partial(_final_kernel, alpha=alpha),
        grid=(T // tm,),
        in_specs=[
            pl.BlockSpec((tm, D_MODEL), lambda i: (i, 0)),
            pl.BlockSpec((tm, TOP_K * D_MODEL), lambda i: (i, 0)),
            pl.BlockSpec((tm, LANES), lambda i: (i, 0)),
            pl.BlockSpec((tm, PLE_DIM), lambda i: (i, 0)),
            const(PLE_DIM, D_MODEL), const(D_MODEL, D_MODEL), const(1, D_MODEL),
            const(1, D_MODEL), const(1, D_MODEL), const(1, D_MODEL), const(1, D_MODEL),
        ],
        out_specs=pl.BlockSpec((tm, D_MODEL), lambda i: (i, 0)),
        out_shape=jax.ShapeDtypeStruct((T, D_MODEL), F32),
        compiler_params=_cparams(("arbitrary",)),
        name="final",
    )(h1, yg, gates, p2, wple, wpg, bpg, g2, b2, g3, b3)


def _layer(h2d, p2d, batch, seq, alpha, w_in, b_in, gmlp_ln_g, gmlp_ln_b, w_spatial, b_spatial,
           w_branch_a, w_branch_b, w_out, b_out, ln1_g, ln1_b, w_router, b_router,
           w_gate_up, b_gate_up, w_down, b_down, ln2_g, ln2_b, w_ple, w_ple_gate, b_ple_gate,
           ln3_g, ln3_b):
    T = batch * seq
    tm = math.gcd(T, ROW_TILE_A)
    bm = MOE_BLOCK
    off_f = 3 * FOX_WIDTH
    off_u = off_f + N_HEADS
    row = lambda v: v.reshape(1, -1).astype(F32)

    w1 = jnp.concatenate([w_in[:, :off_u], jnp.zeros((D_MODEL, LANES - N_HEADS), F32)], axis=1).astype(BF16)
    b1 = jnp.concatenate([b_in[:off_u], jnp.zeros((LANES - N_HEADS,), F32)]).reshape(1, -1)
    w2 = w_in[:, off_u:].astype(BF16)
    b2 = row(b_in[off_u:])
    bs_tile = jnp.repeat(b_spatial.T, HEAD_DIM, axis=1)
    wr = jnp.concatenate([w_router, jnp.zeros((D_MODEL, LANES - N_EXPERTS), F32)], axis=1)
    br = jnp.concatenate([b_router, jnp.full((LANES - N_EXPERTS,), -1e30, F32)]).reshape(1, -1)

    q, k, v, f_pad = _qkvf(h2d, w1, b1, tm)
    ccol, crow = _decay(f_pad, batch, seq)
    attn = _attention(q, k, v, ccol, crow, batch, seq)
    h1, idx_o, gate_o, rank_o, cnt_o = _mix(
        h2d, attn, w2, b2, row(gmlp_ln_g), row(gmlp_ln_b), w_spatial.astype(BF16), bs_tile,
        w_branch_a.astype(BF16), w_branch_b.astype(BF16), w_out.astype(BF16), row(b_out),
        row(ln1_g), row(ln1_b), wr, br, tm, alpha)

    counts = cnt_o[0, :N_EXPERTS]
    padded = (counts + bm - 1) // bm * bm
    pad_end = jnp.cumsum(padded)
    pad_start = pad_end - padded
    dest = (jnp.take(pad_start, idx_o[:, :TOP_K]) + rank_o[:, :TOP_K]).reshape(-1)
    n_assign = T * TOP_K
    n_blk = -(-n_assign // bm) + N_EXPERTS
    n_rows = n_blk * bm
    n_used = pad_end[-1] // bm
    blk_src = jnp.minimum(jnp.arange(n_blk, dtype=I32), n_used - 1).astype(I32)
    blk_e = jnp.minimum(jnp.sum(pad_end[None, :] <= (blk_src * bm)[:, None], axis=1), N_EXPERTS - 1).astype(I32)
    dest = dest.astype(I32)

    h1_rows = h1.reshape(T, ROW_TILE, LANES)
    xpad = _scatter_rows((pad_start + counts).astype(I32), (padded - counts).astype(I32), dest, h1_rows, n_rows)
    ypad = _moe(blk_e, blk_src, n_used.reshape(1).astype(I32), xpad.reshape(-1, D_MODEL),
                w_gate_up, b_gate_up.reshape(N_EXPERTS, 1, -1), w_down, b_down.reshape(N_EXPERTS, 1, -1),
                n_blk, bm)
    yg = _gather_rows(dest, ypad.reshape(n_rows, ROW_TILE, LANES))
    return _final(h1, yg.reshape(T, TOP_K * D_MODEL), gate_o, p2d, w_ple.astype(BF16),
                  w_ple_gate.astype(BF16), row(b_ple_gate), row(ln2_g), row(ln2_b),
                  row(ln3_g), row(ln3_b), tm, alpha)


def kernel(x, p, w_in, b_in, gmlp_ln_g, gmlp_ln_b, w_spatial, b_spatial, w_branch_a, w_branch_b, w_out, b_out, ln1_g, ln1_b, w_router, b_router, w_gate_up, b_gate_up, w_down, b_down, ln2_g, ln2_b, w_ple, w_ple_gate, b_ple_gate, ln3_g, ln3_b):
    batch, seq, d = x.shape
    depth = w_in.shape[0]
    assert d == D_MODEL and seq % GMLP_CHUNK == 0
    alpha = (2.0 * depth) ** 0.25
    h = x.reshape(batch * seq, d)
    for i in range(depth):
        h = _layer(h, p[i].reshape(batch * seq, PLE_DIM), batch, seq, alpha,
                   w_in[i], b_in[i], gmlp_ln_g[i], gmlp_ln_b[i], w_spatial[i], b_spatial[i],
                   w_branch_a[i], w_branch_b[i], w_out[i], b_out[i], ln1_g[i], ln1_b[i],
                   w_router[i], b_router[i], w_gate_up[i], b_gate_up[i], w_down[i], b_down[i],
                   ln2_g[i], ln2_b[i], w_ple[i], w_ple_gate[i], b_ple_gate[i], ln3_g[i], ln3_b[i])
    return h.reshape(batch, seq, d)
```

```python
import functools
import math

import jax
import jax.numpy as jnp
import numpy as np
from jax import lax
from jax.experimental import pallas as pl
from jax.experimental.pallas import tpu as pltpu

F32 = jnp.float32
BF16 = jnp.bfloat16
I32 = jnp.int32

D_MODEL = 1024
N_HEADS = 8
HEAD_DIM = 64
FOX_WIDTH = N_HEADS * HEAD_DIM
GMLP_WIDTH = 512
GMLP_CHUNK = 128
N_EXPERTS = 32
TOP_K = 4
D_FF = 1024
PLE_DIM = 256
SWIGLU_LIMIT = 7.0
SWIGLU_ALPHA = 1.702
LN_EPS = 1e-5
LANES = 128
ROW_TILE = 8
VMEM_LIMIT = 56 * 1024 * 1024

MOE_BLOCK = 256
ATTN_Q_TILE = 512
ATTN_K_TILE = 512
ATTN_PAIRS = 2
LOG2E = math.log2(math.e)
ROW_TILE_A = 512
SCATTER_TOKENS = 1024
GATHER_ROWS = 2048
COPY_SUB = 256


def _cparams(sem):
    return pltpu.CompilerParams(dimension_semantics=sem, vmem_limit_bytes=VMEM_LIMIT)


def _gelu(x):
    c = math.sqrt(2.0 / math.pi)
    return 0.5 * x * (1.0 + jnp.tanh(c * (x + 0.044715 * (x * x * x))))


def _layer_norm(x, g, b):
    mu = jnp.mean(x, axis=-1, keepdims=True)
    xc = x - mu
    var = jnp.mean(xc * xc, axis=-1, keepdims=True)
    return xc * lax.rsqrt(var + LN_EPS) * g + b


def _split3(x):
    hi = x.astype(BF16)
    r = x - hi.astype(F32)
    mid = r.astype(BF16)
    lo = (r - mid.astype(F32)).astype(BF16)
    return hi, mid, lo


def _dot(a, b):
    return jnp.dot(a, b, preferred_element_type=F32)


def _qkvf_kernel(x_ref, w_ref, b_ref, q_ref, k_ref, v_ref, f_ref):
    x = x_ref[...].astype(BF16)
    proj = _dot(x, w_ref[...]) + b_ref[...]
    q_ref[...] = (proj[:, :FOX_WIDTH] * (HEAD_DIM ** -0.5 * LOG2E)).astype(BF16)
    k_ref[...] = proj[:, FOX_WIDTH:2 * FOX_WIDTH].astype(BF16)
    v_ref[...] = proj[:, 2 * FOX_WIDTH:3 * FOX_WIDTH].astype(BF16)
    f_ref[...] = proj[:, 3 * FOX_WIDTH:]


def _qkvf(x2, w, b, tm):
    T = x2.shape[0]
    n_out = w.shape[1]
    return pl.pallas_call(
        _qkvf_kernel,
        grid=(T // tm,),
        in_specs=[
            pl.BlockSpec((tm, D_MODEL), lambda i: (i, 0)),
            pl.BlockSpec((D_MODEL, n_out), lambda i: (0, 0)),
            pl.BlockSpec((1, n_out), lambda i: (0, 0)),
        ],
        out_specs=[
            pl.BlockSpec((tm, FOX_WIDTH), lambda i: (i, 0)),
            pl.BlockSpec((tm, FOX_WIDTH), lambda i: (i, 0)),
            pl.BlockSpec((tm, FOX_WIDTH), lambda i: (i, 0)),
            pl.BlockSpec((tm, LANES), lambda i: (i, 0)),
        ],
        out_shape=[
            jax.ShapeDtypeStruct((T, FOX_WIDTH), BF16),
            jax.ShapeDtypeStruct((T, FOX_WIDTH), BF16),
            jax.ShapeDtypeStruct((T, FOX_WIDTH), BF16),
            jax.ShapeDtypeStruct((T, LANES), F32),
        ],
        compiler_params=_cparams(("arbitrary",)),
        name="qkvf",
    )(x2, w, b)


def _decay_placement():
    pq = np.zeros((3 * LANES, FOX_WIDTH), np.float32)
    pk = np.zeros((3 * LANES, FOX_WIDTH), np.float32)
    cq = np.zeros((1, FOX_WIDTH), np.float32)
    ck = np.zeros((1, FOX_WIDTH), np.float32)
    for h in range(N_HEADS):
        base = (h // 2) * LANES + (HEAD_DIM if h % 2 == 0 else 0)
        for piece in range(3):
            pq[piece * LANES + h, base + piece] = 1.0
            pk[piece * LANES + h, base + 3 + piece] = -1.0
            cq[0, base + 3 + piece] = 1.0
            ck[0, base + piece] = 1.0
    return pq, pk, cq, ck


def _decay_kernel(f_ref, pq_ref, pk_ref, cq_ref, ck_ref, auxq_ref, auxk_ref, *, seq, blk):
    r = lax.broadcasted_iota(I32, (blk, blk), 0)
    c = lax.broadcasted_iota(I32, (blk, blk), 1)
    tri = jnp.where(c <= r, 1.0, 0.0).astype(BF16)
    carry = jnp.zeros((1, LANES), F32)
    for i in range(seq // blk):
        f = f_ref[i * blk:(i + 1) * blk, :]
        ls = jnp.minimum(f, 0.0) - jnp.log1p(jnp.exp(-jnp.abs(f)))
        hi, mid, lo = _split3(ls)
        cs = _dot(tri, hi) + _dot(tri, mid) + _dot(tri, lo) + carry
        carry = cs[blk - 1:blk, :]
        pieces = jnp.concatenate(_split3(cs * LOG2E), axis=1)
        auxq_ref[i * blk:(i + 1) * blk, :] = (_dot(pieces, pq_ref[...]) + cq_ref[...]).astype(BF16)
        auxk_ref[i * blk:(i + 1) * blk, :] = (_dot(pieces, pk_ref[...]) + ck_ref[...]).astype(BF16)


def _decay(f_pad, batch, seq):
    blk = 256 if seq % 256 == 0 else LANES
    pq, pk, cq, ck = _decay_placement()
    const = lambda *shape: pl.BlockSpec(shape, lambda b: (0,) * len(shape))
    return pl.pallas_call(
        functools.partial(_decay_kernel, seq=seq, blk=blk),
        grid=(batch,),
        in_specs=[
            pl.BlockSpec((seq, LANES), lambda b: (b, 0)),
            const(3 * LANES, FOX_WIDTH), const(3 * LANES, FOX_WIDTH), const(1, FOX_WIDTH), const(1, FOX_WIDTH),
        ],
        out_specs=[
            pl.BlockSpec((seq, FOX_WIDTH), lambda b: (b, 0)),
            pl.BlockSpec((seq, FOX_WIDTH), lambda b: (b, 0)),
        ],
        out_shape=[
            jax.ShapeDtypeStruct((batch * seq, FOX_WIDTH), BF16),
            jax.ShapeDtypeStruct((batch * seq, FOX_WIDTH), BF16),
        ],
        compiler_params=_cparams(("arbitrary",)),
        name="decay",
    )(f_pad, jnp.asarray(pq, BF16), jnp.asarray(pk, BF16), jnp.asarray(cq), jnp.asarray(ck))


def _attn_kernel(q_ref, auxq_ref, k_ref, auxk_ref, v_ref, o_ref, *, tq, tk):
    qi = pl.program_id(2)
    n_pairs = q_ref.shape[1] // LANES
    low_q = lax.broadcasted_iota(I32, (tq, LANES), 1) < HEAD_DIM
    qs = []
    for pr in range(n_pairs):
        q = q_ref[:, pr * LANES:(pr + 1) * LANES]
        aq = auxq_ref[:, pr * LANES:(pr + 1) * LANES]
        qs += [jnp.where(low_q, q, aq), jnp.where(low_q, aq, q)]

    def step(start, n, carry, masked):
        low_k = lax.broadcasted_iota(I32, (n, LANES), 1) < HEAD_DIM
        ones = jnp.ones((n, LANES), BF16)
        out = []
        for pr in range(n_pairs):
            lanes = slice(pr * LANES, (pr + 1) * LANES)
            kb = k_ref[pl.ds(start, n), lanes]
            ak = auxk_ref[pl.ds(start, n), lanes]
            vb = v_ref[pl.ds(start, n), lanes]
            ks = (jnp.where(low_k, kb, ak), jnp.where(low_k, ak, kb))
            vs = (jnp.where(low_k, vb, ones), jnp.where(low_k, ones, vb))
            for j in range(2):
                m, acc = carry[2 * pr + j]
                s = lax.dot_general(qs[2 * pr + j], ks[j], (((1,), (1,)), ((), ())),
                                    preferred_element_type=F32)
                if masked:
                    row = lax.broadcasted_iota(I32, (tq, n), 0)
                    col = lax.broadcasted_iota(I32, (tq, n), 1)
                    s = jnp.where(col + (start - qi * tq) <= row, s, -jnp.inf)
                m_new = jnp.maximum(m, jnp.max(s, axis=1, keepdims=True))
                p = jnp.exp2(s - m_new)
                acc = jnp.exp2(m - m_new) * acc + _dot(p.astype(BF16), vs[j])
                out.append((m_new, acc))
        return tuple(out)

    init = tuple((jnp.full((tq, 1), -jnp.inf, F32), jnp.zeros((tq, LANES), F32))
                 for _ in range(2 * n_pairs))
    carry = lax.fori_loop(0, qi, lambda t, c: step(pl.multiple_of(t * tq, tq), tq, c, False), init)
    for d in range(tq // tk):
        carry = step(pl.multiple_of(qi * tq + d * tk, tk), tk, carry, True)
    for pr in range(n_pairs):
        acc0, acc1 = carry[2 * pr][1], carry[2 * pr + 1][1]
        out0 = acc0 / acc0[:, HEAD_DIM:HEAD_DIM + 1]
        out1 = acc1 / acc1[:, 0:1]
        o_ref[:, pr * LANES:(pr + 1) * LANES] = jnp.where(low_q, out0, out1).astype(BF16)


def _attention(q, auxq, k, auxk, v, batch, seq):
    tq = math.gcd(seq, ATTN_Q_TILE)
    tk = math.gcd(tq, ATTN_K_TILE)
    nq = seq // tq
    T = batch * seq
    width = ATTN_PAIRS * LANES
    q_spec = pl.BlockSpec((tq, width), lambda b, hp, qi: (b * nq + qi, hp))
    kv_spec = pl.BlockSpec((seq, width), lambda b, hp, qi: (b, hp))
    return pl.pallas_call(
        functools.partial(_attn_kernel, tq=tq, tk=tk),
        grid=(batch, N_HEADS // (2 * ATTN_PAIRS), nq),
        in_specs=[q_spec, q_spec, kv_spec, kv_spec, kv_spec],
        out_specs=q_spec,
        out_shape=jax.ShapeDtypeStruct((T, FOX_WIDTH), BF16),
        compiler_params=_cparams(("arbitrary", "arbitrary", "arbitrary")),
        name="attn",
    )(q, auxq, k, auxk, v)


def _mix_kernel(x_ref, attn_ref, w2_ref, b2_ref, lng_ref, lnb_ref, ws_ref, bs_ref,
                wa_ref, wb_ref, wo_ref, bo_ref, g1_ref, b1_ref, wr_ref, br_ref,
                h_ref, idx_ref, gate_ref, rank_ref, cnt_ref, carry_ref, *, tm, alpha):
    i = pl.program_id(0)

    @pl.when(i == 0)
    def _():
        carry_ref[...] = jnp.zeros_like(carry_ref)

    x = x_ref[...]
    proj = _dot(x.astype(BF16), w2_ref[...]) + b2_ref[...]
    u = _gelu(proj[:, :GMLP_WIDTH])
    gv = _gelu(proj[:, GMLP_WIDTH:2 * GMLP_WIDTH])
    vln = _layer_norm(gv, lng_ref[...], lnb_ref[...]).astype(BF16)

    cr = lax.broadcasted_iota(I32, (GMLP_CHUNK, GMLP_CHUNK), 0)
    cc = lax.broadcasted_iota(I32, (GMLP_CHUNK, GMLP_CHUNK), 1)
    tril = cc <= cr
    lo_half = cc < HEAD_DIM
    zero_w = jnp.zeros((GMLP_CHUNK, GMLP_CHUNK), BF16)
    n_slab = GMLP_WIDTH // LANES
    lhs = []
    for s in range(n_slab):
        w0 = jnp.where(tril, ws_ref[2 * s], zero_w)
        w1 = jnp.where(tril, ws_ref[2 * s + 1], zero_w)
        lhs.append(jnp.concatenate([w0, w1], axis=1))
    bs = bs_ref[...]
    rows = []
    for c in range(tm // GMLP_CHUNK):
        cols = []
        for s in range(n_slab):
            vs = vln[c * GMLP_CHUNK:(c + 1) * GMLP_CHUNK, s * LANES:(s + 1) * LANES]
            rhs = jnp.concatenate([jnp.where(lo_half, vs, zero_w), jnp.where(lo_half, zero_w, vs)], axis=0)
            cols.append(_dot(lhs[s], rhs))
        rows.append(jnp.concatenate(cols, axis=1) + bs)
    sp = jnp.concatenate(rows, axis=0) if len(rows) > 1 else rows[0]
    sgu = (u * sp).astype(BF16)

    ga = jax.nn.sigmoid(proj[:, 2 * GMLP_WIDTH:2 * GMLP_WIDTH + D_MODEL])
    gb = jax.nn.sigmoid(proj[:, 2 * GMLP_WIDTH + D_MODEL:])
    merged = ga * _dot(attn_ref[...], wa_ref[...]) + gb * _dot(sgu, wb_ref[...])
    mix = _dot(merged.astype(BF16), wo_ref[...]) + bo_ref[...]
    h = _layer_norm(alpha * x + mix, g1_ref[...], b1_ref[...])
    h_ref[...] = h

    a_hi = h.astype(BF16)
    a_lo = (h - a_hi.astype(F32)).astype(BF16)
    wr = wr_ref[...]
    w_hi = wr.astype(BF16)
    w_lo = (wr - w_hi.astype(F32)).astype(BF16)
    logits = _dot(a_hi, w_hi) + _dot(a_lo, w_hi) + _dot(a_hi, w_lo) + br_ref[...]

    lane_i = lax.broadcasted_iota(I32, (tm, LANES), 1)
    lane_f = lane_i.astype(F32)
    vals, idxs = [], []
    l = logits
    for _ in range(TOP_K):
        m = jnp.max(l, axis=1, keepdims=True)
        ix = jnp.min(jnp.where(l == m, lane_f, float(LANES)), axis=1, keepdims=True)
        vals.append(m)
        idxs.append(ix)
        l = jnp.where(lane_f == ix, -jnp.inf, l)
    es = [jnp.exp(v - vals[0]) for v in vals]
    den = es[0] + es[1] + es[2] + es[3]

    onehot = jnp.zeros((tm, LANES), F32)
    idx_out = jnp.zeros((tm, LANES), F32)
    gate_out = jnp.zeros((tm, LANES), F32)
    for k in range(TOP_K):
        onehot = onehot + jnp.where(lane_f == idxs[k], 1.0, 0.0)
        idx_out = jnp.where(lane_i == k, idxs[k], idx_out)
        gate_out = jnp.where(lane_i == k, es[k] / den, gate_out)

    tr = lax.broadcasted_iota(I32, (tm, tm), 0)
    tc = lax.broadcasted_iota(I32, (tm, tm), 1)
    strict = jnp.where(tc < tr, 1.0, 0.0).astype(BF16)
    carry = carry_ref[0:1, :]
    before = _dot(strict, onehot.astype(BF16)) + carry
    rank_out = jnp.zeros((tm, LANES), F32)
    for k in range(TOP_K):
        rk = jnp.sum(jnp.where(lane_f == idxs[k], before, 0.0), axis=1, keepdims=True)
        rank_out = jnp.where(lane_i == k, rk, rank_out)
    new_carry = carry + jnp.sum(onehot, axis=0, keepdims=True)
    carry_ref[...] = jnp.broadcast_to(new_carry, carry_ref.shape)
    cnt_ref[...] = jnp.broadcast_to(new_carry, cnt_ref.shape).astype(I32)
    idx_ref[...] = idx_out.astype(I32)
    gate_ref[...] = gate_out
    rank_ref[...] = rank_out.astype(I32)


def _mix(x2, attn, w2, b2, lng, lnb, ws, bs_tile, wa, wb, wo, bo, g1, b1, wr, br, tm, alpha):
    T = x2.shape[0]
    n2 = w2.shape[1]
    const = lambda *shape: pl.BlockSpec(shape, lambda i: (0,) * len(shape))
    return pl.pallas_call(
        functools.partial(_mix_kernel, tm=tm, alpha=alpha),
        grid=(T // tm,),
        in_specs=[
            pl.BlockSpec((tm, D_MODEL), lambda i: (i, 0)),
            pl.BlockSpec((tm, FOX_WIDTH), lambda i: (i, 0)),
            const(D_MODEL, n2), const(1, n2),
            const(1, GMLP_WIDTH), const(1, GMLP_WIDTH),
            const(GMLP_WIDTH // HEAD_DIM, GMLP_CHUNK, GMLP_CHUNK), const(GMLP_CHUNK, GMLP_WIDTH),
            const(FOX_WIDTH, D_MODEL), const(GMLP_WIDTH, D_MODEL),
            const(D_MODEL, D_MODEL), const(1, D_MODEL),
            const(1, D_MODEL), const(1, D_MODEL),
            const(D_MODEL, LANES), const(1, LANES),
        ],
        out_specs=[
            pl.BlockSpec((tm, D_MODEL), lambda i: (i, 0)),
            pl.BlockSpec((tm, LANES), lambda i: (i, 0)),
            pl.BlockSpec((tm, LANES), lambda i: (i, 0)),
            pl.BlockSpec((tm, LANES), lambda i: (i, 0)),
            pl.BlockSpec((ROW_TILE, LANES), lambda i: (0, 0)),
        ],
        out_shape=[
            jax.ShapeDtypeStruct((T, D_MODEL), F32),
            jax.ShapeDtypeStruct((T, LANES), I32),
            jax.ShapeDtypeStruct((T, LANES), F32),
            jax.ShapeDtypeStruct((T, LANES), I32),
            jax.ShapeDtypeStruct((ROW_TILE, LANES), I32),
        ],
        scratch_shapes=[pltpu.VMEM((ROW_TILE, LANES), F32)],
        compiler_params=_cparams(("arbitrary",)),
        name="mix",
    )(x2, attn, w2, b2, lng, lnb, ws, bs_tile, wa, wb, wo, bo, g1, b1, wr, br)


def _pipelined_row_dmas(n_sub, issue, drain):
    issue(0)

    def outer(g, carry):
        @pl.when(g + 1 < n_sub)
        def _():
            issue(g + 1)

        drain()
        return carry

    lax.fori_loop(0, n_sub, outer, 0)


def _scatter_kernel(fbase_ref, fcnt_ref, didx_ref, src_ref, dst_ref, sem, *, tn, sub, unroll):
    def row_wait():
        pltpu.make_async_copy(src_ref.at[0], dst_ref.at[0], sem).wait()

    def issue(g):
        def body(j, carry):
            for u in range(unroll):
                t = g * sub + j * unroll + u
                for k in range(TOP_K):
                    pltpu.make_async_copy(src_ref.at[t], dst_ref.at[didx_ref[t * TOP_K + k]], sem).start(
                        priority=k % 2)
            return carry

        lax.fori_loop(0, sub // unroll, body, 0)

    def drain():
        def body(j, carry):
            for _ in range(unroll * TOP_K):
                row_wait()
            return carry

        lax.fori_loop(0, sub // unroll, body, 0)

    _pipelined_row_dmas(tn // sub, issue, drain)

    @pl.when(pl.program_id(0) == 0)
    def _():
        def per_expert(e, carry):
            base = fbase_ref[e]

            def start_one(r, c):
                pltpu.make_async_copy(src_ref.at[0], dst_ref.at[base + r], sem).start()
                return c

            def wait_one(r, c):
                row_wait()
                return c

            lax.fori_loop(0, fcnt_ref[e], start_one, 0)
            lax.fori_loop(0, fcnt_ref[e], wait_one, 0)
            return carry

        lax.fori_loop(0, N_EXPERTS, per_expert, 0)


def _scatter_rows(fill_base, fill_cnt, didx, src3, n_dst):
    T = src3.shape[0]
    tn = math.gcd(T, SCATTER_TOKENS)
    sub = math.gcd(tn, COPY_SUB)
    unroll = math.gcd(sub, 4)
    grid_spec = pltpu.PrefetchScalarGridSpec(
        num_scalar_prefetch=2,
        grid=(T // tn,),
        in_specs=[
            pl.BlockSpec((tn * TOP_K,), lambda i, fb, fc: (i,), memory_space=pltpu.SMEM),
            pl.BlockSpec((tn, ROW_TILE, LANES), lambda i, fb, fc: (i, 0, 0)),
        ],
        out_specs=pl.BlockSpec(memory_space=pl.ANY),
        scratch_shapes=[pltpu.SemaphoreType.DMA(())],
    )
    return pl.pallas_call(
        functools.partial(_scatter_kernel, tn=tn, sub=sub, unroll=unroll),
        grid_spec=grid_spec,
        out_shape=jax.ShapeDtypeStruct((n_dst, ROW_TILE, LANES), src3.dtype),
        compiler_params=_cparams(("arbitrary",)),
        name="scatter_rows",
    )(fill_base, fill_cnt, didx, src3)


def _gather_kernel(sidx_ref, src_ref, out_ref, sem, *, n, sub, unroll):
    def issue(g):
        def body(j, carry):
            for u in range(unroll):
                r = g * sub + j * unroll + u
                pltpu.make_async_copy(src_ref.at[sidx_ref[r]], out_ref.at[r], sem).start(priority=u % 2)
            return carry

        lax.fori_loop(0, sub // unroll, body, 0)

    def drain():
        def body(j, carry):
            for _ in range(unroll):
                pltpu.make_async_copy(src_ref.at[0], out_ref.at[0], sem).wait()
            return carry

        lax.fori_loop(0, sub // unroll, body, 0)

    _pipelined_row_dmas(n // sub, issue, drain)


def _gather_rows(sidx, src3):
    n_all = sidx.shape[0]
    n = math.gcd(n_all, GATHER_ROWS)
    sub = math.gcd(n, COPY_SUB * TOP_K)
    unroll = math.gcd(sub, 16)
    return pl.pallas_call(
        functools.partial(_gather_kernel, n=n, sub=sub, unroll=unroll),
        grid=(n_all // n,),
        in_specs=[
            pl.BlockSpec((n,), lambda i: (i,), memory_space=pltpu.SMEM),
            pl.BlockSpec(memory_space=pl.ANY),
        ],
        out_specs=pl.BlockSpec((n, ROW_TILE, LANES), lambda i: (i, 0, 0)),
        out_shape=jax.ShapeDtypeStruct((n_all, ROW_TILE, LANES), src3.dtype),
        scratch_shapes=[pltpu.SemaphoreType.DMA(())],
        compiler_params=_cparams(("arbitrary",)),
        name="gather_rows",
    )(sidx, src3)


def _moe_kernel(be_ref, bsrc_ref, nu_ref, x_ref, wgu_ref, bgu_ref, wdn_ref, bdn_ref, y_ref,
                wgu_bf, wdn_bf):
    i = pl.program_id(0)
    active = i < nu_ref[0]
    prev = jnp.maximum(i - 1, 0)
    fresh = (i == 0) | (be_ref[i] != be_ref[prev])

    @pl.when(active & fresh)
    def _():
        wgu_bf[...] = wgu_ref[...].astype(BF16)
        wdn_bf[...] = wdn_ref[...].astype(BF16)

    @pl.when(active)
    def _():
        x = x_ref[...].astype(BF16)
        gu = _dot(x, wgu_bf[...]) + bgu_ref[...]
        gate = jnp.minimum(gu[:, :D_FF], SWIGLU_LIMIT)
        up = jnp.clip(gu[:, D_FF:], -SWIGLU_LIMIT, SWIGLU_LIMIT)
        hid = (up + 1.0) * (gate * jax.nn.sigmoid(SWIGLU_ALPHA * gate))
        y_ref[...] = _dot(hid.astype(BF16), wdn_bf[...]) + bdn_ref[...]


def _moe(blk_e, blk_src, n_used, xpad, wgu, bgu, wdn, bdn, n_blk, bm):
    grid_spec = pltpu.PrefetchScalarGridSpec(
        num_scalar_prefetch=3,
        grid=(n_blk,),
        in_specs=[
            pl.BlockSpec((bm, D_MODEL), lambda i, be, bs, nu: (bs[i], 0)),
            pl.BlockSpec((None, D_MODEL, 2 * D_FF), lambda i, be, bs, nu: (be[i], 0, 0)),
            pl.BlockSpec((None, 1, 2 * D_FF), lambda i, be, bs, nu: (be[i], 0, 0)),
            pl.BlockSpec((None, D_FF, D_MODEL), lambda i, be, bs, nu: (be[i], 0, 0)),
            pl.BlockSpec((None, 1, D_MODEL), lambda i, be, bs, nu: (be[i], 0, 0)),
        ],
        out_specs=pl.BlockSpec((bm, D_MODEL), lambda i, be, bs, nu: (bs[i], 0)),
        scratch_shapes=[pltpu.VMEM((D_MODEL, 2 * D_FF), BF16), pltpu.VMEM((D_FF, D_MODEL), BF16)],
    )
    return pl.pallas_call(
        _moe_kernel,
        grid_spec=grid_spec,
        out_shape=jax.ShapeDtypeStruct((n_blk * bm, D_MODEL), F32),
        compiler_params=_cparams(("arbitrary",)),
        name="moe",
    )(blk_e, blk_src, n_used, xpad, wgu, bgu, wdn, bdn)


def _final_kernel(h_ref, yg_ref, gate_ref, p_ref, wple_ref, wpg_ref, bpg_ref,
                  g2_ref, b2_ref, g3_ref, b3_ref, o_ref, *, alpha):
    h = h_ref[...]
    gates = gate_ref[...]
    ffn = gates[:, 0:1] * yg_ref[:, 0:D_MODEL]
    for k in range(1, TOP_K):
        ffn = ffn + gates[:, k:k + 1] * yg_ref[:, k * D_MODEL:(k + 1) * D_MODEL]
    h2 = _layer_norm(alpha * h + ffn, g2_ref[...], b2_ref[...])
    emb = _dot(p_ref[...].astype(BF16), wple_ref[...])
    pg = jax.nn.sigmoid(_dot(h2.astype(BF16), wpg_ref[...]) + bpg_ref[...])
    o_ref[...] = _layer_norm(alpha * h2 + emb * pg, g3_ref[...], b3_ref[...])


def _final(h1, yg, gates, p2, wple, wpg, bpg, g2, b2, g3, b3, tm, alpha):
    T = h1.shape[0]
    const = lambda *shape: pl.BlockSpec(shape, lambda i: (0,) * len(shape))
    return pl.pallas_call(
        functools.partial(_final_kernel, alpha=alpha),
        grid=(T // tm,),
        in_specs=[
            pl.BlockSpec((tm, D_MODEL), lambda i: (i, 0)),
            pl.BlockSpec((tm, TOP_K * D_MODEL), lambda i: (i, 0)),
            pl.BlockSpec((tm, LANES), lambda i: (i, 0)),
            pl.BlockSpec((tm, PLE_DIM), lambda i: (i, 0)),
            const(PLE_DIM, D_MODEL), const(D_MODEL, D_MODEL), const(1, D_MODEL),
            const(1, D_MODEL), const(1, D_MODEL), const(1, D_MODEL), const(1, D_MODEL),
        ],
        out_specs=pl.BlockSpec((tm, D_MODEL), lambda i: (i, 0)),
        out_shape=jax.ShapeDtypeStruct((T, D_MODEL), F32),
        compiler_params=_cparams(("arbitrary",)),
        name="final",
    )(h1, yg, gates, p2, wple, wpg, bpg, g2, b2, g3, b3)


def _layer(h2d, p2d, batch, seq, alpha, w_in, b_in, gmlp_ln_g, gmlp_ln_b, w_spatial, b_spatial,
           w_branch_a, w_branch_b, w_out, b_out, ln1_g, ln1_b, w_router, b_router,
           w_gate_up, b_gate_up, w_down, b_down, ln2_g, ln2_b, w_ple, w_ple_gate, b_ple_gate,
           ln3_g, ln3_b):
    T = batch * seq
    tm = math.gcd(T, ROW_TILE_A)
    bm = MOE_BLOCK
    off_f = 3 * FOX_WIDTH
    off_u = off_f + N_HEADS
    row = lambda v: v.reshape(1, -1).astype(F32)

    w1 = jnp.concatenate([w_in[:, :off_u], jnp.zeros((D_MODEL, LANES - N_HEADS), F32)], axis=1).astype(BF16)
    b1 = jnp.concatenate([b_in[:off_u], jnp.zeros((LANES - N_HEADS,), F32)]).reshape(1, -1)
    w2 = w_in[:, off_u:].astype(BF16)
    b2 = row(b_in[off_u:])
    bs_tile = jnp.repeat(b_spatial.T, HEAD_DIM, axis=1)
    wr = jnp.concatenate([w_router, jnp.zeros((D_MODEL, LANES - N_EXPERTS), F32)], axis=1)
    br = jnp.concatenate([b_router, jnp.full((LANES - N_EXPERTS,), -1e30, F32)]).reshape(1, -1)

    q, k, v, f_pad = _qkvf(h2d, w1, b1, tm)
    auxq, auxk = _decay(f_pad, batch, seq)
    attn = _attention(q, auxq, k, auxk, v, batch, seq)
    h1, idx_o, gate_o, rank_o, cnt_o = _mix(
        h2d, attn, w2, b2, row(gmlp_ln_g), row(gmlp_ln_b), w_spatial.astype(BF16), bs_tile,
        w_branch_a.astype(BF16), w_branch_b.astype(BF16), w_out.astype(BF16), row(b_out),
        row(ln1_g), row(ln1_b), wr, br, tm, alpha)

    counts = cnt_o[0, :N_EXPERTS]
    padded = (counts + bm - 1) // bm * bm
    pad_end = jnp.cumsum(padded)
    pad_start = pad_end - padded
    dest = (jnp.take(pad_start, idx_o[:, :TOP_K]) + rank_o[:, :TOP_K]).reshape(-1)
    n_assign = T * TOP_K
    n_blk = -(-n_assign // bm) + N_EXPERTS
    n_rows = n_blk * bm
    n_used = pad_end[-1] // bm
    blk_src = jnp.minimum(jnp.arange(n_blk, dtype=I32), n_used - 1).astype(I32)
    blk_e = jnp.minimum(jnp.sum(pad_end[None, :] <= (blk_src * bm)[:, None], axis=1), N_EXPERTS - 1).astype(I32)
    dest = dest.astype(I32)

    h1_rows = h1.reshape(T, ROW_TILE, LANES)
    xpad = _scatter_rows((pad_start + counts).astype(I32), (padded - counts).astype(I32), dest, h1_rows, n_rows)
    ypad = _moe(blk_e, blk_src, n_used.reshape(1).astype(I32), xpad.reshape(-1, D_MODEL),
                w_gate_up, b_gate_up.reshape(N_EXPERTS, 1, -1), w_down, b_down.reshape(N_EXPERTS, 1, -1),
                n_blk, bm)
    yg = _gather_rows(dest, ypad.reshape(n_rows, ROW_TILE, LANES))
    return _final(h1, yg.reshape(T, TOP_K * D_MODEL), gate_o, p2d, w_ple.astype(BF16),
                  w_ple_gate.astype(BF16), row(b_ple_gate), row(ln2_g), row(ln2_b),
                  row(ln3_g), row(ln3_b), tm, alpha)


def kernel(x, p, w_in, b_in, gmlp_ln_g, gmlp_ln_b, w_spatial, b_spatial, w_branch_a, w_branch_b, w_out, b_out, ln1_g, ln1_b, w_router, b_router, w_gate_up, b_gate_up, w_down, b_down, ln2_g, ln2_b, w_ple, w_ple_gate, b_ple_gate, ln3_g, ln3_b):
    batch, seq, d = x.shape
    depth = w_in.shape[0]
    assert d == D_MODEL and seq % GMLP_CHUNK == 0
    alpha = (2.0 * depth) ** 0.25
    h = x.reshape(batch * seq, d)
    for i in range(depth):
        h = _layer(h, p[i].reshape(batch * seq, PLE_DIM), batch, seq, alpha,
                   w_in[i], b_in[i], gmlp_ln_g[i], gmlp_ln_b[i], w_spatial[i], b_spatial[i],
                   w_branch_a[i], w_branch_b[i], w_out[i], b_out[i], ln1_g[i], ln1_b[i],
                   w_router[i], b_router[i], w_gate_up[i], b_gate_up[i], w_down[i], b_down[i],
                   ln2_g[i], ln2_b[i], w_ple[i], w_ple_gate[i], b_ple_gate[i], ln3_g[i], ln3_b[i])
    return h.reshape(batch, seq, d)
```

```python
import functools
import math

import jax
import jax.numpy as jnp
import numpy as np
from jax import lax
from jax.experimental import pallas as pl
from jax.experimental.pallas import tpu as pltpu

F32 = jnp.float32
BF16 = jnp.bfloat16
I32 = jnp.int32

D_MODEL = 1024
N_HEADS = 8
HEAD_DIM = 64
FOX_WIDTH = N_HEADS * HEAD_DIM
GMLP_WIDTH = 512
GMLP_CHUNK = 128
N_EXPERTS = 32
TOP_K = 4
D_FF = 1024
PLE_DIM = 256
SWIGLU_LIMIT = 7.0
SWIGLU_ALPHA = 1.702
LN_EPS = 1e-5
LANES = 128
ROW_TILE = 8
VMEM_LIMIT = 56 * 1024 * 1024

MOE_BLOCK = 256
ATTN_Q_TILE = 512
ATTN_K_TILE = 512
ATTN_PAIRS = 2
LOG2E = math.log2(math.e)
ROW_TILE_A = 512
SCATTER_TOKENS = 1024
GATHER_ROWS = 2048
COPY_SUB = 256


def _cparams(sem):
    return pltpu.CompilerParams(dimension_semantics=sem, vmem_limit_bytes=VMEM_LIMIT)


def _gelu(x):
    c = math.sqrt(2.0 / math.pi)
    return 0.5 * x * (1.0 + jnp.tanh(c * (x + 0.044715 * (x * x * x))))


def _layer_norm(x, g, b):
    mu = jnp.mean(x, axis=-1, keepdims=True)
    xc = x - mu
    var = jnp.mean(xc * xc, axis=-1, keepdims=True)
    return xc * lax.rsqrt(var + LN_EPS) * g + b


def _split3(x):
    hi = x.astype(BF16)
    r = x - hi.astype(F32)
    mid = r.astype(BF16)
    lo = (r - mid.astype(F32)).astype(BF16)
    return hi, mid, lo


def _dot(a, b):
    return jnp.dot(a, b, preferred_element_type=F32)


def _load_token_rows(ref, n, first=0, stride=ROW_TILE):
    return jnp.concatenate([ref[pl.ds(first + j, n, stride=stride), :] for j in range(ROW_TILE)], axis=1)


def _store_token_rows(ref, val):
    n = val.shape[0]
    for j in range(ROW_TILE):
        ref[pl.ds(j, n, stride=ROW_TILE), :] = val[:, j * LANES:(j + 1) * LANES]


def _qkvf_kernel(x_ref, w_ref, b_ref, q_ref, k_ref, v_ref, f_ref):
    x = x_ref[...].astype(BF16)
    proj = _dot(x, w_ref[...]) + b_ref[...]
    q_ref[...] = (proj[:, :FOX_WIDTH] * (HEAD_DIM ** -0.5 * LOG2E)).astype(BF16)
    k_ref[...] = proj[:, FOX_WIDTH:2 * FOX_WIDTH].astype(BF16)
    v_ref[...] = proj[:, 2 * FOX_WIDTH:3 * FOX_WIDTH].astype(BF16)
    f_ref[...] = proj[:, 3 * FOX_WIDTH:]


def _qkvf(x2, w, b, tm):
    T = x2.shape[0]
    n_out = w.shape[1]
    return pl.pallas_call(
        _qkvf_kernel,
        grid=(T // tm,),
        in_specs=[
            pl.BlockSpec((tm, D_MODEL), lambda i: (i, 0)),
            pl.BlockSpec((D_MODEL, n_out), lambda i: (0, 0)),
            pl.BlockSpec((1, n_out), lambda i: (0, 0)),
        ],
        out_specs=[
            pl.BlockSpec((tm, FOX_WIDTH), lambda i: (i, 0)),
            pl.BlockSpec((tm, FOX_WIDTH), lambda i: (i, 0)),
            pl.BlockSpec((tm, FOX_WIDTH), lambda i: (i, 0)),
            pl.BlockSpec((tm, LANES), lambda i: (i, 0)),
        ],
        out_shape=[
            jax.ShapeDtypeStruct((T, FOX_WIDTH), BF16),
            jax.ShapeDtypeStruct((T, FOX_WIDTH), BF16),
            jax.ShapeDtypeStruct((T, FOX_WIDTH), BF16),
            jax.ShapeDtypeStruct((T, LANES), F32),
        ],
        compiler_params=_cparams(("arbitrary",)),
        name="qkvf",
    )(x2, w, b)


def _decay_placement():
    pq = np.zeros((3 * LANES, FOX_WIDTH), np.float32)
    pk = np.zeros((3 * LANES, FOX_WIDTH), np.float32)
    cq = np.zeros((1, FOX_WIDTH), np.float32)
    ck = np.zeros((1, FOX_WIDTH), np.float32)
    for h in range(N_HEADS):
        base = (h // 2) * LANES + (HEAD_DIM if h % 2 == 0 else 0)
        for piece in range(3):
            pq[piece * LANES + h, base + piece] = 1.0
            pk[piece * LANES + h, base + 3 + piece] = -1.0
            cq[0, base + 3 + piece] = 1.0
            ck[0, base + piece] = 1.0
    return pq, pk, cq, ck


def _decay_kernel(f_ref, pq_ref, pk_ref, cq_ref, ck_ref, auxq_ref, auxk_ref, *, seq, blk):
    r = lax.broadcasted_iota(I32, (blk, blk), 0)
    c = lax.broadcasted_iota(I32, (blk, blk), 1)
    tri = jnp.where(c <= r, 1.0, 0.0).astype(BF16)
    carry = jnp.zeros((1, LANES), F32)
    for i in range(seq // blk):
        f = f_ref[i * blk:(i + 1) * blk, :]
        ls = jnp.minimum(f, 0.0) - jnp.log1p(jnp.exp(-jnp.abs(f)))
        hi, mid, lo = _split3(ls)
        cs = _dot(tri, hi) + _dot(tri, mid) + _dot(tri, lo) + carry
        carry = cs[blk - 1:blk, :]
        pieces = jnp.concatenate(_split3(cs * LOG2E), axis=1)
        auxq_ref[i * blk:(i + 1) * blk, :] = (_dot(pieces, pq_ref[...]) + cq_ref[...]).astype(BF16)
        auxk_ref[i * blk:(i + 1) * blk, :] = (_dot(pieces, pk_ref[...]) + ck_ref[...]).astype(BF16)


def _decay(f_pad, batch, seq):
    blk = 256 if seq % 256 == 0 else LANES
    pq, pk, cq, ck = _decay_placement()
    const = lambda *shape: pl.BlockSpec(shape, lambda b: (0,) * len(shape))
    return pl.pallas_call(
        functools.partial(_decay_kernel, seq=seq, blk=blk),
        grid=(batch,),
        in_specs=[
            pl.BlockSpec((seq, LANES), lambda b: (b, 0)),
            const(3 * LANES, FOX_WIDTH), const(3 * LANES, FOX_WIDTH), const(1, FOX_WIDTH), const(1, FOX_WIDTH),
        ],
        out_specs=[
            pl.BlockSpec((seq, FOX_WIDTH), lambda b: (b, 0)),
            pl.BlockSpec((seq, FOX_WIDTH), lambda b: (b, 0)),
        ],
        out_shape=[
            jax.ShapeDtypeStruct((batch * seq, FOX_WIDTH), BF16),
            jax.ShapeDtypeStruct((batch * seq, FOX_WIDTH), BF16),
        ],
        compiler_params=_cparams(("arbitrary",)),
        name="decay",
    )(f_pad, jnp.asarray(pq, BF16), jnp.asarray(pk, BF16), jnp.asarray(cq), jnp.asarray(ck))


def _attn_kernel(q_ref, auxq_ref, k_ref, auxk_ref, v_ref, o_ref, *, tq, tk):
    qi = pl.program_id(2)
    n_pairs = q_ref.shape[1] // LANES
    low_q = lax.broadcasted_iota(I32, (tq, LANES), 1) < HEAD_DIM
    qs = []
    for pr in range(n_pairs):
        q = q_ref[:, pr * LANES:(pr + 1) * LANES]
        aq = auxq_ref[:, pr * LANES:(pr + 1) * LANES]
        qs += [jnp.where(low_q, q, aq), jnp.where(low_q, aq, q)]

    def step(start, n, carry, masked):
        low_k = lax.broadcasted_iota(I32, (n, LANES), 1) < HEAD_DIM
        ones = jnp.ones((n, LANES), BF16)
        out = []
        for pr in range(n_pairs):
            lanes = slice(pr * LANES, (pr + 1) * LANES)
            kb = k_ref[pl.ds(start, n), lanes]
            ak = auxk_ref[pl.ds(start, n), lanes]
            vb = v_ref[pl.ds(start, n), lanes]
            ks = (jnp.where(low_k, kb, ak), jnp.where(low_k, ak, kb))
            vs = (jnp.where(low_k, vb, ones), jnp.where(low_k, ones, vb))
            for j in range(2):
                m, acc = carry[2 * pr + j]
                s = lax.dot_general(qs[2 * pr + j], ks[j], (((1,), (1,)), ((), ())),
                                    preferred_element_type=F32)
                if masked:
                    row = lax.broadcasted_iota(I32, (tq, n), 0)
                    col = lax.broadcasted_iota(I32, (tq, n), 1)
                    s = jnp.where(col + (start - qi * tq) <= row, s, -jnp.inf)
                m_new = jnp.maximum(m, jnp.max(s, axis=1, keepdims=True))
                p = jnp.exp2(s - m_new)
                acc = jnp.exp2(m - m_new) * acc + _dot(p.astype(BF16), vs[j])
                out.append((m_new, acc))
        return tuple(out)

    init = tuple((jnp.full((tq, 1), -jnp.inf, F32), jnp.zeros((tq, LANES), F32))
                 for _ in range(2 * n_pairs))
    carry = lax.fori_loop(0, qi, lambda t, c: step(pl.multiple_of(t * tq, tq), tq, c, False), init)
    for d in range(tq // tk):
        carry = step(pl.multiple_of(qi * tq + d * tk, tk), tk, carry, True)
    for pr in range(n_pairs):
        acc0, acc1 = carry[2 * pr][1], carry[2 * pr + 1][1]
        out0 = acc0 / acc0[:, HEAD_DIM:HEAD_DIM + 1]
        out1 = acc1 / acc1[:, 0:1]
        o_ref[:, pr * LANES:(pr + 1) * LANES] = jnp.where(low_q, out0, out1).astype(BF16)


def _attention(q, auxq, k, auxk, v, batch, seq):
    tq = math.gcd(seq, ATTN_Q_TILE)
    tk = math.gcd(tq, ATTN_K_TILE)
    nq = seq // tq
    T = batch * seq
    width = ATTN_PAIRS * LANES
    q_spec = pl.BlockSpec((tq, width), lambda b, hp, qi: (b * nq + qi, hp))
    kv_spec = pl.BlockSpec((seq, width), lambda b, hp, qi: (b, hp))
    return pl.pallas_call(
        functools.partial(_attn_kernel, tq=tq, tk=tk),
        grid=(batch, N_HEADS // (2 * ATTN_PAIRS), nq),
        in_specs=[q_spec, q_spec, kv_spec, kv_spec, kv_spec],
        out_specs=q_spec,
        out_shape=jax.ShapeDtypeStruct((T, FOX_WIDTH), BF16),
        compiler_params=_cparams(("arbitrary", "arbitrary", "arbitrary")),
        name="attn",
    )(q, auxq, k, auxk, v)


def _mix_kernel(x_ref, attn_ref, w2_ref, b2_ref, lng_ref, lnb_ref, ws_ref, bs_ref,
                wa_ref, wb_ref, wo_ref, bo_ref, g1_ref, b1_ref, wr_ref, br_ref,
                h_ref, idx_ref, gate_ref, rank_ref, cnt_ref, carry_ref, *, tm, alpha):
    i = pl.program_id(0)

    @pl.when(i == 0)
    def _():
        carry_ref[...] = jnp.zeros_like(carry_ref)

    x = x_ref[...]
    proj = _dot(x.astype(BF16), w2_ref[...]) + b2_ref[...]
    u = _gelu(proj[:, :GMLP_WIDTH])
    gv = _gelu(proj[:, GMLP_WIDTH:2 * GMLP_WIDTH])
    vln = _layer_norm(gv, lng_ref[...], lnb_ref[...]).astype(BF16)

    cr = lax.broadcasted_iota(I32, (GMLP_CHUNK, GMLP_CHUNK), 0)
    cc = lax.broadcasted_iota(I32, (GMLP_CHUNK, GMLP_CHUNK), 1)
    tril = cc <= cr
    lo_half = cc < HEAD_DIM
    zero_w = jnp.zeros((GMLP_CHUNK, GMLP_CHUNK), BF16)
    n_slab = GMLP_WIDTH // LANES
    lhs = []
    for s in range(n_slab):
        w0 = jnp.where(tril, ws_ref[2 * s], zero_w)
        w1 = jnp.where(tril, ws_ref[2 * s + 1], zero_w)
        lhs.append(jnp.concatenate([w0, w1], axis=1))
    bs = bs_ref[...]
    rows = []
    for c in range(tm // GMLP_CHUNK):
        cols = []
        for s in range(n_slab):
            vs = vln[c * GMLP_CHUNK:(c + 1) * GMLP_CHUNK, s * LANES:(s + 1) * LANES]
            rhs = jnp.concatenate([jnp.where(lo_half, vs, zero_w), jnp.where(lo_half, zero_w, vs)], axis=0)
            cols.append(_dot(lhs[s], rhs))
        rows.append(jnp.concatenate(cols, axis=1) + bs)
    sp = jnp.concatenate(rows, axis=0) if len(rows) > 1 else rows[0]
    sgu = (u * sp).astype(BF16)

    ga = jax.nn.sigmoid(proj[:, 2 * GMLP_WIDTH:2 * GMLP_WIDTH + D_MODEL])
    gb = jax.nn.sigmoid(proj[:, 2 * GMLP_WIDTH + D_MODEL:])
    merged = ga * _dot(attn_ref[...], wa_ref[...]) + gb * _dot(sgu, wb_ref[...])
    mix = _dot(merged.astype(BF16), wo_ref[...]) + bo_ref[...]
    h = _layer_norm(alpha * x + mix, g1_ref[...], b1_ref[...])
    _store_token_rows(h_ref, h)

    a_hi = h.astype(BF16)
    a_lo = (h - a_hi.astype(F32)).astype(BF16)
    wr = wr_ref[...]
    w_hi = wr.astype(BF16)
    w_lo = (wr - w_hi.astype(F32)).astype(BF16)
    logits = _dot(a_hi, w_hi) + _dot(a_lo, w_hi) + _dot(a_hi, w_lo) + br_ref[...]

    lane_i = lax.broadcasted_iota(I32, (tm, LANES), 1)
    lane_f = lane_i.astype(F32)
    vals, idxs = [], []
    l = logits
    for _ in range(TOP_K):
        m = jnp.max(l, axis=1, keepdims=True)
        ix = jnp.min(jnp.where(l == m, lane_f, float(LANES)), axis=1, keepdims=True)
        vals.append(m)
        idxs.append(ix)
        l = jnp.where(lane_f == ix, -jnp.inf, l)
    es = [jnp.exp(v - vals[0]) for v in vals]
    den = es[0] + es[1] + es[2] + es[3]

    onehot = jnp.zeros((tm, LANES), F32)
    idx_out = jnp.zeros((tm, LANES), F32)
    gate_out = jnp.zeros((tm, LANES), F32)
    for k in range(TOP_K):
        onehot = onehot + jnp.where(lane_f == idxs[k], 1.0, 0.0)
        idx_out = jnp.where(lane_i == k, idxs[k], idx_out)
        gate_out = jnp.where(lane_i == k, es[k] / den, gate_out)

    tr = lax.broadcasted_iota(I32, (tm, tm), 0)
    tc = lax.broadcasted_iota(I32, (tm, tm), 1)
    strict = jnp.where(tc < tr, 1.0, 0.0).astype(BF16)
    carry = carry_ref[0:1, :]
    before = _dot(strict, onehot.astype(BF16)) + carry
    rank_out = jnp.zeros((tm, LANES), F32)
    for k in range(TOP_K):
        rk = jnp.sum(jnp.where(lane_f == idxs[k], before, 0.0), axis=1, keepdims=True)
        rank_out = jnp.where(lane_i == k, rk, rank_out)
    new_carry = carry + jnp.sum(onehot, axis=0, keepdims=True)
    carry_ref[...] = jnp.broadcast_to(new_carry, carry_ref.shape)
    cnt_ref[...] = jnp.broadcast_to(new_carry, cnt_ref.shape).astype(I32)
    idx_ref[...] = idx_out.astype(I32)
    gate_ref[...] = gate_out
    rank_ref[...] = rank_out.astype(I32)


def _mix(x2, attn, w2, b2, lng, lnb, ws, bs_tile, wa, wb, wo, bo, g1, b1, wr, br, tm, alpha):
    T = x2.shape[0]
    n2 = w2.shape[1]
    const = lambda *shape: pl.BlockSpec(shape, lambda i: (0,) * len(shape))
    return pl.pallas_call(
        functools.partial(_mix_kernel, tm=tm, alpha=alpha),
        grid=(T // tm,),
        in_specs=[
            pl.BlockSpec((tm, D_MODEL), lambda i: (i, 0)),
            pl.BlockSpec((tm, FOX_WIDTH), lambda i: (i, 0)),
            const(D_MODEL, n2), const(1, n2),
            const(1, GMLP_WIDTH), const(1, GMLP_WIDTH),
            const(GMLP_WIDTH // HEAD_DIM, GMLP_CHUNK, GMLP_CHUNK), const(GMLP_CHUNK, GMLP_WIDTH),
            const(FOX_WIDTH, D_MODEL), const(GMLP_WIDTH, D_MODEL),
            const(D_MODEL, D_MODEL), const(1, D_MODEL),
            const(1, D_MODEL), const(1, D_MODEL),
            const(D_MODEL, LANES), const(1, LANES),
        ],
        out_specs=[
            pl.BlockSpec((tm * ROW_TILE, LANES), lambda i: (i, 0)),
            pl.BlockSpec((tm, LANES), lambda i: (i, 0)),
            pl.BlockSpec((tm, LANES), lambda i: (i, 0)),
            pl.BlockSpec((tm, LANES), lambda i: (i, 0)),
            pl.BlockSpec((ROW_TILE, LANES), lambda i: (0, 0)),
        ],
        out_shape=[
            jax.ShapeDtypeStruct((T * ROW_TILE, LANES), F32),
            jax.ShapeDtypeStruct((T, LANES), I32),
            jax.ShapeDtypeStruct((T, LANES), F32),
            jax.ShapeDtypeStruct((T, LANES), I32),
            jax.ShapeDtypeStruct((ROW_TILE, LANES), I32),
        ],
        scratch_shapes=[pltpu.VMEM((ROW_TILE, LANES), F32)],
        compiler_params=_cparams(("arbitrary",)),
        name="mix",
    )(x2, attn, w2, b2, lng, lnb, ws, bs_tile, wa, wb, wo, bo, g1, b1, wr, br)


def _pipelined_row_dmas(n_sub, issue, drain):
    issue(0)

    def outer(g, carry):
        @pl.when(g + 1 < n_sub)
        def _():
            issue(g + 1)

        drain()
        return carry

    lax.fori_loop(0, n_sub, outer, 0)


def _scatter_kernel(fbase_ref, fcnt_ref, didx_ref, src_ref, dst_ref, sem, *, tn, sub, unroll):
    def row_wait():
        pltpu.make_async_copy(src_ref.at[0], dst_ref.at[0], sem).wait()

    def issue(g):
        def body(j, carry):
            for u in range(unroll):
                t = g * sub + j * unroll + u
                for k in range(TOP_K):
                    pltpu.make_async_copy(src_ref.at[t], dst_ref.at[didx_ref[t * TOP_K + k]], sem).start(
                        priority=k % 2)
            return carry

        lax.fori_loop(0, sub // unroll, body, 0)

    def drain():
        def body(j, carry):
            for _ in range(unroll * TOP_K):
                row_wait()
            return carry

        lax.fori_loop(0, sub // unroll, body, 0)

    _pipelined_row_dmas(tn // sub, issue, drain)

    @pl.when(pl.program_id(0) == 0)
    def _():
        def per_expert(e, carry):
            base = fbase_ref[e]

            def start_one(r, c):
                pltpu.make_async_copy(src_ref.at[0], dst_ref.at[base + r], sem).start()
                return c

            def wait_one(r, c):
                row_wait()
                return c

            lax.fori_loop(0, fcnt_ref[e], start_one, 0)
            lax.fori_loop(0, fcnt_ref[e], wait_one, 0)
            return carry

        lax.fori_loop(0, N_EXPERTS, per_expert, 0)


def _scatter_rows(fill_base, fill_cnt, didx, src3, n_dst):
    T = src3.shape[0]
    tn = math.gcd(T, SCATTER_TOKENS)
    sub = math.gcd(tn, COPY_SUB)
    unroll = math.gcd(sub, 4)
    grid_spec = pltpu.PrefetchScalarGridSpec(
        num_scalar_prefetch=2,
        grid=(T // tn,),
        in_specs=[
            pl.BlockSpec((tn * TOP_K,), lambda i, fb, fc: (i,), memory_space=pltpu.SMEM),
            pl.BlockSpec((tn, ROW_TILE, LANES), lambda i, fb, fc: (i, 0, 0)),
        ],
        out_specs=pl.BlockSpec(memory_space=pl.ANY),
        scratch_shapes=[pltpu.SemaphoreType.DMA(())],
    )
    return pl.pallas_call(
        functools.partial(_scatter_kernel, tn=tn, sub=sub, unroll=unroll),
        grid_spec=grid_spec,
        out_shape=jax.ShapeDtypeStruct((n_dst, ROW_TILE, LANES), src3.dtype),
        compiler_params=_cparams(("arbitrary",)),
        name="scatter_rows",
    )(fill_base, fill_cnt, didx, src3)


def _gather_kernel(sidx_ref, src_ref, out_ref, sem, *, n, sub, unroll):
    def issue(g):
        def body(j, carry):
            for u in range(unroll):
                r = g * sub + j * unroll + u
                pltpu.make_async_copy(src_ref.at[sidx_ref[r]], out_ref.at[r], sem).start(priority=u % 2)
            return carry

        lax.fori_loop(0, sub // unroll, body, 0)

    def drain():
        def body(j, carry):
            for _ in range(unroll):
                pltpu.make_async_copy(src_ref.at[0], out_ref.at[0], sem).wait()
            return carry

        lax.fori_loop(0, sub // unroll, body, 0)

    _pipelined_row_dmas(n // sub, issue, drain)


def _gather_rows(sidx, src3):
    n_all = sidx.shape[0]
    n = math.gcd(n_all, GATHER_ROWS)
    sub = math.gcd(n, COPY_SUB * TOP_K)
    unroll = math.gcd(sub, 16)
    return pl.pallas_call(
        functools.partial(_gather_kernel, n=n, sub=sub, unroll=unroll),
        grid=(n_all // n,),
        in_specs=[
            pl.BlockSpec((n,), lambda i: (i,), memory_space=pltpu.SMEM),
            pl.BlockSpec(memory_space=pl.ANY),
        ],
        out_specs=pl.BlockSpec((n, ROW_TILE, LANES), lambda i: (i, 0, 0)),
        out_shape=jax.ShapeDtypeStruct((n_all, ROW_TILE, LANES), src3.dtype),
        scratch_shapes=[pltpu.SemaphoreType.DMA(())],
        compiler_params=_cparams(("arbitrary",)),
        name="gather_rows",
    )(sidx, src3)


def _moe_kernel(be_ref, bsrc_ref, nu_ref, x_ref, wgu_ref, bgu_ref, wdn_ref, bdn_ref, y_ref,
                wgu_bf, wdn_bf):
    i = pl.program_id(0)
    active = i < nu_ref[0]
    prev = jnp.maximum(i - 1, 0)
    fresh = (i == 0) | (be_ref[i] != be_ref[prev])

    @pl.when(active & fresh)
    def _():
        wgu_bf[...] = wgu_ref[...].astype(BF16)
        wdn_bf[...] = wdn_ref[...].astype(BF16)

    @pl.when(active)
    def _():
        x = _load_token_rows(x_ref, x_ref.shape[0] // ROW_TILE).astype(BF16)
        gu = _dot(x, wgu_bf[...]) + bgu_ref[...]
        gate = jnp.minimum(gu[:, :D_FF], SWIGLU_LIMIT)
        up = jnp.clip(gu[:, D_FF:], -SWIGLU_LIMIT, SWIGLU_LIMIT)
        hid = (up + 1.0) * (gate * jax.nn.sigmoid(SWIGLU_ALPHA * gate))
        _store_token_rows(y_ref, _dot(hid.astype(BF16), wdn_bf[...]) + bdn_ref[...])


def _moe(blk_e, blk_src, n_used, xpad, wgu, bgu, wdn, bdn, n_blk, bm):
    grid_spec = pltpu.PrefetchScalarGridSpec(
        num_scalar_prefetch=3,
        grid=(n_blk,),
        in_specs=[
            pl.BlockSpec((bm * ROW_TILE, LANES), lambda i, be, bs, nu: (bs[i], 0)),
            pl.BlockSpec((None, D_MODEL, 2 * D_FF), lambda i, be, bs, nu: (be[i], 0, 0)),
            pl.BlockSpec((None, 1, 2 * D_FF), lambda i, be, bs, nu: (be[i], 0, 0)),
            pl.BlockSpec((None, D_FF, D_MODEL), lambda i, be, bs, nu: (be[i], 0, 0)),
            pl.BlockSpec((None, 1, D_MODEL), lambda i, be, bs, nu: (be[i], 0, 0)),
        ],
        out_specs=pl.BlockSpec((bm * ROW_TILE, LANES), lambda i, be, bs, nu: (bs[i], 0)),
        scratch_shapes=[pltpu.VMEM((D_MODEL, 2 * D_FF), BF16), pltpu.VMEM((D_FF, D_MODEL), BF16)],
    )
    return pl.pallas_call(
        _moe_kernel,
        grid_spec=grid_spec,
        out_shape=jax.ShapeDtypeStruct((n_blk * bm * ROW_TILE, LANES), F32),
        compiler_params=_cparams(("arbitrary",)),
        name="moe",
    )(blk_e, blk_src, n_used, xpad, wgu, bgu, wdn, bdn)


def _final_kernel(h_ref, yg_ref, gate_ref, p_ref, wple_ref, wpg_ref, bpg_ref,
                  g2_ref, b2_ref, g3_ref, b3_ref, o_ref, *, tm, alpha):
    h = _load_token_rows(h_ref, tm)
    gates = gate_ref[...]
    slot = TOP_K * ROW_TILE
    ffn = gates[:, 0:1] * _load_token_rows(yg_ref, tm, 0, slot)
    for k in range(1, TOP_K):
        ffn = ffn + gates[:, k:k + 1] * _load_token_rows(yg_ref, tm, k * ROW_TILE, slot)
    h2 = _layer_norm(alpha * h + ffn, g2_ref[...], b2_ref[...])
    emb = _dot(p_ref[...].astype(BF16), wple_ref[...])
    pg = jax.nn.sigmoid(_dot(h2.astype(BF16), wpg_ref[...]) + bpg_ref[...])
    o_ref[...] = _layer_norm(alpha * h2 + emb * pg, g3_ref[...], b3_ref[...])


def _final(h1, yg, gates, p2, wple, wpg, bpg, g2, b2, g3, b3, tm, alpha):
    T = h1.shape[0] // ROW_TILE
    const = lambda *shape: pl.BlockSpec(shape, lambda i: (0,) * len(shape))
    return pl.pallas_call(
        functools.partial(_final_kernel, tm=tm, alpha=alpha),
        grid=(T // tm,),
        in_specs=[
            pl.BlockSpec((tm * ROW_TILE, LANES), lambda i: (i, 0)),
            pl.BlockSpec((tm * TOP_K * ROW_TILE, LANES), lambda i: (i, 0)),
            pl.BlockSpec((tm, LANES), lambda i: (i, 0)),
            pl.BlockSpec((tm, PLE_DIM), lambda i: (i, 0)),
            const(PLE_DIM, D_MODEL), const(D_MODEL, D_MODEL), const(1, D_MODEL),
            const(1, D_MODEL), const(1, D_MODEL), const(1, D_MODEL), const(1, D_MODEL),
        ],
        out_specs=pl.BlockSpec((tm, D_MODEL), lambda i: (i, 0)),
        out_shape=jax.ShapeDtypeStruct((T, D_MODEL), F32),
        compiler_params=_cparams(("arbitrary",)),
        name="final",
    )(h1, yg, gates, p2, wple, wpg, bpg, g2, b2, g3, b3)


def _layer(h2d, p2d, batch, seq, alpha, w_in, b_in, gmlp_ln_g, gmlp_ln_b, w_spatial, b_spatial,
           w_branch_a, w_branch_b, w_out, b_out, ln1_g, ln1_b, w_router, b_router,
           w_gate_up, b_gate_up, w_down, b_down, ln2_g, ln2_b, w_ple, w_ple_gate, b_ple_gate,
           ln3_g, ln3_b):
    T = batch * seq
    tm = math.gcd(T, ROW_TILE_A)
    bm = MOE_BLOCK
    off_f = 3 * FOX_WIDTH
    off_u = off_f + N_HEADS
    row = lambda v: v.reshape(1, -1).astype(F32)

    w1 = jnp.concatenate([w_in[:, :off_u], jnp.zeros((D_MODEL, LANES - N_HEADS), F32)], axis=1).astype(BF16)
    b1 = jnp.concatenate([b_in[:off_u], jnp.zeros((LANES - N_HEADS,), F32)]).reshape(1, -1)
    w2 = w_in[:, off_u:].astype(BF16)
    b2 = row(b_in[off_u:])
    bs_tile = jnp.repeat(b_spatial.T, HEAD_DIM, axis=1)
    wr = jnp.concatenate([w_router, jnp.zeros((D_MODEL, LANES - N_EXPERTS), F32)], axis=1)
    br = jnp.concatenate([b_router, jnp.full((LANES - N_EXPERTS,), -1e30, F32)]).reshape(1, -1)

    q, k, v, f_pad = _qkvf(h2d, w1, b1, tm)
    auxq, auxk = _decay(f_pad, batch, seq)
    attn = _attention(q, auxq, k, auxk, v, batch, seq)
    h1, idx_o, gate_o, rank_o, cnt_o = _mix(
        h2d, attn, w2, b2, row(gmlp_ln_g), row(gmlp_ln_b), w_spatial.astype(BF16), bs_tile,
        w_branch_a.astype(BF16), w_branch_b.astype(BF16), w_out.astype(BF16), row(b_out),
        row(ln1_g), row(ln1_b), wr, br, tm, alpha)

    counts = cnt_o[0, :N_EXPERTS]
    padded = (counts + bm - 1) // bm * bm
    pad_end = jnp.cumsum(padded)
    pad_start = pad_end - padded
    dest = (jnp.take(pad_start, idx_o[:, :TOP_K]) + rank_o[:, :TOP_K]).reshape(-1)
    n_assign = T * TOP_K
    n_blk = -(-n_assign // bm) + N_EXPERTS
    n_rows = n_blk * bm
    n_used = pad_end[-1] // bm
    blk_src = jnp.minimum(jnp.arange(n_blk, dtype=I32), n_used - 1).astype(I32)
    blk_e = jnp.minimum(jnp.sum(pad_end[None, :] <= (blk_src * bm)[:, None], axis=1), N_EXPERTS - 1).astype(I32)
    dest = dest.astype(I32)

    h1_rows = h1.reshape(T, ROW_TILE, LANES)
    xpad = _scatter_rows((pad_start + counts).astype(I32), (padded - counts).astype(I32), dest, h1_rows, n_rows)
    ypad = _moe(blk_e, blk_src, n_used.reshape(1).astype(I32), xpad.reshape(n_rows * ROW_TILE, LANES),
                w_gate_up, b_gate_up.reshape(N_EXPERTS, 1, -1), w_down, b_down.reshape(N_EXPERTS, 1, -1),
                n_blk, bm)
    yg = _gather_rows(dest, ypad.reshape(n_rows, ROW_TILE, LANES))
    return _final(h1, yg.reshape(n_assign * ROW_TILE, LANES), gate_o, p2d, w_ple.astype(BF16),
                  w_ple_gate.astype(BF16), row(b_ple_gate), row(ln2_g), row(ln2_b),
                  row(ln3_g), row(ln3_b), tm, alpha)


def kernel(x, p, w_in, b_in, gmlp_ln_g, gmlp_ln_b, w_spatial, b_spatial, w_branch_a, w_branch_b, w_out, b_out, ln1_g, ln1_b, w_router, b_router, w_gate_up, b_gate_up, w_down, b_down, ln2_g, ln2_b, w_ple, w_ple_gate, b_ple_gate, ln3_g, ln3_b):
    batch, seq, d = x.shape
    depth = w_in.shape[0]
    assert d == D_MODEL and seq % GMLP_CHUNK == 0
    alpha = (2.0 * depth) ** 0.25
    h = x.reshape(batch * seq, d)
    for i in range(depth):
        h = _layer(h, p[i].reshape(batch * seq, PLE_DIM), batch, seq, alpha,
                   w_in[i], b_in[i], gmlp_ln_g[i], gmlp_ln_b[i], w_spatial[i], b_spatial[i],
                   w_branch_a[i], w_branch_b[i], w_out[i], b_out[i], ln1_g[i], ln1_b[i],
                   w_router[i], b_router[i], w_gate_up[i], b_gate_up[i], w_down[i], b_down[i],
                   ln2_g[i], ln2_b[i], w_ple[i], w_ple_gate[i], b_ple_gate[i], ln3_g[i], ln3_b[i])
    return h.reshape(batch, seq, d)
```

```python
import functools
import math

import jax
import jax.numpy as jnp
import numpy as np
from jax import lax
from jax.experimental import pallas as pl
from jax.experimental.pallas import tpu as pltpu

F32 = jnp.float32
BF16 = jnp.bfloat16
I32 = jnp.int32

D_MODEL = 1024
N_HEADS = 8
HEAD_DIM = 64
FOX_WIDTH = N_HEADS * HEAD_DIM
GMLP_WIDTH = 512
GMLP_CHUNK = 128
N_EXPERTS = 32
TOP_K = 4
D_FF = 1024
PLE_DIM = 256
SWIGLU_LIMIT = 7.0
SWIGLU_ALPHA = 1.702
LN_EPS = 1e-5
LANES = 128
ROW_TILE = 8
VMEM_LIMIT = 56 * 1024 * 1024

MOE_BLOCK = 256
ATTN_Q_TILE = 512
ATTN_K_TILE = 512
ATTN_PAIRS = 2
LOG2E = math.log2(math.e)
ROW_TILE_A = 512
SCATTER_TOKENS = 1024
GATHER_ROWS = 2048
COPY_SUB = 256


def _cparams(sem):
    return pltpu.CompilerParams(dimension_semantics=sem, vmem_limit_bytes=VMEM_LIMIT)


def _gelu(x):
    c = math.sqrt(2.0 / math.pi)
    return 0.5 * x * (1.0 + jnp.tanh(c * (x + 0.044715 * (x * x * x))))


def _layer_norm(x, g, b):
    mu = jnp.mean(x, axis=-1, keepdims=True)
    xc = x - mu
    var = jnp.mean(xc * xc, axis=-1, keepdims=True)
    return xc * lax.rsqrt(var + LN_EPS) * g + b


def _split3(x):
    hi = x.astype(BF16)
    r = x - hi.astype(F32)
    mid = r.astype(BF16)
    lo = (r - mid.astype(F32)).astype(BF16)
    return hi, mid, lo


def _dot(a, b):
    return jnp.dot(a, b, preferred_element_type=F32)


def _load_token_rows(ref, n):
    return jnp.concatenate([ref[pl.ds(j, n, stride=ROW_TILE), :] for j in range(ROW_TILE)], axis=1)


def _store_token_rows(ref, val):
    n = val.shape[0]
    for j in range(ROW_TILE):
        ref[pl.ds(j, n, stride=ROW_TILE), :] = val[:, j * LANES:(j + 1) * LANES]


def _qkvf_kernel(x_ref, w_ref, b_ref, q_ref, k_ref, v_ref, f_ref):
    x = x_ref[...].astype(BF16)
    proj = _dot(x, w_ref[...]) + b_ref[...]
    q_ref[...] = (proj[:, :FOX_WIDTH] * (HEAD_DIM ** -0.5 * LOG2E)).astype(BF16)
    k_ref[...] = proj[:, FOX_WIDTH:2 * FOX_WIDTH].astype(BF16)
    v_ref[...] = proj[:, 2 * FOX_WIDTH:3 * FOX_WIDTH].astype(BF16)
    f_ref[...] = proj[:, 3 * FOX_WIDTH:]


def _qkvf(x2, w, b, tm):
    T = x2.shape[0]
    n_out = w.shape[1]
    return pl.pallas_call(
        _qkvf_kernel,
        grid=(T // tm,),
        in_specs=[
            pl.BlockSpec((tm, D_MODEL), lambda i: (i, 0)),
            pl.BlockSpec((D_MODEL, n_out), lambda i: (0, 0)),
            pl.BlockSpec((1, n_out), lambda i: (0, 0)),
        ],
        out_specs=[
            pl.BlockSpec((tm, FOX_WIDTH), lambda i: (i, 0)),
            pl.BlockSpec((tm, FOX_WIDTH), lambda i: (i, 0)),
            pl.BlockSpec((tm, FOX_WIDTH), lambda i: (i, 0)),
            pl.BlockSpec((tm, LANES), lambda i: (i, 0)),
        ],
        out_shape=[
            jax.ShapeDtypeStruct((T, FOX_WIDTH), BF16),
            jax.ShapeDtypeStruct((T, FOX_WIDTH), BF16),
            jax.ShapeDtypeStruct((T, FOX_WIDTH), BF16),
            jax.ShapeDtypeStruct((T, LANES), F32),
        ],
        compiler_params=_cparams(("arbitrary",)),
        name="qkvf",
    )(x2, w, b)


def _decay_placement():
    pq = np.zeros((3 * LANES, FOX_WIDTH), np.float32)
    pk = np.zeros((3 * LANES, FOX_WIDTH), np.float32)
    cq = np.zeros((1, FOX_WIDTH), np.float32)
    ck = np.zeros((1, FOX_WIDTH), np.float32)
    for h in range(N_HEADS):
        base = (h // 2) * LANES + (HEAD_DIM if h % 2 == 0 else 0)
        for piece in range(3):
            pq[piece * LANES + h, base + piece] = 1.0
            pk[piece * LANES + h, base + 3 + piece] = -1.0
            cq[0, base + 3 + piece] = 1.0
            ck[0, base + piece] = 1.0
    return pq, pk, cq, ck


def _decay_kernel(f_ref, pq_ref, pk_ref, cq_ref, ck_ref, auxq_ref, auxk_ref, *, seq, blk):
    r = lax.broadcasted_iota(I32, (blk, blk), 0)
    c = lax.broadcasted_iota(I32, (blk, blk), 1)
    tri = jnp.where(c <= r, 1.0, 0.0).astype(BF16)
    carry = jnp.zeros((1, LANES), F32)
    for i in range(seq // blk):
        f = f_ref[i * blk:(i + 1) * blk, :]
        ls = jnp.minimum(f, 0.0) - jnp.log1p(jnp.exp(-jnp.abs(f)))
        hi, mid, lo = _split3(ls)
        cs = _dot(tri, hi) + _dot(tri, mid) + _dot(tri, lo) + carry
        carry = cs[blk - 1:blk, :]
        pieces = jnp.concatenate(_split3(cs * LOG2E), axis=1)
        auxq_ref[i * blk:(i + 1) * blk, :] = (_dot(pieces, pq_ref[...]) + cq_ref[...]).astype(BF16)
        auxk_ref[i * blk:(i + 1) * blk, :] = (_dot(pieces, pk_ref[...]) + ck_ref[...]).astype(BF16)


def _decay(f_pad, batch, seq):
    blk = 256 if seq % 256 == 0 else LANES
    pq, pk, cq, ck = _decay_placement()
    const = lambda *shape: pl.BlockSpec(shape, lambda b: (0,) * len(shape))
    return pl.pallas_call(
        functools.partial(_decay_kernel, seq=seq, blk=blk),
        grid=(batch,),
        in_specs=[
            pl.BlockSpec((seq, LANES), lambda b: (b, 0)),
            const(3 * LANES, FOX_WIDTH), const(3 * LANES, FOX_WIDTH), const(1, FOX_WIDTH), const(1, FOX_WIDTH),
        ],
        out_specs=[
            pl.BlockSpec((seq, FOX_WIDTH), lambda b: (b, 0)),
            pl.BlockSpec((seq, FOX_WIDTH), lambda b: (b, 0)),
        ],
        out_shape=[
            jax.ShapeDtypeStruct((batch * seq, FOX_WIDTH), BF16),
            jax.ShapeDtypeStruct((batch * seq, FOX_WIDTH), BF16),
        ],
        compiler_params=_cparams(("arbitrary",)),
        name="decay",
    )(f_pad, jnp.asarray(pq, BF16), jnp.asarray(pk, BF16), jnp.asarray(cq), jnp.asarray(ck))


def _attn_kernel(q_ref, auxq_ref, k_ref, auxk_ref, v_ref, o_ref, *, tq, tk):
    qi = pl.program_id(2)
    n_pairs = q_ref.shape[1] // LANES
    low_q = lax.broadcasted_iota(I32, (tq, LANES), 1) < HEAD_DIM
    qs = []
    for pr in range(n_pairs):
        q = q_ref[:, pr * LANES:(pr + 1) * LANES]
        aq = auxq_ref[:, pr * LANES:(pr + 1) * LANES]
        qs += [jnp.where(low_q, q, aq), jnp.where(low_q, aq, q)]

    def step(start, n, carry, masked):
        low_k = lax.broadcasted_iota(I32, (n, LANES), 1) < HEAD_DIM
        ones = jnp.ones((n, LANES), BF16)
        out = []
        for pr in range(n_pairs):
            lanes = slice(pr * LANES, (pr + 1) * LANES)
            kb = k_ref[pl.ds(start, n), lanes]
            ak = auxk_ref[pl.ds(start, n), lanes]
            vb = v_ref[pl.ds(start, n), lanes]
            ks = (jnp.where(low_k, kb, ak), jnp.where(low_k, ak, kb))
            vs = (jnp.where(low_k, vb, ones), jnp.where(low_k, ones, vb))
            for j in range(2):
                m, acc = carry[2 * pr + j]
                s = lax.dot_general(qs[2 * pr + j], ks[j], (((1,), (1,)), ((), ())),
                                    preferred_element_type=F32)
                if masked:
                    row = lax.broadcasted_iota(I32, (tq, n), 0)
                    col = lax.broadcasted_iota(I32, (tq, n), 1)
                    s = jnp.where(col + (start - qi * tq) <= row, s, -jnp.inf)
                m_new = jnp.maximum(m, jnp.max(s, axis=1, keepdims=True))
                p = jnp.exp2(s - m_new)
                acc = jnp.exp2(m - m_new) * acc + _dot(p.astype(BF16), vs[j])
                out.append((m_new, acc))
        return tuple(out)

    init = tuple((jnp.full((tq, 1), -jnp.inf, F32), jnp.zeros((tq, LANES), F32))
                 for _ in range(2 * n_pairs))
    carry = lax.fori_loop(0, qi, lambda t, c: step(pl.multiple_of(t * tq, tq), tq, c, False), init)
    for d in range(tq // tk):
        carry = step(pl.multiple_of(qi * tq + d * tk, tk), tk, carry, True)
    for pr in range(n_pairs):
        acc0, acc1 = carry[2 * pr][1], carry[2 * pr + 1][1]
        out0 = acc0 / acc0[:, HEAD_DIM:HEAD_DIM + 1]
        out1 = acc1 / acc1[:, 0:1]
        o_ref[:, pr * LANES:(pr + 1) * LANES] = jnp.where(low_q, out0, out1).astype(BF16)


def _attention(q, auxq, k, auxk, v, batch, seq):
    tq = math.gcd(seq, ATTN_Q_TILE)
    tk = math.gcd(tq, ATTN_K_TILE)
    nq = seq // tq
    T = batch * seq
    width = ATTN_PAIRS * LANES
    q_spec = pl.BlockSpec((tq, width), lambda b, hp, qi: (b * nq + qi, hp))
    kv_spec = pl.BlockSpec((seq, width), lambda b, hp, qi: (b, hp))
    return pl.pallas_call(
        functools.partial(_attn_kernel, tq=tq, tk=tk),
        grid=(batch, N_HEADS // (2 * ATTN_PAIRS), nq),
        in_specs=[q_spec, q_spec, kv_spec, kv_spec, kv_spec],
        out_specs=q_spec,
        out_shape=jax.ShapeDtypeStruct((T, FOX_WIDTH), BF16),
        compiler_params=_cparams(("arbitrary", "arbitrary", "arbitrary")),
        name="attn",
    )(q, auxq, k, auxk, v)


def _mix_kernel(x_ref, attn_ref, w2_ref, b2_ref, lng_ref, lnb_ref, ws_ref, bs_ref,
                wa_ref, wb_ref, wo_ref, bo_ref, g1_ref, b1_ref, wr_ref, br_ref,
                h_ref, idx_ref, gate_ref, rank_ref, cnt_ref, carry_ref, *, tm, alpha):
    i = pl.program_id(0)

    @pl.when(i == 0)
    def _():
        carry_ref[...] = jnp.zeros_like(carry_ref)

    x = x_ref[...]
    proj = _dot(x.astype(BF16), w2_ref[...]) + b2_ref[...]
    u = _gelu(proj[:, :GMLP_WIDTH])
    gv = _gelu(proj[:, GMLP_WIDTH:2 * GMLP_WIDTH])
    vln = _layer_norm(gv, lng_ref[...], lnb_ref[...]).astype(BF16)

    cr = lax.broadcasted_iota(I32, (GMLP_CHUNK, GMLP_CHUNK), 0)
    cc = lax.broadcasted_iota(I32, (GMLP_CHUNK, GMLP_CHUNK), 1)
    tril = cc <= cr
    lo_half = cc < HEAD_DIM
    zero_w = jnp.zeros((GMLP_CHUNK, GMLP_CHUNK), BF16)
    n_slab = GMLP_WIDTH // LANES
    lhs = []
    for s in range(n_slab):
        w0 = jnp.where(tril, ws_ref[2 * s], zero_w)
        w1 = jnp.where(tril, ws_ref[2 * s + 1], zero_w)
        lhs.append(jnp.concatenate([w0, w1], axis=1))
    bs = bs_ref[...]
    rows = []
    for c in range(tm // GMLP_CHUNK):
        cols = []
        for s in range(n_slab):
            vs = vln[c * GMLP_CHUNK:(c + 1) * GMLP_CHUNK, s * LANES:(s + 1) * LANES]
            rhs = jnp.concatenate([jnp.where(lo_half, vs, zero_w), jnp.where(lo_half, zero_w, vs)], axis=0)
            cols.append(_dot(lhs[s], rhs))
        rows.append(jnp.concatenate(cols, axis=1) + bs)
    sp = jnp.concatenate(rows, axis=0) if len(rows) > 1 else rows[0]
    sgu = (u * sp).astype(BF16)

    ga = jax.nn.sigmoid(proj[:, 2 * GMLP_WIDTH:2 * GMLP_WIDTH + D_MODEL])
    gb = jax.nn.sigmoid(proj[:, 2 * GMLP_WIDTH + D_MODEL:])
    merged = ga * _dot(attn_ref[...], wa_ref[...]) + gb * _dot(sgu, wb_ref[...])
    mix = _dot(merged.astype(BF16), wo_ref[...]) + bo_ref[...]
    h = _layer_norm(alpha * x + mix, g1_ref[...], b1_ref[...])
    _store_token_rows(h_ref, h)

    a_hi = h.astype(BF16)
    a_lo = (h - a_hi.astype(F32)).astype(BF16)
    wr = wr_ref[...]
    w_hi = wr.astype(BF16)
    w_lo = (wr - w_hi.astype(F32)).astype(BF16)
    logits = _dot(a_hi, w_hi) + _dot(a_lo, w_hi) + _dot(a_hi, w_lo) + br_ref[...]

    lane_i = lax.broadcasted_iota(I32, (tm, LANES), 1)
    lane_f = lane_i.astype(F32)
    vals, idxs = [], []
    l = logits
    for _ in range(TOP_K):
        m = jnp.max(l, axis=1, keepdims=True)
        ix = jnp.min(jnp.where(l == m, lane_f, float(LANES)), axis=1, keepdims=True)
        vals.append(m)
        idxs.append(ix)
        l = jnp.where(lane_f == ix, -jnp.inf, l)
    es = [jnp.exp(v - vals[0]) for v in vals]
    den = es[0] + es[1] + es[2] + es[3]

    onehot = jnp.zeros((tm, LANES), F32)
    idx_out = jnp.zeros((tm, LANES), F32)
    gate_out = jnp.zeros((tm, LANES), F32)
    for k in range(TOP_K):
        onehot = onehot + jnp.where(lane_f == idxs[k], 1.0, 0.0)
        idx_out = jnp.where(lane_i == k, idxs[k], idx_out)
        gate_out = jnp.where(lane_i == k, es[k] / den, gate_out)

    tr = lax.broadcasted_iota(I32, (tm, tm), 0)
    tc = lax.broadcasted_iota(I32, (tm, tm), 1)
    strict = jnp.where(tc < tr, 1.0, 0.0).astype(BF16)
    carry = carry_ref[0:1, :]
    before = _dot(strict, onehot.astype(BF16)) + carry
    rank_out = jnp.zeros((tm, LANES), F32)
    for k in range(TOP_K):
        rk = jnp.sum(jnp.where(lane_f == idxs[k], before, 0.0), axis=1, keepdims=True)
        rank_out = jnp.where(lane_i == k, rk, rank_out)
    new_carry = carry + jnp.sum(onehot, axis=0, keepdims=True)
    carry_ref[...] = jnp.broadcast_to(new_carry, carry_ref.shape)
    cnt_ref[...] = jnp.broadcast_to(new_carry, cnt_ref.shape).astype(I32)
    idx_ref[...] = idx_out.astype(I32)
    gate_ref[...] = gate_out
    rank_ref[...] = rank_out.astype(I32)


def _mix(x2, attn, w2, b2, lng, lnb, ws, bs_tile, wa, wb, wo, bo, g1, b1, wr, br, tm, alpha):
    T = x2.shape[0]
    n2 = w2.shape[1]
    const = lambda *shape: pl.BlockSpec(shape, lambda i: (0,) * len(shape))
    return pl.pallas_call(
        functools.partial(_mix_kernel, tm=tm, alpha=alpha),
        grid=(T // tm,),
        in_specs=[
            pl.BlockSpec((tm, D_MODEL), lambda i: (i, 0)),
            pl.BlockSpec((tm, FOX_WIDTH), lambda i: (i, 0)),
            const(D_MODEL, n2), const(1, n2),
            const(1, GMLP_WIDTH), const(1, GMLP_WIDTH),
            const(GMLP_WIDTH // HEAD_DIM, GMLP_CHUNK, GMLP_CHUNK), const(GMLP_CHUNK, GMLP_WIDTH),
            const(FOX_WIDTH, D_MODEL), const(GMLP_WIDTH, D_MODEL),
            const(D_MODEL, D_MODEL), const(1, D_MODEL),
            const(1, D_MODEL), const(1, D_MODEL),
            const(D_MODEL, LANES), const(1, LANES),
        ],
        out_specs=[
            pl.BlockSpec((tm * ROW_TILE, LANES), lambda i: (i, 0)),
            pl.BlockSpec((tm, LANES), lambda i: (i, 0)),
            pl.BlockSpec((tm, LANES), lambda i: (i, 0)),
            pl.BlockSpec((tm, LANES), lambda i: (i, 0)),
            pl.BlockSpec((ROW_TILE, LANES), lambda i: (0, 0)),
        ],
        out_shape=[
            jax.ShapeDtypeStruct((T * ROW_TILE, LANES), F32),
            jax.ShapeDtypeStruct((T, LANES), I32),
            jax.ShapeDtypeStruct((T, LANES), F32),
            jax.ShapeDtypeStruct((T, LANES), I32),
            jax.ShapeDtypeStruct((ROW_TILE, LANES), I32),
        ],
        scratch_shapes=[pltpu.VMEM((ROW_TILE, LANES), F32)],
        compiler_params=_cparams(("arbitrary",)),
        name="mix",
    )(x2, attn, w2, b2, lng, lnb, ws, bs_tile, wa, wb, wo, bo, g1, b1, wr, br)


def _pipelined_row_dmas(n_sub, issue, drain):
    issue(0)

    def outer(g, carry):
        @pl.when(g + 1 < n_sub)
        def _():
            issue(g + 1)

        drain()
        return carry

    lax.fori_loop(0, n_sub, outer, 0)


def _scatter_kernel(fbase_ref, fcnt_ref, didx_ref, src_ref, dst_ref, sem, *, tn, sub, unroll):
    def row_wait():
        pltpu.make_async_copy(src_ref.at[0], dst_ref.at[0], sem).wait()

    def issue(g):
        def body(j, carry):
            for u in range(unroll):
                t = g * sub + j * unroll + u
                for k in range(TOP_K):
                    pltpu.make_async_copy(src_ref.at[t], dst_ref.at[didx_ref[t * TOP_K + k]], sem).start(
                        priority=k % 2)
            return carry

        lax.fori_loop(0, sub // unroll, body, 0)

    def drain():
        def body(j, carry):
            for _ in range(unroll * TOP_K):
                row_wait()
            return carry

        lax.fori_loop(0, sub // unroll, body, 0)

    _pipelined_row_dmas(tn // sub, issue, drain)

    @pl.when(pl.program_id(0) == 0)
    def _():
        def per_expert(e, carry):
            base = fbase_ref[e]

            def start_one(r, c):
                pltpu.make_async_copy(src_ref.at[0], dst_ref.at[base + r], sem).start()
                return c

            def wait_one(r, c):
                row_wait()
                return c

            lax.fori_loop(0, fcnt_ref[e], start_one, 0)
            lax.fori_loop(0, fcnt_ref[e], wait_one, 0)
            return carry

        lax.fori_loop(0, N_EXPERTS, per_expert, 0)


def _scatter_rows(fill_base, fill_cnt, didx, src3, n_dst):
    T = src3.shape[0]
    tn = math.gcd(T, SCATTER_TOKENS)
    sub = math.gcd(tn, COPY_SUB)
    unroll = math.gcd(sub, 4)
    grid_spec = pltpu.PrefetchScalarGridSpec(
        num_scalar_prefetch=2,
        grid=(T // tn,),
        in_specs=[
            pl.BlockSpec((tn * TOP_K,), lambda i, fb, fc: (i,), memory_space=pltpu.SMEM),
            pl.BlockSpec((tn, ROW_TILE, LANES), lambda i, fb, fc: (i, 0, 0)),
        ],
        out_specs=pl.BlockSpec(memory_space=pl.ANY),
        scratch_shapes=[pltpu.SemaphoreType.DMA(())],
    )
    return pl.pallas_call(
        functools.partial(_scatter_kernel, tn=tn, sub=sub, unroll=unroll),
        grid_spec=grid_spec,
        out_shape=jax.ShapeDtypeStruct((n_dst, ROW_TILE, LANES), src3.dtype),
        compiler_params=_cparams(("arbitrary",)),
        name="scatter_rows",
    )(fill_base, fill_cnt, didx, src3)


def _gather_kernel(sidx_ref, src_ref, out_ref, sem, *, n, sub, unroll):
    def issue(g):
        def body(j, carry):
            for u in range(unroll):
                r = g * sub + j * unroll + u
                pltpu.make_async_copy(src_ref.at[sidx_ref[r]], out_ref.at[r], sem).start(priority=u % 2)
            return carry

        lax.fori_loop(0, sub // unroll, body, 0)

    def drain():
        def body(j, carry):
            for _ in range(unroll):
                pltpu.make_async_copy(src_ref.at[0], out_ref.at[0], sem).wait()
            return carry

        lax.fori_loop(0, sub // unroll, body, 0)

    _pipelined_row_dmas(n // sub, issue, drain)


def _gather_rows(sidx, src3):
    n_all = sidx.shape[0]
    n = math.gcd(n_all, GATHER_ROWS)
    sub = math.gcd(n, COPY_SUB * TOP_K)
    unroll = math.gcd(sub, 16)
    return pl.pallas_call(
        functools.partial(_gather_kernel, n=n, sub=sub, unroll=unroll),
        grid=(n_all // n,),
        in_specs=[
            pl.BlockSpec((n,), lambda i: (i,), memory_space=pltpu.SMEM),
            pl.BlockSpec(memory_space=pl.ANY),
        ],
        out_specs=pl.BlockSpec((n, ROW_TILE, LANES), lambda i: (i, 0, 0)),
        out_shape=jax.ShapeDtypeStruct((n_all, ROW_TILE, LANES), src3.dtype),
        scratch_shapes=[pltpu.SemaphoreType.DMA(())],
        compiler_params=_cparams(("arbitrary",)),
        name="gather_rows",
    )(sidx, src3)


def _moe_kernel(be_ref, bsrc_ref, nu_ref, x_ref, wgu_ref, bgu_ref, wdn_ref, bdn_ref, y_ref,
                wgu_bf, wdn_bf):
    i = pl.program_id(0)
    active = i < nu_ref[0]
    prev = jnp.maximum(i - 1, 0)
    fresh = (i == 0) | (be_ref[i] != be_ref[prev])

    @pl.when(active & fresh)
    def _():
        wgu_bf[...] = wgu_ref[...].astype(BF16)
        wdn_bf[...] = wdn_ref[...].astype(BF16)

    @pl.when(active)
    def _():
        x = _load_token_rows(x_ref, x_ref.shape[0] // ROW_TILE).astype(BF16)
        gu = _dot(x, wgu_bf[...]) + bgu_ref[...]
        gate = jnp.minimum(gu[:, :D_FF], SWIGLU_LIMIT)
        up = jnp.clip(gu[:, D_FF:], -SWIGLU_LIMIT, SWIGLU_LIMIT)
        hid = (up + 1.0) * (gate * jax.nn.sigmoid(SWIGLU_ALPHA * gate))
        _store_token_rows(y_ref, _dot(hid.astype(BF16), wdn_bf[...]) + bdn_ref[...])


def _moe(blk_e, blk_src, n_used, xpad, wgu, bgu, wdn, bdn, n_blk, bm):
    grid_spec = pltpu.PrefetchScalarGridSpec(
        num_scalar_prefetch=3,
        grid=(n_blk,),
        in_specs=[
            pl.BlockSpec((bm * ROW_TILE, LANES), lambda i, be, bs, nu: (bs[i], 0)),
            pl.BlockSpec((None, D_MODEL, 2 * D_FF), lambda i, be, bs, nu: (be[i], 0, 0)),
            pl.BlockSpec((None, 1, 2 * D_FF), lambda i, be, bs, nu: (be[i], 0, 0)),
            pl.BlockSpec((None, D_FF, D_MODEL), lambda i, be, bs, nu: (be[i], 0, 0)),
            pl.BlockSpec((None, 1, D_MODEL), lambda i, be, bs, nu: (be[i], 0, 0)),
        ],
        out_specs=pl.BlockSpec((bm * ROW_TILE, LANES), lambda i, be, bs, nu: (bs[i], 0)),
        scratch_shapes=[pltpu.VMEM((D_MODEL, 2 * D_FF), BF16), pltpu.VMEM((D_FF, D_MODEL), BF16)],
    )
    return pl.pallas_call(
        _moe_kernel,
        grid_spec=grid_spec,
        out_shape=jax.ShapeDtypeStruct((n_blk * bm * ROW_TILE, LANES), F32),
        compiler_params=_cparams(("arbitrary",)),
        name="moe",
    )(blk_e, blk_src, n_used, xpad, wgu, bgu, wdn, bdn)


def _final_kernel(h_ref, yg_ref, gate_ref, p_ref, wple_ref, wpg_ref, bpg_ref,
                  g2_ref, b2_ref, g3_ref, b3_ref, o_ref, *, tm, alpha):
    h = _load_token_rows(h_ref, tm)
    gates = gate_ref[...]
    ffn = gates[:, 0:1] * _load_token_rows(yg_ref.at[0], tm)
    for k in range(1, TOP_K):
        ffn = ffn + gates[:, k:k + 1] * _load_token_rows(yg_ref.at[k], tm)
    h2 = _layer_norm(alpha * h + ffn, g2_ref[...], b2_ref[...])
    emb = _dot(p_ref[...].astype(BF16), wple_ref[...])
    pg = jax.nn.sigmoid(_dot(h2.astype(BF16), wpg_ref[...]) + bpg_ref[...])
    o_ref[...] = _layer_norm(alpha * h2 + emb * pg, g3_ref[...], b3_ref[...])


def _final(h1, yg, gates, p2, wple, wpg, bpg, g2, b2, g3, b3, tm, alpha):
    T = h1.shape[0] // ROW_TILE
    const = lambda *shape: pl.BlockSpec(shape, lambda i: (0,) * len(shape))
    return pl.pallas_call(
        functools.partial(_final_kernel, tm=tm, alpha=alpha),
        grid=(T // tm,),
        in_specs=[
            pl.BlockSpec((tm * ROW_TILE, LANES), lambda i: (i, 0)),
            pl.BlockSpec((TOP_K, tm * ROW_TILE, LANES), lambda i: (0, i, 0)),
            pl.BlockSpec((tm, LANES), lambda i: (i, 0)),
            pl.BlockSpec((tm, PLE_DIM), lambda i: (i, 0)),
            const(PLE_DIM, D_MODEL), const(D_MODEL, D_MODEL), const(1, D_MODEL),
            const(1, D_MODEL), const(1, D_MODEL), const(1, D_MODEL), const(1, D_MODEL),
        ],
        out_specs=pl.BlockSpec((tm, D_MODEL), lambda i: (i, 0)),
        out_shape=jax.ShapeDtypeStruct((T, D_MODEL), F32),
        compiler_params=_cparams(("arbitrary",)),
        name="final",
    )(h1, yg, gates, p2, wple, wpg, bpg, g2, b2, g3, b3)


def _layer(h2d, p2d, batch, seq, alpha, w_in, b_in, gmlp_ln_g, gmlp_ln_b, w_spatial, b_spatial,
           w_branch_a, w_branch_b, w_out, b_out, ln1_g, ln1_b, w_router, b_router,
           w_gate_up, b_gate_up, w_down, b_down, ln2_g, ln2_b, w_ple, w_ple_gate, b_ple_gate,
           ln3_g, ln3_b):
    T = batch * seq
    tm = math.gcd(T, ROW_TILE_A)
    bm = MOE_BLOCK
    off_f = 3 * FOX_WIDTH
    off_u = off_f + N_HEADS
    row = lambda v: v.reshape(1, -1).astype(F32)

    w1 = jnp.concatenate([w_in[:, :off_u], jnp.zeros((D_MODEL, LANES - N_HEADS), F32)], axis=1).astype(BF16)
    b1 = jnp.concatenate([b_in[:off_u], jnp.zeros((LANES - N_HEADS,), F32)]).reshape(1, -1)
    w2 = w_in[:, off_u:].astype(BF16)
    b2 = row(b_in[off_u:])
    bs_tile = jnp.repeat(b_spatial.T, HEAD_DIM, axis=1)
    wr = jnp.concatenate([w_router, jnp.zeros((D_MODEL, LANES - N_EXPERTS), F32)], axis=1)
    br = jnp.concatenate([b_router, jnp.full((LANES - N_EXPERTS,), -1e30, F32)]).reshape(1, -1)

    q, k, v, f_pad = _qkvf(h2d, w1, b1, tm)
    auxq, auxk = _decay(f_pad, batch, seq)
    attn = _attention(q, auxq, k, auxk, v, batch, seq)
    h1, idx_o, gate_o, rank_o, cnt_o = _mix(
        h2d, attn, w2, b2, row(gmlp_ln_g), row(gmlp_ln_b), w_spatial.astype(BF16), bs_tile,
        w_branch_a.astype(BF16), w_branch_b.astype(BF16), w_out.astype(BF16), row(b_out),
        row(ln1_g), row(ln1_b), wr, br, tm, alpha)

    counts = cnt_o[0, :N_EXPERTS]
    padded = (counts + bm - 1) // bm * bm
    pad_end = jnp.cumsum(padded)
    pad_start = pad_end - padded
    dest = (jnp.take(pad_start, idx_o[:, :TOP_K]) + rank_o[:, :TOP_K]).reshape(-1)
    n_assign = T * TOP_K
    n_blk = -(-n_assign // bm) + N_EXPERTS
    n_rows = n_blk * bm
    n_used = pad_end[-1] // bm
    blk_src = jnp.minimum(jnp.arange(n_blk, dtype=I32), n_used - 1).astype(I32)
    blk_e = jnp.minimum(jnp.sum(pad_end[None, :] <= (blk_src * bm)[:, None], axis=1), N_EXPERTS - 1).astype(I32)
    dest = dest.astype(I32)

    h1_rows = h1.reshape(T, ROW_TILE, LANES)
    xpad = _scatter_rows((pad_start + counts).astype(I32), (padded - counts).astype(I32), dest, h1_rows, n_rows)
    ypad = _moe(blk_e, blk_src, n_used.reshape(1).astype(I32), xpad.reshape(n_rows * ROW_TILE, LANES),
                w_gate_up, b_gate_up.reshape(N_EXPERTS, 1, -1), w_down, b_down.reshape(N_EXPERTS, 1, -1),
                n_blk, bm)
    yg = _gather_rows(dest.reshape(T, TOP_K).T.reshape(-1), ypad.reshape(n_rows, ROW_TILE, LANES))
    return _final(h1, yg.reshape(TOP_K, T * ROW_TILE, LANES), gate_o, p2d, w_ple.astype(BF16),
                  w_ple_gate.astype(BF16), row(b_ple_gate), row(ln2_g), row(ln2_b),
                  row(ln3_g), row(ln3_b), tm, alpha)


def kernel(x, p, w_in, b_in, gmlp_ln_g, gmlp_ln_b, w_spatial, b_spatial, w_branch_a, w_branch_b, w_out, b_out, ln1_g, ln1_b, w_router, b_router, w_gate_up, b_gate_up, w_down, b_down, ln2_g, ln2_b, w_ple, w_ple_gate, b_ple_gate, ln3_g, ln3_b):
    batch, seq, d = x.shape
    depth = w_in.shape[0]
    assert d == D_MODEL and seq % GMLP_CHUNK == 0
    alpha = (2.0 * depth) ** 0.25
    h = x.reshape(batch * seq, d)
    for i in range(depth):
        h = _layer(h, p[i].reshape(batch * seq, PLE_DIM), batch, seq, alpha,
                   w_in[i], b_in[i], gmlp_ln_g[i], gmlp_ln_b[i], w_spatial[i], b_spatial[i],
                   w_branch_a[i], w_branch_b[i], w_out[i], b_out[i], ln1_g[i], ln1_b[i],
                   w_router[i], b_router[i], w_gate_up[i], b_gate_up[i], w_down[i], b_down[i],
                   ln2_g[i], ln2_b[i], w_ple[i], w_ple_gate[i], b_ple_gate[i], ln3_g[i], ln3_b[i])
    return h.reshape(batch, seq, d)
```

```python
import functools
import math

import jax
import jax.numpy as jnp
import numpy as np
from jax import lax
from jax.experimental import pallas as pl
from jax.experimental.pallas import tpu as pltpu
from jax.experimental.pallas import tpu_sc as plsc

F32 = jnp.float32
BF16 = jnp.bfloat16
I32 = jnp.int32

D_MODEL = 1024
N_HEADS = 8
HEAD_DIM = 64
FOX_WIDTH = N_HEADS * HEAD_DIM
GMLP_WIDTH = 512
GMLP_CHUNK = 128
N_EXPERTS = 32
TOP_K = 4
D_FF = 1024
PLE_DIM = 256
SWIGLU_LIMIT = 7.0
SWIGLU_ALPHA = 1.702
LN_EPS = 1e-5
LANES = 128
ROW_TILE = 8
VMEM_LIMIT = 56 * 1024 * 1024

MOE_BLOCK = 256
ATTN_Q_TILE = 512
ATTN_K_TILE = 512
ATTN_PAIRS = 2
LOG2E = math.log2(math.e)
ROW_TILE_A = 512
SCATTER_TOKENS = 1024
GATHER_ROWS = 2048
COPY_SUB = 256
SC_CORES = 2
SC_SUBCORES = 16
SC_WINDOW = 128


def _cparams(sem):
    return pltpu.CompilerParams(dimension_semantics=sem, vmem_limit_bytes=VMEM_LIMIT)


def _gelu(x):
    c = math.sqrt(2.0 / math.pi)
    return 0.5 * x * (1.0 + jnp.tanh(c * (x + 0.044715 * (x * x * x))))


def _layer_norm(x, g, b):
    mu = jnp.mean(x, axis=-1, keepdims=True)
    xc = x - mu
    var = jnp.mean(xc * xc, axis=-1, keepdims=True)
    return xc * lax.rsqrt(var + LN_EPS) * g + b


def _split3(x):
    hi = x.astype(BF16)
    r = x - hi.astype(F32)
    mid = r.astype(BF16)
    lo = (r - mid.astype(F32)).astype(BF16)
    return hi, mid, lo


def _dot(a, b):
    return jnp.dot(a, b, preferred_element_type=F32)


def _load_token_rows(ref, n):
    return jnp.concatenate([ref[pl.ds(j, n, stride=ROW_TILE), :] for j in range(ROW_TILE)], axis=1)


def _store_token_rows(ref, val):
    n = val.shape[0]
    for j in range(ROW_TILE):
        ref[pl.ds(j, n, stride=ROW_TILE), :] = val[:, j * LANES:(j + 1) * LANES]


def _qkvf_kernel(x_ref, w_ref, b_ref, q_ref, k_ref, v_ref, f_ref):
    x = x_ref[...].astype(BF16)
    proj = _dot(x, w_ref[...]) + b_ref[...]
    q_ref[...] = (proj[:, :FOX_WIDTH] * (HEAD_DIM ** -0.5 * LOG2E)).astype(BF16)
    k_ref[...] = proj[:, FOX_WIDTH:2 * FOX_WIDTH].astype(BF16)
    v_ref[...] = proj[:, 2 * FOX_WIDTH:3 * FOX_WIDTH].astype(BF16)
    f_ref[...] = proj[:, 3 * FOX_WIDTH:]


def _qkvf(x2, w, b, tm):
    T = x2.shape[0]
    n_out = w.shape[1]
    return pl.pallas_call(
        _qkvf_kernel,
        grid=(T // tm,),
        in_specs=[
            pl.BlockSpec((tm, D_MODEL), lambda i: (i, 0)),
            pl.BlockSpec((D_MODEL, n_out), lambda i: (0, 0)),
            pl.BlockSpec((1, n_out), lambda i: (0, 0)),
        ],
        out_specs=[
            pl.BlockSpec((tm, FOX_WIDTH), lambda i: (i, 0)),
            pl.BlockSpec((tm, FOX_WIDTH), lambda i: (i, 0)),
            pl.BlockSpec((tm, FOX_WIDTH), lambda i: (i, 0)),
            pl.BlockSpec((tm, LANES), lambda i: (i, 0)),
        ],
        out_shape=[
            jax.ShapeDtypeStruct((T, FOX_WIDTH), BF16),
            jax.ShapeDtypeStruct((T, FOX_WIDTH), BF16),
            jax.ShapeDtypeStruct((T, FOX_WIDTH), BF16),
            jax.ShapeDtypeStruct((T, LANES), F32),
        ],
        compiler_params=_cparams(("arbitrary",)),
        name="qkvf",
    )(x2, w, b)


def _decay_placement():
    pq = np.zeros((3 * LANES, FOX_WIDTH), np.float32)
    pk = np.zeros((3 * LANES, FOX_WIDTH), np.float32)
    cq = np.zeros((1, FOX_WIDTH), np.float32)
    ck = np.zeros((1, FOX_WIDTH), np.float32)
    for h in range(N_HEADS):
        base = (h // 2) * LANES + (HEAD_DIM if h % 2 == 0 else 0)
        for piece in range(3):
            pq[piece * LANES + h, base + piece] = 1.0
            pk[piece * LANES + h, base + 3 + piece] = -1.0
            cq[0, base + 3 + piece] = 1.0
            ck[0, base + piece] = 1.0
    return pq, pk, cq, ck


def _decay_kernel(f_ref, pq_ref, pk_ref, cq_ref, ck_ref, auxq_ref, auxk_ref, *, seq, blk):
    r = lax.broadcasted_iota(I32, (blk, blk), 0)
    c = lax.broadcasted_iota(I32, (blk, blk), 1)
    tri = jnp.where(c <= r, 1.0, 0.0).astype(BF16)
    carry = jnp.zeros((1, LANES), F32)
    for i in range(seq // blk):
        f = f_ref[i * blk:(i + 1) * blk, :]
        ls = jnp.minimum(f, 0.0) - jnp.log1p(jnp.exp(-jnp.abs(f)))
        hi, mid, lo = _split3(ls)
        cs = _dot(tri, hi) + _dot(tri, mid) + _dot(tri, lo) + carry
        carry = cs[blk - 1:blk, :]
        pieces = jnp.concatenate(_split3(cs * LOG2E), axis=1)
        auxq_ref[i * blk:(i + 1) * blk, :] = (_dot(pieces, pq_ref[...]) + cq_ref[...]).astype(BF16)
        auxk_ref[i * blk:(i + 1) * blk, :] = (_dot(pieces, pk_ref[...]) + ck_ref[...]).astype(BF16)


def _decay(f_pad, batch, seq):
    blk = 256 if seq % 256 == 0 else LANES
    pq, pk, cq, ck = _decay_placement()
    const = lambda *shape: pl.BlockSpec(shape, lambda b: (0,) * len(shape))
    return pl.pallas_call(
        functools.partial(_decay_kernel, seq=seq, blk=blk),
        grid=(batch,),
        in_specs=[
            pl.BlockSpec((seq, LANES), lambda b: (b, 0)),
            const(3 * LANES, FOX_WIDTH), const(3 * LANES, FOX_WIDTH), const(1, FOX_WIDTH), const(1, FOX_WIDTH),
        ],
        out_specs=[
            pl.BlockSpec((seq, FOX_WIDTH), lambda b: (b, 0)),
            pl.BlockSpec((seq, FOX_WIDTH), lambda b: (b, 0)),
        ],
        out_shape=[
            jax.ShapeDtypeStruct((batch * seq, FOX_WIDTH), BF16),
            jax.ShapeDtypeStruct((batch * seq, FOX_WIDTH), BF16),
        ],
        compiler_params=_cparams(("arbitrary",)),
        name="decay",
    )(f_pad, jnp.asarray(pq, BF16), jnp.asarray(pk, BF16), jnp.asarray(cq), jnp.asarray(ck))


def _attn_kernel(q_ref, auxq_ref, k_ref, auxk_ref, v_ref, o_ref, *, tq, tk):
    qi = pl.program_id(2)
    n_pairs = q_ref.shape[1] // LANES
    low_q = lax.broadcasted_iota(I32, (tq, LANES), 1) < HEAD_DIM
    qs = []
    for pr in range(n_pairs):
        q = q_ref[:, pr * LANES:(pr + 1) * LANES]
        aq = auxq_ref[:, pr * LANES:(pr + 1) * LANES]
        qs += [jnp.where(low_q, q, aq), jnp.where(low_q, aq, q)]

    def step(start, n, carry, masked):
        low_k = lax.broadcasted_iota(I32, (n, LANES), 1) < HEAD_DIM
        ones = jnp.ones((n, LANES), BF16)
        out = []
        for pr in range(n_pairs):
            lanes = slice(pr * LANES, (pr + 1) * LANES)
            kb = k_ref[pl.ds(start, n), lanes]
            ak = auxk_ref[pl.ds(start, n), lanes]
            vb = v_ref[pl.ds(start, n), lanes]
            ks = (jnp.where(low_k, kb, ak), jnp.where(low_k, ak, kb))
            vs = (jnp.where(low_k, vb, ones), jnp.where(low_k, ones, vb))
            for j in range(2):
                m, acc = carry[2 * pr + j]
                s = lax.dot_general(qs[2 * pr + j], ks[j], (((1,), (1,)), ((), ())),
                                    preferred_element_type=F32)
                if masked:
                    row = lax.broadcasted_iota(I32, (tq, n), 0)
                    col = lax.broadcasted_iota(I32, (tq, n), 1)
                    s = jnp.where(col + (start - qi * tq) <= row, s, -jnp.inf)
                m_new = jnp.maximum(m, jnp.max(s, axis=1, keepdims=True))
                p = jnp.exp2(s - m_new)
                acc = jnp.exp2(m - m_new) * acc + _dot(p.astype(BF16), vs[j])
                out.append((m_new, acc))
        return tuple(out)

    init = tuple((jnp.full((tq, 1), -jnp.inf, F32), jnp.zeros((tq, LANES), F32))
                 for _ in range(2 * n_pairs))
    carry = lax.fori_loop(0, qi, lambda t, c: step(pl.multiple_of(t * tq, tq), tq, c, False), init)
    for d in range(tq // tk):
        carry = step(pl.multiple_of(qi * tq + d * tk, tk), tk, carry, True)
    for pr in range(n_pairs):
        acc0, acc1 = carry[2 * pr][1], carry[2 * pr + 1][1]
        out0 = acc0 / acc0[:, HEAD_DIM:HEAD_DIM + 1]
        out1 = acc1 / acc1[:, 0:1]
        o_ref[:, pr * LANES:(pr + 1) * LANES] = jnp.where(low_q, out0, out1).astype(BF16)


def _attention(q, auxq, k, auxk, v, batch, seq):
    tq = math.gcd(seq, ATTN_Q_TILE)
    tk = math.gcd(tq, ATTN_K_TILE)
    nq = seq // tq
    T = batch * seq
    width = ATTN_PAIRS * LANES
    q_spec = pl.BlockSpec((tq, width), lambda b, hp, qi: (b * nq + qi, hp))
    kv_spec = pl.BlockSpec((seq, width), lambda b, hp, qi: (b, hp))
    return pl.pallas_call(
        functools.partial(_attn_kernel, tq=tq, tk=tk),
        grid=(batch, N_HEADS // (2 * ATTN_PAIRS), nq),
        in_specs=[q_spec, q_spec, kv_spec, kv_spec, kv_spec],
        out_specs=q_spec,
        out_shape=jax.ShapeDtypeStruct((T, FOX_WIDTH), BF16),
        compiler_params=_cparams(("arbitrary", "arbitrary", "arbitrary")),
        name="attn",
    )(q, auxq, k, auxk, v)


def _mix_kernel(x_ref, attn_ref, w2_ref, b2_ref, lng_ref, lnb_ref, ws_ref, bs_ref,
                wa_ref, wb_ref, wo_ref, bo_ref, g1_ref, b1_ref, wr_ref, br_ref,
                h_ref, idx_ref, gate_ref, rank_ref, cnt_ref, carry_ref, *, tm, alpha):
    i = pl.program_id(0)

    @pl.when(i == 0)
    def _():
        carry_ref[...] = jnp.zeros_like(carry_ref)

    x = x_ref[...]
    proj = _dot(x.astype(BF16), w2_ref[...]) + b2_ref[...]
    u = _gelu(proj[:, :GMLP_WIDTH])
    gv = _gelu(proj[:, GMLP_WIDTH:2 * GMLP_WIDTH])
    vln = _layer_norm(gv, lng_ref[...], lnb_ref[...]).astype(BF16)

    cr = lax.broadcasted_iota(I32, (GMLP_CHUNK, GMLP_CHUNK), 0)
    cc = lax.broadcasted_iota(I32, (GMLP_CHUNK, GMLP_CHUNK), 1)
    tril = cc <= cr
    lo_half = cc < HEAD_DIM
    zero_w = jnp.zeros((GMLP_CHUNK, GMLP_CHUNK), BF16)
    n_slab = GMLP_WIDTH // LANES
    lhs = []
    for s in range(n_slab):
        w0 = jnp.where(tril, ws_ref[2 * s], zero_w)
        w1 = jnp.where(tril, ws_ref[2 * s + 1], zero_w)
        lhs.append(jnp.concatenate([w0, w1], axis=1))
    bs = bs_ref[...]
    rows = []
    for c in range(tm // GMLP_CHUNK):
        cols = []
        for s in range(n_slab):
            vs = vln[c * GMLP_CHUNK:(c + 1) * GMLP_CHUNK, s * LANES:(s + 1) * LANES]
            rhs = jnp.concatenate([jnp.where(lo_half, vs, zero_w), jnp.where(lo_half, zero_w, vs)], axis=0)
            cols.append(_dot(lhs[s], rhs))
        rows.append(jnp.concatenate(cols, axis=1) + bs)
    sp = jnp.concatenate(rows, axis=0) if len(rows) > 1 else rows[0]
    sgu = (u * sp).astype(BF16)

    ga = jax.nn.sigmoid(proj[:, 2 * GMLP_WIDTH:2 * GMLP_WIDTH + D_MODEL])
    gb = jax.nn.sigmoid(proj[:, 2 * GMLP_WIDTH + D_MODEL:])
    merged = ga * _dot(attn_ref[...], wa_ref[...]) + gb * _dot(sgu, wb_ref[...])
    mix = _dot(merged.astype(BF16), wo_ref[...]) + bo_ref[...]
    h = _layer_norm(alpha * x + mix, g1_ref[...], b1_ref[...])
    _store_token_rows(h_ref, h)

    a_hi = h.astype(BF16)
    a_lo = (h - a_hi.astype(F32)).astype(BF16)
    wr = wr_ref[...]
    w_hi = wr.astype(BF16)
    w_lo = (wr - w_hi.astype(F32)).astype(BF16)
    logits = _dot(a_hi, w_hi) + _dot(a_lo, w_hi) + _dot(a_hi, w_lo) + br_ref[...]

    lane_i = lax.broadcasted_iota(I32, (tm, LANES), 1)
    lane_f = lane_i.astype(F32)
    vals, idxs = [], []
    l = logits
    for _ in range(TOP_K):
        m = jnp.max(l, axis=1, keepdims=True)
        ix = jnp.min(jnp.where(l == m, lane_f, float(LANES)), axis=1, keepdims=True)
        vals.append(m)
        idxs.append(ix)
        l = jnp.where(lane_f == ix, -jnp.inf, l)
    es = [jnp.exp(v - vals[0]) for v in vals]
    den = es[0] + es[1] + es[2] + es[3]

    onehot = jnp.zeros((tm, LANES), F32)
    idx_out = jnp.zeros((tm, LANES), F32)
    gate_out = jnp.zeros((tm, LANES), F32)
    for k in range(TOP_K):
        onehot = onehot + jnp.where(lane_f == idxs[k], 1.0, 0.0)
        idx_out = jnp.where(lane_i == k, idxs[k], idx_out)
        gate_out = jnp.where(lane_i == k, es[k] / den, gate_out)

    tr = lax.broadcasted_iota(I32, (tm, tm), 0)
    tc = lax.broadcasted_iota(I32, (tm, tm), 1)
    strict = jnp.where(tc < tr, 1.0, 0.0).astype(BF16)
    carry = carry_ref[0:1, :]
    before = _dot(strict, onehot.astype(BF16)) + carry
    rank_out = jnp.zeros((tm, LANES), F32)
    for k in range(TOP_K):
        rk = jnp.sum(jnp.where(lane_f == idxs[k], before, 0.0), axis=1, keepdims=True)
        rank_out = jnp.where(lane_i == k, rk, rank_out)
    new_carry = carry + jnp.sum(onehot, axis=0, keepdims=True)
    carry_ref[...] = jnp.broadcast_to(new_carry, carry_ref.shape)
    cnt_ref[...] = jnp.broadcast_to(new_carry, cnt_ref.shape).astype(I32)
    idx_ref[...] = idx_out.astype(I32)
    gate_ref[...] = gate_out
    rank_ref[...] = rank_out.astype(I32)


def _mix(x2, attn, w2, b2, lng, lnb, ws, bs_tile, wa, wb, wo, bo, g1, b1, wr, br, tm, alpha):
    T = x2.shape[0]
    n2 = w2.shape[1]
    const = lambda *shape: pl.BlockSpec(shape, lambda i: (0,) * len(shape))
    return pl.pallas_call(
        functools.partial(_mix_kernel, tm=tm, alpha=alpha),
        grid=(T // tm,),
        in_specs=[
            pl.BlockSpec((tm, D_MODEL), lambda i: (i, 0)),
            pl.BlockSpec((tm, FOX_WIDTH), lambda i: (i, 0)),
            const(D_MODEL, n2), const(1, n2),
            const(1, GMLP_WIDTH), const(1, GMLP_WIDTH),
            const(GMLP_WIDTH // HEAD_DIM, GMLP_CHUNK, GMLP_CHUNK), const(GMLP_CHUNK, GMLP_WIDTH),
            const(FOX_WIDTH, D_MODEL), const(GMLP_WIDTH, D_MODEL),
            const(D_MODEL, D_MODEL), const(1, D_MODEL),
            const(1, D_MODEL), const(1, D_MODEL),
            const(D_MODEL, LANES), const(1, LANES),
        ],
        out_specs=[
            pl.BlockSpec((tm * ROW_TILE, LANES), lambda i: (i, 0)),
            pl.BlockSpec((tm, LANES), lambda i: (i, 0)),
            pl.BlockSpec((tm, LANES), lambda i: (i, 0)),
            pl.BlockSpec((tm, LANES), lambda i: (i, 0)),
            pl.BlockSpec((ROW_TILE, LANES), lambda i: (0, 0)),
        ],
        out_shape=[
            jax.ShapeDtypeStruct((T * ROW_TILE, LANES), F32),
            jax.ShapeDtypeStruct((T, LANES), I32),
            jax.ShapeDtypeStruct((T, LANES), F32),
            jax.ShapeDtypeStruct((T, LANES), I32),
            jax.ShapeDtypeStruct((ROW_TILE, LANES), I32),
        ],
        scratch_shapes=[pltpu.VMEM((ROW_TILE, LANES), F32)],
        compiler_params=_cparams(("arbitrary",)),
        name="mix",
    )(x2, attn, w2, b2, lng, lnb, ws, bs_tile, wa, wb, wo, bo, g1, b1, wr, br)


def _pipelined_row_dmas(n_sub, issue, drain):
    issue(0)

    def outer(g, carry):
        @pl.when(g + 1 < n_sub)
        def _():
            issue(g + 1)

        drain()
        return carry

    lax.fori_loop(0, n_sub, outer, 0)


def _scatter_kernel(fbase_ref, fcnt_ref, didx_ref, src_ref, dst_ref, sem, *, tn, sub, unroll):
    def row_wait():
        pltpu.make_async_copy(src_ref.at[0], dst_ref.at[0], sem).wait()

    def issue(g):
        def body(j, carry):
            for u in range(unroll):
                t = g * sub + j * unroll + u
                for k in range(TOP_K):
                    pltpu.make_async_copy(src_ref.at[t], dst_ref.at[didx_ref[t * TOP_K + k]], sem).start(
                        priority=k % 2)
            return carry

        lax.fori_loop(0, sub // unroll, body, 0)

    def drain():
        def body(j, carry):
            for _ in range(unroll * TOP_K):
                row_wait()
            return carry

        lax.fori_loop(0, sub // unroll, body, 0)

    _pipelined_row_dmas(tn // sub, issue, drain)

    @pl.when(pl.program_id(0) == 0)
    def _():
        def per_expert(e, carry):
            base = fbase_ref[e]

            def start_one(r, c):
                pltpu.make_async_copy(src_ref.at[0], dst_ref.at[base + r], sem).start()
                return c

            def wait_one(r, c):
                row_wait()
                return c

            lax.fori_loop(0, fcnt_ref[e], start_one, 0)
            lax.fori_loop(0, fcnt_ref[e], wait_one, 0)
            return carry

        lax.fori_loop(0, N_EXPERTS, per_expert, 0)


def _scatter_rows(fill_base, fill_cnt, didx, src3, n_dst):
    T = src3.shape[0]
    tn = math.gcd(T, SCATTER_TOKENS)
    sub = math.gcd(tn, COPY_SUB)
    unroll = math.gcd(sub, 4)
    grid_spec = pltpu.PrefetchScalarGridSpec(
        num_scalar_prefetch=2,
        grid=(T // tn,),
        in_specs=[
            pl.BlockSpec((tn * TOP_K,), lambda i, fb, fc: (i,), memory_space=pltpu.SMEM),
            pl.BlockSpec((tn, ROW_TILE, LANES), lambda i, fb, fc: (i, 0, 0)),
        ],
        out_specs=pl.BlockSpec(memory_space=pl.ANY),
        scratch_shapes=[pltpu.SemaphoreType.DMA(())],
    )
    return pl.pallas_call(
        functools.partial(_scatter_kernel, tn=tn, sub=sub, unroll=unroll),
        grid_spec=grid_spec,
        out_shape=jax.ShapeDtypeStruct((n_dst, ROW_TILE, LANES), src3.dtype),
        compiler_params=_cparams(("arbitrary",)),
        name="scatter_rows",
    )(fill_base, fill_cnt, didx, src3)


def _gather_kernel(sidx_ref, src_ref, out_ref, sem, *, n, sub, unroll):
    def issue(g):
        def body(j, carry):
            for u in range(unroll):
                r = g * sub + j * unroll + u
                pltpu.make_async_copy(src_ref.at[sidx_ref[r]], out_ref.at[r], sem).start(priority=u % 2)
            return carry

        lax.fori_loop(0, sub // unroll, body, 0)

    def drain():
        def body(j, carry):
            for _ in range(unroll):
                pltpu.make_async_copy(src_ref.at[0], out_ref.at[0], sem).wait()
            return carry

        lax.fori_loop(0, sub // unroll, body, 0)

    _pipelined_row_dmas(n // sub, issue, drain)


def _gather_rows(sidx, src3):
    n_all = sidx.shape[0]
    n = math.gcd(n_all, GATHER_ROWS)
    sub = math.gcd(n, COPY_SUB * TOP_K)
    unroll = math.gcd(sub, 16)
    return pl.pallas_call(
        functools.partial(_gather_kernel, n=n, sub=sub, unroll=unroll),
        grid=(n_all // n,),
        in_specs=[
            pl.BlockSpec((n,), lambda i: (i,), memory_space=pltpu.SMEM),
            pl.BlockSpec(memory_space=pl.ANY),
        ],
        out_specs=pl.BlockSpec((n, ROW_TILE, LANES), lambda i: (i, 0, 0)),
        out_shape=jax.ShapeDtypeStruct((n_all, ROW_TILE, LANES), src3.dtype),
        scratch_shapes=[pltpu.SemaphoreType.DMA(())],
        compiler_params=_cparams(("arbitrary",)),
        name="gather_rows",
    )(sidx, src3)


def _sc_gather_rows(sidx, src3):
    n_rows = sidx.shape[0] * ROW_TILE
    src2 = src3.reshape(-1, LANES)
    idx = (sidx[:, None] * ROW_TILE + jnp.arange(ROW_TILE, dtype=I32)[None, :]).reshape(1, n_rows)
    mesh = plsc.VectorSubcoreMesh(core_axis_name="core", subcore_axis_name="subcore",
                                  num_cores=SC_CORES, num_subcores=SC_SUBCORES)

    @pl.kernel(out_type=jax.ShapeDtypeStruct((n_rows, LANES), src3.dtype), mesh=mesh, name="sc_gather")
    def gather(src_hbm, i_hbm, o_hbm):
        def body(i_vmem, o_vmem):
            pltpu.sync_copy(src_hbm.at[i_vmem.at[0]], o_vmem)

        pltpu.emit_pipeline(
            body,
            grid=(n_rows // SC_WINDOW,),
            in_specs=[pl.BlockSpec((1, SC_WINDOW), lambda i: (0, i))],
            out_specs=[pl.BlockSpec((SC_WINDOW, LANES), lambda i: (i, 0))],
            core_axis_name=("core", "subcore"),
            dimension_semantics=(pltpu.PARALLEL,),
        )(i_hbm, o_hbm)

    return gather(src2, idx).reshape(sidx.shape[0], ROW_TILE, LANES)


def _moe_kernel(be_ref, bsrc_ref, nu_ref, x_ref, wgu_ref, bgu_ref, wdn_ref, bdn_ref, y_ref,
                wgu_bf, wdn_bf):
    i = pl.program_id(0)
    active = i < nu_ref[0]
    prev = jnp.maximum(i - 1, 0)
    fresh = (i == 0) | (be_ref[i] != be_ref[prev])

    @pl.when(active & fresh)
    def _():
        wgu_bf[...] = wgu_ref[...].astype(BF16)
        wdn_bf[...] = wdn_ref[...].astype(BF16)

    @pl.when(active)
    def _():
        x = _load_token_rows(x_ref, x_ref.shape[0] // ROW_TILE).astype(BF16)
        gu = _dot(x, wgu_bf[...]) + bgu_ref[...]
        gate = jnp.minimum(gu[:, :D_FF], SWIGLU_LIMIT)
        up = jnp.clip(gu[:, D_FF:], -SWIGLU_LIMIT, SWIGLU_LIMIT)
        hid = (up + 1.0) * (gate * jax.nn.sigmoid(SWIGLU_ALPHA * gate))
        _store_token_rows(y_ref, _dot(hid.astype(BF16), wdn_bf[...]) + bdn_ref[...])


def _moe(blk_e, blk_src, n_used, xpad, wgu, bgu, wdn, bdn, n_blk, bm):
    grid_spec = pltpu.PrefetchScalarGridSpec(
        num_scalar_prefetch=3,
        grid=(n_blk,),
        in_specs=[
            pl.BlockSpec((bm * ROW_TILE, LANES), lambda i, be, bs, nu: (bs[i], 0)),
            pl.BlockSpec((None, D_MODEL, 2 * D_FF), lambda i, be, bs, nu: (be[i], 0, 0)),
            pl.BlockSpec((None, 1, 2 * D_FF), lambda i, be, bs, nu: (be[i], 0, 0)),
            pl.BlockSpec((None, D_FF, D_MODEL), lambda i, be, bs, nu: (be[i], 0, 0)),
            pl.BlockSpec((None, 1, D_MODEL), lambda i, be, bs, nu: (be[i], 0, 0)),
        ],
        out_specs=pl.BlockSpec((bm * ROW_TILE, LANES), lambda i, be, bs, nu: (bs[i], 0)),
        scratch_shapes=[pltpu.VMEM((D_MODEL, 2 * D_FF), BF16), pltpu.VMEM((D_FF, D_MODEL), BF16)],
    )
    return pl.pallas_call(
        _moe_kernel,
        grid_spec=grid_spec,
        out_shape=jax.ShapeDtypeStruct((n_blk * bm * ROW_TILE, LANES), F32),
        compiler_params=_cparams(("arbitrary",)),
        name="moe",
    )(blk_e, blk_src, n_used, xpad, wgu, bgu, wdn, bdn)


def _final_kernel(h_ref, yg_ref, gate_ref, p_ref, wple_ref, wpg_ref, bpg_ref,
                  g2_ref, b2_ref, g3_ref, b3_ref, o_ref, *, tm, alpha):
    h = _load_token_rows(h_ref, tm)
    gates = gate_ref[...]
    ffn = gates[:, 0:1] * _load_token_rows(yg_ref.at[0], tm)
    for k in range(1, TOP_K):
        ffn = ffn + gates[:, k:k + 1] * _load_token_rows(yg_ref.at[k], tm)
    h2 = _layer_norm(alpha * h + ffn, g2_ref[...], b2_ref[...])
    emb = _dot(p_ref[...].astype(BF16), wple_ref[...])
    pg = jax.nn.sigmoid(_dot(h2.astype(BF16), wpg_ref[...]) + bpg_ref[...])
    o_ref[...] = _layer_norm(alpha * h2 + emb * pg, g3_ref[...], b3_ref[...])


def _final(h1, yg, gates, p2, wple, wpg, bpg, g2, b2, g3, b3, tm, alpha):
    T = h1.shape[0] // ROW_TILE
    const = lambda *shape: pl.BlockSpec(shape, lambda i: (0,) * len(shape))
    return pl.pallas_call(
        functools.partial(_final_kernel, tm=tm, alpha=alpha),
        grid=(T // tm,),
        in_specs=[
            pl.BlockSpec((tm * ROW_TILE, LANES), lambda i: (i, 0)),
            pl.BlockSpec((TOP_K, tm * ROW_TILE, LANES), lambda i: (0, i, 0)),
            pl.BlockSpec((tm, LANES), lambda i: (i, 0)),
            pl.BlockSpec((tm, PLE_DIM), lambda i: (i, 0)),
            const(PLE_DIM, D_MODEL), const(D_MODEL, D_MODEL), const(1, D_MODEL),
            const(1, D_MODEL), const(1, D_MODEL), const(1, D_MODEL), const(1, D_MODEL),
        ],
        out_specs=pl.BlockSpec((tm, D_MODEL), lambda i: (i, 0)),
        out_shape=jax.ShapeDtypeStruct((T, D_MODEL), F32),
        compiler_params=_cparams(("arbitrary",)),
        name="final",
    )(h1, yg, gates, p2, wple, wpg, bpg, g2, b2, g3, b3)


def _layer(h2d, p2d, batch, seq, alpha, w_in, b_in, gmlp_ln_g, gmlp_ln_b, w_spatial, b_spatial,
           w_branch_a, w_branch_b, w_out, b_out, ln1_g, ln1_b, w_router, b_router,
           w_gate_up, b_gate_up, w_down, b_down, ln2_g, ln2_b, w_ple, w_ple_gate, b_ple_gate,
           ln3_g, ln3_b):
    T = batch * seq
    tm = math.gcd(T, ROW_TILE_A)
    bm = MOE_BLOCK
    off_f = 3 * FOX_WIDTH
    off_u = off_f + N_HEADS
    row = lambda v: v.reshape(1, -1).astype(F32)

    w1 = jnp.concatenate([w_in[:, :off_u], jnp.zeros((D_MODEL, LANES - N_HEADS), F32)], axis=1).astype(BF16)
    b1 = jnp.concatenate([b_in[:off_u], jnp.zeros((LANES - N_HEADS,), F32)]).reshape(1, -1)
    w2 = w_in[:, off_u:].astype(BF16)
    b2 = row(b_in[off_u:])
    bs_tile = jnp.repeat(b_spatial.T, HEAD_DIM, axis=1)
    wr = jnp.concatenate([w_router, jnp.zeros((D_MODEL, LANES - N_EXPERTS), F32)], axis=1)
    br = jnp.concatenate([b_router, jnp.full((LANES - N_EXPERTS,), -1e30, F32)]).reshape(1, -1)

    q, k, v, f_pad = _qkvf(h2d, w1, b1, tm)
    auxq, auxk = _decay(f_pad, batch, seq)
    attn = _attention(q, auxq, k, auxk, v, batch, seq)
    h1, idx_o, gate_o, rank_o, cnt_o = _mix(
        h2d, attn, w2, b2, row(gmlp_ln_g), row(gmlp_ln_b), w_spatial.astype(BF16), bs_tile,
        w_branch_a.astype(BF16), w_branch_b.astype(BF16), w_out.astype(BF16), row(b_out),
        row(ln1_g), row(ln1_b), wr, br, tm, alpha)

    counts = cnt_o[0, :N_EXPERTS]
    padded = (counts + bm - 1) // bm * bm
    pad_end = jnp.cumsum(padded)
    pad_start = pad_end - padded
    dest = (jnp.take(pad_start, idx_o[:, :TOP_K]) + rank_o[:, :TOP_K]).reshape(-1)
    n_assign = T * TOP_K
    n_blk = -(-n_assign // bm) + N_EXPERTS
    n_rows = n_blk * bm
    n_used = pad_end[-1] // bm
    blk_src = jnp.minimum(jnp.arange(n_blk, dtype=I32), n_used - 1).astype(I32)
    blk_e = jnp.minimum(jnp.sum(pad_end[None, :] <= (blk_src * bm)[:, None], axis=1), N_EXPERTS - 1).astype(I32)
    dest = dest.astype(I32)

    h1_rows = h1.reshape(T, ROW_TILE, LANES)
    xpad = _scatter_rows((pad_start + counts).astype(I32), (padded - counts).astype(I32), dest, h1_rows, n_rows)
    ypad = _moe(blk_e, blk_src, n_used.reshape(1).astype(I32), xpad.reshape(n_rows * ROW_TILE, LANES),
                w_gate_up, b_gate_up.reshape(N_EXPERTS, 1, -1), w_down, b_down.reshape(N_EXPERTS, 1, -1),
                n_blk, bm)
    yg = _sc_gather_rows(dest.reshape(T, TOP_K).T.reshape(-1), ypad.reshape(n_rows, ROW_TILE, LANES))
    return _final(h1, yg.reshape(TOP_K, T * ROW_TILE, LANES), gate_o, p2d, w_ple.astype(BF16),
                  w_ple_gate.astype(BF16), row(b_ple_gate), row(ln2_g), row(ln2_b),
                  row(ln3_g), row(ln3_b), tm, alpha)


def kernel(x, p, w_in, b_in, gmlp_ln_g, gmlp_ln_b, w_spatial, b_spatial, w_branch_a, w_branch_b, w_out, b_out, ln1_g, ln1_b, w_router, b_router, w_gate_up, b_gate_up, w_down, b_down, ln2_g, ln2_b, w_ple, w_ple_gate, b_ple_gate, ln3_g, ln3_b):
    batch, seq, d = x.shape
    depth = w_in.shape[0]
    assert d == D_MODEL and seq % GMLP_CHUNK == 0
    alpha = (2.0 * depth) ** 0.25
    h = x.reshape(batch * seq, d)
    for i in range(depth):
        h = _layer(h, p[i].reshape(batch * seq, PLE_DIM), batch, seq, alpha,
                   w_in[i], b_in[i], gmlp_ln_g[i], gmlp_ln_b[i], w_spatial[i], b_spatial[i],
                   w_branch_a[i], w_branch_b[i], w_out[i], b_out[i], ln1_g[i], ln1_b[i],
                   w_router[i], b_router[i], w_gate_up[i], b_gate_up[i], w_down[i], b_down[i],
                   ln2_g[i], ln2_b[i], w_ple[i], w_ple_gate[i], b_ple_gate[i], ln3_g[i], ln3_b[i])
    return h.reshape(batch, seq, d)
```

```python
import functools
import math

import jax
import jax.numpy as jnp
import numpy as np
from jax import lax
from jax.experimental import pallas as pl
from jax.experimental.pallas import tpu as pltpu
from jax.experimental.pallas import tpu_sc as plsc

F32 = jnp.float32
BF16 = jnp.bfloat16
I32 = jnp.int32

D_MODEL = 1024
N_HEADS = 8
HEAD_DIM = 64
FOX_WIDTH = N_HEADS * HEAD_DIM
GMLP_WIDTH = 512
GMLP_CHUNK = 128
N_EXPERTS = 32
TOP_K = 4
D_FF = 1024
PLE_DIM = 256
SWIGLU_LIMIT = 7.0
SWIGLU_ALPHA = 1.702
LN_EPS = 1e-5
LANES = 128
ROW_TILE = 8
VMEM_LIMIT = 56 * 1024 * 1024

MOE_BLOCK = 256
ATTN_Q_TILE = 512
ATTN_K_TILE = 512
ATTN_PAIRS = 2
LOG2E = math.log2(math.e)
ROW_TILE_A = 512
SC_CORES = 2
SC_SUBCORES = 16
SC_WINDOW = 32
N_PARTS = 2


def _cparams(sem):
    return pltpu.CompilerParams(dimension_semantics=sem, vmem_limit_bytes=VMEM_LIMIT)


def _gelu(x):
    c = math.sqrt(2.0 / math.pi)
    return 0.5 * x * (1.0 + jnp.tanh(c * (x + 0.044715 * (x * x * x))))


def _layer_norm(x, g, b):
    mu = jnp.mean(x, axis=-1, keepdims=True)
    xc = x - mu
    var = jnp.mean(xc * xc, axis=-1, keepdims=True)
    return xc * lax.rsqrt(var + LN_EPS) * g + b


def _split3(x):
    hi = x.astype(BF16)
    r = x - hi.astype(F32)
    mid = r.astype(BF16)
    lo = (r - mid.astype(F32)).astype(BF16)
    return hi, mid, lo


def _dot(a, b):
    return jnp.dot(a, b, preferred_element_type=F32)


def _load_token_rows(ref, n):
    return jnp.concatenate([ref[pl.ds(j, n, stride=ROW_TILE), :] for j in range(ROW_TILE)], axis=1)


def _store_token_rows(ref, val):
    n = val.shape[0]
    for j in range(ROW_TILE):
        ref[pl.ds(j, n, stride=ROW_TILE), :] = val[:, j * LANES:(j + 1) * LANES]


def _qkvf_kernel(x_ref, w_ref, b_ref, q_ref, k_ref, v_ref, f_ref):
    x = x_ref[...].astype(BF16)
    proj = _dot(x, w_ref[...]) + b_ref[...]
    q_ref[...] = (proj[:, :FOX_WIDTH] * (HEAD_DIM ** -0.5 * LOG2E)).astype(BF16)
    k_ref[...] = proj[:, FOX_WIDTH:2 * FOX_WIDTH].astype(BF16)
    v_ref[...] = proj[:, 2 * FOX_WIDTH:3 * FOX_WIDTH].astype(BF16)
    f_ref[...] = proj[:, 3 * FOX_WIDTH:]


def _qkvf(x2, w, b, tm, T, tile_off):
    n_out = w.shape[1]
    return pl.pallas_call(
        _qkvf_kernel,
        grid=(T // tm,),
        in_specs=[
            pl.BlockSpec((tm, D_MODEL), lambda i: (i + tile_off, 0)),
            pl.BlockSpec((D_MODEL, n_out), lambda i: (0, 0)),
            pl.BlockSpec((1, n_out), lambda i: (0, 0)),
        ],
        out_specs=[
            pl.BlockSpec((tm, FOX_WIDTH), lambda i: (i, 0)),
            pl.BlockSpec((tm, FOX_WIDTH), lambda i: (i, 0)),
            pl.BlockSpec((tm, FOX_WIDTH), lambda i: (i, 0)),
            pl.BlockSpec((tm, LANES), lambda i: (i, 0)),
        ],
        out_shape=[
            jax.ShapeDtypeStruct((T, FOX_WIDTH), BF16),
            jax.ShapeDtypeStruct((T, FOX_WIDTH), BF16),
            jax.ShapeDtypeStruct((T, FOX_WIDTH), BF16),
            jax.ShapeDtypeStruct((T, LANES), F32),
        ],
        compiler_params=_cparams(("arbitrary",)),
        name="qkvf",
    )(x2, w, b)


def _decay_placement():
    pq = np.zeros((3 * LANES, FOX_WIDTH), np.float32)
    pk = np.zeros((3 * LANES, FOX_WIDTH), np.float32)
    cq = np.zeros((1, FOX_WIDTH), np.float32)
    ck = np.zeros((1, FOX_WIDTH), np.float32)
    for h in range(N_HEADS):
        base = (h // 2) * LANES + (HEAD_DIM if h % 2 == 0 else 0)
        for piece in range(3):
            pq[piece * LANES + h, base + piece] = 1.0
            pk[piece * LANES + h, base + 3 + piece] = -1.0
            cq[0, base + 3 + piece] = 1.0
            ck[0, base + piece] = 1.0
    return pq, pk, cq, ck


def _decay_kernel(f_ref, pq_ref, pk_ref, cq_ref, ck_ref, auxq_ref, auxk_ref, *, seq, blk):
    r = lax.broadcasted_iota(I32, (blk, blk), 0)
    c = lax.broadcasted_iota(I32, (blk, blk), 1)
    tri = jnp.where(c <= r, 1.0, 0.0).astype(BF16)
    carry = jnp.zeros((1, LANES), F32)
    for i in range(seq // blk):
        f = f_ref[i * blk:(i + 1) * blk, :]
        ls = jnp.minimum(f, 0.0) - jnp.log1p(jnp.exp(-jnp.abs(f)))
        hi, mid, lo = _split3(ls)
        cs = _dot(tri, hi) + _dot(tri, mid) + _dot(tri, lo) + carry
        carry = cs[blk - 1:blk, :]
        pieces = jnp.concatenate(_split3(cs * LOG2E), axis=1)
        auxq_ref[i * blk:(i + 1) * blk, :] = (_dot(pieces, pq_ref[...]) + cq_ref[...]).astype(BF16)
        auxk_ref[i * blk:(i + 1) * blk, :] = (_dot(pieces, pk_ref[...]) + ck_ref[...]).astype(BF16)


def _decay(f_pad, batch, seq):
    blk = 256 if seq % 256 == 0 else LANES
    pq, pk, cq, ck = _decay_placement()
    const = lambda *shape: pl.BlockSpec(shape, lambda b: (0,) * len(shape))
    return pl.pallas_call(
        functools.partial(_decay_kernel, seq=seq, blk=blk),
        grid=(batch,),
        in_specs=[
            pl.BlockSpec((seq, LANES), lambda b: (b, 0)),
            const(3 * LANES, FOX_WIDTH), const(3 * LANES, FOX_WIDTH), const(1, FOX_WIDTH), const(1, FOX_WIDTH),
        ],
        out_specs=[
            pl.BlockSpec((seq, FOX_WIDTH), lambda b: (b, 0)),
            pl.BlockSpec((seq, FOX_WIDTH), lambda b: (b, 0)),
        ],
        out_shape=[
            jax.ShapeDtypeStruct((batch * seq, FOX_WIDTH), BF16),
            jax.ShapeDtypeStruct((batch * seq, FOX_WIDTH), BF16),
        ],
        compiler_params=_cparams(("arbitrary",)),
        name="decay",
    )(f_pad, jnp.asarray(pq, BF16), jnp.asarray(pk, BF16), jnp.asarray(cq), jnp.asarray(ck))


def _attn_kernel(q_ref, auxq_ref, k_ref, auxk_ref, v_ref, o_ref, *, tq, tk):
    qi = pl.program_id(2)
    n_pairs = q_ref.shape[1] // LANES
    low_q = lax.broadcasted_iota(I32, (tq, LANES), 1) < HEAD_DIM
    qs = []
    for pr in range(n_pairs):
        q = q_ref[:, pr * LANES:(pr + 1) * LANES]
        aq = auxq_ref[:, pr * LANES:(pr + 1) * LANES]
        qs += [jnp.where(low_q, q, aq), jnp.where(low_q, aq, q)]

    def step(start, n, carry, masked):
        low_k = lax.broadcasted_iota(I32, (n, LANES), 1) < HEAD_DIM
        ones = jnp.ones((n, LANES), BF16)
        out = []
        for pr in range(n_pairs):
            lanes = slice(pr * LANES, (pr + 1) * LANES)
            kb = k_ref[pl.ds(start, n), lanes]
            ak = auxk_ref[pl.ds(start, n), lanes]
            vb = v_ref[pl.ds(start, n), lanes]
            ks = (jnp.where(low_k, kb, ak), jnp.where(low_k, ak, kb))
            vs = (jnp.where(low_k, vb, ones), jnp.where(low_k, ones, vb))
            for j in range(2):
                m, acc = carry[2 * pr + j]
                s = lax.dot_general(qs[2 * pr + j], ks[j], (((1,), (1,)), ((), ())),
                                    preferred_element_type=F32)
                if masked:
                    row = lax.broadcasted_iota(I32, (tq, n), 0)
                    col = lax.broadcasted_iota(I32, (tq, n), 1)
                    s = jnp.where(col + (start - qi * tq) <= row, s, -jnp.inf)
                m_new = jnp.maximum(m, jnp.max(s, axis=1, keepdims=True))
                p = jnp.exp2(s - m_new)
                acc = jnp.exp2(m - m_new) * acc + _dot(p.astype(BF16), vs[j])
                out.append((m_new, acc))
        return tuple(out)

    init = tuple((jnp.full((tq, 1), -jnp.inf, F32), jnp.zeros((tq, LANES), F32))
                 for _ in range(2 * n_pairs))
    carry = lax.fori_loop(0, qi, lambda t, c: step(pl.multiple_of(t * tq, tq), tq, c, False), init)
    for d in range(tq // tk):
        carry = step(pl.multiple_of(qi * tq + d * tk, tk), tk, carry, True)
    for pr in range(n_pairs):
        acc0, acc1 = carry[2 * pr][1], carry[2 * pr + 1][1]
        out0 = acc0 / acc0[:, HEAD_DIM:HEAD_DIM + 1]
        out1 = acc1 / acc1[:, 0:1]
        o_ref[:, pr * LANES:(pr + 1) * LANES] = jnp.where(low_q, out0, out1).astype(BF16)


def _attention(q, auxq, k, auxk, v, batch, seq):
    tq = math.gcd(seq, ATTN_Q_TILE)
    tk = math.gcd(tq, ATTN_K_TILE)
    nq = seq // tq
    T = batch * seq
    width = ATTN_PAIRS * LANES
    q_spec = pl.BlockSpec((tq, width), lambda b, hp, qi: (b * nq + qi, hp))
    kv_spec = pl.BlockSpec((seq, width), lambda b, hp, qi: (b, hp))
    return pl.pallas_call(
        functools.partial(_attn_kernel, tq=tq, tk=tk),
        grid=(batch, N_HEADS // (2 * ATTN_PAIRS), nq),
        in_specs=[q_spec, q_spec, kv_spec, kv_spec, kv_spec],
        out_specs=q_spec,
        out_shape=jax.ShapeDtypeStruct((T, FOX_WIDTH), BF16),
        compiler_params=_cparams(("arbitrary", "arbitrary", "arbitrary")),
        name="attn",
    )(q, auxq, k, auxk, v)


def _mix_kernel(x_ref, attn_ref, w2_ref, b2_ref, lng_ref, lnb_ref, ws_ref, bs_ref,
                wa_ref, wb_ref, wo_ref, bo_ref, g1_ref, b1_ref, wr_ref, br_ref,
                h_ref, idx_ref, gate_ref, rank_ref, cnt_ref, carry_ref, *, tm, alpha):
    i = pl.program_id(0)

    @pl.when(i == 0)
    def _():
        carry_ref[...] = jnp.zeros_like(carry_ref)

    x = x_ref[...]
    proj = _dot(x.astype(BF16), w2_ref[...]) + b2_ref[...]
    u = _gelu(proj[:, :GMLP_WIDTH])
    gv = _gelu(proj[:, GMLP_WIDTH:2 * GMLP_WIDTH])
    vln = _layer_norm(gv, lng_ref[...], lnb_ref[...]).astype(BF16)

    cr = lax.broadcasted_iota(I32, (GMLP_CHUNK, GMLP_CHUNK), 0)
    cc = lax.broadcasted_iota(I32, (GMLP_CHUNK, GMLP_CHUNK), 1)
    tril = cc <= cr
    lo_half = cc < HEAD_DIM
    zero_w = jnp.zeros((GMLP_CHUNK, GMLP_CHUNK), BF16)
    n_slab = GMLP_WIDTH // LANES
    lhs = []
    for s in range(n_slab):
        w0 = jnp.where(tril, ws_ref[2 * s], zero_w)
        w1 = jnp.where(tril, ws_ref[2 * s + 1], zero_w)
        lhs.append(jnp.concatenate([w0, w1], axis=1))
    bs = bs_ref[...]
    rows = []
    for c in range(tm // GMLP_CHUNK):
        cols = []
        for s in range(n_slab):
            vs = vln[c * GMLP_CHUNK:(c + 1) * GMLP_CHUNK, s * LANES:(s + 1) * LANES]
            rhs = jnp.concatenate([jnp.where(lo_half, vs, zero_w), jnp.where(lo_half, zero_w, vs)], axis=0)
            cols.append(_dot(lhs[s], rhs))
        rows.append(jnp.concatenate(cols, axis=1) + bs)
    sp = jnp.concatenate(rows, axis=0) if len(rows) > 1 else rows[0]
    sgu = (u * sp).astype(BF16)

    ga = jax.nn.sigmoid(proj[:, 2 * GMLP_WIDTH:2 * GMLP_WIDTH + D_MODEL])
    gb = jax.nn.sigmoid(proj[:, 2 * GMLP_WIDTH + D_MODEL:])
    merged = ga * _dot(attn_ref[...], wa_ref[...]) + gb * _dot(sgu, wb_ref[...])
    mix = _dot(merged.astype(BF16), wo_ref[...]) + bo_ref[...]
    h = _layer_norm(alpha * x + mix, g1_ref[...], b1_ref[...])
    _store_token_rows(h_ref, h)

    a_hi = h.astype(BF16)
    a_lo = (h - a_hi.astype(F32)).astype(BF16)
    wr = wr_ref[...]
    w_hi = wr.astype(BF16)
    w_lo = (wr - w_hi.astype(F32)).astype(BF16)
    logits = _dot(a_hi, w_hi) + _dot(a_lo, w_hi) + _dot(a_hi, w_lo) + br_ref[...]

    lane_i = lax.broadcasted_iota(I32, (tm, LANES), 1)
    lane_f = lane_i.astype(F32)
    vals, idxs = [], []
    l = logits
    for _ in range(TOP_K):
        m = jnp.max(l, axis=1, keepdims=True)
        ix = jnp.min(jnp.where(l == m, lane_f, float(LANES)), axis=1, keepdims=True)
        vals.append(m)
        idxs.append(ix)
        l = jnp.where(lane_f == ix, -jnp.inf, l)
    es = [jnp.exp(v - vals[0]) for v in vals]
    den = es[0] + es[1] + es[2] + es[3]

    onehot = jnp.zeros((tm, LANES), F32)
    idx_out = jnp.zeros((tm, LANES), F32)
    gate_out = jnp.zeros((tm, LANES), F32)
    for k in range(TOP_K):
        onehot = onehot + jnp.where(lane_f == idxs[k], 1.0, 0.0)
        idx_out = jnp.where(lane_i == k, idxs[k], idx_out)
        gate_out = jnp.where(lane_i == k, es[k] / den, gate_out)

    tr = lax.broadcasted_iota(I32, (tm, tm), 0)
    tc = lax.broadcasted_iota(I32, (tm, tm), 1)
    strict = jnp.where(tc < tr, 1.0, 0.0).astype(BF16)
    carry = carry_ref[0:1, :]
    before = _dot(strict, onehot.astype(BF16)) + carry
    rank_out = jnp.zeros((tm, LANES), F32)
    for k in range(TOP_K):
        rk = jnp.sum(jnp.where(lane_f == idxs[k], before, 0.0), axis=1, keepdims=True)
        rank_out = jnp.where(lane_i == k, rk, rank_out)
    new_carry = carry + jnp.sum(onehot, axis=0, keepdims=True)
    carry_ref[...] = jnp.broadcast_to(new_carry, carry_ref.shape)
    cnt_ref[...] = jnp.broadcast_to(new_carry, cnt_ref.shape).astype(I32)
    idx_ref[...] = idx_out.astype(I32)
    gate_ref[...] = gate_out
    rank_ref[...] = rank_out.astype(I32)


def _mix(x2, attn, w2, b2, lng, lnb, ws, bs_tile, wa, wb, wo, bo, g1, b1, wr, br, tm, alpha, tile_off):
    T = attn.shape[0]
    n2 = w2.shape[1]
    const = lambda *shape: pl.BlockSpec(shape, lambda i: (0,) * len(shape))
    return pl.pallas_call(
        functools.partial(_mix_kernel, tm=tm, alpha=alpha),
        grid=(T // tm,),
        in_specs=[
            pl.BlockSpec((tm, D_MODEL), lambda i: (i + tile_off, 0)),
            pl.BlockSpec((tm, FOX_WIDTH), lambda i: (i, 0)),
            const(D_MODEL, n2), const(1, n2),
            const(1, GMLP_WIDTH), const(1, GMLP_WIDTH),
            const(GMLP_WIDTH // HEAD_DIM, GMLP_CHUNK, GMLP_CHUNK), const(GMLP_CHUNK, GMLP_WIDTH),
            const(FOX_WIDTH, D_MODEL), const(GMLP_WIDTH, D_MODEL),
            const(D_MODEL, D_MODEL), const(1, D_MODEL),
            const(1, D_MODEL), const(1, D_MODEL),
            const(D_MODEL, LANES), const(1, LANES),
        ],
        out_specs=[
            pl.BlockSpec((tm * ROW_TILE, LANES), lambda i: (i, 0)),
            pl.BlockSpec((tm, LANES), lambda i: (i, 0)),
            pl.BlockSpec((tm, LANES), lambda i: (i, 0)),
            pl.BlockSpec((tm, LANES), lambda i: (i, 0)),
            pl.BlockSpec((ROW_TILE, LANES), lambda i: (0, 0)),
        ],
        out_shape=[
            jax.ShapeDtypeStruct((T * ROW_TILE, LANES), F32),
            jax.ShapeDtypeStruct((T, LANES), I32),
            jax.ShapeDtypeStruct((T, LANES), F32),
            jax.ShapeDtypeStruct((T, LANES), I32),
            jax.ShapeDtypeStruct((ROW_TILE, LANES), I32),
        ],
        scratch_shapes=[pltpu.VMEM((ROW_TILE, LANES), F32)],
        compiler_params=_cparams(("arbitrary",)),
        name="mix",
    )(x2, attn, w2, b2, lng, lnb, ws, bs_tile, wa, wb, wo, bo, g1, b1, wr, br)


def _sc_window_indices(idx):
    return jnp.pad(idx.reshape(-1, SC_WINDOW), ((0, 0), (0, LANES - SC_WINDOW)))


def _sc_scatter_rows(dest_km, src3, n_dst):
    n_slot, n_tok = dest_km.shape
    n_win = n_tok // SC_WINDOW
    mesh = plsc.VectorSubcoreMesh(core_axis_name="core", subcore_axis_name="subcore",
                                  num_cores=SC_CORES, num_subcores=SC_SUBCORES)

    @pl.kernel(out_type=jax.ShapeDtypeStruct((n_dst, ROW_TILE, LANES), src3.dtype), mesh=mesh, name="sc_scatter")
    def scatter(src_hbm, i_hbm, o_hbm):
        def body(x_vmem, i_vmem):
            pltpu.sync_copy(x_vmem, o_hbm.at[i_vmem.at[0, pl.ds(0, SC_WINDOW)]])

        @pl.when(lax.axis_index("core") == 0)
        def _():
            pltpu.emit_pipeline(
                body,
                grid=(n_slot * n_win,),
                in_specs=[pl.BlockSpec((SC_WINDOW, ROW_TILE, LANES), lambda i: (i % n_win, 0, 0)),
                          pl.BlockSpec((1, LANES), lambda i: (i, 0))],
                out_specs=[],
                core_axis_name="subcore",
                dimension_semantics=(pltpu.PARALLEL,),
            )(src_hbm, i_hbm)

    return scatter(src3, _sc_window_indices(dest_km))


def _sc_gather_rows(sidx, src3):
    n = sidx.shape[0]
    mesh = plsc.VectorSubcoreMesh(core_axis_name="core", subcore_axis_name="subcore",
                                  num_cores=SC_CORES, num_subcores=SC_SUBCORES)

    @pl.kernel(out_type=jax.ShapeDtypeStruct((n, ROW_TILE, LANES), src3.dtype), mesh=mesh, name="sc_gather")
    def gather(src_hbm, i_hbm, o_hbm):
        def body(i_vmem, o_vmem):
            pltpu.sync_copy(src_hbm.at[i_vmem.at[0, pl.ds(0, SC_WINDOW)]], o_vmem)

        pltpu.emit_pipeline(
            body,
            grid=(n // SC_WINDOW,),
            in_specs=[pl.BlockSpec((1, LANES), lambda i: (i, 0))],
            out_specs=[pl.BlockSpec((SC_WINDOW, ROW_TILE, LANES), lambda i: (i, 0, 0))],
            core_axis_name=("core", "subcore"),
            dimension_semantics=(pltpu.PARALLEL,),
        )(i_hbm, o_hbm)

    return gather(src3, _sc_window_indices(sidx))


def _moe_kernel(be_ref, bsrc_ref, bval_ref, bslot_ref, bnext_ref, nu_ref,
                x_ref, wgu_hbm, bgu_ref, wdn_hbm, bdn_ref, y_ref,
                wgu_f32, wdn_f32, wgu_bf, wdn_bf, sems):
    i = pl.program_id(0)
    active = i < nu_ref[0]
    prev = jnp.maximum(i - 1, 0)
    fresh = (i == 0) | (be_ref[i] != be_ref[prev])
    slot = bslot_ref[i]

    def weight_copies(e, s):
        return (pltpu.make_async_copy(wgu_hbm.at[e], wgu_f32.at[s], sems.at[s, 0]),
                pltpu.make_async_copy(wdn_hbm.at[e], wdn_f32.at[s], sems.at[s, 1]))

    @pl.when(i == 0)
    def _():
        for c in weight_copies(be_ref[0], slot):
            c.start()

    @pl.when(active & fresh)
    def _():
        @pl.when(bnext_ref[i] != be_ref[i])
        def _():
            for c in weight_copies(bnext_ref[i], 1 - slot):
                c.start()

        for c in weight_copies(be_ref[i], slot):
            c.wait()
        wgu_bf[...] = wgu_f32[slot].astype(BF16)
        wdn_bf[...] = wdn_f32[slot].astype(BF16)

    @pl.when(active)
    def _():
        bm = x_ref.shape[0] // ROW_TILE
        x = _load_token_rows(x_ref, bm)
        valid = lax.broadcasted_iota(I32, (bm, 1), 0) < bval_ref[i]
        x = jnp.where(valid, x, 0.0).astype(BF16)
        gu = _dot(x, wgu_bf[...]) + bgu_ref[...]
        gate = jnp.minimum(gu[:, :D_FF], SWIGLU_LIMIT)
        up = jnp.clip(gu[:, D_FF:], -SWIGLU_LIMIT, SWIGLU_LIMIT)
        hid = (up + 1.0) * (gate * jax.nn.sigmoid(SWIGLU_ALPHA * gate))
        _store_token_rows(y_ref, _dot(hid.astype(BF16), wdn_bf[...]) + bdn_ref[...])

    @pl.when(jnp.logical_not(active))
    def _():
        y_ref[...] = jnp.zeros_like(y_ref)


def _moe(blk_e, blk_src, blk_valid, blk_slot, blk_next, n_used, xpad, wgu, bgu, wdn, bdn, n_blk, bm):
    rows = lambda i, be, bs, bv, sl, nx, nu: (bs[i], 0)
    bias = lambda i, be, bs, bv, sl, nx, nu: (be[i], 0, 0)
    grid_spec = pltpu.PrefetchScalarGridSpec(
        num_scalar_prefetch=6,
        grid=(n_blk,),
        in_specs=[
            pl.BlockSpec((bm * ROW_TILE, LANES), rows),
            pl.BlockSpec(memory_space=pl.ANY),
            pl.BlockSpec((None, 1, 2 * D_FF), bias),
            pl.BlockSpec(memory_space=pl.ANY),
            pl.BlockSpec((None, 1, D_MODEL), bias),
        ],
        out_specs=pl.BlockSpec((bm * ROW_TILE, LANES), lambda i, be, bs, bv, sl, nx, nu: (i, 0)),
        scratch_shapes=[
            pltpu.VMEM((2, D_MODEL, 2 * D_FF), F32), pltpu.VMEM((2, D_FF, D_MODEL), F32),
            pltpu.VMEM((D_MODEL, 2 * D_FF), BF16), pltpu.VMEM((D_FF, D_MODEL), BF16),
            pltpu.SemaphoreType.DMA((2, 2)),
        ],
    )
    return pl.pallas_call(
        _moe_kernel,
        grid_spec=grid_spec,
        out_shape=jax.ShapeDtypeStruct((n_blk * bm * ROW_TILE, LANES), F32),
        compiler_params=_cparams(("arbitrary",)),
        name="moe",
    )(blk_e, blk_src, blk_valid, blk_slot, blk_next, n_used, xpad, wgu, bgu, wdn, bdn)


def _final_kernel(h_ref, yg_ref, gate_ref, p_ref, wple_ref, wpg_ref, bpg_ref,
                  g2_ref, b2_ref, g3_ref, b3_ref, *rest, tm, alpha):
    o_ref = rest[-1]
    h = _load_token_rows(h_ref, tm)
    gates = gate_ref[...]
    ffn = gates[:, 0:1] * _load_token_rows(yg_ref.at[0], tm)
    for k in range(1, TOP_K):
        ffn = ffn + gates[:, k:k + 1] * _load_token_rows(yg_ref.at[k], tm)
    h2 = _layer_norm(alpha * h + ffn, g2_ref[...], b2_ref[...])
    emb = _dot(p_ref[...].astype(BF16), wple_ref[...])
    pg = jax.nn.sigmoid(_dot(h2.astype(BF16), wpg_ref[...]) + bpg_ref[...])
    o_ref[...] = _layer_norm(alpha * h2 + emb * pg, g3_ref[...], b3_ref[...])


def _final(h1, yg, gates, p2, wple, wpg, bpg, g2, b2, g3, b3, tm, alpha, tile_off, out_prev):
    T = h1.shape[0] // ROW_TILE
    const = lambda *shape: pl.BlockSpec(shape, lambda i: (0,) * len(shape))
    in_specs = [
        pl.BlockSpec((tm * ROW_TILE, LANES), lambda i: (i, 0)),
        pl.BlockSpec((TOP_K, tm * ROW_TILE, LANES), lambda i: (0, i, 0)),
        pl.BlockSpec((tm, LANES), lambda i: (i, 0)),
        pl.BlockSpec((tm, PLE_DIM), lambda i: (i + tile_off, 0)),
        const(PLE_DIM, D_MODEL), const(D_MODEL, D_MODEL), const(1, D_MODEL),
        const(1, D_MODEL), const(1, D_MODEL), const(1, D_MODEL), const(1, D_MODEL),
    ]
    args = [h1, yg, gates, p2, wple, wpg, bpg, g2, b2, g3, b3]
    aliases = {}
    if out_prev is not None:
        in_specs.append(pl.BlockSpec(memory_space=pl.ANY))
        aliases = {len(args): 0}
        args.append(out_prev)
    return pl.pallas_call(
        functools.partial(_final_kernel, tm=tm, alpha=alpha),
        grid=(T // tm,),
        in_specs=in_specs,
        out_specs=pl.BlockSpec((tm, D_MODEL), lambda i: (i + tile_off, 0)),
        out_shape=jax.ShapeDtypeStruct((p2.shape[0], D_MODEL), F32),
        input_output_aliases=aliases,
        compiler_params=_cparams(("arbitrary",)),
        name="final",
    )(*args)


def _layer(h2d, p2d, batch, seq, alpha, w_in, b_in, gmlp_ln_g, gmlp_ln_b, w_spatial, b_spatial,
           w_branch_a, w_branch_b, w_out, b_out, ln1_g, ln1_b, w_router, b_router,
           w_gate_up, b_gate_up, w_down, b_down, ln2_g, ln2_b, w_ple, w_ple_gate, b_ple_gate,
           ln3_g, ln3_b):
    T = batch * seq
    tm = math.gcd(T, ROW_TILE_A)
    bm = MOE_BLOCK
    off_f = 3 * FOX_WIDTH
    off_u = off_f + N_HEADS
    row = lambda v: v.reshape(1, -1).astype(F32)

    w1 = jnp.concatenate([w_in[:, :off_u], jnp.zeros((D_MODEL, LANES - N_HEADS), F32)], axis=1).astype(BF16)
    b1 = jnp.concatenate([b_in[:off_u], jnp.zeros((LANES - N_HEADS,), F32)]).reshape(1, -1)
    w2 = w_in[:, off_u:].astype(BF16)
    b2 = row(b_in[off_u:])
    bs_tile = jnp.repeat(b_spatial.T, HEAD_DIM, axis=1)
    wr = jnp.concatenate([w_router, jnp.zeros((D_MODEL, LANES - N_EXPERTS), F32)], axis=1)
    br = jnp.concatenate([b_router, jnp.full((LANES - N_EXPERTS,), -1e30, F32)]).reshape(1, -1)

    wsb, wab, wbb, wob = (w.astype(BF16) for w in (w_spatial, w_branch_a, w_branch_b, w_out))
    wpleb, wpgb = w_ple.astype(BF16), w_ple_gate.astype(BF16)
    bgu, bdn = b_gate_up.reshape(N_EXPERTS, 1, -1), b_down.reshape(N_EXPERTS, 1, -1)

    n_parts = N_PARTS if batch % N_PARTS == 0 else 1
    pb = batch // n_parts
    Tp = pb * seq
    tiles = Tp // tm
    n_assign = Tp * TOP_K
    n_blk = -(-n_assign // bm) + N_EXPERTS
    n_rows = n_blk * bm

    def front(part):
        q, k, v, f_pad = _qkvf(h2d, w1, b1, tm, Tp, part * tiles)
        auxq, auxk = _decay(f_pad, pb, seq)
        attn = _attention(q, auxq, k, auxk, v, pb, seq)
        h1, idx_o, gate_o, rank_o, cnt_o = _mix(
            h2d, attn, w2, b2, row(gmlp_ln_g), row(gmlp_ln_b), wsb, bs_tile, wab, wbb, wob, row(b_out),
            row(ln1_g), row(ln1_b), wr, br, tm, alpha, part * tiles)
        counts = cnt_o[0, :N_EXPERTS]
        padded = (counts + bm - 1) // bm * bm
        pad_end = jnp.cumsum(padded)
        pad_start = pad_end - padded
        dest = (jnp.take(pad_start, idx_o[:, :TOP_K]) + rank_o[:, :TOP_K]).astype(I32)
        n_used = pad_end[-1] // bm
        blk_src = jnp.minimum(jnp.arange(n_blk, dtype=I32), n_used - 1).astype(I32)
        blk_row = blk_src * bm
        blk_e = jnp.minimum(jnp.sum(pad_end[None, :] <= blk_row[:, None], axis=1), N_EXPERTS - 1).astype(I32)
        blk_valid = jnp.clip(jnp.take(pad_start + counts, blk_e) - blk_row, 0, bm).astype(I32)
        has_rows = counts > 0
        order = jnp.cumsum(has_rows.astype(I32)) - 1
        eid = jnp.arange(N_EXPERTS, dtype=I32)
        later = jnp.where(has_rows[None, :] & (eid[None, :] > eid[:, None]), eid[None, :], N_EXPERTS)
        succ = jnp.min(later, axis=1)
        succ = jnp.where(succ == N_EXPERTS, eid, succ)
        blk_slot = jnp.take(order % 2, blk_e).astype(I32)
        blk_next = jnp.take(succ, blk_e).astype(I32)
        dest_km = dest.T
        xpad = _sc_scatter_rows(dest_km, h1.reshape(Tp, ROW_TILE, LANES), n_rows)
        blocks = (blk_e, blk_src, blk_valid, blk_slot, blk_next, n_used.reshape(1).astype(I32))
        return h1, gate_o, dest_km, blocks, xpad

    def experts(state):
        h1, gate_o, dest_km, blocks, xpad = state
        ypad = _moe(*blocks, xpad.reshape(n_rows * ROW_TILE, LANES), w_gate_up, bgu, w_down, bdn, n_blk, bm)
        yg = _sc_gather_rows(dest_km.reshape(-1), ypad.reshape(n_rows, ROW_TILE, LANES))
        return h1, gate_o, yg

    def back(part, state, out_prev):
        h1, gate_o, yg = state
        return _final(h1, yg.reshape(TOP_K, Tp * ROW_TILE, LANES), gate_o, p2d, wpleb, wpgb,
                      row(b_ple_gate), row(ln2_g), row(ln2_b), row(ln3_g), row(ln3_b), tm, alpha,
                      part * tiles, out_prev)

    fronts = [front(part) for part in range(n_parts)]
    mids = [experts(state) for state in fronts]
    out = None
    for part in range(n_parts):
        out = back(part, mids[part], out)
    return out


def kernel(x, p, w_in, b_in, gmlp_ln_g, gmlp_ln_b, w_spatial, b_spatial, w_branch_a, w_branch_b, w_out, b_out, ln1_g, ln1_b, w_router, b_router, w_gate_up, b_gate_up, w_down, b_down, ln2_g, ln2_b, w_ple, w_ple_gate, b_ple_gate, ln3_g, ln3_b):
    batch, seq, d = x.shape
    depth = w_in.shape[0]
    assert d == D_MODEL and seq % GMLP_CHUNK == 0
    alpha = (2.0 * depth) ** 0.25
    h = x.reshape(batch * seq, d)
    for i in range(depth):
        h = _layer(h, p[i].reshape(batch * seq, PLE_DIM), batch, seq, alpha,
                   w_in[i], b_in[i], gmlp_ln_g[i], gmlp_ln_b[i], w_spatial[i], b_spatial[i],
                   w_branch_a[i], w_branch_b[i], w_out[i], b_out[i], ln1_g[i], ln1_b[i],
                   w_router[i], b_router[i], w_gate_up[i], b_gate_up[i], w_down[i], b_down[i],
                   ln2_g[i], ln2_b[i], w_ple[i], w_ple_gate[i], b_ple_gate[i], ln3_g[i], ln3_b[i])
    return h.reshape(batch, seq, d)
```

```python
import functools
import math

import jax
import jax.numpy as jnp
import numpy as np
from jax import lax
from jax.experimental import pallas as pl
from jax.experimental.pallas import tpu as pltpu
from jax.experimental.pallas import tpu_sc as plsc

F32 = jnp.float32
BF16 = jnp.bfloat16
I32 = jnp.int32

D_MODEL = 1024
N_HEADS = 8
HEAD_DIM = 64
FOX_WIDTH = N_HEADS * HEAD_DIM
GMLP_WIDTH = 512
GMLP_CHUNK = 128
N_EXPERTS = 32
TOP_K = 4
D_FF = 1024
PLE_DIM = 256
SWIGLU_LIMIT = 7.0
SWIGLU_ALPHA = 1.702
LN_EPS = 1e-5
LANES = 128
ROW_TILE = 8
VMEM_LIMIT = 56 * 1024 * 1024

MOE_BLOCK = 256
ATTN_Q_TILE = 512
ATTN_K_TILE = 512
ATTN_PAIRS = 2
LOG2E = math.log2(math.e)
ROW_TILE_A = 512
SC_CORES = 2
SC_SUBCORES = 16
SC_WINDOW = 32
N_PARTS = 2
ROUTE_TILE = 1024


def _cparams(sem):
    return pltpu.CompilerParams(dimension_semantics=sem, vmem_limit_bytes=VMEM_LIMIT)


def _gelu(x):
    c = math.sqrt(2.0 / math.pi)
    return 0.5 * x * (1.0 + jnp.tanh(c * (x + 0.044715 * (x * x * x))))


def _layer_norm(x, g, b):
    mu = jnp.mean(x, axis=-1, keepdims=True)
    xc = x - mu
    var = jnp.mean(xc * xc, axis=-1, keepdims=True)
    return xc * lax.rsqrt(var + LN_EPS) * g + b


def _split3(x):
    hi = x.astype(BF16)
    r = x - hi.astype(F32)
    mid = r.astype(BF16)
    lo = (r - mid.astype(F32)).astype(BF16)
    return hi, mid, lo


def _dot(a, b):
    return jnp.dot(a, b, preferred_element_type=F32)


def _load_token_rows(ref, n):
    return jnp.concatenate([ref[pl.ds(j, n, stride=ROW_TILE), :] for j in range(ROW_TILE)], axis=1)


def _store_token_rows(ref, val):
    n = val.shape[0]
    for j in range(ROW_TILE):
        ref[pl.ds(j, n, stride=ROW_TILE), :] = val[:, j * LANES:(j + 1) * LANES]


def _qkvf_kernel(x_ref, w_ref, b_ref, q_ref, k_ref, v_ref, f_ref):
    x = x_ref[...].astype(BF16)
    proj = _dot(x, w_ref[...]) + b_ref[...]
    q_ref[...] = (proj[:, :FOX_WIDTH] * (HEAD_DIM ** -0.5 * LOG2E)).astype(BF16)
    k_ref[...] = proj[:, FOX_WIDTH:2 * FOX_WIDTH].astype(BF16)
    v_ref[...] = proj[:, 2 * FOX_WIDTH:3 * FOX_WIDTH].astype(BF16)
    f_ref[...] = proj[:, 3 * FOX_WIDTH:]


def _qkvf(x2, w, b, tm, T, tile_off):
    n_out = w.shape[1]
    return pl.pallas_call(
        _qkvf_kernel,
        grid=(T // tm,),
        in_specs=[
            pl.BlockSpec((tm, D_MODEL), lambda i: (i + tile_off, 0)),
            pl.BlockSpec((D_MODEL, n_out), lambda i: (0, 0)),
            pl.BlockSpec((1, n_out), lambda i: (0, 0)),
        ],
        out_specs=[
            pl.BlockSpec((tm, FOX_WIDTH), lambda i: (i, 0)),
            pl.BlockSpec((tm, FOX_WIDTH), lambda i: (i, 0)),
            pl.BlockSpec((tm, FOX_WIDTH), lambda i: (i, 0)),
            pl.BlockSpec((tm, LANES), lambda i: (i, 0)),
        ],
        out_shape=[
            jax.ShapeDtypeStruct((T, FOX_WIDTH), BF16),
            jax.ShapeDtypeStruct((T, FOX_WIDTH), BF16),
            jax.ShapeDtypeStruct((T, FOX_WIDTH), BF16),
            jax.ShapeDtypeStruct((T, LANES), F32),
        ],
        compiler_params=_cparams(("arbitrary",)),
        name="qkvf",
    )(x2, w, b)


def _decay_placement():
    pq = np.zeros((3 * LANES, FOX_WIDTH), np.float32)
    pk = np.zeros((3 * LANES, FOX_WIDTH), np.float32)
    cq = np.zeros((1, FOX_WIDTH), np.float32)
    ck = np.zeros((1, FOX_WIDTH), np.float32)
    for h in range(N_HEADS):
        base = (h // 2) * LANES + (HEAD_DIM if h % 2 == 0 else 0)
        for piece in range(3):
            pq[piece * LANES + h, base + piece] = 1.0
            pk[piece * LANES + h, base + 3 + piece] = -1.0
            cq[0, base + 3 + piece] = 1.0
            ck[0, base + piece] = 1.0
    return pq, pk, cq, ck


def _decay_kernel(f_ref, pq_ref, pk_ref, cq_ref, ck_ref, auxq_ref, auxk_ref, *, seq, blk):
    r = lax.broadcasted_iota(I32, (blk, blk), 0)
    c = lax.broadcasted_iota(I32, (blk, blk), 1)
    tri = jnp.where(c <= r, 1.0, 0.0).astype(BF16)
    carry = jnp.zeros((1, LANES), F32)
    for i in range(seq // blk):
        f = f_ref[i * blk:(i + 1) * blk, :]
        ls = jnp.minimum(f, 0.0) - jnp.log1p(jnp.exp(-jnp.abs(f)))
        hi, mid, lo = _split3(ls)
        cs = _dot(tri, hi) + _dot(tri, mid) + _dot(tri, lo) + carry
        carry = cs[blk - 1:blk, :]
        pieces = jnp.concatenate(_split3(cs * LOG2E), axis=1)
        auxq_ref[i * blk:(i + 1) * blk, :] = (_dot(pieces, pq_ref[...]) + cq_ref[...]).astype(BF16)
        auxk_ref[i * blk:(i + 1) * blk, :] = (_dot(pieces, pk_ref[...]) + ck_ref[...]).astype(BF16)


def _decay(f_pad, batch, seq):
    blk = 256 if seq % 256 == 0 else LANES
    pq, pk, cq, ck = _decay_placement()
    const = lambda *shape: pl.BlockSpec(shape, lambda b: (0,) * len(shape))
    return pl.pallas_call(
        functools.partial(_decay_kernel, seq=seq, blk=blk),
        grid=(batch,),
        in_specs=[
            pl.BlockSpec((seq, LANES), lambda b: (b, 0)),
            const(3 * LANES, FOX_WIDTH), const(3 * LANES, FOX_WIDTH), const(1, FOX_WIDTH), const(1, FOX_WIDTH),
        ],
        out_specs=[
            pl.BlockSpec((seq, FOX_WIDTH), lambda b: (b, 0)),
            pl.BlockSpec((seq, FOX_WIDTH), lambda b: (b, 0)),
        ],
        out_shape=[
            jax.ShapeDtypeStruct((batch * seq, FOX_WIDTH), BF16),
            jax.ShapeDtypeStruct((batch * seq, FOX_WIDTH), BF16),
        ],
        compiler_params=_cparams(("arbitrary",)),
        name="decay",
    )(f_pad, jnp.asarray(pq, BF16), jnp.asarray(pk, BF16), jnp.asarray(cq), jnp.asarray(ck))


def _attn_kernel(q_ref, auxq_ref, k_ref, auxk_ref, v_ref, o_ref, *, tq, tk):
    qi = pl.program_id(2)
    n_pairs = q_ref.shape[1] // LANES
    low_q = lax.broadcasted_iota(I32, (tq, LANES), 1) < HEAD_DIM
    qs = []
    for pr in range(n_pairs):
        q = q_ref[:, pr * LANES:(pr + 1) * LANES]
        aq = auxq_ref[:, pr * LANES:(pr + 1) * LANES]
        qs += [jnp.where(low_q, q, aq), jnp.where(low_q, aq, q)]

    def step(start, n, carry, masked):
        low_k = lax.broadcasted_iota(I32, (n, LANES), 1) < HEAD_DIM
        ones = jnp.ones((n, LANES), BF16)
        out = []
        for pr in range(n_pairs):
            lanes = slice(pr * LANES, (pr + 1) * LANES)
            kb = k_ref[pl.ds(start, n), lanes]
            ak = auxk_ref[pl.ds(start, n), lanes]
            vb = v_ref[pl.ds(start, n), lanes]
            ks = (jnp.where(low_k, kb, ak), jnp.where(low_k, ak, kb))
            vs = (jnp.where(low_k, vb, ones), jnp.where(low_k, ones, vb))
            for j in range(2):
                m, acc = carry[2 * pr + j]
                s = lax.dot_general(qs[2 * pr + j], ks[j], (((1,), (1,)), ((), ())),
                                    preferred_element_type=F32)
                if masked:
                    row = lax.broadcasted_iota(I32, (tq, n), 0)
                    col = lax.broadcasted_iota(I32, (tq, n), 1)
                    s = jnp.where(col + (start - qi * tq) <= row, s, -jnp.inf)
                m_new = jnp.maximum(m, jnp.max(s, axis=1, keepdims=True))
                p = jnp.exp2(s - m_new)
                acc = jnp.exp2(m - m_new) * acc + _dot(p.astype(BF16), vs[j])
                out.append((m_new, acc))
        return tuple(out)

    init = tuple((jnp.full((tq, 1), -jnp.inf, F32), jnp.zeros((tq, LANES), F32))
                 for _ in range(2 * n_pairs))
    carry = lax.fori_loop(0, qi, lambda t, c: step(pl.multiple_of(t * tq, tq), tq, c, False), init)
    for d in range(tq // tk):
        carry = step(pl.multiple_of(qi * tq + d * tk, tk), tk, carry, True)
    for pr in range(n_pairs):
        acc0, acc1 = carry[2 * pr][1], carry[2 * pr + 1][1]
        out0 = acc0 / acc0[:, HEAD_DIM:HEAD_DIM + 1]
        out1 = acc1 / acc1[:, 0:1]
        o_ref[:, pr * LANES:(pr + 1) * LANES] = jnp.where(low_q, out0, out1).astype(BF16)


def _attention(q, auxq, k, auxk, v, batch, seq):
    tq = math.gcd(seq, ATTN_Q_TILE)
    tk = math.gcd(tq, ATTN_K_TILE)
    nq = seq // tq
    T = batch * seq
    width = ATTN_PAIRS * LANES
    q_spec = pl.BlockSpec((tq, width), lambda b, hp, qi: (b * nq + qi, hp))
    kv_spec = pl.BlockSpec((seq, width), lambda b, hp, qi: (b, hp))
    return pl.pallas_call(
        functools.partial(_attn_kernel, tq=tq, tk=tk),
        grid=(batch, N_HEADS // (2 * ATTN_PAIRS), nq),
        in_specs=[q_spec, q_spec, kv_spec, kv_spec, kv_spec],
        out_specs=q_spec,
        out_shape=jax.ShapeDtypeStruct((T, FOX_WIDTH), BF16),
        compiler_params=_cparams(("arbitrary", "arbitrary", "arbitrary")),
        name="attn",
    )(q, auxq, k, auxk, v)


def _mix_kernel(x_ref, attn_ref, w2_ref, b2_ref, lng_ref, lnb_ref, ws_ref, bs_ref,
                wa_ref, wb_ref, wo_ref, bo_ref, g1_ref, b1_ref, wr_ref, br_ref,
                h_ref, idx_ref, gate_ref, rank_ref, cnt_ref, carry_ref, *, tm, alpha):
    i = pl.program_id(0)

    @pl.when(i == 0)
    def _():
        carry_ref[...] = jnp.zeros_like(carry_ref)

    x = x_ref[...]
    proj = _dot(x.astype(BF16), w2_ref[...]) + b2_ref[...]
    u = _gelu(proj[:, :GMLP_WIDTH])
    gv = _gelu(proj[:, GMLP_WIDTH:2 * GMLP_WIDTH])
    vln = _layer_norm(gv, lng_ref[...], lnb_ref[...]).astype(BF16)

    cr = lax.broadcasted_iota(I32, (GMLP_CHUNK, GMLP_CHUNK), 0)
    cc = lax.broadcasted_iota(I32, (GMLP_CHUNK, GMLP_CHUNK), 1)
    tril = cc <= cr
    lo_half = cc < HEAD_DIM
    zero_w = jnp.zeros((GMLP_CHUNK, GMLP_CHUNK), BF16)
    n_slab = GMLP_WIDTH // LANES
    lhs = []
    for s in range(n_slab):
        w0 = jnp.where(tril, ws_ref[2 * s], zero_w)
        w1 = jnp.where(tril, ws_ref[2 * s + 1], zero_w)
        lhs.append(jnp.concatenate([w0, w1], axis=1))
    bs = bs_ref[...]
    rows = []
    for c in range(tm // GMLP_CHUNK):
        cols = []
        for s in range(n_slab):
            vs = vln[c * GMLP_CHUNK:(c + 1) * GMLP_CHUNK, s * LANES:(s + 1) * LANES]
            rhs = jnp.concatenate([jnp.where(lo_half, vs, zero_w), jnp.where(lo_half, zero_w, vs)], axis=0)
            cols.append(_dot(lhs[s], rhs))
        rows.append(jnp.concatenate(cols, axis=1) + bs)
    sp = jnp.concatenate(rows, axis=0) if len(rows) > 1 else rows[0]
    sgu = (u * sp).astype(BF16)

    ga = jax.nn.sigmoid(proj[:, 2 * GMLP_WIDTH:2 * GMLP_WIDTH + D_MODEL])
    gb = jax.nn.sigmoid(proj[:, 2 * GMLP_WIDTH + D_MODEL:])
    merged = ga * _dot(attn_ref[...], wa_ref[...]) + gb * _dot(sgu, wb_ref[...])
    mix = _dot(merged.astype(BF16), wo_ref[...]) + bo_ref[...]
    h = _layer_norm(alpha * x + mix, g1_ref[...], b1_ref[...])
    _store_token_rows(h_ref, h)

    a_hi = h.astype(BF16)
    a_lo = (h - a_hi.astype(F32)).astype(BF16)
    wr = wr_ref[...]
    w_hi = wr.astype(BF16)
    w_lo = (wr - w_hi.astype(F32)).astype(BF16)
    logits = _dot(a_hi, w_hi) + _dot(a_lo, w_hi) + _dot(a_hi, w_lo) + br_ref[...]

    lane_i = lax.broadcasted_iota(I32, (tm, LANES), 1)
    lane_f = lane_i.astype(F32)
    vals, idxs = [], []
    l = logits
    for _ in range(TOP_K):
        m = jnp.max(l, axis=1, keepdims=True)
        ix = jnp.min(jnp.where(l == m, lane_f, float(LANES)), axis=1, keepdims=True)
        vals.append(m)
        idxs.append(ix)
        l = jnp.where(lane_f == ix, -jnp.inf, l)
    es = [jnp.exp(v - vals[0]) for v in vals]
    den = es[0] + es[1] + es[2] + es[3]

    onehot = jnp.zeros((tm, LANES), F32)
    idx_out = jnp.zeros((tm, LANES), F32)
    gate_out = jnp.zeros((tm, LANES), F32)
    for k in range(TOP_K):
        onehot = onehot + jnp.where(lane_f == idxs[k], 1.0, 0.0)
        idx_out = jnp.where(lane_i == k, idxs[k], idx_out)
        gate_out = jnp.where(lane_i == k, es[k] / den, gate_out)

    tr = lax.broadcasted_iota(I32, (tm, tm), 0)
    tc = lax.broadcasted_iota(I32, (tm, tm), 1)
    strict = jnp.where(tc < tr, 1.0, 0.0).astype(BF16)
    carry = carry_ref[0:1, :]
    before = _dot(strict, onehot.astype(BF16)) + carry
    rank_out = jnp.zeros((tm, LANES), F32)
    for k in range(TOP_K):
        rk = jnp.sum(jnp.where(lane_f == idxs[k], before, 0.0), axis=1, keepdims=True)
        rank_out = jnp.where(lane_i == k, rk, rank_out)
    new_carry = carry + jnp.sum(onehot, axis=0, keepdims=True)
    carry_ref[...] = jnp.broadcast_to(new_carry, carry_ref.shape)
    cnt_ref[...] = jnp.broadcast_to(new_carry, cnt_ref.shape).astype(I32)
    idx_ref[...] = idx_out.astype(I32)
    gate_ref[...] = gate_out
    rank_ref[...] = rank_out.astype(I32)


def _mix(x2, attn, w2, b2, lng, lnb, ws, bs_tile, wa, wb, wo, bo, g1, b1, wr, br, tm, alpha, tile_off):
    T = attn.shape[0]
    n2 = w2.shape[1]
    const = lambda *shape: pl.BlockSpec(shape, lambda i: (0,) * len(shape))
    return pl.pallas_call(
        functools.partial(_mix_kernel, tm=tm, alpha=alpha),
        grid=(T // tm,),
        in_specs=[
            pl.BlockSpec((tm, D_MODEL), lambda i: (i + tile_off, 0)),
            pl.BlockSpec((tm, FOX_WIDTH), lambda i: (i, 0)),
            const(D_MODEL, n2), const(1, n2),
            const(1, GMLP_WIDTH), const(1, GMLP_WIDTH),
            const(GMLP_WIDTH // HEAD_DIM, GMLP_CHUNK, GMLP_CHUNK), const(GMLP_CHUNK, GMLP_WIDTH),
            const(FOX_WIDTH, D_MODEL), const(GMLP_WIDTH, D_MODEL),
            const(D_MODEL, D_MODEL), const(1, D_MODEL),
            const(1, D_MODEL), const(1, D_MODEL),
            const(D_MODEL, LANES), const(1, LANES),
        ],
        out_specs=[
            pl.BlockSpec((tm * ROW_TILE, LANES), lambda i: (i, 0)),
            pl.BlockSpec((tm, LANES), lambda i: (i, 0)),
            pl.BlockSpec((tm, LANES), lambda i: (i, 0)),
            pl.BlockSpec((tm, LANES), lambda i: (i, 0)),
            pl.BlockSpec((ROW_TILE, LANES), lambda i: (0, 0)),
        ],
        out_shape=[
            jax.ShapeDtypeStruct((T * ROW_TILE, LANES), F32),
            jax.ShapeDtypeStruct((T, LANES), I32),
            jax.ShapeDtypeStruct((T, LANES), F32),
            jax.ShapeDtypeStruct((T, LANES), I32),
            jax.ShapeDtypeStruct((ROW_TILE, LANES), I32),
        ],
        scratch_shapes=[pltpu.VMEM((ROW_TILE, LANES), F32)],
        compiler_params=_cparams(("arbitrary",)),
        name="mix",
    )(x2, attn, w2, b2, lng, lnb, ws, bs_tile, wa, wb, wo, bo, g1, b1, wr, br)


PLAN_EXPERT, PLAN_SRC, PLAN_VALID, PLAN_SLOT, PLAN_NEXT, PLAN_USED = range(6)


def _lane_cumsum(x):
    lane = lax.broadcasted_iota(I32, x.shape, 1)
    shift = 1
    while shift < LANES:
        x = x + jnp.where(lane >= shift, pltpu.roll(x, shift, 1), 0.0)
        shift *= 2
    return x


def _route_kernel(cnt_ref, idx_ref, rank_ref, dest_ref, plan_ref, start_ref, *, tm, bm, nbp):
    @pl.when(pl.program_id(0) == 0)
    def _():
        lane = lax.broadcasted_iota(I32, (ROW_TILE, LANES), 1)
        counts = jnp.where(lane < N_EXPERTS, cnt_ref[...].astype(F32), 0.0)
        padded = jnp.floor((counts + (bm - 1)) / bm) * bm
        pad_end = _lane_cumsum(padded)
        pad_start = pad_end - padded
        start_ref[...] = pad_start
        has_rows = jnp.where(counts > 0, 1.0, 0.0)
        order = _lane_cumsum(has_rows) - 1.0
        slot = order - 2.0 * jnp.floor(order * 0.5)
        total = jnp.sum(jnp.where(lane == N_EXPERTS - 1, pad_end, 0.0), axis=1, keepdims=True)[0:1, :]
        n_used = total / bm

        col = lambda r: jnp.transpose(r)[:, 0:1]
        sub = lax.broadcasted_iota(I32, (LANES, 1), 0)
        is_expert = sub < N_EXPERTS
        lane_w = lax.broadcasted_iota(I32, (LANES, LANES), 1)
        sub_w = lax.broadcasted_iota(I32, (LANES, LANES), 0)
        later = (lane_w > sub_w) & (lane_w < N_EXPERTS) & (has_rows[0:1, :] > 0)
        succ = jnp.min(jnp.where(later, lane_w, N_EXPERTS).astype(F32), axis=1, keepdims=True)
        succ = jnp.where(succ == N_EXPERTS, sub.astype(F32), succ)

        blk = lax.broadcasted_iota(I32, (1, nbp), 1).astype(F32)
        blk_src = jnp.minimum(blk, n_used - 1.0)
        blk_row = blk_src * bm
        below = is_expert & (col(pad_end) <= blk_row)
        blk_e = jnp.minimum(jnp.sum(jnp.where(below, 1.0, 0.0), axis=0, keepdims=True), N_EXPERTS - 1.0)
        hit = sub.astype(F32) == blk_e
        take = lambda c: jnp.sum(jnp.where(hit, c, 0.0), axis=0, keepdims=True)
        blk_valid = jnp.clip(take(col(pad_start + counts)) - blk_row, 0.0, bm)
        rows = [None] * ROW_TILE
        rows[PLAN_EXPERT], rows[PLAN_SRC], rows[PLAN_VALID] = blk_e, blk_src, blk_valid
        rows[PLAN_SLOT], rows[PLAN_NEXT] = take(col(slot)), take(succ)
        rows[PLAN_USED] = jnp.broadcast_to(n_used, (1, nbp))
        zero = jnp.zeros((1, nbp), F32)
        plan_ref[...] = jnp.concatenate([zero if r is None else r for r in rows], axis=0).astype(I32)

    pad_start = start_ref[0:1, :]
    lane_i = lax.broadcasted_iota(I32, (tm, LANES), 1)
    lane_f = lane_i.astype(F32)
    idx = idx_ref[...].astype(F32)
    rank = rank_ref[...].astype(F32)
    dest = jnp.zeros((tm, LANES), F32)
    for k in range(TOP_K):
        base = jnp.sum(jnp.where(lane_f == idx[:, k:k + 1], pad_start, 0.0), axis=1, keepdims=True)
        dest = jnp.where(lane_i == k, base + rank[:, k:k + 1], dest)
    dest_ref[...] = jnp.transpose(dest)[:ROW_TILE, :].astype(I32)


def _route(cnt_o, idx_o, rank_o, bm, n_blk):
    Tp = idx_o.shape[0]
    tm = math.gcd(Tp, ROUTE_TILE)
    nbp = -(-n_blk // LANES) * LANES
    return pl.pallas_call(
        functools.partial(_route_kernel, tm=tm, bm=bm, nbp=nbp),
        grid=(Tp // tm,),
        in_specs=[
            pl.BlockSpec((ROW_TILE, LANES), lambda i: (0, 0)),
            pl.BlockSpec((tm, LANES), lambda i: (i, 0)),
            pl.BlockSpec((tm, LANES), lambda i: (i, 0)),
        ],
        out_specs=[
            pl.BlockSpec((ROW_TILE, tm), lambda i: (0, i)),
            pl.BlockSpec((ROW_TILE, nbp), lambda i: (0, 0)),
        ],
        out_shape=[
            jax.ShapeDtypeStruct((ROW_TILE, Tp), I32),
            jax.ShapeDtypeStruct((ROW_TILE, nbp), I32),
        ],
        scratch_shapes=[pltpu.VMEM((ROW_TILE, LANES), F32)],
        compiler_params=_cparams(("arbitrary",)),
        name="route",
    )(cnt_o, idx_o, rank_o)


def _sc_window_indices(idx):
    return jnp.pad(idx.reshape(-1, SC_WINDOW), ((0, 0), (0, LANES - SC_WINDOW)))


def _sc_scatter_rows(dest_km, src3, n_dst):
    n_slot, n_tok = dest_km.shape
    n_win = n_tok // SC_WINDOW
    mesh = plsc.VectorSubcoreMesh(core_axis_name="core", subcore_axis_name="subcore",
                                  num_cores=SC_CORES, num_subcores=SC_SUBCORES)

    @pl.kernel(out_type=jax.ShapeDtypeStruct((n_dst, ROW_TILE, LANES), src3.dtype), mesh=mesh, name="sc_scatter")
    def scatter(src_hbm, i_hbm, o_hbm):
        def body(x_vmem, i_vmem):
            pltpu.sync_copy(x_vmem, o_hbm.at[i_vmem.at[0, pl.ds(0, SC_WINDOW)]])

        @pl.when(lax.axis_index("core") == 0)
        def _():
            pltpu.emit_pipeline(
                body,
                grid=(n_slot * n_win,),
                in_specs=[pl.BlockSpec((SC_WINDOW, ROW_TILE, LANES), lambda i: (i % n_win, 0, 0)),
                          pl.BlockSpec((1, LANES), lambda i: (i, 0))],
                out_specs=[],
                core_axis_name="subcore",
                dimension_semantics=(pltpu.PARALLEL,),
            )(src_hbm, i_hbm)

    return scatter(src3, _sc_window_indices(dest_km))


def _sc_gather_rows(sidx, src3):
    n = sidx.shape[0]
    mesh = plsc.VectorSubcoreMesh(core_axis_name="core", subcore_axis_name="subcore",
                                  num_cores=SC_CORES, num_subcores=SC_SUBCORES)

    @pl.kernel(out_type=jax.ShapeDtypeStruct((n, ROW_TILE, LANES), src3.dtype), mesh=mesh, name="sc_gather")
    def gather(src_hbm, i_hbm, o_hbm):
        def body(i_vmem, o_vmem):
            pltpu.sync_copy(src_hbm.at[i_vmem.at[0, pl.ds(0, SC_WINDOW)]], o_vmem)

        pltpu.emit_pipeline(
            body,
            grid=(n // SC_WINDOW,),
            in_specs=[pl.BlockSpec((1, LANES), lambda i: (i, 0))],
            out_specs=[pl.BlockSpec((SC_WINDOW, ROW_TILE, LANES), lambda i: (i, 0, 0))],
            core_axis_name=("core", "subcore"),
            dimension_semantics=(pltpu.PARALLEL,),
        )(i_hbm, o_hbm)

    return gather(src3, _sc_window_indices(sidx))


def _moe_kernel(plan_ref, x_ref, wgu_hbm, bgu_ref, wdn_hbm, bdn_ref, y_ref,
                wgu_f32, wdn_f32, wgu_bf, wdn_bf, sems):
    i = pl.program_id(0)
    expert = plan_ref[PLAN_EXPERT, i]
    active = i < plan_ref[PLAN_USED, i]
    fresh = (i == 0) | (expert != plan_ref[PLAN_EXPERT, jnp.maximum(i - 1, 0)])
    slot = plan_ref[PLAN_SLOT, i]
    successor = plan_ref[PLAN_NEXT, i]

    def weight_copies(e, s):
        return (pltpu.make_async_copy(wgu_hbm.at[e], wgu_f32.at[s], sems.at[s, 0]),
                pltpu.make_async_copy(wdn_hbm.at[e], wdn_f32.at[s], sems.at[s, 1]))

    @pl.when(i == 0)
    def _():
        for c in weight_copies(expert, slot):
            c.start()

    @pl.when(active & fresh)
    def _():
        @pl.when(successor != expert)
        def _():
            for c in weight_copies(successor, 1 - slot):
                c.start()

        for c in weight_copies(expert, slot):
            c.wait()
        wgu_bf[...] = wgu_f32[slot].astype(BF16)
        wdn_bf[...] = wdn_f32[slot].astype(BF16)

    @pl.when(active)
    def _():
        bm = x_ref.shape[0] // ROW_TILE
        x = _load_token_rows(x_ref, bm)
        valid = lax.broadcasted_iota(I32, (bm, 1), 0) < plan_ref[PLAN_VALID, i]
        x = jnp.where(valid, x, 0.0).astype(BF16)
        gu = _dot(x, wgu_bf[...]) + bgu_ref[...]
        gate = jnp.minimum(gu[:, :D_FF], SWIGLU_LIMIT)
        up = jnp.clip(gu[:, D_FF:], -SWIGLU_LIMIT, SWIGLU_LIMIT)
        hid = (up + 1.0) * (gate * jax.nn.sigmoid(SWIGLU_ALPHA * gate))
        _store_token_rows(y_ref, _dot(hid.astype(BF16), wdn_bf[...]) + bdn_ref[...])

    @pl.when(jnp.logical_not(active))
    def _():
        y_ref[...] = jnp.zeros_like(y_ref)


def _moe(plan, xpad, wgu, bgu, wdn, bdn, n_blk, bm):
    rows = lambda i, plan: (plan[PLAN_SRC, i], 0)
    bias = lambda i, plan: (plan[PLAN_EXPERT, i], 0, 0)
    grid_spec = pltpu.PrefetchScalarGridSpec(
        num_scalar_prefetch=1,
        grid=(n_blk,),
        in_specs=[
            pl.BlockSpec((bm * ROW_TILE, LANES), rows),
            pl.BlockSpec(memory_space=pl.ANY),
            pl.BlockSpec((None, 1, 2 * D_FF), bias),
            pl.BlockSpec(memory_space=pl.ANY),
            pl.BlockSpec((None, 1, D_MODEL), bias),
        ],
        out_specs=pl.BlockSpec((bm * ROW_TILE, LANES), lambda i, plan: (i, 0)),
        scratch_shapes=[
            pltpu.VMEM((2, D_MODEL, 2 * D_FF), F32), pltpu.VMEM((2, D_FF, D_MODEL), F32),
            pltpu.VMEM((D_MODEL, 2 * D_FF), BF16), pltpu.VMEM((D_FF, D_MODEL), BF16),
            pltpu.SemaphoreType.DMA((2, 2)),
        ],
    )
    return pl.pallas_call(
        _moe_kernel,
        grid_spec=grid_spec,
        out_shape=jax.ShapeDtypeStruct((n_blk * bm * ROW_TILE, LANES), F32),
        compiler_params=_cparams(("arbitrary",)),
        name="moe",
    )(plan, xpad, wgu, bgu, wdn, bdn)


def _final_kernel(h_ref, yg_ref, gate_ref, p_ref, wple_ref, wpg_ref, bpg_ref,
                  g2_ref, b2_ref, g3_ref, b3_ref, *rest, tm, alpha):
    o_ref = rest[-1]
    h = _load_token_rows(h_ref, tm)
    gates = gate_ref[...]
    ffn = gates[:, 0:1] * _load_token_rows(yg_ref.at[0], tm)
    for k in range(1, TOP_K):
        ffn = ffn + gates[:, k:k + 1] * _load_token_rows(yg_ref.at[k], tm)
    h2 = _layer_norm(alpha * h + ffn, g2_ref[...], b2_ref[...])
    emb = _dot(p_ref[...].astype(BF16), wple_ref[...])
    pg = jax.nn.sigmoid(_dot(h2.astype(BF16), wpg_ref[...]) + bpg_ref[...])
    o_ref[...] = _layer_norm(alpha * h2 + emb * pg, g3_ref[...], b3_ref[...])


def _final(h1, yg, gates, p2, wple, wpg, bpg, g2, b2, g3, b3, tm, alpha, tile_off, out_prev):
    T = h1.shape[0] // ROW_TILE
    const = lambda *shape: pl.BlockSpec(shape, lambda i: (0,) * len(shape))
    in_specs = [
        pl.BlockSpec((tm * ROW_TILE, LANES), lambda i: (i, 0)),
        pl.BlockSpec((TOP_K, tm * ROW_TILE, LANES), lambda i: (0, i, 0)),
        pl.BlockSpec((tm, LANES), lambda i: (i, 0)),
        pl.BlockSpec((tm, PLE_DIM), lambda i: (i + tile_off, 0)),
        const(PLE_DIM, D_MODEL), const(D_MODEL, D_MODEL), const(1, D_MODEL),
        const(1, D_MODEL), const(1, D_MODEL), const(1, D_MODEL), const(1, D_MODEL),
    ]
    args = [h1, yg, gates, p2, wple, wpg, bpg, g2, b2, g3, b3]
    aliases = {}
    if out_prev is not None:
        in_specs.append(pl.BlockSpec(memory_space=pl.ANY))
        aliases = {len(args): 0}
        args.append(out_prev)
    return pl.pallas_call(
        functools.partial(_final_kernel, tm=tm, alpha=alpha),
        grid=(T // tm,),
        in_specs=in_specs,
        out_specs=pl.BlockSpec((tm, D_MODEL), lambda i: (i + tile_off, 0)),
        out_shape=jax.ShapeDtypeStruct((p2.shape[0], D_MODEL), F32),
        input_output_aliases=aliases,
        compiler_params=_cparams(("arbitrary",)),
        name="final",
    )(*args)


def _layer(h2d, p2d, batch, seq, alpha, w_in, b_in, gmlp_ln_g, gmlp_ln_b, w_spatial, b_spatial,
           w_branch_a, w_branch_b, w_out, b_out, ln1_g, ln1_b, w_router, b_router,
           w_gate_up, b_gate_up, w_down, b_down, ln2_g, ln2_b, w_ple, w_ple_gate, b_ple_gate,
           ln3_g, ln3_b):
    T = batch * seq
    tm = math.gcd(T, ROW_TILE_A)
    bm = MOE_BLOCK
    off_f = 3 * FOX_WIDTH
    off_u = off_f + N_HEADS
    row = lambda v: v.reshape(1, -1).astype(F32)

    w1 = jnp.concatenate([w_in[:, :off_u], jnp.zeros((D_MODEL, LANES - N_HEADS), F32)], axis=1).astype(BF16)
    b1 = jnp.concatenate([b_in[:off_u], jnp.zeros((LANES - N_HEADS,), F32)]).reshape(1, -1)
    w2 = w_in[:, off_u:].astype(BF16)
    b2 = row(b_in[off_u:])
    bs_tile = jnp.repeat(b_spatial.T, HEAD_DIM, axis=1)
    wr = jnp.concatenate([w_router, jnp.zeros((D_MODEL, LANES - N_EXPERTS), F32)], axis=1)
    br = jnp.concatenate([b_router, jnp.full((LANES - N_EXPERTS,), -1e30, F32)]).reshape(1, -1)

    wsb, wab, wbb, wob = (w.astype(BF16) for w in (w_spatial, w_branch_a, w_branch_b, w_out))
    wpleb, wpgb = w_ple.astype(BF16), w_ple_gate.astype(BF16)
    bgu, bdn = b_gate_up.reshape(N_EXPERTS, 1, -1), b_down.reshape(N_EXPERTS, 1, -1)

    n_parts = N_PARTS if batch % N_PARTS == 0 else 1
    pb = batch // n_parts
    Tp = pb * seq
    tiles = Tp // tm
    n_assign = Tp * TOP_K
    n_blk = -(-n_assign // bm) + N_EXPERTS
    n_rows = n_blk * bm

    def front(part):
        q, k, v, f_pad = _qkvf(h2d, w1, b1, tm, Tp, part * tiles)
        auxq, auxk = _decay(f_pad, pb, seq)
        attn = _attention(q, auxq, k, auxk, v, pb, seq)
        h1, idx_o, gate_o, rank_o, cnt_o = _mix(
            h2d, attn, w2, b2, row(gmlp_ln_g), row(gmlp_ln_b), wsb, bs_tile, wab, wbb, wob, row(b_out),
            row(ln1_g), row(ln1_b), wr, br, tm, alpha, part * tiles)
        dest8, plan = _route(cnt_o, idx_o, rank_o, bm, n_blk)
        dest_km = dest8[:TOP_K]
        xpad = _sc_scatter_rows(dest_km, h1.reshape(Tp, ROW_TILE, LANES), n_rows)
        return h1, gate_o, dest_km, plan, xpad

    def experts(state):
        h1, gate_o, dest_km, plan, xpad = state
        ypad = _moe(plan, xpad.reshape(n_rows * ROW_TILE, LANES), w_gate_up, bgu, w_down, bdn, n_blk, bm)
        yg = _sc_gather_rows(dest_km.reshape(-1), ypad.reshape(n_rows, ROW_TILE, LANES))
        return h1, gate_o, yg

    def back(part, state, out_prev):
        h1, gate_o, yg = state
        return _final(h1, yg.reshape(TOP_K, Tp * ROW_TILE, LANES), gate_o, p2d, wpleb, wpgb,
                      row(b_ple_gate), row(ln2_g), row(ln2_b), row(ln3_g), row(ln3_b), tm, alpha,
                      part * tiles, out_prev)

    fronts = [front(part) for part in range(n_parts)]
    mids = [experts(state) for state in fronts]
    out = None
    for part in range(n_parts):
        out = back(part, mids[part], out)
    return out


def kernel(x, p, w_in, b_in, gmlp_ln_g, gmlp_ln_b, w_spatial, b_spatial, w_branch_a, w_branch_b, w_out, b_out, ln1_g, ln1_b, w_router, b_router, w_gate_up, b_gate_up, w_down, b_down, ln2_g, ln2_b, w_ple, w_ple_gate, b_ple_gate, ln3_g, ln3_b):
    batch, seq, d = x.shape
    depth = w_in.shape[0]
    assert d == D_MODEL and seq % GMLP_CHUNK == 0
    alpha = (2.0 * depth) ** 0.25
    h = x.reshape(batch * seq, d)
    for i in range(depth):
        h = _layer(h, p[i].reshape(batch * seq, PLE_DIM), batch, seq, alpha,
                   w_in[i], b_in[i], gmlp_ln_g[i], gmlp_ln_b[i], w_spatial[i], b_spatial[i],
                   w_branch_a[i], w_branch_b[i], w_out[i], b_out[i], ln1_g[i], ln1_b[i],
                   w_router[i], b_router[i], w_gate_up[i], b_gate_up[i], w_down[i], b_down[i],
                   ln2_g[i], ln2_b[i], w_ple[i], w_ple_gate[i], b_ple_gate[i], ln3_g[i], ln3_b[i])
    return h.reshape(batch, seq, d)
```

```python
import functools
import math

import jax
import jax.numpy as jnp
import numpy as np
from jax import lax
from jax.experimental import pallas as pl
from jax.experimental.pallas import tpu as pltpu
from jax.experimental.pallas import tpu_sc as plsc

F32 = jnp.float32
BF16 = jnp.bfloat16
I32 = jnp.int32

D_MODEL = 1024
N_HEADS = 8
HEAD_DIM = 64
FOX_WIDTH = N_HEADS * HEAD_DIM
GMLP_WIDTH = 512
GMLP_CHUNK = 128
N_EXPERTS = 32
TOP_K = 4
D_FF = 1024
PLE_DIM = 256
SWIGLU_LIMIT = 7.0
SWIGLU_ALPHA = 1.702
LN_EPS = 1e-5
LANES = 128
ROW_TILE = 8
VMEM_LIMIT = 56 * 1024 * 1024

MOE_BLOCK = 256
MOE_X_DEPTH = 4
ATTN_Q_TILE = 512
ATTN_K_TILE = 512
ATTN_PAIRS = 2
LOG2E = math.log2(math.e)
ROW_TILE_A = 512
SC_CORES = 2
SC_SUBCORES = 16
SC_WINDOW = 32
N_PARTS = 2
ROUTE_TILE = 1024


def _cparams(sem):
    return pltpu.CompilerParams(dimension_semantics=sem, vmem_limit_bytes=VMEM_LIMIT)


def _gelu(x):
    c = math.sqrt(2.0 / math.pi)
    return 0.5 * x * (1.0 + jnp.tanh(c * (x + 0.044715 * (x * x * x))))


def _layer_norm(x, g, b):
    mu = jnp.mean(x, axis=-1, keepdims=True)
    xc = x - mu
    var = jnp.mean(xc * xc, axis=-1, keepdims=True)
    return xc * lax.rsqrt(var + LN_EPS) * g + b


def _split3(x):
    hi = x.astype(BF16)
    r = x - hi.astype(F32)
    mid = r.astype(BF16)
    lo = (r - mid.astype(F32)).astype(BF16)
    return hi, mid, lo


def _dot(a, b):
    return jnp.dot(a, b, preferred_element_type=F32)


def _load_token_rows(ref, n):
    return jnp.concatenate([ref[pl.ds(j, n, stride=ROW_TILE), :] for j in range(ROW_TILE)], axis=1)


def _store_token_rows(ref, val):
    n = val.shape[0]
    for j in range(ROW_TILE):
        ref[pl.ds(j, n, stride=ROW_TILE), :] = val[:, j * LANES:(j + 1) * LANES]


def _qkvf_kernel(x_ref, w_ref, b_ref, q_ref, k_ref, v_ref, f_ref):
    x = x_ref[...].astype(BF16)
    proj = _dot(x, w_ref[...]) + b_ref[...]
    q_ref[...] = (proj[:, :FOX_WIDTH] * (HEAD_DIM ** -0.5 * LOG2E)).astype(BF16)
    k_ref[...] = proj[:, FOX_WIDTH:2 * FOX_WIDTH].astype(BF16)
    v_ref[...] = proj[:, 2 * FOX_WIDTH:3 * FOX_WIDTH].astype(BF16)
    f_ref[...] = proj[:, 3 * FOX_WIDTH:]


def _qkvf(x2, w, b, tm, T, tile_off):
    n_out = w.shape[1]
    return pl.pallas_call(
        _qkvf_kernel,
        grid=(T // tm,),
        in_specs=[
            pl.BlockSpec((tm, D_MODEL), lambda i: (i + tile_off, 0)),
            pl.BlockSpec((D_MODEL, n_out), lambda i: (0, 0)),
            pl.BlockSpec((1, n_out), lambda i: (0, 0)),
        ],
        out_specs=[
            pl.BlockSpec((tm, FOX_WIDTH), lambda i: (i, 0)),
            pl.BlockSpec((tm, FOX_WIDTH), lambda i: (i, 0)),
            pl.BlockSpec((tm, FOX_WIDTH), lambda i: (i, 0)),
            pl.BlockSpec((tm, LANES), lambda i: (i, 0)),
        ],
        out_shape=[
            jax.ShapeDtypeStruct((T, FOX_WIDTH), BF16),
            jax.ShapeDtypeStruct((T, FOX_WIDTH), BF16),
            jax.ShapeDtypeStruct((T, FOX_WIDTH), BF16),
            jax.ShapeDtypeStruct((T, LANES), F32),
        ],
        compiler_params=_cparams(("arbitrary",)),
        name="qkvf",
    )(x2, w, b)


def _decay_placement():
    pq = np.zeros((3 * LANES, FOX_WIDTH), np.float32)
    pk = np.zeros((3 * LANES, FOX_WIDTH), np.float32)
    cq = np.zeros((1, FOX_WIDTH), np.float32)
    ck = np.zeros((1, FOX_WIDTH), np.float32)
    for h in range(N_HEADS):
        base = (h // 2) * LANES + (HEAD_DIM if h % 2 == 0 else 0)
        for piece in range(3):
            pq[piece * LANES + h, base + piece] = 1.0
            pk[piece * LANES + h, base + 3 + piece] = -1.0
            cq[0, base + 3 + piece] = 1.0
            ck[0, base + piece] = 1.0
    return pq, pk, cq, ck


def _decay_kernel(f_ref, pq_ref, pk_ref, cq_ref, ck_ref, auxq_ref, auxk_ref, *, seq, blk):
    r = lax.broadcasted_iota(I32, (blk, blk), 0)
    c = lax.broadcasted_iota(I32, (blk, blk), 1)
    tri = jnp.where(c <= r, 1.0, 0.0).astype(BF16)
    carry = jnp.zeros((1, LANES), F32)
    for i in range(seq // blk):
        f = f_ref[i * blk:(i + 1) * blk, :]
        ls = jnp.minimum(f, 0.0) - jnp.log1p(jnp.exp(-jnp.abs(f)))
        hi, mid, lo = _split3(ls)
        cs = _dot(tri, hi) + _dot(tri, mid) + _dot(tri, lo) + carry
        carry = cs[blk - 1:blk, :]
        pieces = jnp.concatenate(_split3(cs * LOG2E), axis=1)
        auxq_ref[i * blk:(i + 1) * blk, :] = (_dot(pieces, pq_ref[...]) + cq_ref[...]).astype(BF16)
        auxk_ref[i * blk:(i + 1) * blk, :] = (_dot(pieces, pk_ref[...]) + ck_ref[...]).astype(BF16)


def _decay(f_pad, batch, seq):
    blk = 256 if seq % 256 == 0 else LANES
    pq, pk, cq, ck = _decay_placement()
    const = lambda *shape: pl.BlockSpec(shape, lambda b: (0,) * len(shape))
    return pl.pallas_call(
        functools.partial(_decay_kernel, seq=seq, blk=blk),
        grid=(batch,),
        in_specs=[
            pl.BlockSpec((seq, LANES), lambda b: (b, 0)),
            const(3 * LANES, FOX_WIDTH), const(3 * LANES, FOX_WIDTH), const(1, FOX_WIDTH), const(1, FOX_WIDTH),
        ],
        out_specs=[
            pl.BlockSpec((seq, FOX_WIDTH), lambda b: (b, 0)),
            pl.BlockSpec((seq, FOX_WIDTH), lambda b: (b, 0)),
        ],
        out_shape=[
            jax.ShapeDtypeStruct((batch * seq, FOX_WIDTH), BF16),
            jax.ShapeDtypeStruct((batch * seq, FOX_WIDTH), BF16),
        ],
        compiler_params=_cparams(("arbitrary",)),
        name="decay",
    )(f_pad, jnp.asarray(pq, BF16), jnp.asarray(pk, BF16), jnp.asarray(cq), jnp.asarray(ck))


def _attn_kernel(q_ref, auxq_ref, k_ref, auxk_ref, v_ref, o_ref, *, tq, tk):
    qi = pl.program_id(2)
    n_pairs = q_ref.shape[1] // LANES
    low_q = lax.broadcasted_iota(I32, (tq, LANES), 1) < HEAD_DIM
    qs = []
    for pr in range(n_pairs):
        q = q_ref[:, pr * LANES:(pr + 1) * LANES]
        aq = auxq_ref[:, pr * LANES:(pr + 1) * LANES]
        qs += [jnp.where(low_q, q, aq), jnp.where(low_q, aq, q)]

    def step(start, n, carry, masked):
        low_k = lax.broadcasted_iota(I32, (n, LANES), 1) < HEAD_DIM
        ones = jnp.ones((n, LANES), BF16)
        out = []
        for pr in range(n_pairs):
            lanes = slice(pr * LANES, (pr + 1) * LANES)
            kb = k_ref[pl.ds(start, n), lanes]
            ak = auxk_ref[pl.ds(start, n), lanes]
            vb = v_ref[pl.ds(start, n), lanes]
            ks = (jnp.where(low_k, kb, ak), jnp.where(low_k, ak, kb))
            vs = (jnp.where(low_k, vb, ones), jnp.where(low_k, ones, vb))
            for j in range(2):
                m, acc = carry[2 * pr + j]
                s = lax.dot_general(qs[2 * pr + j], ks[j], (((1,), (1,)), ((), ())),
                                    preferred_element_type=F32)
                if masked:
                    row = lax.broadcasted_iota(I32, (tq, n), 0)
                    col = lax.broadcasted_iota(I32, (tq, n), 1)
                    s = jnp.where(col + (start - qi * tq) <= row, s, -jnp.inf)
                m_new = jnp.maximum(m, jnp.max(s, axis=1, keepdims=True))
                p = jnp.exp2(s - m_new)
                acc = jnp.exp2(m - m_new) * acc + _dot(p.astype(BF16), vs[j])
                out.append((m_new, acc))
        return tuple(out)

    init = tuple((jnp.full((tq, 1), -jnp.inf, F32), jnp.zeros((tq, LANES), F32))
                 for _ in range(2 * n_pairs))
    carry = lax.fori_loop(0, qi, lambda t, c: step(pl.multiple_of(t * tq, tq), tq, c, False), init)
    for d in range(tq // tk):
        carry = step(pl.multiple_of(qi * tq + d * tk, tk), tk, carry, True)
    for pr in range(n_pairs):
        acc0, acc1 = carry[2 * pr][1], carry[2 * pr + 1][1]
        out0 = acc0 / acc0[:, HEAD_DIM:HEAD_DIM + 1]
        out1 = acc1 / acc1[:, 0:1]
        o_ref[:, pr * LANES:(pr + 1) * LANES] = jnp.where(low_q, out0, out1).astype(BF16)


def _attention(q, auxq, k, auxk, v, batch, seq):
    tq = math.gcd(seq, ATTN_Q_TILE)
    tk = math.gcd(tq, ATTN_K_TILE)
    nq = seq // tq
    T = batch * seq
    width = ATTN_PAIRS * LANES
    q_spec = pl.BlockSpec((tq, width), lambda b, hp, qi: (b * nq + qi, hp))
    kv_spec = pl.BlockSpec((seq, width), lambda b, hp, qi: (b, hp))
    return pl.pallas_call(
        functools.partial(_attn_kernel, tq=tq, tk=tk),
        grid=(batch, N_HEADS // (2 * ATTN_PAIRS), nq),
        in_specs=[q_spec, q_spec, kv_spec, kv_spec, kv_spec],
        out_specs=q_spec,
        out_shape=jax.ShapeDtypeStruct((T, FOX_WIDTH), BF16),
        compiler_params=_cparams(("arbitrary", "arbitrary", "arbitrary")),
        name="attn",
    )(q, auxq, k, auxk, v)


def _mix_kernel(x_ref, attn_ref, w2_ref, b2_ref, lng_ref, lnb_ref, ws_ref, bs_ref,
                wa_ref, wb_ref, wo_ref, bo_ref, g1_ref, b1_ref, wr_ref, br_ref,
                h_ref, idx_ref, gate_ref, rank_ref, cnt_ref, carry_ref, *, tm, alpha):
    i = pl.program_id(0)

    @pl.when(i == 0)
    def _():
        carry_ref[...] = jnp.zeros_like(carry_ref)

    x = x_ref[...]
    proj = _dot(x.astype(BF16), w2_ref[...]) + b2_ref[...]
    u = _gelu(proj[:, :GMLP_WIDTH])
    gv = _gelu(proj[:, GMLP_WIDTH:2 * GMLP_WIDTH])
    vln = _layer_norm(gv, lng_ref[...], lnb_ref[...]).astype(BF16)

    cr = lax.broadcasted_iota(I32, (GMLP_CHUNK, GMLP_CHUNK), 0)
    cc = lax.broadcasted_iota(I32, (GMLP_CHUNK, GMLP_CHUNK), 1)
    tril = cc <= cr
    lo_half = cc < HEAD_DIM
    zero_w = jnp.zeros((GMLP_CHUNK, GMLP_CHUNK), BF16)
    n_slab = GMLP_WIDTH // LANES
    lhs = []
    for s in range(n_slab):
        w0 = jnp.where(tril, ws_ref[2 * s], zero_w)
        w1 = jnp.where(tril, ws_ref[2 * s + 1], zero_w)
        lhs.append(jnp.concatenate([w0, w1], axis=1))
    bs = bs_ref[...]
    rows = []
    for c in range(tm // GMLP_CHUNK):
        cols = []
        for s in range(n_slab):
            vs = vln[c * GMLP_CHUNK:(c + 1) * GMLP_CHUNK, s * LANES:(s + 1) * LANES]
            rhs = jnp.concatenate([jnp.where(lo_half, vs, zero_w), jnp.where(lo_half, zero_w, vs)], axis=0)
            cols.append(_dot(lhs[s], rhs))
        rows.append(jnp.concatenate(cols, axis=1) + bs)
    sp = jnp.concatenate(rows, axis=0) if len(rows) > 1 else rows[0]
    sgu = (u * sp).astype(BF16)

    ga = jax.nn.sigmoid(proj[:, 2 * GMLP_WIDTH:2 * GMLP_WIDTH + D_MODEL])
    gb = jax.nn.sigmoid(proj[:, 2 * GMLP_WIDTH + D_MODEL:])
    merged = ga * _dot(attn_ref[...], wa_ref[...]) + gb * _dot(sgu, wb_ref[...])
    mix = _dot(merged.astype(BF16), wo_ref[...]) + bo_ref[...]
    h = _layer_norm(alpha * x + mix, g1_ref[...], b1_ref[...])
    _store_token_rows(h_ref, h)

    a_hi = h.astype(BF16)
    a_lo = (h - a_hi.astype(F32)).astype(BF16)
    wr = wr_ref[...]
    w_hi = wr.astype(BF16)
    w_lo = (wr - w_hi.astype(F32)).astype(BF16)
    logits = _dot(a_hi, w_hi) + _dot(a_lo, w_hi) + _dot(a_hi, w_lo) + br_ref[...]

    lane_i = lax.broadcasted_iota(I32, (tm, LANES), 1)
    lane_f = lane_i.astype(F32)
    vals, idxs = [], []
    l = logits
    for _ in range(TOP_K):
        m = jnp.max(l, axis=1, keepdims=True)
        ix = jnp.min(jnp.where(l == m, lane_f, float(LANES)), axis=1, keepdims=True)
        vals.append(m)
        idxs.append(ix)
        l = jnp.where(lane_f == ix, -jnp.inf, l)
    es = [jnp.exp(v - vals[0]) for v in vals]
    den = es[0] + es[1] + es[2] + es[3]

    onehot = jnp.zeros((tm, LANES), F32)
    idx_out = jnp.zeros((tm, LANES), F32)
    gate_out = jnp.zeros((tm, LANES), F32)
    for k in range(TOP_K):
        onehot = onehot + jnp.where(lane_f == idxs[k], 1.0, 0.0)
        idx_out = jnp.where(lane_i == k, idxs[k], idx_out)
        gate_out = jnp.where(lane_i == k, es[k] / den, gate_out)

    tr = lax.broadcasted_iota(I32, (tm, tm), 0)
    tc = lax.broadcasted_iota(I32, (tm, tm), 1)
    strict = jnp.where(tc < tr, 1.0, 0.0).astype(BF16)
    carry = carry_ref[0:1, :]
    before = _dot(strict, onehot.astype(BF16)) + carry
    rank_out = jnp.zeros((tm, LANES), F32)
    for k in range(TOP_K):
        rk = jnp.sum(jnp.where(lane_f == idxs[k], before, 0.0), axis=1, keepdims=True)
        rank_out = jnp.where(lane_i == k, rk, rank_out)
    new_carry = carry + jnp.sum(onehot, axis=0, keepdims=True)
    carry_ref[...] = jnp.broadcast_to(new_carry, carry_ref.shape)
    cnt_ref[...] = jnp.broadcast_to(new_carry, cnt_ref.shape).astype(I32)
    idx_ref[...] = idx_out.astype(I32)
    gate_ref[...] = gate_out
    rank_ref[...] = rank_out.astype(I32)


def _mix(x2, attn, w2, b2, lng, lnb, ws, bs_tile, wa, wb, wo, bo, g1, b1, wr, br, tm, alpha, tile_off):
    T = attn.shape[0]
    n2 = w2.shape[1]
    const = lambda *shape: pl.BlockSpec(shape, lambda i: (0,) * len(shape))
    return pl.pallas_call(
        functools.partial(_mix_kernel, tm=tm, alpha=alpha),
        grid=(T // tm,),
        in_specs=[
            pl.BlockSpec((tm, D_MODEL), lambda i: (i + tile_off, 0)),
            pl.BlockSpec((tm, FOX_WIDTH), lambda i: (i, 0)),
            const(D_MODEL, n2), const(1, n2),
            const(1, GMLP_WIDTH), const(1, GMLP_WIDTH),
            const(GMLP_WIDTH // HEAD_DIM, GMLP_CHUNK, GMLP_CHUNK), const(GMLP_CHUNK, GMLP_WIDTH),
            const(FOX_WIDTH, D_MODEL), const(GMLP_WIDTH, D_MODEL),
            const(D_MODEL, D_MODEL), const(1, D_MODEL),
            const(1, D_MODEL), const(1, D_MODEL),
            const(D_MODEL, LANES), const(1, LANES),
        ],
        out_specs=[
            pl.BlockSpec((tm * ROW_TILE, LANES), lambda i: (i, 0)),
            pl.BlockSpec((tm, LANES), lambda i: (i, 0)),
            pl.BlockSpec((tm, LANES), lambda i: (i, 0)),
            pl.BlockSpec((tm, LANES), lambda i: (i, 0)),
            pl.BlockSpec((ROW_TILE, LANES), lambda i: (0, 0)),
        ],
        out_shape=[
            jax.ShapeDtypeStruct((T * ROW_TILE, LANES), F32),
            jax.ShapeDtypeStruct((T, LANES), I32),
            jax.ShapeDtypeStruct((T, LANES), F32),
            jax.ShapeDtypeStruct((T, LANES), I32),
            jax.ShapeDtypeStruct((ROW_TILE, LANES), I32),
        ],
        scratch_shapes=[pltpu.VMEM((ROW_TILE, LANES), F32)],
        compiler_params=_cparams(("arbitrary",)),
        name="mix",
    )(x2, attn, w2, b2, lng, lnb, ws, bs_tile, wa, wb, wo, bo, g1, b1, wr, br)


PLAN_EXPERT, PLAN_VALID, PLAN_SLOT, PLAN_NEXT, PLAN_USED = range(5)


def _lane_cumsum(x):
    lane = lax.broadcasted_iota(I32, x.shape, 1)
    shift = 1
    while shift < LANES:
        x = x + jnp.where(lane >= shift, pltpu.roll(x, shift, 1), 0.0)
        shift *= 2
    return x


def _route_kernel(cnt_ref, idx_ref, rank_ref, dest_ref, plan_ref, start_ref, *, tm, bm, nbp):
    @pl.when(pl.program_id(0) == 0)
    def _():
        lane = lax.broadcasted_iota(I32, (ROW_TILE, LANES), 1)
        counts = jnp.where(lane < N_EXPERTS, cnt_ref[...].astype(F32), 0.0)
        padded = jnp.floor((counts + (bm - 1)) / bm) * bm
        pad_end = _lane_cumsum(padded)
        pad_start = pad_end - padded
        start_ref[...] = pad_start
        has_rows = jnp.where(counts > 0, 1.0, 0.0)
        order = _lane_cumsum(has_rows) - 1.0
        slot = order - 2.0 * jnp.floor(order * 0.5)
        total = jnp.sum(jnp.where(lane == N_EXPERTS - 1, pad_end, 0.0), axis=1, keepdims=True)[0:1, :]
        n_used = total / bm

        col = lambda r: jnp.transpose(r)[:, 0:1]
        sub = lax.broadcasted_iota(I32, (LANES, 1), 0)
        is_expert = sub < N_EXPERTS
        lane_w = lax.broadcasted_iota(I32, (LANES, LANES), 1)
        sub_w = lax.broadcasted_iota(I32, (LANES, LANES), 0)
        later = (lane_w > sub_w) & (lane_w < N_EXPERTS) & (has_rows[0:1, :] > 0)
        succ = jnp.min(jnp.where(later, lane_w, N_EXPERTS).astype(F32), axis=1, keepdims=True)
        succ = jnp.where(succ == N_EXPERTS, sub.astype(F32), succ)

        blk = lax.broadcasted_iota(I32, (1, nbp), 1).astype(F32)
        blk_src = jnp.minimum(blk, n_used - 1.0)
        blk_row = blk_src * bm
        below = is_expert & (col(pad_end) <= blk_row)
        blk_e = jnp.minimum(jnp.sum(jnp.where(below, 1.0, 0.0), axis=0, keepdims=True), N_EXPERTS - 1.0)
        hit = sub.astype(F32) == blk_e
        take = lambda c: jnp.sum(jnp.where(hit, c, 0.0), axis=0, keepdims=True)
        blk_valid = jnp.clip(take(col(pad_start + counts)) - blk_row, 0.0, bm)
        rows = [None] * ROW_TILE
        rows[PLAN_EXPERT], rows[PLAN_VALID] = blk_e, blk_valid
        rows[PLAN_SLOT], rows[PLAN_NEXT] = take(col(slot)), take(succ)
        rows[PLAN_USED] = jnp.broadcast_to(n_used, (1, nbp))
        zero = jnp.zeros((1, nbp), F32)
        plan_ref[...] = jnp.concatenate([zero if r is None else r for r in rows], axis=0).astype(I32)

    pad_start = start_ref[0:1, :]
    lane_i = lax.broadcasted_iota(I32, (tm, LANES), 1)
    lane_f = lane_i.astype(F32)
    idx = idx_ref[...].astype(F32)
    rank = rank_ref[...].astype(F32)
    dest = jnp.zeros((tm, LANES), F32)
    for k in range(TOP_K):
        base = jnp.sum(jnp.where(lane_f == idx[:, k:k + 1], pad_start, 0.0), axis=1, keepdims=True)
        dest = jnp.where(lane_i == k, base + rank[:, k:k + 1], dest)
    dest_ref[...] = jnp.transpose(dest)[:ROW_TILE, :].astype(I32)


def _route(cnt_o, idx_o, rank_o, bm, n_blk):
    Tp = idx_o.shape[0]
    tm = math.gcd(Tp, ROUTE_TILE)
    nbp = -(-n_blk // LANES) * LANES
    return pl.pallas_call(
        functools.partial(_route_kernel, tm=tm, bm=bm, nbp=nbp),
        grid=(Tp // tm,),
        in_specs=[
            pl.BlockSpec((ROW_TILE, LANES), lambda i: (0, 0)),
            pl.BlockSpec((tm, LANES), lambda i: (i, 0)),
            pl.BlockSpec((tm, LANES), lambda i: (i, 0)),
        ],
        out_specs=[
            pl.BlockSpec((ROW_TILE, tm), lambda i: (0, i)),
            pl.BlockSpec((ROW_TILE, nbp), lambda i: (0, 0)),
        ],
        out_shape=[
            jax.ShapeDtypeStruct((ROW_TILE, Tp), I32),
            jax.ShapeDtypeStruct((ROW_TILE, nbp), I32),
        ],
        scratch_shapes=[pltpu.VMEM((ROW_TILE, LANES), F32)],
        compiler_params=_cparams(("arbitrary",)),
        name="route",
    )(cnt_o, idx_o, rank_o)


def _sc_window_indices(idx):
    return jnp.pad(idx.reshape(-1, SC_WINDOW), ((0, 0), (0, LANES - SC_WINDOW)))


def _sc_scatter_rows(dest_km, src3, n_dst):
    n_slot, n_tok = dest_km.shape
    n_win = n_tok // SC_WINDOW
    mesh = plsc.VectorSubcoreMesh(core_axis_name="core", subcore_axis_name="subcore",
                                  num_cores=SC_CORES, num_subcores=SC_SUBCORES)

    @pl.kernel(out_type=jax.ShapeDtypeStruct((n_dst, ROW_TILE, LANES), src3.dtype), mesh=mesh, name="sc_scatter")
    def scatter(src_hbm, i_hbm, o_hbm):
        def body(x_vmem, i_vmem):
            for k in range(n_slot):
                pltpu.sync_copy(x_vmem, o_hbm.at[i_vmem.at[k, pl.ds(0, SC_WINDOW)]])

        @pl.when(lax.axis_index("core") == 0)
        def _():
            pltpu.emit_pipeline(
                body,
                grid=(n_win,),
                in_specs=[pl.BlockSpec((SC_WINDOW, ROW_TILE, LANES), lambda i: (i, 0, 0)),
                          pl.BlockSpec((ROW_TILE, LANES), lambda i: (i, 0))],
                out_specs=[],
                core_axis_name="subcore",
                dimension_semantics=(pltpu.PARALLEL,),
            )(src_hbm, i_hbm)

    idx = dest_km.reshape(n_slot, n_win, SC_WINDOW).transpose(1, 0, 2)
    idx = jnp.pad(idx, ((0, 0), (0, ROW_TILE - n_slot), (0, LANES - SC_WINDOW)))
    return scatter(src3, idx.reshape(n_win * ROW_TILE, LANES))


def _sc_gather_rows(sidx, src3):
    n = sidx.shape[0]
    mesh = plsc.VectorSubcoreMesh(core_axis_name="core", subcore_axis_name="subcore",
                                  num_cores=SC_CORES, num_subcores=SC_SUBCORES)

    @pl.kernel(out_type=jax.ShapeDtypeStruct((n, ROW_TILE, LANES), src3.dtype), mesh=mesh, name="sc_gather")
    def gather(src_hbm, i_hbm, o_hbm):
        def body(i_vmem, o_vmem):
            pltpu.sync_copy(src_hbm.at[i_vmem.at[0, pl.ds(0, SC_WINDOW)]], o_vmem)

        pltpu.emit_pipeline(
            body,
            grid=(n // SC_WINDOW,),
            in_specs=[pl.BlockSpec((1, LANES), lambda i: (i, 0))],
            out_specs=[pl.BlockSpec((SC_WINDOW, ROW_TILE, LANES), lambda i: (i, 0, 0))],
            core_axis_name=("core", "subcore"),
            dimension_semantics=(pltpu.PARALLEL,),
        )(i_hbm, o_hbm)

    return gather(src3, _sc_window_indices(sidx))


def _moe_kernel(plan_ref, x_hbm, wgu_hbm, bgu_ref, wdn_hbm, bdn_ref, y_ref,
                x_buf, wgu_f32, wdn_f32, wgu_bf, wdn_bf, x_sems, sems):
    i = pl.program_id(0)
    n_used = plan_ref[PLAN_USED, i]
    expert = plan_ref[PLAN_EXPERT, i]
    active = i < n_used
    fresh = (i == 0) | (expert != plan_ref[PLAN_EXPERT, jnp.maximum(i - 1, 0)])
    slot = plan_ref[PLAN_SLOT, i]
    successor = plan_ref[PLAN_NEXT, i]
    rows = x_buf.shape[1]
    depth = x_buf.shape[0]

    def x_copy(blk):
        s = blk % depth
        return pltpu.make_async_copy(x_hbm.at[pl.ds(pl.multiple_of(blk * rows, rows), rows)],
                                     x_buf.at[s], x_sems.at[s])

    def weight_copies(e, s):
        return (pltpu.make_async_copy(wgu_hbm.at[e], wgu_f32.at[s], sems.at[s, 0]),
                pltpu.make_async_copy(wdn_hbm.at[e], wdn_f32.at[s], sems.at[s, 1]))

    @pl.when(i == 0)
    def _():
        for ahead in range(depth - 1):
            @pl.when(ahead < n_used)
            def _():
                x_copy(ahead).start()

    @pl.when(i + (depth - 1) < n_used)
    def _():
        x_copy(i + (depth - 1)).start()

    @pl.when(i == 0)
    def _():
        for c in weight_copies(expert, slot):
            c.start()

    @pl.when(active & fresh)
    def _():
        @pl.when(successor != expert)
        def _():
            for c in weight_copies(successor, 1 - slot):
                c.start()

        for c in weight_copies(expert, slot):
            c.wait()
        wgu_bf[...] = wgu_f32[slot].astype(BF16)
        wdn_bf[...] = wdn_f32[slot].astype(BF16)

    @pl.when(active)
    def _():
        bm = rows // ROW_TILE
        x_copy(i).wait()
        x = _load_token_rows(x_buf.at[i % depth], bm)
        valid = lax.broadcasted_iota(I32, (bm, 1), 0) < plan_ref[PLAN_VALID, i]
        x = jnp.where(valid, x, 0.0).astype(BF16)
        gu = _dot(x, wgu_bf[...]) + bgu_ref[...]
        gate = jnp.minimum(gu[:, :D_FF], SWIGLU_LIMIT)
        up = jnp.clip(gu[:, D_FF:], -SWIGLU_LIMIT, SWIGLU_LIMIT)
        hid = (up + 1.0) * (gate * jax.nn.sigmoid(SWIGLU_ALPHA * gate))
        _store_token_rows(y_ref, _dot(hid.astype(BF16), wdn_bf[...]) + bdn_ref[...])

    @pl.when(jnp.logical_not(active))
    def _():
        y_ref[...] = jnp.zeros_like(y_ref)


def _moe(plan, xpad, wgu, bgu, wdn, bdn, n_blk, bm):
    bias = lambda i, plan: (plan[PLAN_EXPERT, i], 0, 0)
    grid_spec = pltpu.PrefetchScalarGridSpec(
        num_scalar_prefetch=1,
        grid=(n_blk,),
        in_specs=[
            pl.BlockSpec(memory_space=pl.ANY),
            pl.BlockSpec(memory_space=pl.ANY),
            pl.BlockSpec((None, 1, 2 * D_FF), bias),
            pl.BlockSpec(memory_space=pl.ANY),
            pl.BlockSpec((None, 1, D_MODEL), bias),
        ],
        out_specs=pl.BlockSpec((bm * ROW_TILE, LANES), lambda i, plan: (i, 0)),
        scratch_shapes=[
            pltpu.VMEM((MOE_X_DEPTH, bm * ROW_TILE, LANES), F32),
            pltpu.VMEM((2, D_MODEL, 2 * D_FF), F32), pltpu.VMEM((2, D_FF, D_MODEL), F32),
            pltpu.VMEM((D_MODEL, 2 * D_FF), BF16), pltpu.VMEM((D_FF, D_MODEL), BF16),
            pltpu.SemaphoreType.DMA((MOE_X_DEPTH,)), pltpu.SemaphoreType.DMA((2, 2)),
        ],
    )
    return pl.pallas_call(
        _moe_kernel,
        grid_spec=grid_spec,
        out_shape=jax.ShapeDtypeStruct((n_blk * bm * ROW_TILE, LANES), F32),
        compiler_params=_cparams(("arbitrary",)),
        name="moe",
    )(plan, xpad, wgu, bgu, wdn, bdn)


def _final_kernel(h_ref, yg_ref, gate_ref, p_ref, wple_ref, wpg_ref, bpg_ref,
                  g2_ref, b2_ref, g3_ref, b3_ref, *rest, tm, alpha):
    o_ref = rest[-1]
    h = _load_token_rows(h_ref, tm)
    gates = gate_ref[...]
    ffn = gates[:, 0:1] * _load_token_rows(yg_ref.at[0], tm)
    for k in range(1, TOP_K):
        ffn = ffn + gates[:, k:k + 1] * _load_token_rows(yg_ref.at[k], tm)
    h2 = _layer_norm(alpha * h + ffn, g2_ref[...], b2_ref[...])
    emb = _dot(p_ref[...].astype(BF16), wple_ref[...])
    pg = jax.nn.sigmoid(_dot(h2.astype(BF16), wpg_ref[...]) + bpg_ref[...])
    o_ref[...] = _layer_norm(alpha * h2 + emb * pg, g3_ref[...], b3_ref[...])


def _final(h1, yg, gates, p2, wple, wpg, bpg, g2, b2, g3, b3, tm, alpha, tile_off, out_prev):
    T = h1.shape[0] // ROW_TILE
    const = lambda *shape: pl.BlockSpec(shape, lambda i: (0,) * len(shape))
    in_specs = [
        pl.BlockSpec((tm * ROW_TILE, LANES), lambda i: (i, 0)),
        pl.BlockSpec((TOP_K, tm * ROW_TILE, LANES), lambda i: (0, i, 0)),
        pl.BlockSpec((tm, LANES), lambda i: (i, 0)),
        pl.BlockSpec((tm, PLE_DIM), lambda i: (i + tile_off, 0)),
        const(PLE_DIM, D_MODEL), const(D_MODEL, D_MODEL), const(1, D_MODEL),
        const(1, D_MODEL), const(1, D_MODEL), const(1, D_MODEL), const(1, D_MODEL),
    ]
    args = [h1, yg, gates, p2, wple, wpg, bpg, g2, b2, g3, b3]
    aliases = {}
    if out_prev is not None:
        in_specs.append(pl.BlockSpec(memory_space=pl.ANY))
        aliases = {len(args): 0}
        args.append(out_prev)
    return pl.pallas_call(
        functools.partial(_final_kernel, tm=tm, alpha=alpha),
        grid=(T // tm,),
        in_specs=in_specs,
        out_specs=pl.BlockSpec((tm, D_MODEL), lambda i: (i + tile_off, 0)),
        out_shape=jax.ShapeDtypeStruct((p2.shape[0], D_MODEL), F32),
        input_output_aliases=aliases,
        compiler_params=_cparams(("arbitrary",)),
        name="final",
    )(*args)


def _layer(h2d, p2d, batch, seq, alpha, w_in, b_in, gmlp_ln_g, gmlp_ln_b, w_spatial, b_spatial,
           w_branch_a, w_branch_b, w_out, b_out, ln1_g, ln1_b, w_router, b_router,
           w_gate_up, b_gate_up, w_down, b_down, ln2_g, ln2_b, w_ple, w_ple_gate, b_ple_gate,
           ln3_g, ln3_b):
    T = batch * seq
    tm = math.gcd(T, ROW_TILE_A)
    bm = MOE_BLOCK
    off_f = 3 * FOX_WIDTH
    off_u = off_f + N_HEADS
    row = lambda v: v.reshape(1, -1).astype(F32)

    w1 = jnp.concatenate([w_in[:, :off_u], jnp.zeros((D_MODEL, LANES - N_HEADS), F32)], axis=1).astype(BF16)
    b1 = jnp.concatenate([b_in[:off_u], jnp.zeros((LANES - N_HEADS,), F32)]).reshape(1, -1)
    w2 = w_in[:, off_u:].astype(BF16)
    b2 = row(b_in[off_u:])
    bs_tile = jnp.repeat(b_spatial.T, HEAD_DIM, axis=1)
    wr = jnp.concatenate([w_router, jnp.zeros((D_MODEL, LANES - N_EXPERTS), F32)], axis=1)
    br = jnp.concatenate([b_router, jnp.full((LANES - N_EXPERTS,), -1e30, F32)]).reshape(1, -1)

    wsb, wab, wbb, wob = (w.astype(BF16) for w in (w_spatial, w_branch_a, w_branch_b, w_out))
    wpleb, wpgb = w_ple.astype(BF16), w_ple_gate.astype(BF16)
    bgu, bdn = b_gate_up.reshape(N_EXPERTS, 1, -1), b_down.reshape(N_EXPERTS, 1, -1)

    n_parts = N_PARTS if batch % N_PARTS == 0 else 1
    pb = batch // n_parts
    Tp = pb * seq
    tiles = Tp // tm
    n_assign = Tp * TOP_K
    n_blk = -(-n_assign // bm) + N_EXPERTS
    n_rows = n_blk * bm

    def front(part):
        q, k, v, f_pad = _qkvf(h2d, w1, b1, tm, Tp, part * tiles)
        auxq, auxk = _decay(f_pad, pb, seq)
        attn = _attention(q, auxq, k, auxk, v, pb, seq)
        h1, idx_o, gate_o, rank_o, cnt_o = _mix(
            h2d, attn, w2, b2, row(gmlp_ln_g), row(gmlp_ln_b), wsb, bs_tile, wab, wbb, wob, row(b_out),
            row(ln1_g), row(ln1_b), wr, br, tm, alpha, part * tiles)
        dest8, plan = _route(cnt_o, idx_o, rank_o, bm, n_blk)
        dest_km = dest8[:TOP_K]
        xpad = _sc_scatter_rows(dest_km, h1.reshape(Tp, ROW_TILE, LANES), n_rows)
        return h1, gate_o, dest_km, plan, xpad

    def experts(state):
        h1, gate_o, dest_km, plan, xpad = state
        ypad = _moe(plan, xpad.reshape(n_rows * ROW_TILE, LANES), w_gate_up, bgu, w_down, bdn, n_blk, bm)
        yg = _sc_gather_rows(dest_km.reshape(-1), ypad.reshape(n_rows, ROW_TILE, LANES))
        return h1, gate_o, yg

    def back(part, state, out_prev):
        h1, gate_o, yg = state
        return _final(h1, yg.reshape(TOP_K, Tp * ROW_TILE, LANES), gate_o, p2d, wpleb, wpgb,
                      row(b_ple_gate), row(ln2_g), row(ln2_b), row(ln3_g), row(ln3_b), tm, alpha,
                      part * tiles, out_prev)

    fronts = [front(part) for part in range(n_parts)]
    mids = [experts(state) for state in fronts]
    out = None
    for part in range(n_parts):
        out = back(part, mids[part], out)
    return out


def kernel(x, p, w_in, b_in, gmlp_ln_g, gmlp_ln_b, w_spatial, b_spatial, w_branch_a, w_branch_b, w_out, b_out, ln1_g, ln1_b, w_router, b_router, w_gate_up, b_gate_up, w_down, b_down, ln2_g, ln2_b, w_ple, w_ple_gate, b_ple_gate, ln3_g, ln3_b):
    batch, seq, d = x.shape
    depth = w_in.shape[0]
    assert d == D_MODEL and seq % GMLP_CHUNK == 0
    alpha = (2.0 * depth) ** 0.25
    h = x.reshape(batch * seq, d)
    for i in range(depth):
        h = _layer(h, p[i].reshape(batch * seq, PLE_DIM), batch, seq, alpha,
                   w_in[i], b_in[i], gmlp_ln_g[i], gmlp_ln_b[i], w_spatial[i], b_spatial[i],
                   w_branch_a[i], w_branch_b[i], w_out[i], b_out[i], ln1_g[i], ln1_b[i],
                   w_router[i], b_router[i], w_gate_up[i], b_gate_up[i], w_down[i], b_down[i],
                   ln2_g[i], ln2_b[i], w_ple[i], w_ple_gate[i], b_ple_gate[i], ln3_g[i], ln3_b[i])
    return h.reshape(batch, seq, d)
```

```python
import functools
import math

import jax
import jax.numpy as jnp
import numpy as np
from jax import lax
from jax.experimental import pallas as pl
from jax.experimental.pallas import tpu as pltpu
from jax.experimental.pallas import tpu_sc as plsc

F32 = jnp.float32
BF16 = jnp.bfloat16
I32 = jnp.int32

D_MODEL = 1024
N_HEADS = 8
HEAD_DIM = 64
FOX_WIDTH = N_HEADS * HEAD_DIM
GMLP_WIDTH = 512
GMLP_CHUNK = 128
N_EXPERTS = 32
TOP_K = 4
D_FF = 1024
PLE_DIM = 256
SWIGLU_LIMIT = 7.0
SWIGLU_ALPHA = 1.702
LN_EPS = 1e-5
LANES = 128
ROW_TILE = 8
VMEM_LIMIT = 56 * 1024 * 1024

MOE_BLOCK = 256
MOE_X_DEPTH = 4
ATTN_Q_TILE = 512
ATTN_K_TILE = 512
ATTN_PAIRS = 2
LOG2E = math.log2(math.e)
ROW_TILE_A = 512
SC_CORES = 2
SC_SUBCORES = 16
SC_WINDOW = 32
N_PARTS = 2
ROUTE_TILE = 1024


def _cparams(sem):
    return pltpu.CompilerParams(dimension_semantics=sem, vmem_limit_bytes=VMEM_LIMIT)


def _gelu(x):
    c = math.sqrt(2.0 / math.pi)
    return 0.5 * x * (1.0 + jnp.tanh(c * (x + 0.044715 * (x * x * x))))


def _layer_norm(x, g, b):
    mu = jnp.mean(x, axis=-1, keepdims=True)
    xc = x - mu
    var = jnp.mean(xc * xc, axis=-1, keepdims=True)
    return xc * lax.rsqrt(var + LN_EPS) * g + b


def _split3(x):
    hi = x.astype(BF16)
    r = x - hi.astype(F32)
    mid = r.astype(BF16)
    lo = (r - mid.astype(F32)).astype(BF16)
    return hi, mid, lo


def _dot(a, b):
    return jnp.dot(a, b, preferred_element_type=F32)


def _load_token_rows(ref, n):
    return jnp.concatenate([ref[pl.ds(j, n, stride=ROW_TILE), :] for j in range(ROW_TILE)], axis=1)


def _store_token_rows(ref, val):
    n = val.shape[0]
    for j in range(ROW_TILE):
        ref[pl.ds(j, n, stride=ROW_TILE), :] = val[:, j * LANES:(j + 1) * LANES]


def _qkvf_kernel(x_ref, w_ref, b_ref, q_ref, k_ref, v_ref, f_ref):
    x = x_ref[...].astype(BF16)
    proj = _dot(x, w_ref[...]) + b_ref[...]
    q_ref[...] = (proj[:, :FOX_WIDTH] * (HEAD_DIM ** -0.5 * LOG2E)).astype(BF16)
    k_ref[...] = proj[:, FOX_WIDTH:2 * FOX_WIDTH].astype(BF16)
    v_ref[...] = proj[:, 2 * FOX_WIDTH:3 * FOX_WIDTH].astype(BF16)
    f_ref[...] = proj[:, 3 * FOX_WIDTH:]


def _qkvf(x2, w, b, tm, T, tile_off):
    n_out = w.shape[1]
    return pl.pallas_call(
        _qkvf_kernel,
        grid=(T // tm,),
        in_specs=[
            pl.BlockSpec((tm, D_MODEL), lambda i: (i + tile_off, 0)),
            pl.BlockSpec((D_MODEL, n_out), lambda i: (0, 0)),
            pl.BlockSpec((1, n_out), lambda i: (0, 0)),
        ],
        out_specs=[
            pl.BlockSpec((tm, FOX_WIDTH), lambda i: (i, 0)),
            pl.BlockSpec((tm, FOX_WIDTH), lambda i: (i, 0)),
            pl.BlockSpec((tm, FOX_WIDTH), lambda i: (i, 0)),
            pl.BlockSpec((tm, LANES), lambda i: (i, 0)),
        ],
        out_shape=[
            jax.ShapeDtypeStruct((T, FOX_WIDTH), BF16),
            jax.ShapeDtypeStruct((T, FOX_WIDTH), BF16),
            jax.ShapeDtypeStruct((T, FOX_WIDTH), BF16),
            jax.ShapeDtypeStruct((T, LANES), F32),
        ],
        compiler_params=_cparams(("arbitrary",)),
        name="qkvf",
    )(x2, w, b)


def _decay_placement():
    pq = np.zeros((3 * LANES, FOX_WIDTH), np.float32)
    pk = np.zeros((3 * LANES, FOX_WIDTH), np.float32)
    cq = np.zeros((1, FOX_WIDTH), np.float32)
    ck = np.zeros((1, FOX_WIDTH), np.float32)
    for h in range(N_HEADS):
        base = (h // 2) * LANES + (HEAD_DIM if h % 2 == 0 else 0)
        for piece in range(3):
            pq[piece * LANES + h, base + piece] = 1.0
            pk[piece * LANES + h, base + 3 + piece] = -1.0
            cq[0, base + 3 + piece] = 1.0
            ck[0, base + piece] = 1.0
    return pq, pk, cq, ck


def _decay_kernel(f_ref, pq_ref, pk_ref, cq_ref, ck_ref, auxq_ref, auxk_ref, *, seq, blk):
    r = lax.broadcasted_iota(I32, (blk, blk), 0)
    c = lax.broadcasted_iota(I32, (blk, blk), 1)
    tri = jnp.where(c <= r, 1.0, 0.0).astype(BF16)
    carry = jnp.zeros((1, LANES), F32)
    for i in range(seq // blk):
        f = f_ref[i * blk:(i + 1) * blk, :]
        ls = jnp.minimum(f, 0.0) - jnp.log1p(jnp.exp(-jnp.abs(f)))
        hi, mid, lo = _split3(ls)
        cs = _dot(tri, hi) + _dot(tri, mid) + _dot(tri, lo) + carry
        carry = cs[blk - 1:blk, :]
        pieces = jnp.concatenate(_split3(cs * LOG2E), axis=1)
        auxq_ref[i * blk:(i + 1) * blk, :] = (_dot(pieces, pq_ref[...]) + cq_ref[...]).astype(BF16)
        auxk_ref[i * blk:(i + 1) * blk, :] = (_dot(pieces, pk_ref[...]) + ck_ref[...]).astype(BF16)


def _decay(f_pad, batch, seq):
    blk = 256 if seq % 256 == 0 else LANES
    pq, pk, cq, ck = _decay_placement()
    const = lambda *shape: pl.BlockSpec(shape, lambda b: (0,) * len(shape))
    return pl.pallas_call(
        functools.partial(_decay_kernel, seq=seq, blk=blk),
        grid=(batch,),
        in_specs=[
            pl.BlockSpec((seq, LANES), lambda b: (b, 0)),
            const(3 * LANES, FOX_WIDTH), const(3 * LANES, FOX_WIDTH), const(1, FOX_WIDTH), const(1, FOX_WIDTH),
        ],
        out_specs=[
            pl.BlockSpec((seq, FOX_WIDTH), lambda b: (b, 0)),
            pl.BlockSpec((seq, FOX_WIDTH), lambda b: (b, 0)),
        ],
        out_shape=[
            jax.ShapeDtypeStruct((batch * seq, FOX_WIDTH), BF16),
            jax.ShapeDtypeStruct((batch * seq, FOX_WIDTH), BF16),
        ],
        compiler_params=_cparams(("arbitrary",)),
        name="decay",
    )(f_pad, jnp.asarray(pq, BF16), jnp.asarray(pk, BF16), jnp.asarray(cq), jnp.asarray(ck))


def _attn_kernel(q_ref, auxq_ref, k_ref, auxk_ref, v_ref, o_ref, *, tq, tk):
    qi = pl.program_id(2)
    n_pairs = q_ref.shape[1] // LANES
    low_q = lax.broadcasted_iota(I32, (tq, LANES), 1) < HEAD_DIM
    qs = []
    for pr in range(n_pairs):
        q = q_ref[:, pr * LANES:(pr + 1) * LANES]
        aq = auxq_ref[:, pr * LANES:(pr + 1) * LANES]
        qs += [jnp.where(low_q, q, aq), jnp.where(low_q, aq, q)]

    def step(start, n, carry, masked):
        low_k = lax.broadcasted_iota(I32, (n, LANES), 1) < HEAD_DIM
        ones = jnp.ones((n, LANES), BF16)
        out = []
        for pr in range(n_pairs):
            lanes = slice(pr * LANES, (pr + 1) * LANES)
            kb = k_ref[pl.ds(start, n), lanes]
            ak = auxk_ref[pl.ds(start, n), lanes]
            vb = v_ref[pl.ds(start, n), lanes]
            ks = (jnp.where(low_k, kb, ak), jnp.where(low_k, ak, kb))
            vs = (jnp.where(low_k, vb, ones), jnp.where(low_k, ones, vb))
            for j in range(2):
                m, acc = carry[2 * pr + j]
                s = lax.dot_general(qs[2 * pr + j], ks[j], (((1,), (1,)), ((), ())),
                                    preferred_element_type=F32)
                if masked:
                    row = lax.broadcasted_iota(I32, (tq, n), 0)
                    col = lax.broadcasted_iota(I32, (tq, n), 1)
                    s = jnp.where(col + (start - qi * tq) <= row, s, -jnp.inf)
                m_new = jnp.maximum(m, jnp.max(s, axis=1, keepdims=True))
                p = jnp.exp2(s - m_new)
                acc = jnp.exp2(m - m_new) * acc + _dot(p.astype(BF16), vs[j])
                out.append((m_new, acc))
        return tuple(out)

    init = tuple((jnp.full((tq, 1), -jnp.inf, F32), jnp.zeros((tq, LANES), F32))
                 for _ in range(2 * n_pairs))
    carry = lax.fori_loop(0, qi, lambda t, c: step(pl.multiple_of(t * tq, tq), tq, c, False), init)
    for d in range(tq // tk):
        carry = step(pl.multiple_of(qi * tq + d * tk, tk), tk, carry, True)
    for pr in range(n_pairs):
        acc0, acc1 = carry[2 * pr][1], carry[2 * pr + 1][1]
        out0 = acc0 / acc0[:, HEAD_DIM:HEAD_DIM + 1]
        out1 = acc1 / acc1[:, 0:1]
        o_ref[:, pr * LANES:(pr + 1) * LANES] = jnp.where(low_q, out0, out1).astype(BF16)


def _attention(q, auxq, k, auxk, v, batch, seq):
    tq = math.gcd(seq, ATTN_Q_TILE)
    tk = math.gcd(tq, ATTN_K_TILE)
    nq = seq // tq
    T = batch * seq
    width = ATTN_PAIRS * LANES
    q_spec = pl.BlockSpec((tq, width), lambda b, hp, qi: (b * nq + qi, hp))
    kv_spec = pl.BlockSpec((seq, width), lambda b, hp, qi: (b, hp))
    return pl.pallas_call(
        functools.partial(_attn_kernel, tq=tq, tk=tk),
        grid=(batch, N_HEADS // (2 * ATTN_PAIRS), nq),
        in_specs=[q_spec, q_spec, kv_spec, kv_spec, kv_spec],
        out_specs=q_spec,
        out_shape=jax.ShapeDtypeStruct((T, FOX_WIDTH), BF16),
        compiler_params=_cparams(("arbitrary", "arbitrary", "arbitrary")),
        name="attn",
    )(q, auxq, k, auxk, v)


def _mix_kernel(x_ref, attn_ref, w2_ref, b2_ref, lng_ref, lnb_ref, ws_ref, bs_ref,
                wa_ref, wb_ref, wo_ref, bo_ref, g1_ref, b1_ref, wr_ref, br_ref,
                h_ref, idx_ref, gate_ref, rank_ref, cnt_ref, carry_ref, *, tm, alpha):
    i = pl.program_id(0)

    @pl.when(i == 0)
    def _():
        carry_ref[...] = jnp.zeros_like(carry_ref)

    x = x_ref[...]
    proj = _dot(x.astype(BF16), w2_ref[...]) + b2_ref[...]
    u = _gelu(proj[:, :GMLP_WIDTH])
    gv = _gelu(proj[:, GMLP_WIDTH:2 * GMLP_WIDTH])
    vln = _layer_norm(gv, lng_ref[...], lnb_ref[...]).astype(BF16)

    cr = lax.broadcasted_iota(I32, (GMLP_CHUNK, GMLP_CHUNK), 0)
    cc = lax.broadcasted_iota(I32, (GMLP_CHUNK, GMLP_CHUNK), 1)
    tril = cc <= cr
    lo_half = cc < HEAD_DIM
    zero_w = jnp.zeros((GMLP_CHUNK, GMLP_CHUNK), BF16)
    n_slab = GMLP_WIDTH // LANES
    lhs = []
    for s in range(n_slab):
        w0 = jnp.where(tril, ws_ref[2 * s], zero_w)
        w1 = jnp.where(tril, ws_ref[2 * s + 1], zero_w)
        lhs.append(jnp.concatenate([w0, w1], axis=1))
    bs = bs_ref[...]
    rows = []
    for c in range(tm // GMLP_CHUNK):
        cols = []
        for s in range(n_slab):
            vs = vln[c * GMLP_CHUNK:(c + 1) * GMLP_CHUNK, s * LANES:(s + 1) * LANES]
            rhs = jnp.concatenate([jnp.where(lo_half, vs, zero_w), jnp.where(lo_half, zero_w, vs)], axis=0)
            cols.append(_dot(lhs[s], rhs))
        rows.append(jnp.concatenate(cols, axis=1) + bs)
    sp = jnp.concatenate(rows, axis=0) if len(rows) > 1 else rows[0]
    sgu = (u * sp).astype(BF16)

    ga = jax.nn.sigmoid(proj[:, 2 * GMLP_WIDTH:2 * GMLP_WIDTH + D_MODEL])
    gb = jax.nn.sigmoid(proj[:, 2 * GMLP_WIDTH + D_MODEL:])
    merged = ga * _dot(attn_ref[...], wa_ref[...]) + gb * _dot(sgu, wb_ref[...])
    mix = _dot(merged.astype(BF16), wo_ref[...]) + bo_ref[...]
    h = _layer_norm(alpha * x + mix, g1_ref[...], b1_ref[...])
    _store_token_rows(h_ref, h)

    a_hi = h.astype(BF16)
    a_lo = (h - a_hi.astype(F32)).astype(BF16)
    wr = wr_ref[...]
    w_hi = wr.astype(BF16)
    w_lo = (wr - w_hi.astype(F32)).astype(BF16)
    w_cat = jnp.concatenate([w_hi, w_lo], axis=1)
    r_hi = _dot(a_hi, w_cat)
    r_lo = _dot(a_lo, w_cat)
    logits = (r_hi[:, :LANES] + r_hi[:, LANES:]) + (r_lo[:, :LANES] + r_lo[:, LANES:]) + br_ref[...]

    lane_i = lax.broadcasted_iota(I32, (tm, LANES), 1)
    lane_f = lane_i.astype(F32)
    vals, idxs = [], []
    l = logits
    for _ in range(TOP_K):
        m = jnp.max(l, axis=1, keepdims=True)
        ix = jnp.min(jnp.where(l == m, lane_f, float(LANES)), axis=1, keepdims=True)
        vals.append(m)
        idxs.append(ix)
        l = jnp.where(lane_f == ix, -jnp.inf, l)
    es = [jnp.exp(v - vals[0]) for v in vals]
    den = es[0] + es[1] + es[2] + es[3]

    onehot = jnp.zeros((tm, LANES), F32)
    idx_out = jnp.zeros((tm, LANES), F32)
    gate_out = jnp.zeros((tm, LANES), F32)
    for k in range(TOP_K):
        onehot = onehot + jnp.where(lane_f == idxs[k], 1.0, 0.0)
        idx_out = jnp.where(lane_i == k, idxs[k], idx_out)
        gate_out = jnp.where(lane_i == k, es[k] / den, gate_out)

    tr = lax.broadcasted_iota(I32, (tm, tm), 0)
    tc = lax.broadcasted_iota(I32, (tm, tm), 1)
    strict = jnp.where(tc < tr, 1.0, 0.0).astype(BF16)
    carry = carry_ref[0:1, :]
    before = _dot(strict, onehot.astype(BF16)) + carry
    rank_out = jnp.zeros((tm, LANES), F32)
    for k in range(TOP_K):
        rk = jnp.sum(jnp.where(lane_f == idxs[k], before, 0.0), axis=1, keepdims=True)
        rank_out = jnp.where(lane_i == k, rk, rank_out)
    new_carry = carry + jnp.sum(onehot, axis=0, keepdims=True)
    carry_ref[...] = jnp.broadcast_to(new_carry, carry_ref.shape)
    cnt_ref[...] = jnp.broadcast_to(new_carry, cnt_ref.shape).astype(I32)
    idx_ref[...] = idx_out.astype(I32)
    gate_ref[...] = gate_out
    rank_ref[...] = rank_out.astype(I32)


def _mix(x2, attn, w2, b2, lng, lnb, ws, bs_tile, wa, wb, wo, bo, g1, b1, wr, br, tm, alpha, tile_off):
    T = attn.shape[0]
    n2 = w2.shape[1]
    const = lambda *shape: pl.BlockSpec(shape, lambda i: (0,) * len(shape))
    return pl.pallas_call(
        functools.partial(_mix_kernel, tm=tm, alpha=alpha),
        grid=(T // tm,),
        in_specs=[
            pl.BlockSpec((tm, D_MODEL), lambda i: (i + tile_off, 0)),
            pl.BlockSpec((tm, FOX_WIDTH), lambda i: (i, 0)),
            const(D_MODEL, n2), const(1, n2),
            const(1, GMLP_WIDTH), const(1, GMLP_WIDTH),
            const(GMLP_WIDTH // HEAD_DIM, GMLP_CHUNK, GMLP_CHUNK), const(GMLP_CHUNK, GMLP_WIDTH),
            const(FOX_WIDTH, D_MODEL), const(GMLP_WIDTH, D_MODEL),
            const(D_MODEL, D_MODEL), const(1, D_MODEL),
            const(1, D_MODEL), const(1, D_MODEL),
            const(D_MODEL, LANES), const(1, LANES),
        ],
        out_specs=[
            pl.BlockSpec((tm * ROW_TILE, LANES), lambda i: (i, 0)),
            pl.BlockSpec((tm, LANES), lambda i: (i, 0)),
            pl.BlockSpec((tm, LANES), lambda i: (i, 0)),
            pl.BlockSpec((tm, LANES), lambda i: (i, 0)),
            pl.BlockSpec((ROW_TILE, LANES), lambda i: (0, 0)),
        ],
        out_shape=[
            jax.ShapeDtypeStruct((T * ROW_TILE, LANES), F32),
            jax.ShapeDtypeStruct((T, LANES), I32),
            jax.ShapeDtypeStruct((T, LANES), F32),
            jax.ShapeDtypeStruct((T, LANES), I32),
            jax.ShapeDtypeStruct((ROW_TILE, LANES), I32),
        ],
        scratch_shapes=[pltpu.VMEM((ROW_TILE, LANES), F32)],
        compiler_params=_cparams(("arbitrary",)),
        name="mix",
    )(x2, attn, w2, b2, lng, lnb, ws, bs_tile, wa, wb, wo, bo, g1, b1, wr, br)


PLAN_EXPERT, PLAN_VALID, PLAN_SLOT, PLAN_NEXT, PLAN_USED = range(5)


def _lane_cumsum(x):
    lane = lax.broadcasted_iota(I32, x.shape, 1)
    shift = 1
    while shift < LANES:
        x = x + jnp.where(lane >= shift, pltpu.roll(x, shift, 1), 0.0)
        shift *= 2
    return x


def _route_kernel(cnt_ref, idx_ref, rank_ref, dest_ref, plan_ref, start_ref, *, tm, bm, nbp):
    @pl.when(pl.program_id(0) == 0)
    def _():
        lane = lax.broadcasted_iota(I32, (ROW_TILE, LANES), 1)
        counts = jnp.where(lane < N_EXPERTS, cnt_ref[...].astype(F32), 0.0)
        padded = jnp.floor((counts + (bm - 1)) / bm) * bm
        pad_end = _lane_cumsum(padded)
        pad_start = pad_end - padded
        start_ref[...] = pad_start
        has_rows = jnp.where(counts > 0, 1.0, 0.0)
        order = _lane_cumsum(has_rows) - 1.0
        slot = order - 2.0 * jnp.floor(order * 0.5)
        total = jnp.sum(jnp.where(lane == N_EXPERTS - 1, pad_end, 0.0), axis=1, keepdims=True)[0:1, :]
        n_used = total / bm

        col = lambda r: jnp.transpose(r)[:, 0:1]
        sub = lax.broadcasted_iota(I32, (LANES, 1), 0)
        is_expert = sub < N_EXPERTS
        lane_w = lax.broadcasted_iota(I32, (LANES, LANES), 1)
        sub_w = lax.broadcasted_iota(I32, (LANES, LANES), 0)
        later = (lane_w > sub_w) & (lane_w < N_EXPERTS) & (has_rows[0:1, :] > 0)
        succ = jnp.min(jnp.where(later, lane_w, N_EXPERTS).astype(F32), axis=1, keepdims=True)
        succ = jnp.where(succ == N_EXPERTS, sub.astype(F32), succ)

        blk = lax.broadcasted_iota(I32, (1, nbp), 1).astype(F32)
        blk_src = jnp.minimum(blk, n_used - 1.0)
        blk_row = blk_src * bm
        below = is_expert & (col(pad_end) <= blk_row)
        blk_e = jnp.minimum(jnp.sum(jnp.where(below, 1.0, 0.0), axis=0, keepdims=True), N_EXPERTS - 1.0)
        hit = sub.astype(F32) == blk_e
        take = lambda c: jnp.sum(jnp.where(hit, c, 0.0), axis=0, keepdims=True)
        blk_valid = jnp.clip(take(col(pad_start + counts)) - blk_row, 0.0, bm)
        rows = [None] * ROW_TILE
        rows[PLAN_EXPERT], rows[PLAN_VALID] = blk_e, blk_valid
        rows[PLAN_SLOT], rows[PLAN_NEXT] = take(col(slot)), take(succ)
        rows[PLAN_USED] = jnp.broadcast_to(n_used, (1, nbp))
        zero = jnp.zeros((1, nbp), F32)
        plan_ref[...] = jnp.concatenate([zero if r is None else r for r in rows], axis=0).astype(I32)

    pad_start = start_ref[0:1, :]
    lane_i = lax.broadcasted_iota(I32, (tm, LANES), 1)
    lane_f = lane_i.astype(F32)
    idx = idx_ref[...].astype(F32)
    rank = rank_ref[...].astype(F32)
    dest = jnp.zeros((tm, LANES), F32)
    for k in range(TOP_K):
        base = jnp.sum(jnp.where(lane_f == idx[:, k:k + 1], pad_start, 0.0), axis=1, keepdims=True)
        dest = jnp.where(lane_i == k, base + rank[:, k:k + 1], dest)
    dest_ref[...] = jnp.transpose(dest)[:ROW_TILE, :].astype(I32)


def _route(cnt_o, idx_o, rank_o, bm, n_blk):
    Tp = idx_o.shape[0]
    tm = math.gcd(Tp, ROUTE_TILE)
    nbp = -(-n_blk // LANES) * LANES
    return pl.pallas_call(
        functools.partial(_route_kernel, tm=tm, bm=bm, nbp=nbp),
        grid=(Tp // tm,),
        in_specs=[
            pl.BlockSpec((ROW_TILE, LANES), lambda i: (0, 0)),
            pl.BlockSpec((tm, LANES), lambda i: (i, 0)),
            pl.BlockSpec((tm, LANES), lambda i: (i, 0)),
        ],
        out_specs=[
            pl.BlockSpec((ROW_TILE, tm), lambda i: (0, i)),
            pl.BlockSpec((ROW_TILE, nbp), lambda i: (0, 0)),
        ],
        out_shape=[
            jax.ShapeDtypeStruct((ROW_TILE, Tp), I32),
            jax.ShapeDtypeStruct((ROW_TILE, nbp), I32),
        ],
        scratch_shapes=[pltpu.VMEM((ROW_TILE, LANES), F32)],
        compiler_params=_cparams(("arbitrary",)),
        name="route",
    )(cnt_o, idx_o, rank_o)


def _sc_window_indices(idx):
    return jnp.pad(idx.reshape(-1, SC_WINDOW), ((0, 0), (0, LANES - SC_WINDOW)))


def _sc_scatter_rows(dest_km, src3, n_dst):
    n_slot, n_tok = dest_km.shape
    n_win = n_tok // SC_WINDOW
    mesh = plsc.VectorSubcoreMesh(core_axis_name="core", subcore_axis_name="subcore",
                                  num_cores=SC_CORES, num_subcores=SC_SUBCORES)

    @pl.kernel(out_type=jax.ShapeDtypeStruct((n_dst, ROW_TILE, LANES), src3.dtype), mesh=mesh, name="sc_scatter")
    def scatter(src_hbm, i_hbm, o_hbm):
        def body(x_vmem, i_vmem):
            for k in range(n_slot):
                pltpu.sync_copy(x_vmem, o_hbm.at[i_vmem.at[k, pl.ds(0, SC_WINDOW)]])

        @pl.when(lax.axis_index("core") == 0)
        def _():
            pltpu.emit_pipeline(
                body,
                grid=(n_win,),
                in_specs=[pl.BlockSpec((SC_WINDOW, ROW_TILE, LANES), lambda i: (i, 0, 0)),
                          pl.BlockSpec((ROW_TILE, LANES), lambda i: (i, 0))],
                out_specs=[],
                core_axis_name="subcore",
                dimension_semantics=(pltpu.PARALLEL,),
            )(src_hbm, i_hbm)

    idx = dest_km.reshape(n_slot, n_win, SC_WINDOW).transpose(1, 0, 2)
    idx = jnp.pad(idx, ((0, 0), (0, ROW_TILE - n_slot), (0, LANES - SC_WINDOW)))
    return scatter(src3, idx.reshape(n_win * ROW_TILE, LANES))


def _sc_gather_rows(sidx, src3, both_cores):
    n = sidx.shape[0]
    mesh = plsc.VectorSubcoreMesh(core_axis_name="core", subcore_axis_name="subcore",
                                  num_cores=SC_CORES, num_subcores=SC_SUBCORES)

    @pl.kernel(out_type=jax.ShapeDtypeStruct((n, ROW_TILE, LANES), src3.dtype), mesh=mesh, name="sc_gather")
    def gather(src_hbm, i_hbm, o_hbm):
        def body(i_vmem, o_vmem):
            pltpu.sync_copy(src_hbm.at[i_vmem.at[0, pl.ds(0, SC_WINDOW)]], o_vmem)

        def run(axes):
            pltpu.emit_pipeline(
                body,
                grid=(n // SC_WINDOW,),
                in_specs=[pl.BlockSpec((1, LANES), lambda i: (i, 0))],
                out_specs=[pl.BlockSpec((SC_WINDOW, ROW_TILE, LANES), lambda i: (i, 0, 0))],
                core_axis_name=axes,
                dimension_semantics=(pltpu.PARALLEL,),
            )(i_hbm, o_hbm)

        if both_cores:
            run(("core", "subcore"))
        else:
            pl.when(lax.axis_index("core") == 0)(lambda: run("subcore"))

    return gather(src3, _sc_window_indices(sidx))


def _moe_kernel(plan_ref, x_hbm, wgu_hbm, bgu_ref, wdn_hbm, bdn_ref, y_ref,
                x_buf, wgu_f32, wdn_f32, wgu_bf, wdn_bf, x_sems, sems):
    i = pl.program_id(0)
    n_used = plan_ref[PLAN_USED, i]
    expert = plan_ref[PLAN_EXPERT, i]
    active = i < n_used
    fresh = (i == 0) | (expert != plan_ref[PLAN_EXPERT, jnp.maximum(i - 1, 0)])
    slot = plan_ref[PLAN_SLOT, i]
    successor = plan_ref[PLAN_NEXT, i]
    rows = x_buf.shape[1]
    depth = x_buf.shape[0]

    def x_copy(blk):
        s = blk % depth
        return pltpu.make_async_copy(x_hbm.at[pl.ds(pl.multiple_of(blk * rows, rows), rows)],
                                     x_buf.at[s], x_sems.at[s])

    def weight_copies(e, s):
        return (pltpu.make_async_copy(wgu_hbm.at[e], wgu_f32.at[s], sems.at[s, 0]),
                pltpu.make_async_copy(wdn_hbm.at[e], wdn_f32.at[s], sems.at[s, 1]))

    @pl.when(i == 0)
    def _():
        for ahead in range(depth - 1):
            @pl.when(ahead < n_used)
            def _():
                x_copy(ahead).start()

    @pl.when(i + (depth - 1) < n_used)
    def _():
        x_copy(i + (depth - 1)).start()

    @pl.when(i == 0)
    def _():
        for c in weight_copies(expert, slot):
            c.start()

    @pl.when(active & fresh)
    def _():
        @pl.when(successor != expert)
        def _():
            for c in weight_copies(successor, 1 - slot):
                c.start()

        for c in weight_copies(expert, slot):
            c.wait()
        wgu_bf[...] = wgu_f32[slot].astype(BF16)
        wdn_bf[...] = wdn_f32[slot].astype(BF16)

    @pl.when(active)
    def _():
        bm = rows // ROW_TILE
        x_copy(i).wait()
        x = _load_token_rows(x_buf.at[i % depth], bm)
        valid = lax.broadcasted_iota(I32, (bm, 1), 0) < plan_ref[PLAN_VALID, i]
        x = jnp.where(valid, x, 0.0).astype(BF16)
        gu = _dot(x, wgu_bf[...]) + bgu_ref[...]
        gate = jnp.minimum(gu[:, :D_FF], SWIGLU_LIMIT)
        up = jnp.clip(gu[:, D_FF:], -SWIGLU_LIMIT, SWIGLU_LIMIT)
        hid = (up + 1.0) * (gate * jax.nn.sigmoid(SWIGLU_ALPHA * gate))
        _store_token_rows(y_ref, _dot(hid.astype(BF16), wdn_bf[...]) + bdn_ref[...])

    @pl.when(jnp.logical_not(active))
    def _():
        y_ref[...] = jnp.zeros_like(y_ref)


def _moe(plan, xpad, wgu, bgu, wdn, bdn, n_blk, bm):
    bias = lambda i, plan: (plan[PLAN_EXPERT, i], 0, 0)
    grid_spec = pltpu.PrefetchScalarGridSpec(
        num_scalar_prefetch=1,
        grid=(n_blk,),
        in_specs=[
            pl.BlockSpec(memory_space=pl.ANY),
            pl.BlockSpec(memory_space=pl.ANY),
            pl.BlockSpec((None, 1, 2 * D_FF), bias),
            pl.BlockSpec(memory_space=pl.ANY),
            pl.BlockSpec((None, 1, D_MODEL), bias),
        ],
        out_specs=pl.BlockSpec((bm * ROW_TILE, LANES), lambda i, plan: (i, 0)),
        scratch_shapes=[
            pltpu.VMEM((MOE_X_DEPTH, bm * ROW_TILE, LANES), F32),
            pltpu.VMEM((2, D_MODEL, 2 * D_FF), F32), pltpu.VMEM((2, D_FF, D_MODEL), F32),
            pltpu.VMEM((D_MODEL, 2 * D_FF), BF16), pltpu.VMEM((D_FF, D_MODEL), BF16),
            pltpu.SemaphoreType.DMA((MOE_X_DEPTH,)), pltpu.SemaphoreType.DMA((2, 2)),
        ],
    )
    return pl.pallas_call(
        _moe_kernel,
        grid_spec=grid_spec,
        out_shape=jax.ShapeDtypeStruct((n_blk * bm * ROW_TILE, LANES), F32),
        compiler_params=_cparams(("arbitrary",)),
        name="moe",
    )(plan, xpad, wgu, bgu, wdn, bdn)


def _final_kernel(h_ref, yg_ref, gate_ref, p_ref, wple_ref, wpg_ref, bpg_ref,
                  g2_ref, b2_ref, g3_ref, b3_ref, *rest, tm, alpha):
    o_ref = rest[-1]
    h = _load_token_rows(h_ref, tm)
    gates = gate_ref[...]
    ffn = gates[:, 0:1] * _load_token_rows(yg_ref.at[0], tm)
    for k in range(1, TOP_K):
        ffn = ffn + gates[:, k:k + 1] * _load_token_rows(yg_ref.at[k], tm)
    h2 = _layer_norm(alpha * h + ffn, g2_ref[...], b2_ref[...])
    emb = _dot(p_ref[...].astype(BF16), wple_ref[...])
    pg = jax.nn.sigmoid(_dot(h2.astype(BF16), wpg_ref[...]) + bpg_ref[...])
    o_ref[...] = _layer_norm(alpha * h2 + emb * pg, g3_ref[...], b3_ref[...])


def _final(h1, yg, gates, p2, wple, wpg, bpg, g2, b2, g3, b3, tm, alpha, tile_off, out_prev):
    T = h1.shape[0] // ROW_TILE
    const = lambda *shape: pl.BlockSpec(shape, lambda i: (0,) * len(shape))
    in_specs = [
        pl.BlockSpec((tm * ROW_TILE, LANES), lambda i: (i, 0)),
        pl.BlockSpec((TOP_K, tm * ROW_TILE, LANES), lambda i: (0, i, 0)),
        pl.BlockSpec((tm, LANES), lambda i: (i, 0)),
        pl.BlockSpec((tm, PLE_DIM), lambda i: (i + tile_off, 0)),
        const(PLE_DIM, D_MODEL), const(D_MODEL, D_MODEL), const(1, D_MODEL),
        const(1, D_MODEL), const(1, D_MODEL), const(1, D_MODEL), const(1, D_MODEL),
    ]
    args = [h1, yg, gates, p2, wple, wpg, bpg, g2, b2, g3, b3]
    aliases = {}
    if out_prev is not None:
        in_specs.append(pl.BlockSpec(memory_space=pl.ANY))
        aliases = {len(args): 0}
        args.append(out_prev)
    return pl.pallas_call(
        functools.partial(_final_kernel, tm=tm, alpha=alpha),
        grid=(T // tm,),
        in_specs=in_specs,
        out_specs=pl.BlockSpec((tm, D_MODEL), lambda i: (i + tile_off, 0)),
        out_shape=jax.ShapeDtypeStruct((p2.shape[0], D_MODEL), F32),
        input_output_aliases=aliases,
        compiler_params=_cparams(("arbitrary",)),
        name="final",
    )(*args)


def _layer(h2d, p2d, batch, seq, alpha, w_in, b_in, gmlp_ln_g, gmlp_ln_b, w_spatial, b_spatial,
           w_branch_a, w_branch_b, w_out, b_out, ln1_g, ln1_b, w_router, b_router,
           w_gate_up, b_gate_up, w_down, b_down, ln2_g, ln2_b, w_ple, w_ple_gate, b_ple_gate,
           ln3_g, ln3_b):
    T = batch * seq
    tm = math.gcd(T, ROW_TILE_A)
    bm = MOE_BLOCK
    off_f = 3 * FOX_WIDTH
    off_u = off_f + N_HEADS
    row = lambda v: v.reshape(1, -1).astype(F32)

    w1 = jnp.concatenate([w_in[:, :off_u], jnp.zeros((D_MODEL, LANES - N_HEADS), F32)], axis=1).astype(BF16)
    b1 = jnp.concatenate([b_in[:off_u], jnp.zeros((LANES - N_HEADS,), F32)]).reshape(1, -1)
    w2 = w_in[:, off_u:].astype(BF16)
    b2 = row(b_in[off_u:])
    bs_tile = jnp.repeat(b_spatial.T, HEAD_DIM, axis=1)
    wr = jnp.concatenate([w_router, jnp.zeros((D_MODEL, LANES - N_EXPERTS), F32)], axis=1)
    br = jnp.concatenate([b_router, jnp.full((LANES - N_EXPERTS,), -1e30, F32)]).reshape(1, -1)

    wsb, wab, wbb, wob = (w.astype(BF16) for w in (w_spatial, w_branch_a, w_branch_b, w_out))
    wpleb, wpgb = w_ple.astype(BF16), w_ple_gate.astype(BF16)
    bgu, bdn = b_gate_up.reshape(N_EXPERTS, 1, -1), b_down.reshape(N_EXPERTS, 1, -1)

    n_parts = N_PARTS if batch % N_PARTS == 0 else 1
    pb = batch // n_parts
    Tp = pb * seq
    tiles = Tp // tm
    n_assign = Tp * TOP_K
    n_blk = -(-n_assign // bm) + N_EXPERTS
    n_rows = n_blk * bm

    def front(part):
        q, k, v, f_pad = _qkvf(h2d, w1, b1, tm, Tp, part * tiles)
        auxq, auxk = _decay(f_pad, pb, seq)
        attn = _attention(q, auxq, k, auxk, v, pb, seq)
        h1, idx_o, gate_o, rank_o, cnt_o = _mix(
            h2d, attn, w2, b2, row(gmlp_ln_g), row(gmlp_ln_b), wsb, bs_tile, wab, wbb, wob, row(b_out),
            row(ln1_g), row(ln1_b), wr, br, tm, alpha, part * tiles)
        dest8, plan = _route(cnt_o, idx_o, rank_o, bm, n_blk)
        dest_km = dest8[:TOP_K]
        xpad = _sc_scatter_rows(dest_km, h1.reshape(Tp, ROW_TILE, LANES), n_rows)
        return h1, gate_o, dest_km, plan, xpad

    def experts(part, state):
        h1, gate_o, dest_km, plan, xpad = state
        ypad = _moe(plan, xpad.reshape(n_rows * ROW_TILE, LANES), w_gate_up, bgu, w_down, bdn, n_blk, bm)
        yg = _sc_gather_rows(dest_km.reshape(-1), ypad.reshape(n_rows, ROW_TILE, LANES),
                             both_cores=part == n_parts - 1)
        return h1, gate_o, yg

    def back(part, state, out_prev):
        h1, gate_o, yg = state
        return _final(h1, yg.reshape(TOP_K, Tp * ROW_TILE, LANES), gate_o, p2d, wpleb, wpgb,
                      row(b_ple_gate), row(ln2_g), row(ln2_b), row(ln3_g), row(ln3_b), tm, alpha,
                      part * tiles, out_prev)

    fronts = [front(part) for part in range(n_parts)]
    mids = [experts(part, state) for part, state in enumerate(fronts)]
    out = None
    for part in range(n_parts):
        out = back(part, mids[part], out)
    return out


def kernel(x, p, w_in, b_in, gmlp_ln_g, gmlp_ln_b, w_spatial, b_spatial, w_branch_a, w_branch_b, w_out, b_out, ln1_g, ln1_b, w_router, b_router, w_gate_up, b_gate_up, w_down, b_down, ln2_g, ln2_b, w_ple, w_ple_gate, b_ple_gate, ln3_g, ln3_b):
    batch, seq, d = x.shape
    depth = w_in.shape[0]
    assert d == D_MODEL and seq % GMLP_CHUNK == 0
    alpha = (2.0 * depth) ** 0.25
    h = x.reshape(batch * seq, d)
    for i in range(depth):
        h = _layer(h, p[i].reshape(batch * seq, PLE_DIM), batch, seq, alpha,
                   w_in[i], b_in[i], gmlp_ln_g[i], gmlp_ln_b[i], w_spatial[i], b_spatial[i],
                   w_branch_a[i], w_branch_b[i], w_out[i], b_out[i], ln1_g[i], ln1_b[i],
                   w_router[i], b_router[i], w_gate_up[i], b_gate_up[i], w_down[i], b_down[i],
                   ln2_g[i], ln2_b[i], w_ple[i], w_ple_gate[i], b_ple_gate[i], ln3_g[i], ln3_b[i])
    return h.reshape(batch, seq, d)
```

```python
import functools
import math

import jax
import jax.numpy as jnp
import numpy as np
from jax import lax
from jax.experimental import pallas as pl
from jax.experimental.pallas import tpu as pltpu
from jax.experimental.pallas import tpu_sc as plsc

F32 = jnp.float32
BF16 = jnp.bfloat16
I32 = jnp.int32

D_MODEL = 1024
N_HEADS = 8
HEAD_DIM = 64
FOX_WIDTH = N_HEADS * HEAD_DIM
GMLP_WIDTH = 512
GMLP_CHUNK = 128
N_EXPERTS = 32
TOP_K = 4
D_FF = 1024
PLE_DIM = 256
SWIGLU_LIMIT = 7.0
SWIGLU_ALPHA = 1.702
LN_EPS = 1e-5
LANES = 128
ROW_TILE = 8
VMEM_LIMIT = 56 * 1024 * 1024

MOE_BLOCK = 512
MOE_X_DEPTH = 3
ATTN_Q_TILE = 512
ATTN_K_TILE = 512
ATTN_PAIRS = 2
LOG2E = math.log2(math.e)
ROW_TILE_A = 512
SC_CORES = 2
SC_SUBCORES = 16
SC_WINDOW = 32
N_PARTS = 2
ROUTE_TILE = 1024
FINAL_CHUNK = 256


def _cparams(sem):
    return pltpu.CompilerParams(dimension_semantics=sem, vmem_limit_bytes=VMEM_LIMIT)


def _gelu(x):
    c = math.sqrt(2.0 / math.pi)
    return 0.5 * x * (1.0 + jnp.tanh(c * (x + 0.044715 * (x * x * x))))


def _layer_norm(x, g, b):
    mu = jnp.mean(x, axis=-1, keepdims=True)
    xc = x - mu
    var = jnp.mean(xc * xc, axis=-1, keepdims=True)
    return xc * lax.rsqrt(var + LN_EPS) * g + b


def _split3(x):
    hi = x.astype(BF16)
    r = x - hi.astype(F32)
    mid = r.astype(BF16)
    lo = (r - mid.astype(F32)).astype(BF16)
    return hi, mid, lo


def _dot(a, b):
    return jnp.dot(a, b, preferred_element_type=F32)


def _load_token_rows(ref, n, first=0):
    return jnp.concatenate([ref[pl.ds(first * ROW_TILE + j, n, stride=ROW_TILE), :]
                            for j in range(ROW_TILE)], axis=1)


def _store_token_rows(ref, val):
    n = val.shape[0]
    for j in range(ROW_TILE):
        ref[pl.ds(j, n, stride=ROW_TILE), :] = val[:, j * LANES:(j + 1) * LANES]


def _qkvf_kernel(x_ref, w_ref, b_ref, q_ref, k_ref, v_ref, f_ref):
    x = x_ref[...].astype(BF16)
    proj = _dot(x, w_ref[...]) + b_ref[...]
    q_ref[...] = (proj[:, :FOX_WIDTH] * (HEAD_DIM ** -0.5 * LOG2E)).astype(BF16)
    k_ref[...] = proj[:, FOX_WIDTH:2 * FOX_WIDTH].astype(BF16)
    v_ref[...] = proj[:, 2 * FOX_WIDTH:3 * FOX_WIDTH].astype(BF16)
    f_ref[...] = proj[:, 3 * FOX_WIDTH:]


def _qkvf(x2, w, b, tm, T, tile_off):
    n_out = w.shape[1]
    return pl.pallas_call(
        _qkvf_kernel,
        grid=(T // tm,),
        in_specs=[
            pl.BlockSpec((tm, D_MODEL), lambda i: (i + tile_off, 0)),
            pl.BlockSpec((D_MODEL, n_out), lambda i: (0, 0)),
            pl.BlockSpec((1, n_out), lambda i: (0, 0)),
        ],
        out_specs=[
            pl.BlockSpec((tm, FOX_WIDTH), lambda i: (i, 0)),
            pl.BlockSpec((tm, FOX_WIDTH), lambda i: (i, 0)),
            pl.BlockSpec((tm, FOX_WIDTH), lambda i: (i, 0)),
            pl.BlockSpec((tm, LANES), lambda i: (i, 0)),
        ],
        out_shape=[
            jax.ShapeDtypeStruct((T, FOX_WIDTH), BF16),
            jax.ShapeDtypeStruct((T, FOX_WIDTH), BF16),
            jax.ShapeDtypeStruct((T, FOX_WIDTH), BF16),
            jax.ShapeDtypeStruct((T, LANES), F32),
        ],
        compiler_params=_cparams(("arbitrary",)),
        name="qkvf",
    )(x2, w, b)


def _decay_placement():
    pq = np.zeros((3 * LANES, FOX_WIDTH), np.float32)
    pk = np.zeros((3 * LANES, FOX_WIDTH), np.float32)
    cq = np.zeros((1, FOX_WIDTH), np.float32)
    ck = np.zeros((1, FOX_WIDTH), np.float32)
    for h in range(N_HEADS):
        base = (h // 2) * LANES + (HEAD_DIM if h % 2 == 0 else 0)
        for piece in range(3):
            pq[piece * LANES + h, base + piece] = 1.0
            pk[piece * LANES + h, base + 3 + piece] = -1.0
            cq[0, base + 3 + piece] = 1.0
            ck[0, base + piece] = 1.0
    return pq, pk, cq, ck


def _decay_kernel(f_ref, pq_ref, pk_ref, cq_ref, ck_ref, auxq_ref, auxk_ref, *, seq, blk):
    r = lax.broadcasted_iota(I32, (blk, blk), 0)
    c = lax.broadcasted_iota(I32, (blk, blk), 1)
    tri = jnp.where(c <= r, 1.0, 0.0).astype(BF16)
    carry = jnp.zeros((1, LANES), F32)
    for i in range(seq // blk):
        f = f_ref[i * blk:(i + 1) * blk, :]
        ls = jnp.minimum(f, 0.0) - jnp.log1p(jnp.exp(-jnp.abs(f)))
        hi, mid, lo = _split3(ls)
        cs = _dot(tri, hi) + _dot(tri, mid) + _dot(tri, lo) + carry
        carry = cs[blk - 1:blk, :]
        pieces = jnp.concatenate(_split3(cs * LOG2E), axis=1)
        auxq_ref[i * blk:(i + 1) * blk, :] = (_dot(pieces, pq_ref[...]) + cq_ref[...]).astype(BF16)
        auxk_ref[i * blk:(i + 1) * blk, :] = (_dot(pieces, pk_ref[...]) + ck_ref[...]).astype(BF16)


def _decay(f_pad, batch, seq):
    blk = 256 if seq % 256 == 0 else LANES
    pq, pk, cq, ck = _decay_placement()
    const = lambda *shape: pl.BlockSpec(shape, lambda b: (0,) * len(shape))
    return pl.pallas_call(
        functools.partial(_decay_kernel, seq=seq, blk=blk),
        grid=(batch,),
        in_specs=[
            pl.BlockSpec((seq, LANES), lambda b: (b, 0)),
            const(3 * LANES, FOX_WIDTH), const(3 * LANES, FOX_WIDTH), const(1, FOX_WIDTH), const(1, FOX_WIDTH),
        ],
        out_specs=[
            pl.BlockSpec((seq, FOX_WIDTH), lambda b: (b, 0)),
            pl.BlockSpec((seq, FOX_WIDTH), lambda b: (b, 0)),
        ],
        out_shape=[
            jax.ShapeDtypeStruct((batch * seq, FOX_WIDTH), BF16),
            jax.ShapeDtypeStruct((batch * seq, FOX_WIDTH), BF16),
        ],
        compiler_params=_cparams(("arbitrary",)),
        name="decay",
    )(f_pad, jnp.asarray(pq, BF16), jnp.asarray(pk, BF16), jnp.asarray(cq), jnp.asarray(ck))


def _attn_kernel(q_ref, auxq_ref, k_ref, auxk_ref, v_ref, o_ref, *, tq, tk):
    qi = pl.program_id(2)
    n_pairs = q_ref.shape[1] // LANES
    low_q = lax.broadcasted_iota(I32, (tq, LANES), 1) < HEAD_DIM
    qs = []
    for pr in range(n_pairs):
        q = q_ref[:, pr * LANES:(pr + 1) * LANES]
        aq = auxq_ref[:, pr * LANES:(pr + 1) * LANES]
        qs += [jnp.where(low_q, q, aq), jnp.where(low_q, aq, q)]

    def step(start, n, carry, masked):
        low_k = lax.broadcasted_iota(I32, (n, LANES), 1) < HEAD_DIM
        ones = jnp.ones((n, LANES), BF16)
        out = []
        for pr in range(n_pairs):
            lanes = slice(pr * LANES, (pr + 1) * LANES)
            kb = k_ref[pl.ds(start, n), lanes]
            ak = auxk_ref[pl.ds(start, n), lanes]
            vb = v_ref[pl.ds(start, n), lanes]
            ks = (jnp.where(low_k, kb, ak), jnp.where(low_k, ak, kb))
            vs = (jnp.where(low_k, vb, ones), jnp.where(low_k, ones, vb))
            for j in range(2):
                m, acc = carry[2 * pr + j]
                s = lax.dot_general(qs[2 * pr + j], ks[j], (((1,), (1,)), ((), ())),
                                    preferred_element_type=F32)
                if masked:
                    row = lax.broadcasted_iota(I32, (tq, n), 0)
                    col = lax.broadcasted_iota(I32, (tq, n), 1)
                    s = jnp.where(col + (start - qi * tq) <= row, s, -jnp.inf)
                m_new = jnp.maximum(m, jnp.max(s, axis=1, keepdims=True))
                p = jnp.exp2(s - m_new)
                acc = jnp.exp2(m - m_new) * acc + _dot(p.astype(BF16), vs[j])
                out.append((m_new, acc))
        return tuple(out)

    init = tuple((jnp.full((tq, 1), -jnp.inf, F32), jnp.zeros((tq, LANES), F32))
                 for _ in range(2 * n_pairs))
    carry = lax.fori_loop(0, qi, lambda t, c: step(pl.multiple_of(t * tq, tq), tq, c, False), init)
    for d in range(tq // tk):
        carry = step(pl.multiple_of(qi * tq + d * tk, tk), tk, carry, True)
    for pr in range(n_pairs):
        acc0, acc1 = carry[2 * pr][1], carry[2 * pr + 1][1]
        out0 = acc0 / acc0[:, HEAD_DIM:HEAD_DIM + 1]
        out1 = acc1 / acc1[:, 0:1]
        o_ref[:, pr * LANES:(pr + 1) * LANES] = jnp.where(low_q, out0, out1).astype(BF16)


def _attention(q, auxq, k, auxk, v, batch, seq):
    tq = math.gcd(seq, ATTN_Q_TILE)
    tk = math.gcd(tq, ATTN_K_TILE)
    nq = seq // tq
    T = batch * seq
    width = ATTN_PAIRS * LANES
    q_spec = pl.BlockSpec((tq, width), lambda b, hp, qi: (b * nq + qi, hp))
    kv_spec = pl.BlockSpec((seq, width), lambda b, hp, qi: (b, hp))
    return pl.pallas_call(
        functools.partial(_attn_kernel, tq=tq, tk=tk),
        grid=(batch, N_HEADS // (2 * ATTN_PAIRS), nq),
        in_specs=[q_spec, q_spec, kv_spec, kv_spec, kv_spec],
        out_specs=q_spec,
        out_shape=jax.ShapeDtypeStruct((T, FOX_WIDTH), BF16),
        compiler_params=_cparams(("arbitrary", "arbitrary", "arbitrary")),
        name="attn",
    )(q, auxq, k, auxk, v)


def _mix_kernel(x_ref, attn_ref, w2_ref, b2_ref, lng_ref, lnb_ref, ws_ref, bs_ref,
                wa_ref, wb_ref, wo_ref, bo_ref, g1_ref, b1_ref, wr_ref, br_ref,
                h_ref, idx_ref, gate_ref, rank_ref, cnt_ref, carry_ref, *, tm, alpha):
    i = pl.program_id(0)

    @pl.when(i == 0)
    def _():
        carry_ref[...] = jnp.zeros_like(carry_ref)

    x = x_ref[...]
    proj = _dot(x.astype(BF16), w2_ref[...]) + b2_ref[...]
    u = _gelu(proj[:, :GMLP_WIDTH])
    gv = _gelu(proj[:, GMLP_WIDTH:2 * GMLP_WIDTH])
    vln = _layer_norm(gv, lng_ref[...], lnb_ref[...]).astype(BF16)

    cr = lax.broadcasted_iota(I32, (GMLP_CHUNK, GMLP_CHUNK), 0)
    cc = lax.broadcasted_iota(I32, (GMLP_CHUNK, GMLP_CHUNK), 1)
    tril = cc <= cr
    lo_half = cc < HEAD_DIM
    zero_w = jnp.zeros((GMLP_CHUNK, GMLP_CHUNK), BF16)
    n_slab = GMLP_WIDTH // LANES
    lhs = []
    for s in range(n_slab):
        w0 = jnp.where(tril, ws_ref[2 * s], zero_w)
        w1 = jnp.where(tril, ws_ref[2 * s + 1], zero_w)
        lhs.append(jnp.concatenate([w0, w1], axis=1))
    bs = bs_ref[...]
    rows = []
    for c in range(tm // GMLP_CHUNK):
        cols = []
        for s in range(n_slab):
            vs = vln[c * GMLP_CHUNK:(c + 1) * GMLP_CHUNK, s * LANES:(s + 1) * LANES]
            rhs = jnp.concatenate([jnp.where(lo_half, vs, zero_w), jnp.where(lo_half, zero_w, vs)], axis=0)
            cols.append(_dot(lhs[s], rhs))
        rows.append(jnp.concatenate(cols, axis=1) + bs)
    sp = jnp.concatenate(rows, axis=0) if len(rows) > 1 else rows[0]
    sgu = (u * sp).astype(BF16)

    ga = jax.nn.sigmoid(proj[:, 2 * GMLP_WIDTH:2 * GMLP_WIDTH + D_MODEL])
    gb = jax.nn.sigmoid(proj[:, 2 * GMLP_WIDTH + D_MODEL:])
    merged = ga * _dot(attn_ref[...], wa_ref[...]) + gb * _dot(sgu, wb_ref[...])
    mix = _dot(merged.astype(BF16), wo_ref[...]) + bo_ref[...]
    h = _layer_norm(alpha * x + mix, g1_ref[...], b1_ref[...])
    _store_token_rows(h_ref, h)

    a_hi = h.astype(BF16)
    a_lo = (h - a_hi.astype(F32)).astype(BF16)
    wr = wr_ref[...]
    w_hi = wr.astype(BF16)
    w_lo = (wr - w_hi.astype(F32)).astype(BF16)
    w_cat = jnp.concatenate([w_hi, w_lo], axis=1)
    r_hi = _dot(a_hi, w_cat)
    r_lo = _dot(a_lo, w_cat)
    logits = (r_hi[:, :LANES] + r_hi[:, LANES:]) + (r_lo[:, :LANES] + r_lo[:, LANES:]) + br_ref[...]

    lane_i = lax.broadcasted_iota(I32, (tm, LANES), 1)
    lane_f = lane_i.astype(F32)
    vals, idxs = [], []
    l = logits
    for _ in range(TOP_K):
        m = jnp.max(l, axis=1, keepdims=True)
        ix = jnp.min(jnp.where(l == m, lane_f, float(LANES)), axis=1, keepdims=True)
        vals.append(m)
        idxs.append(ix)
        l = jnp.where(lane_f == ix, -jnp.inf, l)
    es = [jnp.exp(v - vals[0]) for v in vals]
    den = es[0] + es[1] + es[2] + es[3]

    onehot = jnp.zeros((tm, LANES), F32)
    idx_out = jnp.zeros((tm, LANES), F32)
    gate_out = jnp.zeros((tm, LANES), F32)
    for k in range(TOP_K):
        onehot = onehot + jnp.where(lane_f == idxs[k], 1.0, 0.0)
        idx_out = jnp.where(lane_i == k, idxs[k], idx_out)
        gate_out = jnp.where(lane_i == k, es[k] / den, gate_out)

    tr = lax.broadcasted_iota(I32, (tm, tm), 0)
    tc = lax.broadcasted_iota(I32, (tm, tm), 1)
    strict = jnp.where(tc < tr, 1.0, 0.0).astype(BF16)
    carry = carry_ref[0:1, :]
    before = _dot(strict, onehot.astype(BF16)) + carry
    rank_out = jnp.zeros((tm, LANES), F32)
    for k in range(TOP_K):
        rk = jnp.sum(jnp.where(lane_f == idxs[k], before, 0.0), axis=1, keepdims=True)
        rank_out = jnp.where(lane_i == k, rk, rank_out)
    new_carry = carry + jnp.sum(onehot, axis=0, keepdims=True)
    carry_ref[...] = jnp.broadcast_to(new_carry, carry_ref.shape)
    cnt_ref[...] = jnp.broadcast_to(new_carry, cnt_ref.shape).astype(I32)
    idx_ref[...] = idx_out.astype(I32)
    gate_ref[...] = gate_out
    rank_ref[...] = rank_out.astype(I32)


def _mix(x2, attn, w2, b2, lng, lnb, ws, bs_tile, wa, wb, wo, bo, g1, b1, wr, br, tm, alpha, tile_off):
    T = attn.shape[0]
    n2 = w2.shape[1]
    const = lambda *shape: pl.BlockSpec(shape, lambda i: (0,) * len(shape))
    return pl.pallas_call(
        functools.partial(_mix_kernel, tm=tm, alpha=alpha),
        grid=(T // tm,),
        in_specs=[
            pl.BlockSpec((tm, D_MODEL), lambda i: (i + tile_off, 0)),
            pl.BlockSpec((tm, FOX_WIDTH), lambda i: (i, 0)),
            const(D_MODEL, n2), const(1, n2),
            const(1, GMLP_WIDTH), const(1, GMLP_WIDTH),
            const(GMLP_WIDTH // HEAD_DIM, GMLP_CHUNK, GMLP_CHUNK), const(GMLP_CHUNK, GMLP_WIDTH),
            const(FOX_WIDTH, D_MODEL), const(GMLP_WIDTH, D_MODEL),
            const(D_MODEL, D_MODEL), const(1, D_MODEL),
            const(1, D_MODEL), const(1, D_MODEL),
            const(D_MODEL, LANES), const(1, LANES),
        ],
        out_specs=[
            pl.BlockSpec((tm * ROW_TILE, LANES), lambda i: (i, 0)),
            pl.BlockSpec((tm, LANES), lambda i: (i, 0)),
            pl.BlockSpec((tm, LANES), lambda i: (i, 0)),
            pl.BlockSpec((tm, LANES), lambda i: (i, 0)),
            pl.BlockSpec((ROW_TILE, LANES), lambda i: (0, 0)),
        ],
        out_shape=[
            jax.ShapeDtypeStruct((T * ROW_TILE, LANES), F32),
            jax.ShapeDtypeStruct((T, LANES), I32),
            jax.ShapeDtypeStruct((T, LANES), F32),
            jax.ShapeDtypeStruct((T, LANES), I32),
            jax.ShapeDtypeStruct((ROW_TILE, LANES), I32),
        ],
        scratch_shapes=[pltpu.VMEM((ROW_TILE, LANES), F32)],
        compiler_params=_cparams(("arbitrary",)),
        name="mix",
    )(x2, attn, w2, b2, lng, lnb, ws, bs_tile, wa, wb, wo, bo, g1, b1, wr, br)


PLAN_EXPERT, PLAN_VALID, PLAN_SLOT, PLAN_NEXT, PLAN_USED = range(5)


def _lane_cumsum(x):
    lane = lax.broadcasted_iota(I32, x.shape, 1)
    shift = 1
    while shift < LANES:
        x = x + jnp.where(lane >= shift, pltpu.roll(x, shift, 1), 0.0)
        shift *= 2
    return x


def _route_kernel(cnt_ref, idx_ref, rank_ref, dest_ref, plan_ref, start_ref, *, tm, bm, nbp):
    @pl.when(pl.program_id(0) == 0)
    def _():
        lane = lax.broadcasted_iota(I32, (ROW_TILE, LANES), 1)
        counts = jnp.where(lane < N_EXPERTS, cnt_ref[...].astype(F32), 0.0)
        padded = jnp.floor((counts + (bm - 1)) / bm) * bm
        pad_end = _lane_cumsum(padded)
        pad_start = pad_end - padded
        start_ref[...] = pad_start
        has_rows = jnp.where(counts > 0, 1.0, 0.0)
        order = _lane_cumsum(has_rows) - 1.0
        slot = order - 2.0 * jnp.floor(order * 0.5)
        total = jnp.sum(jnp.where(lane == N_EXPERTS - 1, pad_end, 0.0), axis=1, keepdims=True)[0:1, :]
        n_used = total / bm

        col = lambda r: jnp.transpose(r)[:, 0:1]
        sub = lax.broadcasted_iota(I32, (LANES, 1), 0)
        is_expert = sub < N_EXPERTS
        lane_w = lax.broadcasted_iota(I32, (LANES, LANES), 1)
        sub_w = lax.broadcasted_iota(I32, (LANES, LANES), 0)
        later = (lane_w > sub_w) & (lane_w < N_EXPERTS) & (has_rows[0:1, :] > 0)
        succ = jnp.min(jnp.where(later, lane_w, N_EXPERTS).astype(F32), axis=1, keepdims=True)
        succ = jnp.where(succ == N_EXPERTS, sub.astype(F32), succ)

        blk = lax.broadcasted_iota(I32, (1, nbp), 1).astype(F32)
        blk_src = jnp.minimum(blk, n_used - 1.0)
        blk_row = blk_src * bm
        below = is_expert & (col(pad_end) <= blk_row)
        blk_e = jnp.minimum(jnp.sum(jnp.where(below, 1.0, 0.0), axis=0, keepdims=True), N_EXPERTS - 1.0)
        hit = sub.astype(F32) == blk_e
        take = lambda c: jnp.sum(jnp.where(hit, c, 0.0), axis=0, keepdims=True)
        blk_valid = jnp.clip(take(col(pad_start + counts)) - blk_row, 0.0, bm)
        rows = [None] * ROW_TILE
        rows[PLAN_EXPERT], rows[PLAN_VALID] = blk_e, blk_valid
        rows[PLAN_SLOT], rows[PLAN_NEXT] = take(col(slot)), take(succ)
        rows[PLAN_USED] = jnp.broadcast_to(n_used, (1, nbp))
        zero = jnp.zeros((1, nbp), F32)
        plan_ref[...] = jnp.concatenate([zero if r is None else r for r in rows], axis=0).astype(I32)

    pad_start = start_ref[0:1, :]
    lane_i = lax.broadcasted_iota(I32, (tm, LANES), 1)
    lane_f = lane_i.astype(F32)
    idx = idx_ref[...].astype(F32)
    rank = rank_ref[...].astype(F32)
    dest = jnp.zeros((tm, LANES), F32)
    for k in range(TOP_K):
        base = jnp.sum(jnp.where(lane_f == idx[:, k:k + 1], pad_start, 0.0), axis=1, keepdims=True)
        dest = jnp.where(lane_i == k, base + rank[:, k:k + 1], dest)
    dest_ref[...] = jnp.transpose(dest)[:ROW_TILE, :].astype(I32)


def _route(cnt_o, idx_o, rank_o, bm, n_blk):
    Tp = idx_o.shape[0]
    tm = math.gcd(Tp, ROUTE_TILE)
    nbp = -(-n_blk // LANES) * LANES
    return pl.pallas_call(
        functools.partial(_route_kernel, tm=tm, bm=bm, nbp=nbp),
        grid=(Tp // tm,),
        in_specs=[
            pl.BlockSpec((ROW_TILE, LANES), lambda i: (0, 0)),
            pl.BlockSpec((tm, LANES), lambda i: (i, 0)),
            pl.BlockSpec((tm, LANES), lambda i: (i, 0)),
        ],
        out_specs=[
            pl.BlockSpec((ROW_TILE, tm), lambda i: (0, i)),
            pl.BlockSpec((ROW_TILE, nbp), lambda i: (0, 0)),
        ],
        out_shape=[
            jax.ShapeDtypeStruct((ROW_TILE, Tp), I32),
            jax.ShapeDtypeStruct((ROW_TILE, nbp), I32),
        ],
        scratch_shapes=[pltpu.VMEM((ROW_TILE, LANES), F32)],
        compiler_params=_cparams(("arbitrary",)),
        name="route",
    )(cnt_o, idx_o, rank_o)


def _sc_window_indices(idx):
    return jnp.pad(idx.reshape(-1, SC_WINDOW), ((0, 0), (0, LANES - SC_WINDOW)))


def _sc_scatter_rows(dest_km, src3, n_dst):
    n_slot, n_tok = dest_km.shape
    n_win = n_tok // SC_WINDOW
    mesh = plsc.VectorSubcoreMesh(core_axis_name="core", subcore_axis_name="subcore",
                                  num_cores=SC_CORES, num_subcores=SC_SUBCORES)

    @pl.kernel(out_type=jax.ShapeDtypeStruct((n_dst, ROW_TILE, LANES), src3.dtype), mesh=mesh, name="sc_scatter")
    def scatter(src_hbm, i_hbm, o_hbm):
        def body(x_vmem, i_vmem):
            for k in range(n_slot):
                pltpu.sync_copy(x_vmem, o_hbm.at[i_vmem.at[k, pl.ds(0, SC_WINDOW)]])

        @pl.when(lax.axis_index("core") == 0)
        def _():
            pltpu.emit_pipeline(
                body,
                grid=(n_win,),
                in_specs=[pl.BlockSpec((SC_WINDOW, ROW_TILE, LANES), lambda i: (i, 0, 0)),
                          pl.BlockSpec((ROW_TILE, LANES), lambda i: (i, 0))],
                out_specs=[],
                core_axis_name="subcore",
                dimension_semantics=(pltpu.PARALLEL,),
            )(src_hbm, i_hbm)

    idx = dest_km.reshape(n_slot, n_win, SC_WINDOW).transpose(1, 0, 2)
    idx = jnp.pad(idx, ((0, 0), (0, ROW_TILE - n_slot), (0, LANES - SC_WINDOW)))
    return scatter(src3, idx.reshape(n_win * ROW_TILE, LANES))


def _sc_gather_rows(sidx, src3, both_cores):
    n = sidx.shape[0]
    mesh = plsc.VectorSubcoreMesh(core_axis_name="core", subcore_axis_name="subcore",
                                  num_cores=SC_CORES, num_subcores=SC_SUBCORES)

    @pl.kernel(out_type=jax.ShapeDtypeStruct((n, ROW_TILE, LANES), src3.dtype), mesh=mesh, name="sc_gather")
    def gather(src_hbm, i_hbm, o_hbm):
        def body(i_vmem, o_vmem):
            pltpu.sync_copy(src_hbm.at[i_vmem.at[0, pl.ds(0, SC_WINDOW)]], o_vmem)

        def run(axes):
            pltpu.emit_pipeline(
                body,
                grid=(n // SC_WINDOW,),
                in_specs=[pl.BlockSpec((1, LANES), lambda i: (i, 0))],
                out_specs=[pl.BlockSpec((SC_WINDOW, ROW_TILE, LANES), lambda i: (i, 0, 0))],
                core_axis_name=axes,
                dimension_semantics=(pltpu.PARALLEL,),
            )(i_hbm, o_hbm)

        if both_cores:
            run(("core", "subcore"))
        else:
            pl.when(lax.axis_index("core") == 0)(lambda: run("subcore"))

    return gather(src3, _sc_window_indices(sidx))


def _moe_kernel(plan_ref, x_hbm, wgu_hbm, bgu_ref, wdn_hbm, bdn_ref, y_ref,
                x_buf, wgu_f32, wdn_f32, wgu_bf, wdn_bf, x_sems, sems):
    i = pl.program_id(0)
    n_used = plan_ref[PLAN_USED, i]
    expert = plan_ref[PLAN_EXPERT, i]
    active = i < n_used
    fresh = (i == 0) | (expert != plan_ref[PLAN_EXPERT, jnp.maximum(i - 1, 0)])
    slot = plan_ref[PLAN_SLOT, i]
    successor = plan_ref[PLAN_NEXT, i]
    rows = x_buf.shape[1]
    depth = x_buf.shape[0]

    def x_copy(blk):
        s = blk % depth
        return pltpu.make_async_copy(x_hbm.at[pl.ds(pl.multiple_of(blk * rows, rows), rows)],
                                     x_buf.at[s], x_sems.at[s])

    def weight_copies(e, s):
        return (pltpu.make_async_copy(wgu_hbm.at[e], wgu_f32.at[s], sems.at[s, 0]),
                pltpu.make_async_copy(wdn_hbm.at[e], wdn_f32.at[s], sems.at[s, 1]))

    @pl.when(i == 0)
    def _():
        for ahead in range(depth - 1):
            @pl.when(ahead < n_used)
            def _():
                x_copy(ahead).start()

    @pl.when(i + (depth - 1) < n_used)
    def _():
        x_copy(i + (depth - 1)).start()

    @pl.when(i == 0)
    def _():
        for c in weight_copies(expert, slot):
            c.start()

    @pl.when(active & fresh)
    def _():
        @pl.when(successor != expert)
        def _():
            for c in weight_copies(successor, 1 - slot):
                c.start()

        for c in weight_copies(expert, slot):
            c.wait()
        wgu_bf[...] = wgu_f32[slot].astype(BF16)
        wdn_bf[...] = wdn_f32[slot].astype(BF16)

    @pl.when(active)
    def _():
        bm = rows // ROW_TILE
        x_copy(i).wait()
        x = _load_token_rows(x_buf.at[i % depth], bm)
        valid = lax.broadcasted_iota(I32, (bm, 1), 0) < plan_ref[PLAN_VALID, i]
        x = jnp.where(valid, x, 0.0).astype(BF16)
        gu = _dot(x, wgu_bf[...]) + bgu_ref[...]
        gate = jnp.minimum(gu[:, :D_FF], SWIGLU_LIMIT)
        up = jnp.clip(gu[:, D_FF:], -SWIGLU_LIMIT, SWIGLU_LIMIT)
        hid = (up + 1.0) * (gate * jax.nn.sigmoid(SWIGLU_ALPHA * gate))
        _store_token_rows(y_ref, _dot(hid.astype(BF16), wdn_bf[...]) + bdn_ref[...])

    @pl.when(jnp.logical_not(active))
    def _():
        y_ref[...] = jnp.zeros_like(y_ref)


def _moe(plan, xpad, wgu, bgu, wdn, bdn, n_blk, bm):
    bias = lambda i, plan: (plan[PLAN_EXPERT, i], 0, 0)
    grid_spec = pltpu.PrefetchScalarGridSpec(
        num_scalar_prefetch=1,
        grid=(n_blk,),
        in_specs=[
            pl.BlockSpec(memory_space=pl.ANY),
            pl.BlockSpec(memory_space=pl.ANY),
            pl.BlockSpec((None, 1, 2 * D_FF), bias),
            pl.BlockSpec(memory_space=pl.ANY),
            pl.BlockSpec((None, 1, D_MODEL), bias),
        ],
        out_specs=pl.BlockSpec((bm * ROW_TILE, LANES), lambda i, plan: (i, 0)),
        scratch_shapes=[
            pltpu.VMEM((MOE_X_DEPTH, bm * ROW_TILE, LANES), F32),
            pltpu.VMEM((2, D_MODEL, 2 * D_FF), F32), pltpu.VMEM((2, D_FF, D_MODEL), F32),
            pltpu.VMEM((D_MODEL, 2 * D_FF), BF16), pltpu.VMEM((D_FF, D_MODEL), BF16),
            pltpu.SemaphoreType.DMA((MOE_X_DEPTH,)), pltpu.SemaphoreType.DMA((2, 2)),
        ],
    )
    return pl.pallas_call(
        _moe_kernel,
        grid_spec=grid_spec,
        out_shape=jax.ShapeDtypeStruct((n_blk * bm * ROW_TILE, LANES), F32),
        compiler_params=_cparams(("arbitrary",)),
        name="moe",
    )(plan, xpad, wgu, bgu, wdn, bdn)


def _final_kernel(h_ref, yg_ref, gate_ref, p_ref, wple_ref, wpg_ref, bpg_ref,
                  g2_ref, b2_ref, g3_ref, b3_ref, *rest, tm, alpha):
    o_ref = rest[-1]
    rc = FINAL_CHUNK
    for c in range(tm // rc):
        rows = slice(c * rc, (c + 1) * rc)
        gates = gate_ref[rows, :]
        z = alpha * _load_token_rows(h_ref, rc, c * rc)
        for k in range(TOP_K):
            z = z + gates[:, k:k + 1] * _load_token_rows(yg_ref.at[k], rc, c * rc)
        h2 = _layer_norm(z, g2_ref[...], b2_ref[...])
        emb = _dot(p_ref[rows, :].astype(BF16), wple_ref[...])
        pg = jax.nn.sigmoid(_dot(h2.astype(BF16), wpg_ref[...]) + bpg_ref[...])
        o_ref[rows, :] = _layer_norm(alpha * h2 + emb * pg, g3_ref[...], b3_ref[...])


def _final(h1, yg, gates, p2, wple, wpg, bpg, g2, b2, g3, b3, tm, alpha, tile_off, out_prev):
    T = h1.shape[0] // ROW_TILE
    const = lambda *shape: pl.BlockSpec(shape, lambda i: (0,) * len(shape))
    in_specs = [
        pl.BlockSpec((tm * ROW_TILE, LANES), lambda i: (i, 0)),
        pl.BlockSpec((TOP_K, tm * ROW_TILE, LANES), lambda i: (0, i, 0)),
        pl.BlockSpec((tm, LANES), lambda i: (i, 0)),
        pl.BlockSpec((tm, PLE_DIM), lambda i: (i + tile_off, 0)),
        const(PLE_DIM, D_MODEL), const(D_MODEL, D_MODEL), const(1, D_MODEL),
        const(1, D_MODEL), const(1, D_MODEL), const(1, D_MODEL), const(1, D_MODEL),
    ]
    args = [h1, yg, gates, p2, wple, wpg, bpg, g2, b2, g3, b3]
    aliases = {}
    if out_prev is not None:
        in_specs.append(pl.BlockSpec(memory_space=pl.ANY))
        aliases = {len(args): 0}
        args.append(out_prev)
    return pl.pallas_call(
        functools.partial(_final_kernel, tm=tm, alpha=alpha),
        grid=(T // tm,),
        in_specs=in_specs,
        out_specs=pl.BlockSpec((tm, D_MODEL), lambda i: (i + tile_off, 0)),
        out_shape=jax.ShapeDtypeStruct((p2.shape[0], D_MODEL), F32),
        input_output_aliases=aliases,
        compiler_params=_cparams(("arbitrary",)),
        name="final",
    )(*args)


def _layer(h2d, p2d, batch, seq, alpha, w_in, b_in, gmlp_ln_g, gmlp_ln_b, w_spatial, b_spatial,
           w_branch_a, w_branch_b, w_out, b_out, ln1_g, ln1_b, w_router, b_router,
           w_gate_up, b_gate_up, w_down, b_down, ln2_g, ln2_b, w_ple, w_ple_gate, b_ple_gate,
           ln3_g, ln3_b):
    T = batch * seq
    tm = math.gcd(T, ROW_TILE_A)
    bm = MOE_BLOCK
    off_f = 3 * FOX_WIDTH
    off_u = off_f + N_HEADS
    row = lambda v: v.reshape(1, -1).astype(F32)

    w1 = jnp.concatenate([w_in[:, :off_u], jnp.zeros((D_MODEL, LANES - N_HEADS), F32)], axis=1).astype(BF16)
    b1 = jnp.concatenate([b_in[:off_u], jnp.zeros((LANES - N_HEADS,), F32)]).reshape(1, -1)
    w2 = w_in[:, off_u:].astype(BF16)
    b2 = row(b_in[off_u:])
    bs_tile = jnp.repeat(b_spatial.T, HEAD_DIM, axis=1)
    wr = jnp.concatenate([w_router, jnp.zeros((D_MODEL, LANES - N_EXPERTS), F32)], axis=1)
    br = jnp.concatenate([b_router, jnp.full((LANES - N_EXPERTS,), -1e30, F32)]).reshape(1, -1)

    wsb, wab, wbb, wob = (w.astype(BF16) for w in (w_spatial, w_branch_a, w_branch_b, w_out))
    wpleb, wpgb = w_ple.astype(BF16), w_ple_gate.astype(BF16)
    bgu, bdn = b_gate_up.reshape(N_EXPERTS, 1, -1), b_down.reshape(N_EXPERTS, 1, -1)

    n_parts = N_PARTS if batch % N_PARTS == 0 else 1
    pb = batch // n_parts
    Tp = pb * seq
    tiles = Tp // tm
    n_assign = Tp * TOP_K
    n_blk = -(-n_assign // bm) + N_EXPERTS
    n_rows = n_blk * bm

    def front(part):
        q, k, v, f_pad = _qkvf(h2d, w1, b1, tm, Tp, part * tiles)
        auxq, auxk = _decay(f_pad, pb, seq)
        attn = _attention(q, auxq, k, auxk, v, pb, seq)
        h1, idx_o, gate_o, rank_o, cnt_o = _mix(
            h2d, attn, w2, b2, row(gmlp_ln_g), row(gmlp_ln_b), wsb, bs_tile, wab, wbb, wob, row(b_out),
            row(ln1_g), row(ln1_b), wr, br, tm, alpha, part * tiles)
        dest8, plan = _route(cnt_o, idx_o, rank_o, bm, n_blk)
        dest_km = dest8[:TOP_K]
        xpad = _sc_scatter_rows(dest_km, h1.reshape(Tp, ROW_TILE, LANES), n_rows)
        return h1, gate_o, dest_km, plan, xpad

    def experts(part, state):
        h1, gate_o, dest_km, plan, xpad = state
        ypad = _moe(plan, xpad.reshape(n_rows * ROW_TILE, LANES), w_gate_up, bgu, w_down, bdn, n_blk, bm)
        yg = _sc_gather_rows(dest_km.reshape(-1), ypad.reshape(n_rows, ROW_TILE, LANES),
                             both_cores=part == n_parts - 1)
        return h1, gate_o, yg

    def back(part, state, out_prev):
        h1, gate_o, yg = state
        return _final(h1, yg.reshape(TOP_K, Tp * ROW_TILE, LANES), gate_o, p2d, wpleb, wpgb,
                      row(b_ple_gate), row(ln2_g), row(ln2_b), row(ln3_g), row(ln3_b), tm, alpha,
                      part * tiles, out_prev)

    fronts = [front(part) for part in range(n_parts)]
    mids = [experts(part, state) for part, state in enumerate(fronts)]
    out = None
    for part in range(n_parts):
        out = back(part, mids[part], out)
    return out


def kernel(x, p, w_in, b_in, gmlp_ln_g, gmlp_ln_b, w_spatial, b_spatial, w_branch_a, w_branch_b, w_out, b_out, ln1_g, ln1_b, w_router, b_router, w_gate_up, b_gate_up, w_down, b_down, ln2_g, ln2_b, w_ple, w_ple_gate, b_ple_gate, ln3_g, ln3_b):
    batch, seq, d = x.shape
    depth = w_in.shape[0]
    assert d == D_MODEL and seq % GMLP_CHUNK == 0
    alpha = (2.0 * depth) ** 0.25
    h = x.reshape(batch * seq, d)
    for i in range(depth):
        h = _layer(h, p[i].reshape(batch * seq, PLE_DIM), batch, seq, alpha,
                   w_in[i], b_in[i], gmlp_ln_g[i], gmlp_ln_b[i], w_spatial[i], b_spatial[i],
                   w_branch_a[i], w_branch_b[i], w_out[i], b_out[i], ln1_g[i], ln1_b[i],
                   w_router[i], b_router[i], w_gate_up[i], b_gate_up[i], w_down[i], b_down[i],
                   ln2_g[i], ln2_b[i], w_ple[i], w_ple_gate[i], b_ple_gate[i], ln3_g[i], ln3_b[i])
    return h.reshape(batch, seq, d)
```

```python
import functools
import math

import jax
import jax.numpy as jnp
import numpy as np
from jax import lax
from jax.experimental import pallas as pl
from jax.experimental.pallas import tpu as pltpu
from jax.experimental.pallas import tpu_sc as plsc

F32 = jnp.float32
BF16 = jnp.bfloat16
I32 = jnp.int32
U32 = jnp.uint32

D_MODEL = 1024
N_HEADS = 8
HEAD_DIM = 64
FOX_WIDTH = N_HEADS * HEAD_DIM
GMLP_WIDTH = 512
GMLP_CHUNK = 128
N_EXPERTS = 32
TOP_K = 4
D_FF = 1024
PLE_DIM = 256
SWIGLU_LIMIT = 7.0
SWIGLU_ALPHA = 1.702
LN_EPS = 1e-5
LANES = 128
ROW_TILE = 8
PACK_TILE = D_MODEL // 2 // LANES
VMEM_LIMIT = 56 * 1024 * 1024

MOE_BLOCK = 512
MOE_X_DEPTH = 3
ATTN_Q_TILE = 512
ATTN_K_TILE = 512
ATTN_PAIRS = 2
LOG2E = math.log2(math.e)
ROW_TILE_A = 512
SC_CORES = 2
SC_SUBCORES = 16
SC_WINDOW = 32
N_PARTS = 2
ROUTE_TILE = 1024
FINAL_CHUNK = 256


def _cparams(sem):
    return pltpu.CompilerParams(dimension_semantics=sem, vmem_limit_bytes=VMEM_LIMIT)


def _gelu(x):
    c = math.sqrt(2.0 / math.pi)
    return 0.5 * x * (1.0 + jnp.tanh(c * (x + 0.044715 * (x * x * x))))


def _layer_norm(x, g, b):
    mu = jnp.mean(x, axis=-1, keepdims=True)
    xc = x - mu
    var = jnp.mean(xc * xc, axis=-1, keepdims=True)
    return xc * lax.rsqrt(var + LN_EPS) * g + b


def _split3(x):
    hi = x.astype(BF16)
    r = x - hi.astype(F32)
    mid = r.astype(BF16)
    lo = (r - mid.astype(F32)).astype(BF16)
    return hi, mid, lo


def _dot(a, b):
    return jnp.dot(a, b, preferred_element_type=F32)


def _load_token_rows(ref, n, first=0, r=ROW_TILE):
    return jnp.concatenate([ref[pl.ds(first * r + j, n, stride=r), :] for j in range(r)], axis=1)


def _store_token_rows(ref, val):
    n, r = val.shape[0], val.shape[1] // LANES
    for j in range(r):
        ref[pl.ds(j, n, stride=r), :] = val[:, j * LANES:(j + 1) * LANES]


def _pack_bf16_pairs(x):
    half = x.shape[1] // 2
    hi = pltpu.bitcast(x[:, :half].astype(BF16).astype(F32), U32)
    lo = pltpu.bitcast(x[:, half:].astype(BF16).astype(F32), U32)
    return hi | (lo >> 16)


def _unpack_bf16_pairs(w):
    a = pltpu.bitcast(w & jnp.uint32(0xFFFF0000), F32)
    b = pltpu.bitcast(w << 16, F32)
    return jnp.concatenate([a, b], axis=1)


def _qkvf_kernel(x_ref, w_ref, b_ref, q_ref, k_ref, v_ref, f_ref):
    x = x_ref[...].astype(BF16)
    proj = _dot(x, w_ref[...]) + b_ref[...]
    q_ref[...] = (proj[:, :FOX_WIDTH] * (HEAD_DIM ** -0.5 * LOG2E)).astype(BF16)
    k_ref[...] = proj[:, FOX_WIDTH:2 * FOX_WIDTH].astype(BF16)
    v_ref[...] = proj[:, 2 * FOX_WIDTH:3 * FOX_WIDTH].astype(BF16)
    f_ref[...] = proj[:, 3 * FOX_WIDTH:]


def _qkvf(x2, w, b, tm, T, tile_off):
    n_out = w.shape[1]
    return pl.pallas_call(
        _qkvf_kernel,
        grid=(T // tm,),
        in_specs=[
            pl.BlockSpec((tm, D_MODEL), lambda i: (i + tile_off, 0)),
            pl.BlockSpec((D_MODEL, n_out), lambda i: (0, 0)),
            pl.BlockSpec((1, n_out), lambda i: (0, 0)),
        ],
        out_specs=[
            pl.BlockSpec((tm, FOX_WIDTH), lambda i: (i, 0)),
            pl.BlockSpec((tm, FOX_WIDTH), lambda i: (i, 0)),
            pl.BlockSpec((tm, FOX_WIDTH), lambda i: (i, 0)),
            pl.BlockSpec((tm, LANES), lambda i: (i, 0)),
        ],
        out_shape=[
            jax.ShapeDtypeStruct((T, FOX_WIDTH), BF16),
            jax.ShapeDtypeStruct((T, FOX_WIDTH), BF16),
            jax.ShapeDtypeStruct((T, FOX_WIDTH), BF16),
            jax.ShapeDtypeStruct((T, LANES), F32),
        ],
        compiler_params=_cparams(("arbitrary",)),
        name="qkvf",
    )(x2, w, b)


def _decay_placement():
    pq = np.zeros((3 * LANES, FOX_WIDTH), np.float32)
    pk = np.zeros((3 * LANES, FOX_WIDTH), np.float32)
    cq = np.zeros((1, FOX_WIDTH), np.float32)
    ck = np.zeros((1, FOX_WIDTH), np.float32)
    for h in range(N_HEADS):
        base = (h // 2) * LANES + (HEAD_DIM if h % 2 == 0 else 0)
        for piece in range(3):
            pq[piece * LANES + h, base + piece] = 1.0
            pk[piece * LANES + h, base + 3 + piece] = -1.0
            cq[0, base + 3 + piece] = 1.0
            ck[0, base + piece] = 1.0
    return pq, pk, cq, ck


def _decay_kernel(f_ref, pq_ref, pk_ref, cq_ref, ck_ref, auxq_ref, auxk_ref, *, seq, blk):
    r = lax.broadcasted_iota(I32, (blk, blk), 0)
    c = lax.broadcasted_iota(I32, (blk, blk), 1)
    tri = jnp.where(c <= r, 1.0, 0.0).astype(BF16)
    carry = jnp.zeros((1, LANES), F32)
    for i in range(seq // blk):
        f = f_ref[i * blk:(i + 1) * blk, :]
        ls = jnp.minimum(f, 0.0) - jnp.log1p(jnp.exp(-jnp.abs(f)))
        hi, mid, lo = _split3(ls)
        cs = _dot(tri, hi) + _dot(tri, mid) + _dot(tri, lo) + carry
        carry = cs[blk - 1:blk, :]
        pieces = jnp.concatenate(_split3(cs * LOG2E), axis=1)
        auxq_ref[i * blk:(i + 1) * blk, :] = (_dot(pieces, pq_ref[...]) + cq_ref[...]).astype(BF16)
        auxk_ref[i * blk:(i + 1) * blk, :] = (_dot(pieces, pk_ref[...]) + ck_ref[...]).astype(BF16)


def _decay(f_pad, batch, seq):
    blk = 256 if seq % 256 == 0 else LANES
    pq, pk, cq, ck = _decay_placement()
    const = lambda *shape: pl.BlockSpec(shape, lambda b: (0,) * len(shape))
    return pl.pallas_call(
        functools.partial(_decay_kernel, seq=seq, blk=blk),
        grid=(batch,),
        in_specs=[
            pl.BlockSpec((seq, LANES), lambda b: (b, 0)),
            const(3 * LANES, FOX_WIDTH), const(3 * LANES, FOX_WIDTH), const(1, FOX_WIDTH), const(1, FOX_WIDTH),
        ],
        out_specs=[
            pl.BlockSpec((seq, FOX_WIDTH), lambda b: (b, 0)),
            pl.BlockSpec((seq, FOX_WIDTH), lambda b: (b, 0)),
        ],
        out_shape=[
            jax.ShapeDtypeStruct((batch * seq, FOX_WIDTH), BF16),
            jax.ShapeDtypeStruct((batch * seq, FOX_WIDTH), BF16),
        ],
        compiler_params=_cparams(("arbitrary",)),
        name="decay",
    )(f_pad, jnp.asarray(pq, BF16), jnp.asarray(pk, BF16), jnp.asarray(cq), jnp.asarray(ck))


def _attn_kernel(q_ref, auxq_ref, k_ref, auxk_ref, v_ref, o_ref, *, tq, tk):
    qi = pl.program_id(2)
    n_pairs = q_ref.shape[1] // LANES
    low_q = lax.broadcasted_iota(I32, (tq, LANES), 1) < HEAD_DIM
    qs = []
    for pr in range(n_pairs):
        q = q_ref[:, pr * LANES:(pr + 1) * LANES]
        aq = auxq_ref[:, pr * LANES:(pr + 1) * LANES]
        qs += [jnp.where(low_q, q, aq), jnp.where(low_q, aq, q)]

    def step(start, n, carry, masked):
        low_k = lax.broadcasted_iota(I32, (n, LANES), 1) < HEAD_DIM
        ones = jnp.ones((n, LANES), BF16)
        out = []
        for pr in range(n_pairs):
            lanes = slice(pr * LANES, (pr + 1) * LANES)
            kb = k_ref[pl.ds(start, n), lanes]
            ak = auxk_ref[pl.ds(start, n), lanes]
            vb = v_ref[pl.ds(start, n), lanes]
            ks = (jnp.where(low_k, kb, ak), jnp.where(low_k, ak, kb))
            vs = (jnp.where(low_k, vb, ones), jnp.where(low_k, ones, vb))
            for j in range(2):
                m, acc = carry[2 * pr + j]
                s = lax.dot_general(qs[2 * pr + j], ks[j], (((1,), (1,)), ((), ())),
                                    preferred_element_type=F32)
                if masked:
                    row = lax.broadcasted_iota(I32, (tq, n), 0)
                    col = lax.broadcasted_iota(I32, (tq, n), 1)
                    s = jnp.where(col + (start - qi * tq) <= row, s, -jnp.inf)
                m_new = jnp.maximum(m, jnp.max(s, axis=1, keepdims=True))
                p = jnp.exp2(s - m_new)
                acc = jnp.exp2(m - m_new) * acc + _dot(p.astype(BF16), vs[j])
                out.append((m_new, acc))
        return tuple(out)

    init = tuple((jnp.full((tq, 1), -jnp.inf, F32), jnp.zeros((tq, LANES), F32))
                 for _ in range(2 * n_pairs))
    carry = lax.fori_loop(0, qi, lambda t, c: step(pl.multiple_of(t * tq, tq), tq, c, False), init)
    for d in range(tq // tk):
        carry = step(pl.multiple_of(qi * tq + d * tk, tk), tk, carry, True)
    for pr in range(n_pairs):
        acc0, acc1 = carry[2 * pr][1], carry[2 * pr + 1][1]
        out0 = acc0 / acc0[:, HEAD_DIM:HEAD_DIM + 1]
        out1 = acc1 / acc1[:, 0:1]
        o_ref[:, pr * LANES:(pr + 1) * LANES] = jnp.where(low_q, out0, out1).astype(BF16)


def _attention(q, auxq, k, auxk, v, batch, seq):
    tq = math.gcd(seq, ATTN_Q_TILE)
    tk = math.gcd(tq, ATTN_K_TILE)
    nq = seq // tq
    T = batch * seq
    width = ATTN_PAIRS * LANES
    q_spec = pl.BlockSpec((tq, width), lambda b, hp, qi: (b * nq + qi, hp))
    kv_spec = pl.BlockSpec((seq, width), lambda b, hp, qi: (b, hp))
    return pl.pallas_call(
        functools.partial(_attn_kernel, tq=tq, tk=tk),
        grid=(batch, N_HEADS // (2 * ATTN_PAIRS), nq),
        in_specs=[q_spec, q_spec, kv_spec, kv_spec, kv_spec],
        out_specs=q_spec,
        out_shape=jax.ShapeDtypeStruct((T, FOX_WIDTH), BF16),
        compiler_params=_cparams(("arbitrary", "arbitrary", "arbitrary")),
        name="attn",
    )(q, auxq, k, auxk, v)


def _mix_kernel(x_ref, attn_ref, w2_ref, b2_ref, lng_ref, lnb_ref, ws_ref, bs_ref,
                wa_ref, wb_ref, wo_ref, bo_ref, g1_ref, b1_ref, wr_ref, br_ref,
                h_ref, hx_ref, idx_ref, gate_ref, rank_ref, cnt_ref, carry_ref, *, tm, alpha):
    i = pl.program_id(0)

    @pl.when(i == 0)
    def _():
        carry_ref[...] = jnp.zeros_like(carry_ref)

    x = x_ref[...]
    proj = _dot(x.astype(BF16), w2_ref[...]) + b2_ref[...]
    u = _gelu(proj[:, :GMLP_WIDTH])
    gv = _gelu(proj[:, GMLP_WIDTH:2 * GMLP_WIDTH])
    vln = _layer_norm(gv, lng_ref[...], lnb_ref[...]).astype(BF16)

    cr = lax.broadcasted_iota(I32, (GMLP_CHUNK, GMLP_CHUNK), 0)
    cc = lax.broadcasted_iota(I32, (GMLP_CHUNK, GMLP_CHUNK), 1)
    tril = cc <= cr
    lo_half = cc < HEAD_DIM
    zero_w = jnp.zeros((GMLP_CHUNK, GMLP_CHUNK), BF16)
    n_slab = GMLP_WIDTH // LANES
    lhs = []
    for s in range(n_slab):
        w0 = jnp.where(tril, ws_ref[2 * s], zero_w)
        w1 = jnp.where(tril, ws_ref[2 * s + 1], zero_w)
        lhs.append(jnp.concatenate([w0, w1], axis=1))
    bs = bs_ref[...]
    rows = []
    for c in range(tm // GMLP_CHUNK):
        cols = []
        for s in range(n_slab):
            vs = vln[c * GMLP_CHUNK:(c + 1) * GMLP_CHUNK, s * LANES:(s + 1) * LANES]
            rhs = jnp.concatenate([jnp.where(lo_half, vs, zero_w), jnp.where(lo_half, zero_w, vs)], axis=0)
            cols.append(_dot(lhs[s], rhs))
        rows.append(jnp.concatenate(cols, axis=1) + bs)
    sp = jnp.concatenate(rows, axis=0) if len(rows) > 1 else rows[0]
    sgu = (u * sp).astype(BF16)

    ga = jax.nn.sigmoid(proj[:, 2 * GMLP_WIDTH:2 * GMLP_WIDTH + D_MODEL])
    gb = jax.nn.sigmoid(proj[:, 2 * GMLP_WIDTH + D_MODEL:])
    merged = ga * _dot(attn_ref[...], wa_ref[...]) + gb * _dot(sgu, wb_ref[...])
    mix = _dot(merged.astype(BF16), wo_ref[...]) + bo_ref[...]
    h = _layer_norm(alpha * x + mix, g1_ref[...], b1_ref[...])
    _store_token_rows(h_ref, h)
    _store_token_rows(hx_ref, _pack_bf16_pairs(h))

    a_hi = h.astype(BF16)
    a_lo = (h - a_hi.astype(F32)).astype(BF16)
    wr = wr_ref[...]
    w_hi = wr.astype(BF16)
    w_lo = (wr - w_hi.astype(F32)).astype(BF16)
    w_cat = jnp.concatenate([w_hi, w_lo], axis=1)
    r_hi = _dot(a_hi, w_cat)
    r_lo = _dot(a_lo, w_cat)
    logits = (r_hi[:, :LANES] + r_hi[:, LANES:]) + (r_lo[:, :LANES] + r_lo[:, LANES:]) + br_ref[...]

    lane_i = lax.broadcasted_iota(I32, (tm, LANES), 1)
    lane_f = lane_i.astype(F32)
    vals, idxs = [], []
    l = logits
    for _ in range(TOP_K):
        m = jnp.max(l, axis=1, keepdims=True)
        ix = jnp.min(jnp.where(l == m, lane_f, float(LANES)), axis=1, keepdims=True)
        vals.append(m)
        idxs.append(ix)
        l = jnp.where(lane_f == ix, -jnp.inf, l)
    es = [jnp.exp(v - vals[0]) for v in vals]
    den = es[0] + es[1] + es[2] + es[3]

    onehot = jnp.zeros((tm, LANES), F32)
    idx_out = jnp.zeros((tm, LANES), F32)
    gate_out = jnp.zeros((tm, LANES), F32)
    for k in range(TOP_K):
        onehot = onehot + jnp.where(lane_f == idxs[k], 1.0, 0.0)
        idx_out = jnp.where(lane_i == k, idxs[k], idx_out)
        gate_out = jnp.where(lane_i == k, es[k] / den, gate_out)

    tr = lax.broadcasted_iota(I32, (tm, tm), 0)
    tc = lax.broadcasted_iota(I32, (tm, tm), 1)
    strict = jnp.where(tc < tr, 1.0, 0.0).astype(BF16)
    carry = carry_ref[0:1, :]
    before = _dot(strict, onehot.astype(BF16)) + carry
    rank_out = jnp.zeros((tm, LANES), F32)
    for k in range(TOP_K):
        rk = jnp.sum(jnp.where(lane_f == idxs[k], before, 0.0), axis=1, keepdims=True)
        rank_out = jnp.where(lane_i == k, rk, rank_out)
    new_carry = carry + jnp.sum(onehot, axis=0, keepdims=True)
    carry_ref[...] = jnp.broadcast_to(new_carry, carry_ref.shape)
    cnt_ref[...] = jnp.broadcast_to(new_carry, cnt_ref.shape).astype(I32)
    idx_ref[...] = idx_out.astype(I32)
    gate_ref[...] = gate_out
    rank_ref[...] = rank_out.astype(I32)


def _mix(x2, attn, w2, b2, lng, lnb, ws, bs_tile, wa, wb, wo, bo, g1, b1, wr, br, tm, alpha, tile_off):
    T = attn.shape[0]
    n2 = w2.shape[1]
    const = lambda *shape: pl.BlockSpec(shape, lambda i: (0,) * len(shape))
    return pl.pallas_call(
        functools.partial(_mix_kernel, tm=tm, alpha=alpha),
        grid=(T // tm,),
        in_specs=[
            pl.BlockSpec((tm, D_MODEL), lambda i: (i + tile_off, 0)),
            pl.BlockSpec((tm, FOX_WIDTH), lambda i: (i, 0)),
            const(D_MODEL, n2), const(1, n2),
            const(1, GMLP_WIDTH), const(1, GMLP_WIDTH),
            const(GMLP_WIDTH // HEAD_DIM, GMLP_CHUNK, GMLP_CHUNK), const(GMLP_CHUNK, GMLP_WIDTH),
            const(FOX_WIDTH, D_MODEL), const(GMLP_WIDTH, D_MODEL),
            const(D_MODEL, D_MODEL), const(1, D_MODEL),
            const(1, D_MODEL), const(1, D_MODEL),
            const(D_MODEL, LANES), const(1, LANES),
        ],
        out_specs=[
            pl.BlockSpec((tm * ROW_TILE, LANES), lambda i: (i, 0)),
            pl.BlockSpec((tm * PACK_TILE, LANES), lambda i: (i, 0)),
            pl.BlockSpec((tm, LANES), lambda i: (i, 0)),
            pl.BlockSpec((tm, LANES), lambda i: (i, 0)),
            pl.BlockSpec((tm, LANES), lambda i: (i, 0)),
            pl.BlockSpec((ROW_TILE, LANES), lambda i: (0, 0)),
        ],
        out_shape=[
            jax.ShapeDtypeStruct((T * ROW_TILE, LANES), F32),
            jax.ShapeDtypeStruct((T * PACK_TILE, LANES), U32),
            jax.ShapeDtypeStruct((T, LANES), I32),
            jax.ShapeDtypeStruct((T, LANES), F32),
            jax.ShapeDtypeStruct((T, LANES), I32),
            jax.ShapeDtypeStruct((ROW_TILE, LANES), I32),
        ],
        scratch_shapes=[pltpu.VMEM((ROW_TILE, LANES), F32)],
        compiler_params=_cparams(("arbitrary",)),
        name="mix",
    )(x2, attn, w2, b2, lng, lnb, ws, bs_tile, wa, wb, wo, bo, g1, b1, wr, br)


PLAN_EXPERT, PLAN_VALID, PLAN_SLOT, PLAN_NEXT, PLAN_USED = range(5)


def _lane_cumsum(x):
    lane = lax.broadcasted_iota(I32, x.shape, 1)
    shift = 1
    while shift < LANES:
        x = x + jnp.where(lane >= shift, pltpu.roll(x, shift, 1), 0.0)
        shift *= 2
    return x


def _route_kernel(cnt_ref, idx_ref, rank_ref, dest_ref, plan_ref, start_ref, *, tm, bm, nbp):
    @pl.when(pl.program_id(0) == 0)
    def _():
        lane = lax.broadcasted_iota(I32, (ROW_TILE, LANES), 1)
        counts = jnp.where(lane < N_EXPERTS, cnt_ref[...].astype(F32), 0.0)
        padded = jnp.floor((counts + (bm - 1)) / bm) * bm
        pad_end = _lane_cumsum(padded)
        pad_start = pad_end - padded
        start_ref[...] = pad_start
        has_rows = jnp.where(counts > 0, 1.0, 0.0)
        order = _lane_cumsum(has_rows) - 1.0
        slot = order - 2.0 * jnp.floor(order * 0.5)
        total = jnp.sum(jnp.where(lane == N_EXPERTS - 1, pad_end, 0.0), axis=1, keepdims=True)[0:1, :]
        n_used = total / bm

        col = lambda r: jnp.transpose(r)[:, 0:1]
        sub = lax.broadcasted_iota(I32, (LANES, 1), 0)
        is_expert = sub < N_EXPERTS
        lane_w = lax.broadcasted_iota(I32, (LANES, LANES), 1)
        sub_w = lax.broadcasted_iota(I32, (LANES, LANES), 0)
        later = (lane_w > sub_w) & (lane_w < N_EXPERTS) & (has_rows[0:1, :] > 0)
        succ = jnp.min(jnp.where(later, lane_w, N_EXPERTS).astype(F32), axis=1, keepdims=True)
        succ = jnp.where(succ == N_EXPERTS, sub.astype(F32), succ)

        blk = lax.broadcasted_iota(I32, (1, nbp), 1).astype(F32)
        blk_src = jnp.minimum(blk, n_used - 1.0)
        blk_row = blk_src * bm
        below = is_expert & (col(pad_end) <= blk_row)
        blk_e = jnp.minimum(jnp.sum(jnp.where(below, 1.0, 0.0), axis=0, keepdims=True), N_EXPERTS - 1.0)
        hit = sub.astype(F32) == blk_e
        take = lambda c: jnp.sum(jnp.where(hit, c, 0.0), axis=0, keepdims=True)
        blk_valid = jnp.clip(take(col(pad_start + counts)) - blk_row, 0.0, bm)
        rows = [None] * ROW_TILE
        rows[PLAN_EXPERT], rows[PLAN_VALID] = blk_e, blk_valid
        rows[PLAN_SLOT], rows[PLAN_NEXT] = take(col(slot)), take(succ)
        rows[PLAN_USED] = jnp.broadcast_to(n_used, (1, nbp))
        zero = jnp.zeros((1, nbp), F32)
        plan_ref[...] = jnp.concatenate([zero if r is None else r for r in rows], axis=0).astype(I32)

    pad_start = start_ref[0:1, :]
    lane_i = lax.broadcasted_iota(I32, (tm, LANES), 1)
    lane_f = lane_i.astype(F32)
    idx = idx_ref[...].astype(F32)
    rank = rank_ref[...].astype(F32)
    dest = jnp.zeros((tm, LANES), F32)
    for k in range(TOP_K):
        base = jnp.sum(jnp.where(lane_f == idx[:, k:k + 1], pad_start, 0.0), axis=1, keepdims=True)
        dest = jnp.where(lane_i == k, base + rank[:, k:k + 1], dest)
    dest_ref[...] = jnp.transpose(dest)[:ROW_TILE, :].astype(I32)


def _route(cnt_o, idx_o, rank_o, bm, n_blk):
    Tp = idx_o.shape[0]
    tm = math.gcd(Tp, ROUTE_TILE)
    nbp = -(-n_blk // LANES) * LANES
    return pl.pallas_call(
        functools.partial(_route_kernel, tm=tm, bm=bm, nbp=nbp),
        grid=(Tp // tm,),
        in_specs=[
            pl.BlockSpec((ROW_TILE, LANES), lambda i: (0, 0)),
            pl.BlockSpec((tm, LANES), lambda i: (i, 0)),
            pl.BlockSpec((tm, LANES), lambda i: (i, 0)),
        ],
        out_specs=[
            pl.BlockSpec((ROW_TILE, tm), lambda i: (0, i)),
            pl.BlockSpec((ROW_TILE, nbp), lambda i: (0, 0)),
        ],
        out_shape=[
            jax.ShapeDtypeStruct((ROW_TILE, Tp), I32),
            jax.ShapeDtypeStruct((ROW_TILE, nbp), I32),
        ],
        scratch_shapes=[pltpu.VMEM((ROW_TILE, LANES), F32)],
        compiler_params=_cparams(("arbitrary",)),
        name="route",
    )(cnt_o, idx_o, rank_o)


def _sc_window_indices(idx):
    return jnp.pad(idx.reshape(-1, SC_WINDOW), ((0, 0), (0, LANES - SC_WINDOW)))


def _sc_scatter_rows(dest_km, src3, n_dst):
    n_slot, n_tok = dest_km.shape
    n_win = n_tok // SC_WINDOW
    row_shape = src3.shape[1:]
    mesh = plsc.VectorSubcoreMesh(core_axis_name="core", subcore_axis_name="subcore",
                                  num_cores=SC_CORES, num_subcores=SC_SUBCORES)

    @pl.kernel(out_type=jax.ShapeDtypeStruct((n_dst,) + row_shape, src3.dtype), mesh=mesh, name="sc_scatter")
    def scatter(src_hbm, i_hbm, o_hbm):
        def body(x_vmem, i_vmem):
            for k in range(n_slot):
                pltpu.sync_copy(x_vmem, o_hbm.at[i_vmem.at[k, pl.ds(0, SC_WINDOW)]])

        @pl.when(lax.axis_index("core") == 0)
        def _():
            pltpu.emit_pipeline(
                body,
                grid=(n_win,),
                in_specs=[pl.BlockSpec((SC_WINDOW,) + row_shape, lambda i: (i, 0, 0)),
                          pl.BlockSpec((ROW_TILE, LANES), lambda i: (i, 0))],
                out_specs=[],
                core_axis_name="subcore",
                dimension_semantics=(pltpu.PARALLEL,),
            )(src_hbm, i_hbm)

    idx = dest_km.reshape(n_slot, n_win, SC_WINDOW).transpose(1, 0, 2)
    idx = jnp.pad(idx, ((0, 0), (0, ROW_TILE - n_slot), (0, LANES - SC_WINDOW)))
    return scatter(src3, idx.reshape(n_win * ROW_TILE, LANES))


def _sc_gather_rows(sidx, src3, both_cores):
    n = sidx.shape[0]
    row_shape = src3.shape[1:]
    mesh = plsc.VectorSubcoreMesh(core_axis_name="core", subcore_axis_name="subcore",
                                  num_cores=SC_CORES, num_subcores=SC_SUBCORES)

    @pl.kernel(out_type=jax.ShapeDtypeStruct((n,) + row_shape, src3.dtype), mesh=mesh, name="sc_gather")
    def gather(src_hbm, i_hbm, o_hbm):
        def body(i_vmem, o_vmem):
            pltpu.sync_copy(src_hbm.at[i_vmem.at[0, pl.ds(0, SC_WINDOW)]], o_vmem)

        def run(axes):
            pltpu.emit_pipeline(
                body,
                grid=(n // SC_WINDOW,),
                in_specs=[pl.BlockSpec((1, LANES), lambda i: (i, 0))],
                out_specs=[pl.BlockSpec((SC_WINDOW,) + row_shape, lambda i: (i, 0, 0))],
                core_axis_name=axes,
                dimension_semantics=(pltpu.PARALLEL,),
            )(i_hbm, o_hbm)

        if both_cores:
            run(("core", "subcore"))
        else:
            pl.when(lax.axis_index("core") == 0)(lambda: run("subcore"))

    return gather(src3, _sc_window_indices(sidx))


def _moe_kernel(plan_ref, x_hbm, wgu_hbm, bgu_ref, wdn_hbm, bdn_ref, y_ref,
                x_buf, wgu_f32, wdn_f32, wgu_bf, wdn_bf, x_sems, sems):
    i = pl.program_id(0)
    n_used = plan_ref[PLAN_USED, i]
    expert = plan_ref[PLAN_EXPERT, i]
    active = i < n_used
    fresh = (i == 0) | (expert != plan_ref[PLAN_EXPERT, jnp.maximum(i - 1, 0)])
    slot = plan_ref[PLAN_SLOT, i]
    successor = plan_ref[PLAN_NEXT, i]
    rows = x_buf.shape[1]
    depth = x_buf.shape[0]

    def x_copy(blk):
        s = blk % depth
        return pltpu.make_async_copy(x_hbm.at[pl.ds(pl.multiple_of(blk * rows, rows), rows)],
                                     x_buf.at[s], x_sems.at[s])

    def weight_copies(e, s):
        return (pltpu.make_async_copy(wgu_hbm.at[e], wgu_f32.at[s], sems.at[s, 0]),
                pltpu.make_async_copy(wdn_hbm.at[e], wdn_f32.at[s], sems.at[s, 1]))

    @pl.when(i == 0)
    def _():
        for ahead in range(depth - 1):
            @pl.when(ahead < n_used)
            def _():
                x_copy(ahead).start()

    @pl.when(i + (depth - 1) < n_used)
    def _():
        x_copy(i + (depth - 1)).start()

    @pl.when(i == 0)
    def _():
        for c in weight_copies(expert, slot):
            c.start()

    @pl.when(active & fresh)
    def _():
        @pl.when(successor != expert)
        def _():
            for c in weight_copies(successor, 1 - slot):
                c.start()

        for c in weight_copies(expert, slot):
            c.wait()
        wgu_bf[...] = wgu_f32[slot].astype(BF16)
        wdn_bf[...] = wdn_f32[slot].astype(BF16)

    @pl.when(active)
    def _():
        bm = rows // PACK_TILE
        x_copy(i).wait()
        words = _load_token_rows(x_buf.at[i % depth], bm, r=PACK_TILE)
        valid = lax.broadcasted_iota(I32, (bm, 1), 0) < plan_ref[PLAN_VALID, i]
        x = _unpack_bf16_pairs(jnp.where(valid, words, jnp.uint32(0))).astype(BF16)
        gu = _dot(x, wgu_bf[...]) + bgu_ref[...]
        gate = jnp.minimum(gu[:, :D_FF], SWIGLU_LIMIT)
        up = jnp.clip(gu[:, D_FF:], -SWIGLU_LIMIT, SWIGLU_LIMIT)
        hid = (up + 1.0) * (gate * jax.nn.sigmoid(SWIGLU_ALPHA * gate))
        _store_token_rows(y_ref, _pack_bf16_pairs(_dot(hid.astype(BF16), wdn_bf[...]) + bdn_ref[...]))

    @pl.when(jnp.logical_not(active))
    def _():
        y_ref[...] = jnp.zeros_like(y_ref)


def _moe(plan, xpad, wgu, bgu, wdn, bdn, n_blk, bm):
    bias = lambda i, plan: (plan[PLAN_EXPERT, i], 0, 0)
    grid_spec = pltpu.PrefetchScalarGridSpec(
        num_scalar_prefetch=1,
        grid=(n_blk,),
        in_specs=[
            pl.BlockSpec(memory_space=pl.ANY),
            pl.BlockSpec(memory_space=pl.ANY),
            pl.BlockSpec((None, 1, 2 * D_FF), bias),
            pl.BlockSpec(memory_space=pl.ANY),
            pl.BlockSpec((None, 1, D_MODEL), bias),
        ],
        out_specs=pl.BlockSpec((bm * PACK_TILE, LANES), lambda i, plan: (i, 0)),
        scratch_shapes=[
            pltpu.VMEM((MOE_X_DEPTH, bm * PACK_TILE, LANES), U32),
            pltpu.VMEM((2, D_MODEL, 2 * D_FF), F32), pltpu.VMEM((2, D_FF, D_MODEL), F32),
            pltpu.VMEM((D_MODEL, 2 * D_FF), BF16), pltpu.VMEM((D_FF, D_MODEL), BF16),
            pltpu.SemaphoreType.DMA((MOE_X_DEPTH,)), pltpu.SemaphoreType.DMA((2, 2)),
        ],
    )
    return pl.pallas_call(
        _moe_kernel,
        grid_spec=grid_spec,
        out_shape=jax.ShapeDtypeStruct((n_blk * bm * PACK_TILE, LANES), U32),
        compiler_params=_cparams(("arbitrary",)),
        name="moe",
    )(plan, xpad, wgu, bgu, wdn, bdn)


def _final_kernel(h_ref, yg_ref, gate_ref, p_ref, wple_ref, wpg_ref, bpg_ref,
                  g2_ref, b2_ref, g3_ref, b3_ref, *rest, tm, alpha):
    o_ref = rest[-1]
    rc = FINAL_CHUNK
    for c in range(tm // rc):
        rows = slice(c * rc, (c + 1) * rc)
        gates = gate_ref[rows, :]
        z = alpha * _load_token_rows(h_ref, rc, c * rc)
        for k in range(TOP_K):
            y = _unpack_bf16_pairs(_load_token_rows(yg_ref.at[k], rc, c * rc, r=PACK_TILE))
            z = z + gates[:, k:k + 1] * y
        h2 = _layer_norm(z, g2_ref[...], b2_ref[...])
        emb = _dot(p_ref[rows, :].astype(BF16), wple_ref[...])
        pg = jax.nn.sigmoid(_dot(h2.astype(BF16), wpg_ref[...]) + bpg_ref[...])
        o_ref[rows, :] = _layer_norm(alpha * h2 + emb * pg, g3_ref[...], b3_ref[...])


def _final(h1, yg, gates, p2, wple, wpg, bpg, g2, b2, g3, b3, tm, alpha, tile_off, out_prev):
    T = h1.shape[0] // ROW_TILE
    const = lambda *shape: pl.BlockSpec(shape, lambda i: (0,) * len(shape))
    in_specs = [
        pl.BlockSpec((tm * ROW_TILE, LANES), lambda i: (i, 0)),
        pl.BlockSpec((TOP_K, tm * PACK_TILE, LANES), lambda i: (0, i, 0)),
        pl.BlockSpec((tm, LANES), lambda i: (i, 0)),
        pl.BlockSpec((tm, PLE_DIM), lambda i: (i + tile_off, 0)),
        const(PLE_DIM, D_MODEL), const(D_MODEL, D_MODEL), const(1, D_MODEL),
        const(1, D_MODEL), const(1, D_MODEL), const(1, D_MODEL), const(1, D_MODEL),
    ]
    args = [h1, yg, gates, p2, wple, wpg, bpg, g2, b2, g3, b3]
    aliases = {}
    if out_prev is not None:
        in_specs.append(pl.BlockSpec(memory_space=pl.ANY))
        aliases = {len(args): 0}
        args.append(out_prev)
    return pl.pallas_call(
        functools.partial(_final_kernel, tm=tm, alpha=alpha),
        grid=(T // tm,),
        in_specs=in_specs,
        out_specs=pl.BlockSpec((tm, D_MODEL), lambda i: (i + tile_off, 0)),
        out_shape=jax.ShapeDtypeStruct((p2.shape[0], D_MODEL), F32),
        input_output_aliases=aliases,
        compiler_params=_cparams(("arbitrary",)),
        name="final",
    )(*args)


def _layer(h2d, p2d, batch, seq, alpha, w_in, b_in, gmlp_ln_g, gmlp_ln_b, w_spatial, b_spatial,
           w_branch_a, w_branch_b, w_out, b_out, ln1_g, ln1_b, w_router, b_router,
           w_gate_up, b_gate_up, w_down, b_down, ln2_g, ln2_b, w_ple, w_ple_gate, b_ple_gate,
           ln3_g, ln3_b):
    T = batch * seq
    tm = math.gcd(T, ROW_TILE_A)
    bm = MOE_BLOCK
    off_f = 3 * FOX_WIDTH
    off_u = off_f + N_HEADS
    row = lambda v: v.reshape(1, -1).astype(F32)

    w1 = jnp.concatenate([w_in[:, :off_u], jnp.zeros((D_MODEL, LANES - N_HEADS), F32)], axis=1).astype(BF16)
    b1 = jnp.concatenate([b_in[:off_u], jnp.zeros((LANES - N_HEADS,), F32)]).reshape(1, -1)
    w2 = w_in[:, off_u:].astype(BF16)
    b2 = row(b_in[off_u:])
    bs_tile = jnp.repeat(b_spatial.T, HEAD_DIM, axis=1)
    wr = jnp.concatenate([w_router, jnp.zeros((D_MODEL, LANES - N_EXPERTS), F32)], axis=1)
    br = jnp.concatenate([b_router, jnp.full((LANES - N_EXPERTS,), -1e30, F32)]).reshape(1, -1)

    wsb, wab, wbb, wob = (w.astype(BF16) for w in (w_spatial, w_branch_a, w_branch_b, w_out))
    wpleb, wpgb = w_ple.astype(BF16), w_ple_gate.astype(BF16)
    bgu, bdn = b_gate_up.reshape(N_EXPERTS, 1, -1), b_down.reshape(N_EXPERTS, 1, -1)

    n_parts = N_PARTS if batch % N_PARTS == 0 else 1
    pb = batch // n_parts
    Tp = pb * seq
    tiles = Tp // tm
    n_assign = Tp * TOP_K
    n_blk = -(-n_assign // bm) + N_EXPERTS
    n_rows = n_blk * bm

    def front(part):
        q, k, v, f_pad = _qkvf(h2d, w1, b1, tm, Tp, part * tiles)
        auxq, auxk = _decay(f_pad, pb, seq)
        attn = _attention(q, auxq, k, auxk, v, pb, seq)
        h1, hx, idx_o, gate_o, rank_o, cnt_o = _mix(
            h2d, attn, w2, b2, row(gmlp_ln_g), row(gmlp_ln_b), wsb, bs_tile, wab, wbb, wob, row(b_out),
            row(ln1_g), row(ln1_b), wr, br, tm, alpha, part * tiles)
        dest8, plan = _route(cnt_o, idx_o, rank_o, bm, n_blk)
        dest_km = dest8[:TOP_K]
        xpad = _sc_scatter_rows(dest_km, hx.reshape(Tp, PACK_TILE, LANES), n_rows)
        return h1, gate_o, dest_km, plan, xpad

    def experts(part, state):
        h1, gate_o, dest_km, plan, xpad = state
        ypad = _moe(plan, xpad.reshape(n_rows * PACK_TILE, LANES), w_gate_up, bgu, w_down, bdn, n_blk, bm)
        yg = _sc_gather_rows(dest_km.reshape(-1), ypad.reshape(n_rows, PACK_TILE, LANES),
                             both_cores=part == n_parts - 1)
        return h1, gate_o, yg

    def back(part, state, out_prev):
        h1, gate_o, yg = state
        return _final(h1, yg.reshape(TOP_K, Tp * PACK_TILE, LANES), gate_o, p2d, wpleb, wpgb,
                      row(b_ple_gate), row(ln2_g), row(ln2_b), row(ln3_g), row(ln3_b), tm, alpha,
                      part * tiles, out_prev)

    fronts = [front(part) for part in range(n_parts)]
    mids = [experts(part, state) for part, state in enumerate(fronts)]
    out = None
    for part in range(n_parts):
        out = back(part, mids[part], out)
    return out


def kernel(x, p, w_in, b_in, gmlp_ln_g, gmlp_ln_b, w_spatial, b_spatial, w_branch_a, w_branch_b, w_out, b_out, ln1_g, ln1_b, w_router, b_router, w_gate_up, b_gate_up, w_down, b_down, ln2_g, ln2_b, w_ple, w_ple_gate, b_ple_gate, ln3_g, ln3_b):
    batch, seq, d = x.shape
    depth = w_in.shape[0]
    assert d == D_MODEL and seq % GMLP_CHUNK == 0
    alpha = (2.0 * depth) ** 0.25
    h = x.reshape(batch * seq, d)
    for i in range(depth):
        h = _layer(h, p[i].reshape(batch * seq, PLE_DIM), batch, seq, alpha,
                   w_in[i], b_in[i], gmlp_ln_g[i], gmlp_ln_b[i], w_spatial[i], b_spatial[i],
                   w_branch_a[i], w_branch_b[i], w_out[i], b_out[i], ln1_g[i], ln1_b[i],
                   w_router[i], b_router[i], w_gate_up[i], b_gate_up[i], w_down[i], b_down[i],
                   ln2_g[i], ln2_b[i], w_ple[i], w_ple_gate[i], b_ple_gate[i], ln3_g[i], ln3_b[i])
    return h.reshape(batch, seq, d)
```

```python
import functools
import math

import jax
import jax.numpy as jnp
import numpy as np
from jax import lax
from jax.experimental import pallas as pl
from jax.experimental.pallas import tpu as pltpu
from jax.experimental.pallas import tpu_sc as plsc

F32 = jnp.float32
BF16 = jnp.bfloat16
I32 = jnp.int32
U32 = jnp.uint32

D_MODEL = 1024
N_HEADS = 8
HEAD_DIM = 64
FOX_WIDTH = N_HEADS * HEAD_DIM
GMLP_WIDTH = 512
GMLP_CHUNK = 128
N_EXPERTS = 32
TOP_K = 4
D_FF = 1024
PLE_DIM = 256
SWIGLU_LIMIT = 7.0
SWIGLU_ALPHA = 1.702
LN_EPS = 1e-5
LANES = 128
ROW_TILE = 8
PACK_TILE = D_MODEL // 2 // LANES
VMEM_LIMIT = 56 * 1024 * 1024

MOE_BLOCK = 512
MOE_X_DEPTH = 3
ATTN_Q_TILE = 1024
ATTN_K_TILE = 1024
ATTN_PAIRS = 2
LOG2E = math.log2(math.e)
ROW_TILE_A = 512
SC_CORES = 2
SC_SUBCORES = 16
SC_WINDOW = 32
N_PARTS = 2
ROUTE_TILE = 1024
FINAL_CHUNK = 256


def _cparams(sem):
    return pltpu.CompilerParams(dimension_semantics=sem, vmem_limit_bytes=VMEM_LIMIT)


def _gelu(x):
    c = math.sqrt(2.0 / math.pi)
    return 0.5 * x * (1.0 + jnp.tanh(c * (x + 0.044715 * (x * x * x))))


def _layer_norm(x, g, b):
    mu = jnp.mean(x, axis=-1, keepdims=True)
    xc = x - mu
    var = jnp.mean(xc * xc, axis=-1, keepdims=True)
    return xc * lax.rsqrt(var + LN_EPS) * g + b


def _split3(x):
    hi = x.astype(BF16)
    r = x - hi.astype(F32)
    mid = r.astype(BF16)
    lo = (r - mid.astype(F32)).astype(BF16)
    return hi, mid, lo


def _dot(a, b):
    return jnp.dot(a, b, preferred_element_type=F32)


def _load_token_rows(ref, n, first=0, r=ROW_TILE):
    return jnp.concatenate([ref[pl.ds(first * r + j, n, stride=r), :] for j in range(r)], axis=1)


def _store_token_rows(ref, val):
    n, r = val.shape[0], val.shape[1] // LANES
    for j in range(r):
        ref[pl.ds(j, n, stride=r), :] = val[:, j * LANES:(j + 1) * LANES]


def _pack_bf16_pairs(x):
    half = x.shape[1] // 2
    hi = pltpu.bitcast(x[:, :half].astype(BF16).astype(F32), U32)
    lo = pltpu.bitcast(x[:, half:].astype(BF16).astype(F32), U32)
    return hi | (lo >> 16)


def _unpack_bf16_pairs(w):
    a = pltpu.bitcast(w & jnp.uint32(0xFFFF0000), F32)
    b = pltpu.bitcast(w << 16, F32)
    return jnp.concatenate([a, b], axis=1)


def _qkvf_kernel(x_ref, w_ref, b_ref, q_ref, k_ref, v_ref, f_ref):
    x = x_ref[...].astype(BF16)
    proj = _dot(x, w_ref[...]) + b_ref[...]
    q_ref[...] = (proj[:, :FOX_WIDTH] * (HEAD_DIM ** -0.5 * LOG2E)).astype(BF16)
    k_ref[...] = proj[:, FOX_WIDTH:2 * FOX_WIDTH].astype(BF16)
    v_ref[...] = proj[:, 2 * FOX_WIDTH:3 * FOX_WIDTH].astype(BF16)
    f_ref[...] = proj[:, 3 * FOX_WIDTH:]


def _qkvf(x2, w, b, tm, T, tile_off):
    n_out = w.shape[1]
    return pl.pallas_call(
        _qkvf_kernel,
        grid=(T // tm,),
        in_specs=[
            pl.BlockSpec((tm, D_MODEL), lambda i: (i + tile_off, 0)),
            pl.BlockSpec((D_MODEL, n_out), lambda i: (0, 0)),
            pl.BlockSpec((1, n_out), lambda i: (0, 0)),
        ],
        out_specs=[
            pl.BlockSpec((tm, FOX_WIDTH), lambda i: (i, 0)),
            pl.BlockSpec((tm, FOX_WIDTH), lambda i: (i, 0)),
            pl.BlockSpec((tm, FOX_WIDTH), lambda i: (i, 0)),
            pl.BlockSpec((tm, LANES), lambda i: (i, 0)),
        ],
        out_shape=[
            jax.ShapeDtypeStruct((T, FOX_WIDTH), BF16),
            jax.ShapeDtypeStruct((T, FOX_WIDTH), BF16),
            jax.ShapeDtypeStruct((T, FOX_WIDTH), BF16),
            jax.ShapeDtypeStruct((T, LANES), F32),
        ],
        compiler_params=_cparams(("arbitrary",)),
        name="qkvf",
    )(x2, w, b)


def _decay_placement():
    pq = np.zeros((3 * LANES, FOX_WIDTH), np.float32)
    pk = np.zeros((3 * LANES, FOX_WIDTH), np.float32)
    cq = np.zeros((1, FOX_WIDTH), np.float32)
    ck = np.zeros((1, FOX_WIDTH), np.float32)
    for h in range(N_HEADS):
        base = (h // 2) * LANES + (HEAD_DIM if h % 2 == 0 else 0)
        for piece in range(3):
            pq[piece * LANES + h, base + piece] = 1.0
            pk[piece * LANES + h, base + 3 + piece] = -1.0
            cq[0, base + 3 + piece] = 1.0
            ck[0, base + piece] = 1.0
    return pq, pk, cq, ck


def _decay_kernel(f_ref, pq_ref, pk_ref, cq_ref, ck_ref, auxq_ref, auxk_ref, *, seq, blk):
    r = lax.broadcasted_iota(I32, (blk, blk), 0)
    c = lax.broadcasted_iota(I32, (blk, blk), 1)
    tri = jnp.where(c <= r, 1.0, 0.0).astype(BF16)
    carry = jnp.zeros((1, LANES), F32)
    for i in range(seq // blk):
        f = f_ref[i * blk:(i + 1) * blk, :]
        ls = jnp.minimum(f, 0.0) - jnp.log1p(jnp.exp(-jnp.abs(f)))
        hi, mid, lo = _split3(ls)
        cs = _dot(tri, hi) + _dot(tri, mid) + _dot(tri, lo) + carry
        carry = cs[blk - 1:blk, :]
        pieces = jnp.concatenate(_split3(cs * LOG2E), axis=1)
        auxq_ref[i * blk:(i + 1) * blk, :] = (_dot(pieces, pq_ref[...]) + cq_ref[...]).astype(BF16)
        auxk_ref[i * blk:(i + 1) * blk, :] = (_dot(pieces, pk_ref[...]) + ck_ref[...]).astype(BF16)


def _decay(f_pad, batch, seq):
    blk = 256 if seq % 256 == 0 else LANES
    pq, pk, cq, ck = _decay_placement()
    const = lambda *shape: pl.BlockSpec(shape, lambda b: (0,) * len(shape))
    return pl.pallas_call(
        functools.partial(_decay_kernel, seq=seq, blk=blk),
        grid=(batch,),
        in_specs=[
            pl.BlockSpec((seq, LANES), lambda b: (b, 0)),
            const(3 * LANES, FOX_WIDTH), const(3 * LANES, FOX_WIDTH), const(1, FOX_WIDTH), const(1, FOX_WIDTH),
        ],
        out_specs=[
            pl.BlockSpec((seq, FOX_WIDTH), lambda b: (b, 0)),
            pl.BlockSpec((seq, FOX_WIDTH), lambda b: (b, 0)),
        ],
        out_shape=[
            jax.ShapeDtypeStruct((batch * seq, FOX_WIDTH), BF16),
            jax.ShapeDtypeStruct((batch * seq, FOX_WIDTH), BF16),
        ],
        compiler_params=_cparams(("arbitrary",)),
        name="decay",
    )(f_pad, jnp.asarray(pq, BF16), jnp.asarray(pk, BF16), jnp.asarray(cq), jnp.asarray(ck))


def _attn_kernel(q_ref, auxq_ref, k_ref, auxk_ref, v_ref, o_ref, *, tq, tk):
    qi = pl.program_id(2)
    n_pairs = q_ref.shape[1] // LANES
    low_q = lax.broadcasted_iota(I32, (tq, LANES), 1) < HEAD_DIM
    qs = []
    for pr in range(n_pairs):
        q = q_ref[:, pr * LANES:(pr + 1) * LANES]
        aq = auxq_ref[:, pr * LANES:(pr + 1) * LANES]
        qs += [jnp.where(low_q, q, aq), jnp.where(low_q, aq, q)]

    def step(start, n, carry, masked):
        low_k = lax.broadcasted_iota(I32, (n, LANES), 1) < HEAD_DIM
        ones = jnp.ones((n, LANES), BF16)
        out = []
        for pr in range(n_pairs):
            lanes = slice(pr * LANES, (pr + 1) * LANES)
            kb = k_ref[pl.ds(start, n), lanes]
            ak = auxk_ref[pl.ds(start, n), lanes]
            vb = v_ref[pl.ds(start, n), lanes]
            ks = (jnp.where(low_k, kb, ak), jnp.where(low_k, ak, kb))
            vs = (jnp.where(low_k, vb, ones), jnp.where(low_k, ones, vb))
            for j in range(2):
                m, acc = carry[2 * pr + j]
                s = lax.dot_general(qs[2 * pr + j], ks[j], (((1,), (1,)), ((), ())),
                                    preferred_element_type=F32)
                if masked:
                    row = lax.broadcasted_iota(I32, (tq, n), 0)
                    col = lax.broadcasted_iota(I32, (tq, n), 1)
                    s = jnp.where(col + (start - qi * tq) <= row, s, -jnp.inf)
                m_new = jnp.maximum(m, jnp.max(s, axis=1, keepdims=True))
                p = jnp.exp2(s - m_new)
                acc = jnp.exp2(m - m_new) * acc + _dot(p.astype(BF16), vs[j])
                out.append((m_new, acc))
        return tuple(out)

    init = tuple((jnp.full((tq, 1), -jnp.inf, F32), jnp.zeros((tq, LANES), F32))
                 for _ in range(2 * n_pairs))
    carry = lax.fori_loop(0, qi, lambda t, c: step(pl.multiple_of(t * tq, tq), tq, c, False), init)
    for d in range(tq // tk):
        carry = step(pl.multiple_of(qi * tq + d * tk, tk), tk, carry, True)
    for pr in range(n_pairs):
        acc0, acc1 = carry[2 * pr][1], carry[2 * pr + 1][1]
        out0 = acc0 / acc0[:, HEAD_DIM:HEAD_DIM + 1]
        out1 = acc1 / acc1[:, 0:1]
        o_ref[:, pr * LANES:(pr + 1) * LANES] = jnp.where(low_q, out0, out1).astype(BF16)


def _attention(q, auxq, k, auxk, v, batch, seq):
    tq = math.gcd(seq, ATTN_Q_TILE)
    tk = math.gcd(tq, ATTN_K_TILE)
    nq = seq // tq
    T = batch * seq
    width = ATTN_PAIRS * LANES
    q_spec = pl.BlockSpec((tq, width), lambda b, hp, qi: (b * nq + qi, hp))
    kv_spec = pl.BlockSpec((seq, width), lambda b, hp, qi: (b, hp))
    return pl.pallas_call(
        functools.partial(_attn_kernel, tq=tq, tk=tk),
        grid=(batch, N_HEADS // (2 * ATTN_PAIRS), nq),
        in_specs=[q_spec, q_spec, kv_spec, kv_spec, kv_spec],
        out_specs=q_spec,
        out_shape=jax.ShapeDtypeStruct((T, FOX_WIDTH), BF16),
        compiler_params=_cparams(("arbitrary", "arbitrary", "arbitrary")),
        name="attn",
    )(q, auxq, k, auxk, v)


def _mix_kernel(x_ref, attn_ref, w2_ref, b2_ref, lng_ref, lnb_ref, ws_ref, bs_ref,
                wa_ref, wb_ref, wo_ref, bo_ref, g1_ref, b1_ref, wr_ref, br_ref,
                h_ref, hx_ref, idx_ref, gate_ref, rank_ref, cnt_ref, carry_ref, *, tm, alpha):
    i = pl.program_id(0)

    @pl.when(i == 0)
    def _():
        carry_ref[...] = jnp.zeros_like(carry_ref)

    x = x_ref[...]
    proj = _dot(x.astype(BF16), w2_ref[...]) + b2_ref[...]
    u = _gelu(proj[:, :GMLP_WIDTH])
    gv = _gelu(proj[:, GMLP_WIDTH:2 * GMLP_WIDTH])
    vln = _layer_norm(gv, lng_ref[...], lnb_ref[...]).astype(BF16)

    cr = lax.broadcasted_iota(I32, (GMLP_CHUNK, GMLP_CHUNK), 0)
    cc = lax.broadcasted_iota(I32, (GMLP_CHUNK, GMLP_CHUNK), 1)
    tril = cc <= cr
    lo_half = cc < HEAD_DIM
    zero_w = jnp.zeros((GMLP_CHUNK, GMLP_CHUNK), BF16)
    n_slab = GMLP_WIDTH // LANES
    lhs = []
    for s in range(n_slab):
        w0 = jnp.where(tril, ws_ref[2 * s], zero_w)
        w1 = jnp.where(tril, ws_ref[2 * s + 1], zero_w)
        lhs.append(jnp.concatenate([w0, w1], axis=1))
    bs = bs_ref[...]
    rows = []
    for c in range(tm // GMLP_CHUNK):
        cols = []
        for s in range(n_slab):
            vs = vln[c * GMLP_CHUNK:(c + 1) * GMLP_CHUNK, s * LANES:(s + 1) * LANES]
            rhs = jnp.concatenate([jnp.where(lo_half, vs, zero_w), jnp.where(lo_half, zero_w, vs)], axis=0)
            cols.append(_dot(lhs[s], rhs))
        rows.append(jnp.concatenate(cols, axis=1) + bs)
    sp = jnp.concatenate(rows, axis=0) if len(rows) > 1 else rows[0]
    sgu = (u * sp).astype(BF16)

    ga = jax.nn.sigmoid(proj[:, 2 * GMLP_WIDTH:2 * GMLP_WIDTH + D_MODEL])
    gb = jax.nn.sigmoid(proj[:, 2 * GMLP_WIDTH + D_MODEL:])
    merged = ga * _dot(attn_ref[...], wa_ref[...]) + gb * _dot(sgu, wb_ref[...])
    mix = _dot(merged.astype(BF16), wo_ref[...]) + bo_ref[...]
    h = _layer_norm(alpha * x + mix, g1_ref[...], b1_ref[...])
    _store_token_rows(h_ref, h)
    _store_token_rows(hx_ref, _pack_bf16_pairs(h))

    a_hi = h.astype(BF16)
    a_lo = (h - a_hi.astype(F32)).astype(BF16)
    wr = wr_ref[...]
    w_hi = wr.astype(BF16)
    w_lo = (wr - w_hi.astype(F32)).astype(BF16)
    w_cat = jnp.concatenate([w_hi, w_lo], axis=1)
    r_hi = _dot(a_hi, w_cat)
    r_lo = _dot(a_lo, w_cat)
    logits = (r_hi[:, :LANES] + r_hi[:, LANES:]) + (r_lo[:, :LANES] + r_lo[:, LANES:]) + br_ref[...]

    lane_i = lax.broadcasted_iota(I32, (tm, LANES), 1)
    lane_f = lane_i.astype(F32)
    vals, idxs = [], []
    l = logits
    for _ in range(TOP_K):
        m = jnp.max(l, axis=1, keepdims=True)
        ix = jnp.min(jnp.where(l == m, lane_f, float(LANES)), axis=1, keepdims=True)
        vals.append(m)
        idxs.append(ix)
        l = jnp.where(lane_f == ix, -jnp.inf, l)
    es = [jnp.exp(v - vals[0]) for v in vals]
    den = es[0] + es[1] + es[2] + es[3]

    onehot = jnp.zeros((tm, LANES), F32)
    idx_out = jnp.zeros((tm, LANES), F32)
    gate_out = jnp.zeros((tm, LANES), F32)
    for k in range(TOP_K):
        onehot = onehot + jnp.where(lane_f == idxs[k], 1.0, 0.0)
        idx_out = jnp.where(lane_i == k, idxs[k], idx_out)
        gate_out = jnp.where(lane_i == k, es[k] / den, gate_out)

    tr = lax.broadcasted_iota(I32, (tm, tm), 0)
    tc = lax.broadcasted_iota(I32, (tm, tm), 1)
    strict = jnp.where(tc < tr, 1.0, 0.0).astype(BF16)
    carry = carry_ref[0:1, :]
    before = _dot(strict, onehot.astype(BF16)) + carry
    rank_out = jnp.zeros((tm, LANES), F32)
    for k in range(TOP_K):
        rk = jnp.sum(jnp.where(lane_f == idxs[k], before, 0.0), axis=1, keepdims=True)
        rank_out = jnp.where(lane_i == k, rk, rank_out)
    new_carry = carry + jnp.sum(onehot, axis=0, keepdims=True)
    carry_ref[...] = jnp.broadcast_to(new_carry, carry_ref.shape)
    cnt_ref[...] = jnp.broadcast_to(new_carry, cnt_ref.shape).astype(I32)
    idx_ref[...] = idx_out.astype(I32)
    gate_ref[...] = gate_out
    rank_ref[...] = rank_out.astype(I32)


def _mix(x2, attn, w2, b2, lng, lnb, ws, bs_tile, wa, wb, wo, bo, g1, b1, wr, br, tm, alpha, tile_off):
    T = attn.shape[0]
    n2 = w2.shape[1]
    const = lambda *shape: pl.BlockSpec(shape, lambda i: (0,) * len(shape))
    return pl.pallas_call(
        functools.partial(_mix_kernel, tm=tm, alpha=alpha),
        grid=(T // tm,),
        in_specs=[
            pl.BlockSpec((tm, D_MODEL), lambda i: (i + tile_off, 0)),
            pl.BlockSpec((tm, FOX_WIDTH), lambda i: (i, 0)),
            const(D_MODEL, n2), const(1, n2),
            const(1, GMLP_WIDTH), const(1, GMLP_WIDTH),
            const(GMLP_WIDTH // HEAD_DIM, GMLP_CHUNK, GMLP_CHUNK), const(GMLP_CHUNK, GMLP_WIDTH),
            const(FOX_WIDTH, D_MODEL), const(GMLP_WIDTH, D_MODEL),
            const(D_MODEL, D_MODEL), const(1, D_MODEL),
            const(1, D_MODEL), const(1, D_MODEL),
            const(D_MODEL, LANES), const(1, LANES),
        ],
        out_specs=[
            pl.BlockSpec((tm * ROW_TILE, LANES), lambda i: (i, 0)),
            pl.BlockSpec((tm * PACK_TILE, LANES), lambda i: (i, 0)),
            pl.BlockSpec((tm, LANES), lambda i: (i, 0)),
            pl.BlockSpec((tm, LANES), lambda i: (i, 0)),
            pl.BlockSpec((tm, LANES), lambda i: (i, 0)),
            pl.BlockSpec((ROW_TILE, LANES), lambda i: (0, 0)),
        ],
        out_shape=[
            jax.ShapeDtypeStruct((T * ROW_TILE, LANES), F32),
            jax.ShapeDtypeStruct((T * PACK_TILE, LANES), U32),
            jax.ShapeDtypeStruct((T, LANES), I32),
            jax.ShapeDtypeStruct((T, LANES), F32),
            jax.ShapeDtypeStruct((T, LANES), I32),
            jax.ShapeDtypeStruct((ROW_TILE, LANES), I32),
        ],
        scratch_shapes=[pltpu.VMEM((ROW_TILE, LANES), F32)],
        compiler_params=_cparams(("arbitrary",)),
        name="mix",
    )(x2, attn, w2, b2, lng, lnb, ws, bs_tile, wa, wb, wo, bo, g1, b1, wr, br)


PLAN_EXPERT, PLAN_VALID, PLAN_SLOT, PLAN_NEXT, PLAN_USED = range(5)


def _lane_cumsum(x):
    lane = lax.broadcasted_iota(I32, x.shape, 1)
    shift = 1
    while shift < LANES:
        x = x + jnp.where(lane >= shift, pltpu.roll(x, shift, 1), 0.0)
        shift *= 2
    return x


def _route_kernel(cnt_ref, idx_ref, rank_ref, dest_ref, plan_ref, start_ref, *, tm, bm, nbp):
    @pl.when(pl.program_id(0) == 0)
    def _():
        lane = lax.broadcasted_iota(I32, (ROW_TILE, LANES), 1)
        counts = jnp.where(lane < N_EXPERTS, cnt_ref[...].astype(F32), 0.0)
        padded = jnp.floor((counts + (bm - 1)) / bm) * bm
        pad_end = _lane_cumsum(padded)
        pad_start = pad_end - padded
        start_ref[...] = pad_start
        has_rows = jnp.where(counts > 0, 1.0, 0.0)
        order = _lane_cumsum(has_rows) - 1.0
        slot = order - 2.0 * jnp.floor(order * 0.5)
        total = jnp.sum(jnp.where(lane == N_EXPERTS - 1, pad_end, 0.0), axis=1, keepdims=True)[0:1, :]
        n_used = total / bm

        col = lambda r: jnp.transpose(r)[:, 0:1]
        sub = lax.broadcasted_iota(I32, (LANES, 1), 0)
        is_expert = sub < N_EXPERTS
        lane_w = lax.broadcasted_iota(I32, (LANES, LANES), 1)
        sub_w = lax.broadcasted_iota(I32, (LANES, LANES), 0)
        later = (lane_w > sub_w) & (lane_w < N_EXPERTS) & (has_rows[0:1, :] > 0)
        succ = jnp.min(jnp.where(later, lane_w, N_EXPERTS).astype(F32), axis=1, keepdims=True)
        succ = jnp.where(succ == N_EXPERTS, sub.astype(F32), succ)

        blk = lax.broadcasted_iota(I32, (1, nbp), 1).astype(F32)
        blk_src = jnp.minimum(blk, n_used - 1.0)
        blk_row = blk_src * bm
        below = is_expert & (col(pad_end) <= blk_row)
        blk_e = jnp.minimum(jnp.sum(jnp.where(below, 1.0, 0.0), axis=0, keepdims=True), N_EXPERTS - 1.0)
        hit = sub.astype(F32) == blk_e
        take = lambda c: jnp.sum(jnp.where(hit, c, 0.0), axis=0, keepdims=True)
        blk_valid = jnp.clip(take(col(pad_start + counts)) - blk_row, 0.0, bm)
        rows = [None] * ROW_TILE
        rows[PLAN_EXPERT], rows[PLAN_VALID] = blk_e, blk_valid
        rows[PLAN_SLOT], rows[PLAN_NEXT] = take(col(slot)), take(succ)
        rows[PLAN_USED] = jnp.broadcast_to(n_used, (1, nbp))
        zero = jnp.zeros((1, nbp), F32)
        plan_ref[...] = jnp.concatenate([zero if r is None else r for r in rows], axis=0).astype(I32)

    pad_start = start_ref[0:1, :]
    lane_i = lax.broadcasted_iota(I32, (tm, LANES), 1)
    lane_f = lane_i.astype(F32)
    idx = idx_ref[...].astype(F32)
    rank = rank_ref[...].astype(F32)
    dest = jnp.zeros((tm, LANES), F32)
    for k in range(TOP_K):
        base = jnp.sum(jnp.where(lane_f == idx[:, k:k + 1], pad_start, 0.0), axis=1, keepdims=True)
        dest = jnp.where(lane_i == k, base + rank[:, k:k + 1], dest)
    dest_ref[...] = jnp.transpose(dest)[:ROW_TILE, :].astype(I32)


def _route(cnt_o, idx_o, rank_o, bm, n_blk):
    Tp = idx_o.shape[0]
    tm = math.gcd(Tp, ROUTE_TILE)
    nbp = -(-n_blk // LANES) * LANES
    return pl.pallas_call(
        functools.partial(_route_kernel, tm=tm, bm=bm, nbp=nbp),
        grid=(Tp // tm,),
        in_specs=[
            pl.BlockSpec((ROW_TILE, LANES), lambda i: (0, 0)),
            pl.BlockSpec((tm, LANES), lambda i: (i, 0)),
            pl.BlockSpec((tm, LANES), lambda i: (i, 0)),
        ],
        out_specs=[
            pl.BlockSpec((ROW_TILE, tm), lambda i: (0, i)),
            pl.BlockSpec((ROW_TILE, nbp), lambda i: (0, 0)),
        ],
        out_shape=[
            jax.ShapeDtypeStruct((ROW_TILE, Tp), I32),
            jax.ShapeDtypeStruct((ROW_TILE, nbp), I32),
        ],
        scratch_shapes=[pltpu.VMEM((ROW_TILE, LANES), F32)],
        compiler_params=_cparams(("arbitrary",)),
        name="route",
    )(cnt_o, idx_o, rank_o)


def _sc_window_indices(idx):
    return jnp.pad(idx.reshape(-1, SC_WINDOW), ((0, 0), (0, LANES - SC_WINDOW)))


def _sc_scatter_rows(dest_km, src3, n_dst):
    n_slot, n_tok = dest_km.shape
    n_win = n_tok // SC_WINDOW
    row_shape = src3.shape[1:]
    mesh = plsc.VectorSubcoreMesh(core_axis_name="core", subcore_axis_name="subcore",
                                  num_cores=SC_CORES, num_subcores=SC_SUBCORES)

    @pl.kernel(out_type=jax.ShapeDtypeStruct((n_dst,) + row_shape, src3.dtype), mesh=mesh, name="sc_scatter")
    def scatter(src_hbm, i_hbm, o_hbm):
        def body(x_vmem, i_vmem):
            for k in range(n_slot):
                pltpu.sync_copy(x_vmem, o_hbm.at[i_vmem.at[k, pl.ds(0, SC_WINDOW)]])

        @pl.when(lax.axis_index("core") == 0)
        def _():
            pltpu.emit_pipeline(
                body,
                grid=(n_win,),
                in_specs=[pl.BlockSpec((SC_WINDOW,) + row_shape, lambda i: (i, 0, 0)),
                          pl.BlockSpec((ROW_TILE, LANES), lambda i: (i, 0))],
                out_specs=[],
                core_axis_name="subcore",
                dimension_semantics=(pltpu.PARALLEL,),
            )(src_hbm, i_hbm)

    idx = dest_km.reshape(n_slot, n_win, SC_WINDOW).transpose(1, 0, 2)
    idx = jnp.pad(idx, ((0, 0), (0, ROW_TILE - n_slot), (0, LANES - SC_WINDOW)))
    return scatter(src3, idx.reshape(n_win * ROW_TILE, LANES))


def _sc_gather_rows(sidx, src3, both_cores):
    n = sidx.shape[0]
    row_shape = src3.shape[1:]
    mesh = plsc.VectorSubcoreMesh(core_axis_name="core", subcore_axis_name="subcore",
                                  num_cores=SC_CORES, num_subcores=SC_SUBCORES)

    @pl.kernel(out_type=jax.ShapeDtypeStruct((n,) + row_shape, src3.dtype), mesh=mesh, name="sc_gather")
    def gather(src_hbm, i_hbm, o_hbm):
        def body(i_vmem, o_vmem):
            pltpu.sync_copy(src_hbm.at[i_vmem.at[0, pl.ds(0, SC_WINDOW)]], o_vmem)

        def run(axes):
            pltpu.emit_pipeline(
                body,
                grid=(n // SC_WINDOW,),
                in_specs=[pl.BlockSpec((1, LANES), lambda i: (i, 0))],
                out_specs=[pl.BlockSpec((SC_WINDOW,) + row_shape, lambda i: (i, 0, 0))],
                core_axis_name=axes,
                dimension_semantics=(pltpu.PARALLEL,),
            )(i_hbm, o_hbm)

        if both_cores:
            run(("core", "subcore"))
        else:
            pl.when(lax.axis_index("core") == 0)(lambda: run("subcore"))

    return gather(src3, _sc_window_indices(sidx))


def _moe_kernel(plan_ref, x_hbm, wgu_hbm, bgu_ref, wdn_hbm, bdn_ref, y_ref,
                x_buf, wgu_f32, wdn_f32, wgu_bf, wdn_bf, x_sems, sems):
    i = pl.program_id(0)
    n_used = plan_ref[PLAN_USED, i]
    expert = plan_ref[PLAN_EXPERT, i]
    active = i < n_used
    fresh = (i == 0) | (expert != plan_ref[PLAN_EXPERT, jnp.maximum(i - 1, 0)])
    slot = plan_ref[PLAN_SLOT, i]
    successor = plan_ref[PLAN_NEXT, i]
    rows = x_buf.shape[1]
    depth = x_buf.shape[0]

    def x_copy(blk):
        s = blk % depth
        return pltpu.make_async_copy(x_hbm.at[pl.ds(pl.multiple_of(blk * rows, rows), rows)],
                                     x_buf.at[s], x_sems.at[s])

    def weight_copies(e, s):
        return (pltpu.make_async_copy(wgu_hbm.at[e], wgu_f32.at[s], sems.at[s, 0]),
                pltpu.make_async_copy(wdn_hbm.at[e], wdn_f32.at[s], sems.at[s, 1]))

    @pl.when(i == 0)
    def _():
        for ahead in range(depth - 1):
            @pl.when(ahead < n_used)
            def _():
                x_copy(ahead).start()

    @pl.when(i + (depth - 1) < n_used)
    def _():
        x_copy(i + (depth - 1)).start()

    @pl.when(i == 0)
    def _():
        for c in weight_copies(expert, slot):
            c.start()

    @pl.when(active & fresh)
    def _():
        @pl.when(successor != expert)
        def _():
            for c in weight_copies(successor, 1 - slot):
                c.start()

        for c in weight_copies(expert, slot):
            c.wait()
        wgu_bf[...] = wgu_f32[slot].astype(BF16)
        wdn_bf[...] = wdn_f32[slot].astype(BF16)

    @pl.when(active)
    def _():
        bm = rows // PACK_TILE
        x_copy(i).wait()
        words = _load_token_rows(x_buf.at[i % depth], bm, r=PACK_TILE)
        valid = lax.broadcasted_iota(I32, (bm, 1), 0) < plan_ref[PLAN_VALID, i]
        x = _unpack_bf16_pairs(jnp.where(valid, words, jnp.uint32(0))).astype(BF16)
        gu = _dot(x, wgu_bf[...]) + bgu_ref[...]
        gate = jnp.minimum(gu[:, :D_FF], SWIGLU_LIMIT)
        up = jnp.clip(gu[:, D_FF:], -SWIGLU_LIMIT, SWIGLU_LIMIT)
        hid = (up + 1.0) * (gate * jax.nn.sigmoid(SWIGLU_ALPHA * gate))
        _store_token_rows(y_ref, _pack_bf16_pairs(_dot(hid.astype(BF16), wdn_bf[...]) + bdn_ref[...]))

    @pl.when(jnp.logical_not(active))
    def _():
        y_ref[...] = jnp.zeros_like(y_ref)


def _moe(plan, xpad, wgu, bgu, wdn, bdn, n_blk, bm):
    bias = lambda i, plan: (plan[PLAN_EXPERT, i], 0, 0)
    grid_spec = pltpu.PrefetchScalarGridSpec(
        num_scalar_prefetch=1,
        grid=(n_blk,),
        in_specs=[
            pl.BlockSpec(memory_space=pl.ANY),
            pl.BlockSpec(memory_space=pl.ANY),
            pl.BlockSpec((None, 1, 2 * D_FF), bias),
            pl.BlockSpec(memory_space=pl.ANY),
            pl.BlockSpec((None, 1, D_MODEL), bias),
        ],
        out_specs=pl.BlockSpec((bm * PACK_TILE, LANES), lambda i, plan: (i, 0)),
        scratch_shapes=[
            pltpu.VMEM((MOE_X_DEPTH, bm * PACK_TILE, LANES), U32),
            pltpu.VMEM((2, D_MODEL, 2 * D_FF), F32), pltpu.VMEM((2, D_FF, D_MODEL), F32),
            pltpu.VMEM((D_MODEL, 2 * D_FF), BF16), pltpu.VMEM((D_FF, D_MODEL), BF16),
            pltpu.SemaphoreType.DMA((MOE_X_DEPTH,)), pltpu.SemaphoreType.DMA((2, 2)),
        ],
    )
    return pl.pallas_call(
        _moe_kernel,
        grid_spec=grid_spec,
        out_shape=jax.ShapeDtypeStruct((n_blk * bm * PACK_TILE, LANES), U32),
        compiler_params=_cparams(("arbitrary",)),
        name="moe",
    )(plan, xpad, wgu, bgu, wdn, bdn)


def _final_kernel(h_ref, yg_ref, gate_ref, p_ref, wple_ref, wpg_ref, bpg_ref,
                  g2_ref, b2_ref, g3_ref, b3_ref, *rest, tm, alpha):
    o_ref = rest[-1]
    rc = FINAL_CHUNK
    for c in range(tm // rc):
        rows = slice(c * rc, (c + 1) * rc)
        gates = gate_ref[rows, :]
        z = alpha * _load_token_rows(h_ref, rc, c * rc)
        for k in range(TOP_K):
            y = _unpack_bf16_pairs(_load_token_rows(yg_ref.at[k], rc, c * rc, r=PACK_TILE))
            z = z + gates[:, k:k + 1] * y
        h2 = _layer_norm(z, g2_ref[...], b2_ref[...])
        emb = _dot(p_ref[rows, :].astype(BF16), wple_ref[...])
        pg = jax.nn.sigmoid(_dot(h2.astype(BF16), wpg_ref[...]) + bpg_ref[...])
        o_ref[rows, :] = _layer_norm(alpha * h2 + emb * pg, g3_ref[...], b3_ref[...])


def _final(h1, yg, gates, p2, wple, wpg, bpg, g2, b2, g3, b3, tm, alpha, tile_off, out_prev):
    T = h1.shape[0] // ROW_TILE
    const = lambda *shape: pl.BlockSpec(shape, lambda i: (0,) * len(shape))
    in_specs = [
        pl.BlockSpec((tm * ROW_TILE, LANES), lambda i: (i, 0)),
        pl.BlockSpec((TOP_K, tm * PACK_TILE, LANES), lambda i: (0, i, 0)),
        pl.BlockSpec((tm, LANES), lambda i: (i, 0)),
        pl.BlockSpec((tm, PLE_DIM), lambda i: (i + tile_off, 0)),
        const(PLE_DIM, D_MODEL), const(D_MODEL, D_MODEL), const(1, D_MODEL),
        const(1, D_MODEL), const(1, D_MODEL), const(1, D_MODEL), const(1, D_MODEL),
    ]
    args = [h1, yg, gates, p2, wple, wpg, bpg, g2, b2, g3, b3]
    aliases = {}
    if out_prev is not None:
        in_specs.append(pl.BlockSpec(memory_space=pl.ANY))
        aliases = {len(args): 0}
        args.append(out_prev)
    return pl.pallas_call(
        functools.partial(_final_kernel, tm=tm, alpha=alpha),
        grid=(T // tm,),
        in_specs=in_specs,
        out_specs=pl.BlockSpec((tm, D_MODEL), lambda i: (i + tile_off, 0)),
        out_shape=jax.ShapeDtypeStruct((p2.shape[0], D_MODEL), F32),
        input_output_aliases=aliases,
        compiler_params=_cparams(("arbitrary",)),
        name="final",
    )(*args)


def _layer(h2d, p2d, batch, seq, alpha, w_in, b_in, gmlp_ln_g, gmlp_ln_b, w_spatial, b_spatial,
           w_branch_a, w_branch_b, w_out, b_out, ln1_g, ln1_b, w_router, b_router,
           w_gate_up, b_gate_up, w_down, b_down, ln2_g, ln2_b, w_ple, w_ple_gate, b_ple_gate,
           ln3_g, ln3_b):
    T = batch * seq
    tm = math.gcd(T, ROW_TILE_A)
    bm = MOE_BLOCK
    off_f = 3 * FOX_WIDTH
    off_u = off_f + N_HEADS
    row = lambda v: v.reshape(1, -1).astype(F32)

    w1 = jnp.concatenate([w_in[:, :off_u], jnp.zeros((D_MODEL, LANES - N_HEADS), F32)], axis=1).astype(BF16)
    b1 = jnp.concatenate([b_in[:off_u], jnp.zeros((LANES - N_HEADS,), F32)]).reshape(1, -1)
    w2 = w_in[:, off_u:].astype(BF16)
    b2 = row(b_in[off_u:])
    bs_tile = jnp.repeat(b_spatial.T, HEAD_DIM, axis=1)
    wr = jnp.concatenate([w_router, jnp.zeros((D_MODEL, LANES - N_EXPERTS), F32)], axis=1)
    br = jnp.concatenate([b_router, jnp.full((LANES - N_EXPERTS,), -1e30, F32)]).reshape(1, -1)

    wsb, wab, wbb, wob = (w.astype(BF16) for w in (w_spatial, w_branch_a, w_branch_b, w_out))
    wpleb, wpgb = w_ple.astype(BF16), w_ple_gate.astype(BF16)
    bgu, bdn = b_gate_up.reshape(N_EXPERTS, 1, -1), b_down.reshape(N_EXPERTS, 1, -1)

    n_parts = N_PARTS if batch % N_PARTS == 0 else 1
    pb = batch // n_parts
    Tp = pb * seq
    tiles = Tp // tm
    n_assign = Tp * TOP_K
    n_blk = -(-n_assign // bm) + N_EXPERTS
    n_rows = n_blk * bm

    def front(part):
        q, k, v, f_pad = _qkvf(h2d, w1, b1, tm, Tp, part * tiles)
        auxq, auxk = _decay(f_pad, pb, seq)
        attn = _attention(q, auxq, k, auxk, v, pb, seq)
        h1, hx, idx_o, gate_o, rank_o, cnt_o = _mix(
            h2d, attn, w2, b2, row(gmlp_ln_g), row(gmlp_ln_b), wsb, bs_tile, wab, wbb, wob, row(b_out),
            row(ln1_g), row(ln1_b), wr, br, tm, alpha, part * tiles)
        dest8, plan = _route(cnt_o, idx_o, rank_o, bm, n_blk)
        dest_km = dest8[:TOP_K]
        xpad = _sc_scatter_rows(dest_km, hx.reshape(Tp, PACK_TILE, LANES), n_rows)
        return h1, gate_o, dest_km, plan, xpad

    def experts(part, state):
        h1, gate_o, dest_km, plan, xpad = state
        ypad = _moe(plan, xpad.reshape(n_rows * PACK_TILE, LANES), w_gate_up, bgu, w_down, bdn, n_blk, bm)
        yg = _sc_gather_rows(dest_km.reshape(-1), ypad.reshape(n_rows, PACK_TILE, LANES),
                             both_cores=part == n_parts - 1)
        return h1, gate_o, yg

    def back(part, state, out_prev):
        h1, gate_o, yg = state
        return _final(h1, yg.reshape(TOP_K, Tp * PACK_TILE, LANES), gate_o, p2d, wpleb, wpgb,
                      row(b_ple_gate), row(ln2_g), row(ln2_b), row(ln3_g), row(ln3_b), tm, alpha,
                      part * tiles, out_prev)

    fronts = [front(part) for part in range(n_parts)]
    mids = [experts(part, state) for part, state in enumerate(fronts)]
    out = None
    for part in range(n_parts):
        out = back(part, mids[part], out)
    return out


def kernel(x, p, w_in, b_in, gmlp_ln_g, gmlp_ln_b, w_spatial, b_spatial, w_branch_a, w_branch_b, w_out, b_out, ln1_g, ln1_b, w_router, b_router, w_gate_up, b_gate_up, w_down, b_down, ln2_g, ln2_b, w_ple, w_ple_gate, b_ple_gate, ln3_g, ln3_b):
    batch, seq, d = x.shape
    depth = w_in.shape[0]
    assert d == D_MODEL and seq % GMLP_CHUNK == 0
    alpha = (2.0 * depth) ** 0.25
    h = x.reshape(batch * seq, d)
    for i in range(depth):
        h = _layer(h, p[i].reshape(batch * seq, PLE_DIM), batch, seq, alpha,
                   w_in[i], b_in[i], gmlp_ln_g[i], gmlp_ln_b[i], w_spatial[i], b_spatial[i],
                   w_branch_a[i], w_branch_b[i], w_out[i], b_out[i], ln1_g[i], ln1_b[i],
                   w_router[i], b_router[i], w_gate_up[i], b_gate_up[i], w_down[i], b_down[i],
                   ln2_g[i], ln2_b[i], w_ple[i], w_ple_gate[i], b_ple_gate[i], ln3_g[i], ln3_b[i])
    return h.reshape(batch, seq, d)
```

```python
import functools
import math

import jax
import jax.numpy as jnp
import numpy as np
from jax import lax
from jax.experimental import pallas as pl
from jax.experimental.pallas import tpu as pltpu
from jax.experimental.pallas import tpu_sc as plsc

F32 = jnp.float32
BF16 = jnp.bfloat16
I32 = jnp.int32
U32 = jnp.uint32

D_MODEL = 1024
N_HEADS = 8
HEAD_DIM = 64
FOX_WIDTH = N_HEADS * HEAD_DIM
GMLP_WIDTH = 512
GMLP_CHUNK = 128
N_EXPERTS = 32
TOP_K = 4
D_FF = 1024
PLE_DIM = 256
SWIGLU_LIMIT = 7.0
SWIGLU_ALPHA = 1.702
LN_EPS = 1e-5
LANES = 128
ROW_TILE = 8
PACK_TILE = D_MODEL // 2 // LANES
VMEM_LIMIT = 56 * 1024 * 1024

MOE_BLOCK = 512
MOE_X_DEPTH = 3
ATTN_Q_TILE = 1024
ATTN_K_TILE = 1024
ATTN_PAIRS = 2
LOG2E = math.log2(math.e)
ROW_TILE_A = 512
SC_CORES = 2
SC_SUBCORES = 16
SC_WINDOW = 32
N_PARTS = 2
ROUTE_TILE = 1024
FINAL_CHUNK = 256


def _cparams(sem):
    return pltpu.CompilerParams(dimension_semantics=sem, vmem_limit_bytes=VMEM_LIMIT)


def _gelu(x):
    c = math.sqrt(2.0 / math.pi)
    return 0.5 * x * (1.0 + jnp.tanh(c * (x + 0.044715 * (x * x * x))))


def _layer_norm(x, g, b):
    mu = jnp.mean(x, axis=-1, keepdims=True)
    xc = x - mu
    var = jnp.mean(xc * xc, axis=-1, keepdims=True)
    return xc * lax.rsqrt(var + LN_EPS) * g + b


def _split3(x):
    hi = x.astype(BF16)
    r = x - hi.astype(F32)
    mid = r.astype(BF16)
    lo = (r - mid.astype(F32)).astype(BF16)
    return hi, mid, lo


def _dot(a, b):
    return jnp.dot(a, b, preferred_element_type=F32)


def _load_token_rows(ref, n, first=0, r=ROW_TILE):
    return jnp.concatenate([ref[pl.ds(first * r + j, n, stride=r), :] for j in range(r)], axis=1)


def _store_token_rows(ref, val):
    n, r = val.shape[0], val.shape[1] // LANES
    for j in range(r):
        ref[pl.ds(j, n, stride=r), :] = val[:, j * LANES:(j + 1) * LANES]


def _pack_bf16_pairs(x):
    half = x.shape[1] // 2
    hi = pltpu.bitcast(x[:, :half].astype(BF16).astype(F32), U32)
    lo = pltpu.bitcast(x[:, half:].astype(BF16).astype(F32), U32)
    return hi | (lo >> 16)


def _unpack_bf16_pairs(w):
    a = pltpu.bitcast(w & jnp.uint32(0xFFFF0000), F32)
    b = pltpu.bitcast(w << 16, F32)
    return jnp.concatenate([a, b], axis=1)


def _qkvf_kernel(x_ref, w_ref, b_ref, q_ref, k_ref, v_ref, f_ref):
    x = x_ref[...].astype(BF16)
    proj = _dot(x, w_ref[...]) + b_ref[...]
    q_ref[...] = (proj[:, :FOX_WIDTH] * (HEAD_DIM ** -0.5 * LOG2E)).astype(BF16)
    k_ref[...] = proj[:, FOX_WIDTH:2 * FOX_WIDTH].astype(BF16)
    v_ref[...] = proj[:, 2 * FOX_WIDTH:3 * FOX_WIDTH].astype(BF16)
    f_ref[...] = proj[:, 3 * FOX_WIDTH:]


def _qkvf(x2, w, b, tm, T, tile_off):
    n_out = w.shape[1]
    return pl.pallas_call(
        _qkvf_kernel,
        grid=(T // tm,),
        in_specs=[
            pl.BlockSpec((tm, D_MODEL), lambda i: (i + tile_off, 0)),
            pl.BlockSpec((D_MODEL, n_out), lambda i: (0, 0)),
            pl.BlockSpec((1, n_out), lambda i: (0, 0)),
        ],
        out_specs=[
            pl.BlockSpec((tm, FOX_WIDTH), lambda i: (i, 0)),
            pl.BlockSpec((tm, FOX_WIDTH), lambda i: (i, 0)),
            pl.BlockSpec((tm, FOX_WIDTH), lambda i: (i, 0)),
            pl.BlockSpec((tm, LANES), lambda i: (i, 0)),
        ],
        out_shape=[
            jax.ShapeDtypeStruct((T, FOX_WIDTH), BF16),
            jax.ShapeDtypeStruct((T, FOX_WIDTH), BF16),
            jax.ShapeDtypeStruct((T, FOX_WIDTH), BF16),
            jax.ShapeDtypeStruct((T, LANES), F32),
        ],
        compiler_params=_cparams(("arbitrary",)),
        name="qkvf",
    )(x2, w, b)


def _decay_placement():
    pq = np.zeros((3 * LANES, FOX_WIDTH), np.float32)
    pk = np.zeros((3 * LANES, FOX_WIDTH), np.float32)
    cq = np.zeros((1, FOX_WIDTH), np.float32)
    ck = np.zeros((1, FOX_WIDTH), np.float32)
    for h in range(N_HEADS):
        base = (h // 2) * LANES + (HEAD_DIM if h % 2 == 0 else 0)
        for piece in range(3):
            pq[piece * LANES + h, base + piece] = 1.0
            pk[piece * LANES + h, base + 3 + piece] = -1.0
            cq[0, base + 3 + piece] = 1.0
            ck[0, base + piece] = 1.0
    return pq, pk, cq, ck


def _decay_kernel(f_ref, pq_ref, pk_ref, cq_ref, ck_ref, auxq_ref, auxk_ref, *, seq, blk):
    r = lax.broadcasted_iota(I32, (blk, blk), 0)
    c = lax.broadcasted_iota(I32, (blk, blk), 1)
    tri = jnp.where(c <= r, 1.0, 0.0).astype(BF16)
    carry = jnp.zeros((1, LANES), F32)
    for i in range(seq // blk):
        f = f_ref[i * blk:(i + 1) * blk, :]
        ls = jnp.minimum(f, 0.0) - jnp.log1p(jnp.exp(-jnp.abs(f)))
        hi, mid, lo = _split3(ls)
        cs = _dot(tri, hi) + _dot(tri, mid) + _dot(tri, lo) + carry
        carry = cs[blk - 1:blk, :]
        pieces = jnp.concatenate(_split3(cs * LOG2E), axis=1)
        auxq_ref[i * blk:(i + 1) * blk, :] = (_dot(pieces, pq_ref[...]) + cq_ref[...]).astype(BF16)
        auxk_ref[i * blk:(i + 1) * blk, :] = (_dot(pieces, pk_ref[...]) + ck_ref[...]).astype(BF16)


def _decay(f_pad, batch, seq):
    blk = 256 if seq % 256 == 0 else LANES
    pq, pk, cq, ck = _decay_placement()
    const = lambda *shape: pl.BlockSpec(shape, lambda b: (0,) * len(shape))
    return pl.pallas_call(
        functools.partial(_decay_kernel, seq=seq, blk=blk),
        grid=(batch,),
        in_specs=[
            pl.BlockSpec((seq, LANES), lambda b: (b, 0)),
            const(3 * LANES, FOX_WIDTH), const(3 * LANES, FOX_WIDTH), const(1, FOX_WIDTH), const(1, FOX_WIDTH),
        ],
        out_specs=[
            pl.BlockSpec((seq, FOX_WIDTH), lambda b: (b, 0)),
            pl.BlockSpec((seq, FOX_WIDTH), lambda b: (b, 0)),
        ],
        out_shape=[
            jax.ShapeDtypeStruct((batch * seq, FOX_WIDTH), BF16),
            jax.ShapeDtypeStruct((batch * seq, FOX_WIDTH), BF16),
        ],
        compiler_params=_cparams(("arbitrary",)),
        name="decay",
    )(f_pad, jnp.asarray(pq, BF16), jnp.asarray(pk, BF16), jnp.asarray(cq), jnp.asarray(ck))


def _attn_kernel(q_ref, auxq_ref, k_ref, auxk_ref, v_ref, o_ref, *, tq, tk):
    qi = pl.program_id(2)
    n_pairs = q_ref.shape[1] // LANES
    low_q = lax.broadcasted_iota(I32, (tq, LANES), 1) < HEAD_DIM
    qs = []
    for pr in range(n_pairs):
        q = q_ref[:, pr * LANES:(pr + 1) * LANES]
        aq = auxq_ref[:, pr * LANES:(pr + 1) * LANES]
        qs += [jnp.where(low_q, q, aq), jnp.where(low_q, aq, q)]

    def step(start, n, carry, masked):
        low_k = lax.broadcasted_iota(I32, (n, LANES), 1) < HEAD_DIM
        ones = jnp.ones((n, LANES), BF16)
        out = []
        for pr in range(n_pairs):
            lanes = slice(pr * LANES, (pr + 1) * LANES)
            kb = k_ref[pl.ds(start, n), lanes]
            ak = auxk_ref[pl.ds(start, n), lanes]
            vb = v_ref[pl.ds(start, n), lanes]
            ks = (jnp.where(low_k, kb, ak), jnp.where(low_k, ak, kb))
            vs = (jnp.where(low_k, vb, ones), jnp.where(low_k, ones, vb))
            for j in range(2):
                m, acc = carry[2 * pr + j]
                s = lax.dot_general(qs[2 * pr + j], ks[j], (((1,), (1,)), ((), ())),
                                    preferred_element_type=F32)
                if masked:
                    row = lax.broadcasted_iota(I32, (tq, n), 0)
                    col = lax.broadcasted_iota(I32, (tq, n), 1)
                    s = jnp.where(col + (start - qi * tq) <= row, s, -jnp.inf)
                m_new = jnp.maximum(m, jnp.max(s, axis=1, keepdims=True))
                p = jnp.exp2(s - m_new)
                acc = jnp.exp2(m - m_new) * acc + _dot(p.astype(BF16), vs[j])
                out.append((m_new, acc))
        return tuple(out)

    init = tuple((jnp.full((tq, 1), -jnp.inf, F32), jnp.zeros((tq, LANES), F32))
                 for _ in range(2 * n_pairs))
    carry = lax.fori_loop(0, qi, lambda t, c: step(pl.multiple_of(t * tq, tq), tq, c, False), init)
    for d in range(tq // tk):
        carry = step(pl.multiple_of(qi * tq + d * tk, tk), tk, carry, True)
    for pr in range(n_pairs):
        acc0, acc1 = carry[2 * pr][1], carry[2 * pr + 1][1]
        out0 = acc0 / acc0[:, HEAD_DIM:HEAD_DIM + 1]
        out1 = acc1 / acc1[:, 0:1]
        o_ref[:, pr * LANES:(pr + 1) * LANES] = jnp.where(low_q, out0, out1).astype(BF16)


def _attention(q, auxq, k, auxk, v, batch, seq):
    tq = math.gcd(seq, ATTN_Q_TILE)
    tk = math.gcd(tq, ATTN_K_TILE)
    nq = seq // tq
    T = batch * seq
    width = ATTN_PAIRS * LANES
    q_spec = pl.BlockSpec((tq, width), lambda b, hp, qi: (b * nq + qi, hp))
    kv_spec = pl.BlockSpec((seq, width), lambda b, hp, qi: (b, hp))
    return pl.pallas_call(
        functools.partial(_attn_kernel, tq=tq, tk=tk),
        grid=(batch, N_HEADS // (2 * ATTN_PAIRS), nq),
        in_specs=[q_spec, q_spec, kv_spec, kv_spec, kv_spec],
        out_specs=q_spec,
        out_shape=jax.ShapeDtypeStruct((T, FOX_WIDTH), BF16),
        compiler_params=_cparams(("arbitrary", "arbitrary", "arbitrary")),
        name="attn",
    )(q, auxq, k, auxk, v)


def _mix_kernel(x_ref, attn_ref, w2_ref, b2_ref, lng_ref, lnb_ref, ws_ref, bs_ref,
                wa_ref, wb_ref, wo_ref, bo_ref, g1_ref, b1_ref, wr_ref, br_ref,
                h_ref, hx_ref, idx_ref, gate_ref, rank_ref, cnt_ref, carry_ref, *, tm, alpha):
    i = pl.program_id(0)

    @pl.when(i == 0)
    def _():
        carry_ref[...] = jnp.zeros_like(carry_ref)

    x = x_ref[...]
    proj = _dot(x.astype(BF16), w2_ref[...]) + b2_ref[...]
    u = _gelu(proj[:, :GMLP_WIDTH])
    gv = _gelu(proj[:, GMLP_WIDTH:2 * GMLP_WIDTH])
    vln = _layer_norm(gv, lng_ref[...], lnb_ref[...]).astype(BF16)

    cr = lax.broadcasted_iota(I32, (GMLP_CHUNK, GMLP_CHUNK), 0)
    cc = lax.broadcasted_iota(I32, (GMLP_CHUNK, GMLP_CHUNK), 1)
    tril = cc <= cr
    lo_half = cc < HEAD_DIM
    zero_w = jnp.zeros((GMLP_CHUNK, GMLP_CHUNK), BF16)
    n_slab = GMLP_WIDTH // LANES
    lhs = []
    for s in range(n_slab):
        w0 = jnp.where(tril, ws_ref[2 * s], zero_w)
        w1 = jnp.where(tril, ws_ref[2 * s + 1], zero_w)
        lhs.append(jnp.concatenate([w0, w1], axis=1))
    bs = bs_ref[...]
    rows = []
    for c in range(tm // GMLP_CHUNK):
        cols = []
        for s in range(n_slab):
            vs = vln[c * GMLP_CHUNK:(c + 1) * GMLP_CHUNK, s * LANES:(s + 1) * LANES]
            rhs = jnp.concatenate([jnp.where(lo_half, vs, zero_w), jnp.where(lo_half, zero_w, vs)], axis=0)
            cols.append(_dot(lhs[s], rhs))
        rows.append(jnp.concatenate(cols, axis=1) + bs)
    sp = jnp.concatenate(rows, axis=0) if len(rows) > 1 else rows[0]
    sgu = (u * sp).astype(BF16)

    ga = jax.nn.sigmoid(proj[:, 2 * GMLP_WIDTH:2 * GMLP_WIDTH + D_MODEL])
    gb = jax.nn.sigmoid(proj[:, 2 * GMLP_WIDTH + D_MODEL:])
    merged = ga * _dot(attn_ref[...], wa_ref[...]) + gb * _dot(sgu, wb_ref[...])
    mix = _dot(merged.astype(BF16), wo_ref[...]) + bo_ref[...]
    h = _layer_norm(alpha * x + mix, g1_ref[...], b1_ref[...])
    _store_token_rows(h_ref, h)
    _store_token_rows(hx_ref, _pack_bf16_pairs(h))

    a_hi = h.astype(BF16)
    a_lo = (h - a_hi.astype(F32)).astype(BF16)
    wr = wr_ref[...]
    w_hi = wr.astype(BF16)
    w_lo = (wr - w_hi.astype(F32)).astype(BF16)
    w_cat = jnp.concatenate([w_hi, w_lo], axis=1)
    r_hi = _dot(a_hi, w_cat)
    r_lo = _dot(a_lo, w_cat)
    logits = (r_hi[:, :LANES] + r_hi[:, LANES:]) + (r_lo[:, :LANES] + r_lo[:, LANES:]) + br_ref[...]

    lane_i = lax.broadcasted_iota(I32, (tm, LANES), 1)
    lane_f = lane_i.astype(F32)
    vals, idxs = [], []
    l = logits
    for _ in range(TOP_K):
        m = jnp.max(l, axis=1, keepdims=True)
        ix = jnp.min(jnp.where(l == m, lane_f, float(LANES)), axis=1, keepdims=True)
        vals.append(m)
        idxs.append(ix)
        l = jnp.where(lane_f == ix, -jnp.inf, l)
    es = [jnp.exp(v - vals[0]) for v in vals]
    den = es[0] + es[1] + es[2] + es[3]

    onehot = jnp.zeros((tm, LANES), F32)
    idx_out = jnp.zeros((tm, LANES), F32)
    gate_out = jnp.zeros((tm, LANES), F32)
    for k in range(TOP_K):
        onehot = onehot + jnp.where(lane_f == idxs[k], 1.0, 0.0)
        idx_out = jnp.where(lane_i == k, idxs[k], idx_out)
        gate_out = jnp.where(lane_i == k, es[k] / den, gate_out)

    tr = lax.broadcasted_iota(I32, (tm, tm), 0)
    tc = lax.broadcasted_iota(I32, (tm, tm), 1)
    strict = jnp.where(tc < tr, 1.0, 0.0).astype(BF16)
    carry = carry_ref[0:1, :]
    before = _dot(strict, onehot.astype(BF16)) + carry
    rank_out = jnp.zeros((tm, LANES), F32)
    for k in range(TOP_K):
        rk = jnp.sum(jnp.where(lane_f == idxs[k], before, 0.0), axis=1, keepdims=True)
        rank_out = jnp.where(lane_i == k, rk, rank_out)
    new_carry = carry + jnp.sum(onehot, axis=0, keepdims=True)
    carry_ref[...] = jnp.broadcast_to(new_carry, carry_ref.shape)
    cnt_ref[...] = jnp.broadcast_to(new_carry, cnt_ref.shape).astype(I32)
    idx_ref[...] = idx_out.astype(I32)
    gate_ref[...] = gate_out
    rank_ref[...] = rank_out.astype(I32)


def _mix(x2, attn, w2, b2, lng, lnb, ws, bs_tile, wa, wb, wo, bo, g1, b1, wr, br, tm, alpha, tile_off):
    T = attn.shape[0]
    n2 = w2.shape[1]
    const = lambda *shape: pl.BlockSpec(shape, lambda i: (0,) * len(shape))
    return pl.pallas_call(
        functools.partial(_mix_kernel, tm=tm, alpha=alpha),
        grid=(T // tm,),
        in_specs=[
            pl.BlockSpec((tm, D_MODEL), lambda i: (i + tile_off, 0)),
            pl.BlockSpec((tm, FOX_WIDTH), lambda i: (i, 0)),
            const(D_MODEL, n2), const(1, n2),
            const(1, GMLP_WIDTH), const(1, GMLP_WIDTH),
            const(GMLP_WIDTH // HEAD_DIM, GMLP_CHUNK, GMLP_CHUNK), const(GMLP_CHUNK, GMLP_WIDTH),
            const(FOX_WIDTH, D_MODEL), const(GMLP_WIDTH, D_MODEL),
            const(D_MODEL, D_MODEL), const(1, D_MODEL),
            const(1, D_MODEL), const(1, D_MODEL),
            const(D_MODEL, LANES), const(1, LANES),
        ],
        out_specs=[
            pl.BlockSpec((tm * ROW_TILE, LANES), lambda i: (i, 0)),
            pl.BlockSpec((tm * PACK_TILE, LANES), lambda i: (i, 0)),
            pl.BlockSpec((tm, LANES), lambda i: (i, 0)),
            pl.BlockSpec((tm, LANES), lambda i: (i, 0)),
            pl.BlockSpec((tm, LANES), lambda i: (i, 0)),
            pl.BlockSpec((ROW_TILE, LANES), lambda i: (0, 0)),
        ],
        out_shape=[
            jax.ShapeDtypeStruct((T * ROW_TILE, LANES), F32),
            jax.ShapeDtypeStruct((T * PACK_TILE, LANES), U32),
            jax.ShapeDtypeStruct((T, LANES), I32),
            jax.ShapeDtypeStruct((T, LANES), F32),
            jax.ShapeDtypeStruct((T, LANES), I32),
            jax.ShapeDtypeStruct((ROW_TILE, LANES), I32),
        ],
        scratch_shapes=[pltpu.VMEM((ROW_TILE, LANES), F32)],
        compiler_params=_cparams(("arbitrary",)),
        name="mix",
    )(x2, attn, w2, b2, lng, lnb, ws, bs_tile, wa, wb, wo, bo, g1, b1, wr, br)


PLAN_EXPERT, PLAN_VALID, PLAN_SLOT, PLAN_NEXT, PLAN_USED = range(5)


def _lane_cumsum(x):
    lane = lax.broadcasted_iota(I32, x.shape, 1)
    shift = 1
    while shift < LANES:
        x = x + jnp.where(lane >= shift, pltpu.roll(x, shift, 1), 0.0)
        shift *= 2
    return x


def _route_kernel(cnt_ref, idx_ref, rank_ref, dest_ref, plan_ref, start_ref, *, tm, bm, nbp):
    @pl.when(pl.program_id(0) == 0)
    def _():
        lane = lax.broadcasted_iota(I32, (ROW_TILE, LANES), 1)
        counts = jnp.where(lane < N_EXPERTS, cnt_ref[...].astype(F32), 0.0)
        padded = jnp.floor((counts + (bm - 1)) / bm) * bm
        pad_end = _lane_cumsum(padded)
        pad_start = pad_end - padded
        start_ref[...] = pad_start
        has_rows = jnp.where(counts > 0, 1.0, 0.0)
        order = _lane_cumsum(has_rows) - 1.0
        slot = order - 2.0 * jnp.floor(order * 0.5)
        total = jnp.sum(jnp.where(lane == N_EXPERTS - 1, pad_end, 0.0), axis=1, keepdims=True)[0:1, :]
        n_used = total / bm

        col = lambda r: jnp.transpose(r)[:, 0:1]
        sub = lax.broadcasted_iota(I32, (LANES, 1), 0)
        is_expert = sub < N_EXPERTS
        lane_w = lax.broadcasted_iota(I32, (LANES, LANES), 1)
        sub_w = lax.broadcasted_iota(I32, (LANES, LANES), 0)
        later = (lane_w > sub_w) & (lane_w < N_EXPERTS) & (has_rows[0:1, :] > 0)
        succ = jnp.min(jnp.where(later, lane_w, N_EXPERTS).astype(F32), axis=1, keepdims=True)
        succ = jnp.where(succ == N_EXPERTS, sub.astype(F32), succ)

        blk = lax.broadcasted_iota(I32, (1, nbp), 1).astype(F32)
        blk_src = jnp.minimum(blk, n_used - 1.0)
        blk_row = blk_src * bm
        below = is_expert & (col(pad_end) <= blk_row)
        blk_e = jnp.minimum(jnp.sum(jnp.where(below, 1.0, 0.0), axis=0, keepdims=True), N_EXPERTS - 1.0)
        hit = sub.astype(F32) == blk_e
        take = lambda c: jnp.sum(jnp.where(hit, c, 0.0), axis=0, keepdims=True)
        blk_valid = jnp.clip(take(col(pad_start + counts)) - blk_row, 0.0, bm)
        rows = [None] * ROW_TILE
        rows[PLAN_EXPERT], rows[PLAN_VALID] = blk_e, blk_valid
        rows[PLAN_SLOT], rows[PLAN_NEXT] = take(col(slot)), take(succ)
        rows[PLAN_USED] = jnp.broadcast_to(n_used, (1, nbp))
        zero = jnp.zeros((1, nbp), F32)
        plan_ref[...] = jnp.concatenate([zero if r is None else r for r in rows], axis=0).astype(I32)

    pad_start = start_ref[0:1, :]
    lane_i = lax.broadcasted_iota(I32, (tm, LANES), 1)
    lane_f = lane_i.astype(F32)
    idx = idx_ref[...].astype(F32)
    rank = rank_ref[...].astype(F32)
    dest = jnp.zeros((tm, LANES), F32)
    for k in range(TOP_K):
        base = jnp.sum(jnp.where(lane_f == idx[:, k:k + 1], pad_start, 0.0), axis=1, keepdims=True)
        dest = jnp.where(lane_i == k, base + rank[:, k:k + 1], dest)
    dest_ref[...] = jnp.transpose(dest)[:ROW_TILE, :].astype(I32)


def _route(cnt_o, idx_o, rank_o, bm, n_blk):
    Tp = idx_o.shape[0]
    tm = math.gcd(Tp, ROUTE_TILE)
    nbp = -(-n_blk // LANES) * LANES
    return pl.pallas_call(
        functools.partial(_route_kernel, tm=tm, bm=bm, nbp=nbp),
        grid=(Tp // tm,),
        in_specs=[
            pl.BlockSpec((ROW_TILE, LANES), lambda i: (0, 0)),
            pl.BlockSpec((tm, LANES), lambda i: (i, 0)),
            pl.BlockSpec((tm, LANES), lambda i: (i, 0)),
        ],
        out_specs=[
            pl.BlockSpec((ROW_TILE, tm), lambda i: (0, i)),
            pl.BlockSpec((ROW_TILE, nbp), lambda i: (0, 0)),
        ],
        out_shape=[
            jax.ShapeDtypeStruct((ROW_TILE, Tp), I32),
            jax.ShapeDtypeStruct((ROW_TILE, nbp), I32),
        ],
        scratch_shapes=[pltpu.VMEM((ROW_TILE, LANES), F32)],
        compiler_params=_cparams(("arbitrary",)),
        name="route",
    )(cnt_o, idx_o, rank_o)


def _sc_window_indices(idx):
    return jnp.pad(idx.reshape(-1, SC_WINDOW), ((0, 0), (0, LANES - SC_WINDOW)))


def _sc_scatter_rows(dest_km, src3, n_dst):
    n_slot, n_tok = dest_km.shape
    n_win = n_tok // SC_WINDOW
    row_shape = src3.shape[1:]
    mesh = plsc.VectorSubcoreMesh(core_axis_name="core", subcore_axis_name="subcore",
                                  num_cores=SC_CORES, num_subcores=SC_SUBCORES)

    @pl.kernel(out_type=jax.ShapeDtypeStruct((n_dst,) + row_shape, src3.dtype), mesh=mesh, name="sc_scatter")
    def scatter(src_hbm, i_hbm, o_hbm):
        def body(x_vmem, i_vmem):
            for k in range(n_slot):
                pltpu.sync_copy(x_vmem, o_hbm.at[i_vmem.at[k, pl.ds(0, SC_WINDOW)]])

        @pl.when(lax.axis_index("core") == 0)
        def _():
            pltpu.emit_pipeline(
                body,
                grid=(n_win,),
                in_specs=[pl.BlockSpec((SC_WINDOW,) + row_shape, lambda i: (i, 0, 0)),
                          pl.BlockSpec((ROW_TILE, LANES), lambda i: (i, 0))],
                out_specs=[],
                core_axis_name="subcore",
                dimension_semantics=(pltpu.PARALLEL,),
            )(src_hbm, i_hbm)

    idx = dest_km.reshape(n_slot, n_win, SC_WINDOW).transpose(1, 0, 2)
    idx = jnp.pad(idx, ((0, 0), (0, ROW_TILE - n_slot), (0, LANES - SC_WINDOW)))
    return scatter(src3, idx.reshape(n_win * ROW_TILE, LANES))


def _sc_gather_rows(sidx, src3, both_cores):
    n = sidx.shape[0]
    row_shape = src3.shape[1:]
    mesh = plsc.VectorSubcoreMesh(core_axis_name="core", subcore_axis_name="subcore",
                                  num_cores=SC_CORES, num_subcores=SC_SUBCORES)

    @pl.kernel(out_type=jax.ShapeDtypeStruct((n,) + row_shape, src3.dtype), mesh=mesh, name="sc_gather")
    def gather(src_hbm, i_hbm, o_hbm):
        def body(i_vmem, o_vmem):
            pltpu.sync_copy(src_hbm.at[i_vmem.at[0, pl.ds(0, SC_WINDOW)]], o_vmem)

        def run(axes):
            pltpu.emit_pipeline(
                body,
                grid=(n // SC_WINDOW,),
                in_specs=[pl.BlockSpec((1, LANES), lambda i: (i, 0))],
                out_specs=[pl.BlockSpec((SC_WINDOW,) + row_shape, lambda i: (i, 0, 0))],
                core_axis_name=axes,
                dimension_semantics=(pltpu.PARALLEL,),
            )(i_hbm, o_hbm)

        if both_cores:
            run(("core", "subcore"))
        else:
            pl.when(lax.axis_index("core") == 0)(lambda: run("subcore"))

    return gather(src3, _sc_window_indices(sidx))


def _moe_kernel(plan_ref, x_hbm, wgu_hbm, bgu_ref, wdn_hbm, bdn_ref, y_ref,
                x_buf, wgu_f32, wdn_f32, wgu_bf, wdn_bf, x_sems, sems):
    i = pl.program_id(0)
    n_used = plan_ref[PLAN_USED, i]
    expert = plan_ref[PLAN_EXPERT, i]
    active = i < n_used
    fresh = (i == 0) | (expert != plan_ref[PLAN_EXPERT, jnp.maximum(i - 1, 0)])
    slot = plan_ref[PLAN_SLOT, i]
    successor = plan_ref[PLAN_NEXT, i]
    rows = x_buf.shape[1]
    depth = x_buf.shape[0]

    def x_copy(blk):
        s = blk % depth
        return pltpu.make_async_copy(x_hbm.at[pl.ds(pl.multiple_of(blk * rows, rows), rows)],
                                     x_buf.at[s], x_sems.at[s])

    def weight_copies(e, s):
        return (pltpu.make_async_copy(wgu_hbm.at[e], wgu_f32.at[s], sems.at[s, 0]),
                pltpu.make_async_copy(wdn_hbm.at[e], wdn_f32.at[s], sems.at[s, 1]))

    @pl.when(i == 0)
    def _():
        for ahead in range(depth - 1):
            @pl.when(ahead < n_used)
            def _():
                x_copy(ahead).start()

    @pl.when(i + (depth - 1) < n_used)
    def _():
        x_copy(i + (depth - 1)).start()

    @pl.when(i == 0)
    def _():
        for c in weight_copies(expert, slot):
            c.start()

    @pl.when(active & fresh)
    def _():
        @pl.when(successor != expert)
        def _():
            for c in weight_copies(successor, 1 - slot):
                c.start()

        for c in weight_copies(expert, slot):
            c.wait()
        wgu_bf[...] = wgu_f32[slot].astype(BF16)
        wdn_bf[...] = wdn_f32[slot].astype(BF16)

    bm = rows // PACK_TILE
    n_valid = plan_ref[PLAN_VALID, i]

    def expert_ffn(n):
        words = _load_token_rows(x_buf.at[i % depth], n, r=PACK_TILE)
        valid = lax.broadcasted_iota(I32, (n, 1), 0) < n_valid
        x = _unpack_bf16_pairs(jnp.where(valid, words, jnp.uint32(0))).astype(BF16)
        gu = _dot(x, wgu_bf[...]) + bgu_ref[...]
        gate = jnp.minimum(gu[:, :D_FF], SWIGLU_LIMIT)
        up = jnp.clip(gu[:, D_FF:], -SWIGLU_LIMIT, SWIGLU_LIMIT)
        hid = (up + 1.0) * (gate * jax.nn.sigmoid(SWIGLU_ALPHA * gate))
        _store_token_rows(y_ref, _pack_bf16_pairs(_dot(hid.astype(BF16), wdn_bf[...]) + bdn_ref[...]))

    @pl.when(active)
    def _():
        x_copy(i).wait()

    half = bm // 2

    @pl.when(active & (n_valid > half))
    def _():
        expert_ffn(bm)

    @pl.when(active & (n_valid <= half))
    def _():
        expert_ffn(half)
        y_ref[pl.ds(half * PACK_TILE, half * PACK_TILE), :] = jnp.zeros((half * PACK_TILE, LANES), U32)

    @pl.when(jnp.logical_not(active))
    def _():
        y_ref[...] = jnp.zeros_like(y_ref)


def _moe(plan, xpad, wgu, bgu, wdn, bdn, n_blk, bm):
    bias = lambda i, plan: (plan[PLAN_EXPERT, i], 0, 0)
    grid_spec = pltpu.PrefetchScalarGridSpec(
        num_scalar_prefetch=1,
        grid=(n_blk,),
        in_specs=[
            pl.BlockSpec(memory_space=pl.ANY),
            pl.BlockSpec(memory_space=pl.ANY),
            pl.BlockSpec((None, 1, 2 * D_FF), bias),
            pl.BlockSpec(memory_space=pl.ANY),
            pl.BlockSpec((None, 1, D_MODEL), bias),
        ],
        out_specs=pl.BlockSpec((bm * PACK_TILE, LANES), lambda i, plan: (i, 0)),
        scratch_shapes=[
            pltpu.VMEM((MOE_X_DEPTH, bm * PACK_TILE, LANES), U32),
            pltpu.VMEM((2, D_MODEL, 2 * D_FF), F32), pltpu.VMEM((2, D_FF, D_MODEL), F32),
            pltpu.VMEM((D_MODEL, 2 * D_FF), BF16), pltpu.VMEM((D_FF, D_MODEL), BF16),
            pltpu.SemaphoreType.DMA((MOE_X_DEPTH,)), pltpu.SemaphoreType.DMA((2, 2)),
        ],
    )
    return pl.pallas_call(
        _moe_kernel,
        grid_spec=grid_spec,
        out_shape=jax.ShapeDtypeStruct((n_blk * bm * PACK_TILE, LANES), U32),
        compiler_params=_cparams(("arbitrary",)),
        name="moe",
    )(plan, xpad, wgu, bgu, wdn, bdn)


def _final_kernel(h_ref, yg_ref, gate_ref, p_ref, wple_ref, wpg_ref, bpg_ref,
                  g2_ref, b2_ref, g3_ref, b3_ref, *rest, tm, alpha):
    o_ref = rest[-1]
    rc = FINAL_CHUNK
    for c in range(tm // rc):
        rows = slice(c * rc, (c + 1) * rc)
        gates = gate_ref[rows, :]
        z = alpha * _load_token_rows(h_ref, rc, c * rc)
        for k in range(TOP_K):
            y = _unpack_bf16_pairs(_load_token_rows(yg_ref.at[k], rc, c * rc, r=PACK_TILE))
            z = z + gates[:, k:k + 1] * y
        h2 = _layer_norm(z, g2_ref[...], b2_ref[...])
        emb = _dot(p_ref[rows, :].astype(BF16), wple_ref[...])
        pg = jax.nn.sigmoid(_dot(h2.astype(BF16), wpg_ref[...]) + bpg_ref[...])
        o_ref[rows, :] = _layer_norm(alpha * h2 + emb * pg, g3_ref[...], b3_ref[...])


def _final(h1, yg, gates, p2, wple, wpg, bpg, g2, b2, g3, b3, tm, alpha, tile_off, out_prev):
    T = h1.shape[0] // ROW_TILE
    const = lambda *shape: pl.BlockSpec(shape, lambda i: (0,) * len(shape))
    in_specs = [
        pl.BlockSpec((tm * ROW_TILE, LANES), lambda i: (i, 0)),
        pl.BlockSpec((TOP_K, tm * PACK_TILE, LANES), lambda i: (0, i, 0)),
        pl.BlockSpec((tm, LANES), lambda i: (i, 0)),
        pl.BlockSpec((tm, PLE_DIM), lambda i: (i + tile_off, 0)),
        const(PLE_DIM, D_MODEL), const(D_MODEL, D_MODEL), const(1, D_MODEL),
        const(1, D_MODEL), const(1, D_MODEL), const(1, D_MODEL), const(1, D_MODEL),
    ]
    args = [h1, yg, gates, p2, wple, wpg, bpg, g2, b2, g3, b3]
    aliases = {}
    if out_prev is not None:
        in_specs.append(pl.BlockSpec(memory_space=pl.ANY))
        aliases = {len(args): 0}
        args.append(out_prev)
    return pl.pallas_call(
        functools.partial(_final_kernel, tm=tm, alpha=alpha),
        grid=(T // tm,),
        in_specs=in_specs,
        out_specs=pl.BlockSpec((tm, D_MODEL), lambda i: (i + tile_off, 0)),
        out_shape=jax.ShapeDtypeStruct((p2.shape[0], D_MODEL), F32),
        input_output_aliases=aliases,
        compiler_params=_cparams(("arbitrary",)),
        name="final",
    )(*args)


def _layer(h2d, p2d, batch, seq, alpha, w_in, b_in, gmlp_ln_g, gmlp_ln_b, w_spatial, b_spatial,
           w_branch_a, w_branch_b, w_out, b_out, ln1_g, ln1_b, w_router, b_router,
           w_gate_up, b_gate_up, w_down, b_down, ln2_g, ln2_b, w_ple, w_ple_gate, b_ple_gate,
           ln3_g, ln3_b):
    T = batch * seq
    tm = math.gcd(T, ROW_TILE_A)
    bm = MOE_BLOCK
    off_f = 3 * FOX_WIDTH
    off_u = off_f + N_HEADS
    row = lambda v: v.reshape(1, -1).astype(F32)

    w1 = jnp.concatenate([w_in[:, :off_u], jnp.zeros((D_MODEL, LANES - N_HEADS), F32)], axis=1).astype(BF16)
    b1 = jnp.concatenate([b_in[:off_u], jnp.zeros((LANES - N_HEADS,), F32)]).reshape(1, -1)
    w2 = w_in[:, off_u:].astype(BF16)
    b2 = row(b_in[off_u:])
    bs_tile = jnp.repeat(b_spatial.T, HEAD_DIM, axis=1)
    wr = jnp.concatenate([w_router, jnp.zeros((D_MODEL, LANES - N_EXPERTS), F32)], axis=1)
    br = jnp.concatenate([b_router, jnp.full((LANES - N_EXPERTS,), -1e30, F32)]).reshape(1, -1)

    wsb, wab, wbb, wob = (w.astype(BF16) for w in (w_spatial, w_branch_a, w_branch_b, w_out))
    wpleb, wpgb = w_ple.astype(BF16), w_ple_gate.astype(BF16)
    bgu, bdn = b_gate_up.reshape(N_EXPERTS, 1, -1), b_down.reshape(N_EXPERTS, 1, -1)

    n_parts = N_PARTS if batch % N_PARTS == 0 else 1
    pb = batch // n_parts
    Tp = pb * seq
    tiles = Tp // tm
    n_assign = Tp * TOP_K
    n_blk = -(-n_assign // bm) + N_EXPERTS
    n_rows = n_blk * bm

    def front(part):
        q, k, v, f_pad = _qkvf(h2d, w1, b1, tm, Tp, part * tiles)
        auxq, auxk = _decay(f_pad, pb, seq)
        attn = _attention(q, auxq, k, auxk, v, pb, seq)
        h1, hx, idx_o, gate_o, rank_o, cnt_o = _mix(
            h2d, attn, w2, b2, row(gmlp_ln_g), row(gmlp_ln_b), wsb, bs_tile, wab, wbb, wob, row(b_out),
            row(ln1_g), row(ln1_b), wr, br, tm, alpha, part * tiles)
        dest8, plan = _route(cnt_o, idx_o, rank_o, bm, n_blk)
        dest_km = dest8[:TOP_K]
        xpad = _sc_scatter_rows(dest_km, hx.reshape(Tp, PACK_TILE, LANES), n_rows)
        return h1, gate_o, dest_km, plan, xpad

    def experts(part, state):
        h1, gate_o, dest_km, plan, xpad = state
        ypad = _moe(plan, xpad.reshape(n_rows * PACK_TILE, LANES), w_gate_up, bgu, w_down, bdn, n_blk, bm)
        yg = _sc_gather_rows(dest_km.reshape(-1), ypad.reshape(n_rows, PACK_TILE, LANES),
                             both_cores=part == n_parts - 1)
        return h1, gate_o, yg

    def back(part, state, out_prev):
        h1, gate_o, yg = state
        return _final(h1, yg.reshape(TOP_K, Tp * PACK_TILE, LANES), gate_o, p2d, wpleb, wpgb,
                      row(b_ple_gate), row(ln2_g), row(ln2_b), row(ln3_g), row(ln3_b), tm, alpha,
                      part * tiles, out_prev)

    fronts = [front(part) for part in range(n_parts)]
    mids = [experts(part, state) for part, state in enumerate(fronts)]
    out = None
    for part in range(n_parts):
        out = back(part, mids[part], out)
    return out


def kernel(x, p, w_in, b_in, gmlp_ln_g, gmlp_ln_b, w_spatial, b_spatial, w_branch_a, w_branch_b, w_out, b_out, ln1_g, ln1_b, w_router, b_router, w_gate_up, b_gate_up, w_down, b_down, ln2_g, ln2_b, w_ple, w_ple_gate, b_ple_gate, ln3_g, ln3_b):
    batch, seq, d = x.shape
    depth = w_in.shape[0]
    assert d == D_MODEL and seq % GMLP_CHUNK == 0
    alpha = (2.0 * depth) ** 0.25
    h = x.reshape(batch * seq, d)
    for i in range(depth):
        h = _layer(h, p[i].reshape(batch * seq, PLE_DIM), batch, seq, alpha,
                   w_in[i], b_in[i], gmlp_ln_g[i], gmlp_ln_b[i], w_spatial[i], b_spatial[i],
                   w_branch_a[i], w_branch_b[i], w_out[i], b_out[i], ln1_g[i], ln1_b[i],
                   w_router[i], b_router[i], w_gate_up[i], b_gate_up[i], w_down[i], b_down[i],
                   ln2_g[i], ln2_b[i], w_ple[i], w_ple_gate[i], b_ple_gate[i], ln3_g[i], ln3_b[i])
    return h.reshape(batch, seq, d)
```

```python
import functools
import math

import jax
import jax.numpy as jnp
import numpy as np
from jax import lax
from jax.experimental import pallas as pl
from jax.experimental.pallas import tpu as pltpu
from jax.experimental.pallas import tpu_sc as plsc

F32 = jnp.float32
BF16 = jnp.bfloat16
I32 = jnp.int32
U32 = jnp.uint32

D_MODEL = 1024
N_HEADS = 8
HEAD_DIM = 64
FOX_WIDTH = N_HEADS * HEAD_DIM
GMLP_WIDTH = 512
GMLP_CHUNK = 128
N_EXPERTS = 32
TOP_K = 4
D_FF = 1024
PLE_DIM = 256
SWIGLU_LIMIT = 7.0
SWIGLU_ALPHA = 1.702
LN_EPS = 1e-5
LANES = 128
ROW_TILE = 8
PACK_TILE = D_MODEL // 2 // LANES
VMEM_LIMIT = 56 * 1024 * 1024

MOE_BLOCK = 512
MOE_X_DEPTH = 3
ATTN_Q_TILE = 1024
ATTN_K_TILE = 1024
ATTN_PAIRS = 2
LOG2E = math.log2(math.e)
ROW_TILE_A = 512
SC_CORES = 2
SC_SUBCORES = 16
SC_WINDOW = 32
N_PARTS = 2
ROUTE_TILE = 1024
FINAL_CHUNK = 256


def _cparams(sem):
    return pltpu.CompilerParams(dimension_semantics=sem, vmem_limit_bytes=VMEM_LIMIT)


def _gelu(x):
    c = math.sqrt(2.0 / math.pi)
    return 0.5 * x * (1.0 + jnp.tanh(c * (x + 0.044715 * (x * x * x))))


def _layer_norm(x, g, b):
    mu = jnp.mean(x, axis=-1, keepdims=True)
    xc = x - mu
    var = jnp.mean(xc * xc, axis=-1, keepdims=True)
    return xc * lax.rsqrt(var + LN_EPS) * g + b


def _split3(x):
    hi = x.astype(BF16)
    r = x - hi.astype(F32)
    mid = r.astype(BF16)
    lo = (r - mid.astype(F32)).astype(BF16)
    return hi, mid, lo


def _dot(a, b):
    return jnp.dot(a, b, preferred_element_type=F32)


def _load_token_rows(ref, n, first=0, r=ROW_TILE):
    return jnp.concatenate([ref[pl.ds(first * r + j, n, stride=r), :] for j in range(r)], axis=1)


def _store_token_rows(ref, val):
    n, r = val.shape[0], val.shape[1] // LANES
    for j in range(r):
        ref[pl.ds(j, n, stride=r), :] = val[:, j * LANES:(j + 1) * LANES]


def _pack_bf16_pairs(x):
    half = x.shape[1] // 2
    hi = pltpu.bitcast(x[:, :half].astype(BF16).astype(F32), U32)
    lo = pltpu.bitcast(x[:, half:].astype(BF16).astype(F32), U32)
    return hi | (lo >> 16)


def _unpack_bf16_pairs(w):
    a = pltpu.bitcast(w & jnp.uint32(0xFFFF0000), F32)
    b = pltpu.bitcast(w << 16, F32)
    return jnp.concatenate([a, b], axis=1)


def _qkvf_kernel(x_ref, w_ref, b_ref, q_ref, k_ref, v_ref, f_ref):
    x = x_ref[...].astype(BF16)
    proj = _dot(x, w_ref[...]) + b_ref[...]
    q_ref[...] = (proj[:, :FOX_WIDTH] * (HEAD_DIM ** -0.5 * LOG2E)).astype(BF16)
    k_ref[...] = proj[:, FOX_WIDTH:2 * FOX_WIDTH].astype(BF16)
    v_ref[...] = proj[:, 2 * FOX_WIDTH:3 * FOX_WIDTH].astype(BF16)
    f_ref[...] = proj[:, 3 * FOX_WIDTH:]


def _qkvf(x2, w, b, tm, T, tile_off):
    n_out = w.shape[1]
    return pl.pallas_call(
        _qkvf_kernel,
        grid=(T // tm,),
        in_specs=[
            pl.BlockSpec((tm, D_MODEL), lambda i: (i + tile_off, 0)),
            pl.BlockSpec((D_MODEL, n_out), lambda i: (0, 0)),
            pl.BlockSpec((1, n_out), lambda i: (0, 0)),
        ],
        out_specs=[
            pl.BlockSpec((tm, FOX_WIDTH), lambda i: (i, 0)),
            pl.BlockSpec((tm, FOX_WIDTH), lambda i: (i, 0)),
            pl.BlockSpec((tm, FOX_WIDTH), lambda i: (i, 0)),
            pl.BlockSpec((tm, LANES), lambda i: (i, 0)),
        ],
        out_shape=[
            jax.ShapeDtypeStruct((T, FOX_WIDTH), BF16),
            jax.ShapeDtypeStruct((T, FOX_WIDTH), BF16),
            jax.ShapeDtypeStruct((T, FOX_WIDTH), BF16),
            jax.ShapeDtypeStruct((T, LANES), F32),
        ],
        compiler_params=_cparams(("arbitrary",)),
        name="qkvf",
    )(x2, w, b)


def _decay_placement():
    pq = np.zeros((3 * LANES, FOX_WIDTH), np.float32)
    pk = np.zeros((3 * LANES, FOX_WIDTH), np.float32)
    cq = np.zeros((1, FOX_WIDTH), np.float32)
    ck = np.zeros((1, FOX_WIDTH), np.float32)
    for h in range(N_HEADS):
        base = (h // 2) * LANES + (HEAD_DIM if h % 2 == 0 else 0)
        for piece in range(3):
            pq[piece * LANES + h, base + piece] = 1.0
            pk[piece * LANES + h, base + 3 + piece] = -1.0
            cq[0, base + 3 + piece] = 1.0
            ck[0, base + piece] = 1.0
    return pq, pk, cq, ck


def _decay_kernel(f_ref, pq_ref, pk_ref, cq_ref, ck_ref, auxq_ref, auxk_ref, *, seq, blk):
    r = lax.broadcasted_iota(I32, (blk, blk), 0)
    c = lax.broadcasted_iota(I32, (blk, blk), 1)
    tri = jnp.where(c <= r, 1.0, 0.0).astype(BF16)
    carry = jnp.zeros((1, LANES), F32)
    for i in range(seq // blk):
        f = f_ref[i * blk:(i + 1) * blk, :]
        ls = jnp.minimum(f, 0.0) - jnp.log1p(jnp.exp(-jnp.abs(f)))
        hi, mid, lo = _split3(ls)
        cs = _dot(tri, hi) + _dot(tri, mid) + _dot(tri, lo) + carry
        carry = cs[blk - 1:blk, :]
        pieces = jnp.concatenate(_split3(cs * LOG2E), axis=1)
        auxq_ref[i * blk:(i + 1) * blk, :] = (_dot(pieces, pq_ref[...]) + cq_ref[...]).astype(BF16)
        auxk_ref[i * blk:(i + 1) * blk, :] = (_dot(pieces, pk_ref[...]) + ck_ref[...]).astype(BF16)


def _decay(f_pad, batch, seq):
    blk = 256 if seq % 256 == 0 else LANES
    pq, pk, cq, ck = _decay_placement()
    const = lambda *shape: pl.BlockSpec(shape, lambda b: (0,) * len(shape))
    return pl.pallas_call(
        functools.partial(_decay_kernel, seq=seq, blk=blk),
        grid=(batch,),
        in_specs=[
            pl.BlockSpec((seq, LANES), lambda b: (b, 0)),
            const(3 * LANES, FOX_WIDTH), const(3 * LANES, FOX_WIDTH), const(1, FOX_WIDTH), const(1, FOX_WIDTH),
        ],
        out_specs=[
            pl.BlockSpec((seq, FOX_WIDTH), lambda b: (b, 0)),
            pl.BlockSpec((seq, FOX_WIDTH), lambda b: (b, 0)),
        ],
        out_shape=[
            jax.ShapeDtypeStruct((batch * seq, FOX_WIDTH), BF16),
            jax.ShapeDtypeStruct((batch * seq, FOX_WIDTH), BF16),
        ],
        compiler_params=_cparams(("arbitrary",)),
        name="decay",
    )(f_pad, jnp.asarray(pq, BF16), jnp.asarray(pk, BF16), jnp.asarray(cq), jnp.asarray(ck))


def _attn_kernel(q_ref, auxq_ref, k_ref, auxk_ref, v_ref, o_ref, *, tq, tk):
    qi = pl.program_id(2)
    n_pairs = q_ref.shape[1] // LANES
    low_q = lax.broadcasted_iota(I32, (tq, LANES), 1) < HEAD_DIM
    qs = []
    for pr in range(n_pairs):
        q = q_ref[:, pr * LANES:(pr + 1) * LANES]
        aq = auxq_ref[:, pr * LANES:(pr + 1) * LANES]
        qs += [jnp.where(low_q, q, aq), jnp.where(low_q, aq, q)]

    def step(start, n, carry, masked):
        low_k = lax.broadcasted_iota(I32, (n, LANES), 1) < HEAD_DIM
        ones = jnp.ones((n, LANES), BF16)
        out = []
        for pr in range(n_pairs):
            lanes = slice(pr * LANES, (pr + 1) * LANES)
            kb = k_ref[pl.ds(start, n), lanes]
            ak = auxk_ref[pl.ds(start, n), lanes]
            vb = v_ref[pl.ds(start, n), lanes]
            ks = (jnp.where(low_k, kb, ak), jnp.where(low_k, ak, kb))
            vs = (jnp.where(low_k, vb, ones), jnp.where(low_k, ones, vb))
            for j in range(2):
                m, acc = carry[2 * pr + j]
                s = lax.dot_general(qs[2 * pr + j], ks[j], (((1,), (1,)), ((), ())),
                                    preferred_element_type=F32)
                if masked:
                    row = lax.broadcasted_iota(I32, (tq, n), 0)
                    col = lax.broadcasted_iota(I32, (tq, n), 1)
                    s = jnp.where(col + (start - qi * tq) <= row, s, -jnp.inf)
                m_new = jnp.maximum(m, jnp.max(s, axis=1, keepdims=True))
                p = jnp.exp2(s - m_new)
                acc = jnp.exp2(m - m_new) * acc + _dot(p.astype(BF16), vs[j])
                out.append((m_new, acc))
        return tuple(out)

    init = tuple((jnp.full((tq, 1), -jnp.inf, F32), jnp.zeros((tq, LANES), F32))
                 for _ in range(2 * n_pairs))
    carry = lax.fori_loop(0, qi, lambda t, c: step(pl.multiple_of(t * tq, tq), tq, c, False), init)
    for d in range(tq // tk):
        carry = step(pl.multiple_of(qi * tq + d * tk, tk), tk, carry, True)
    for pr in range(n_pairs):
        acc0, acc1 = carry[2 * pr][1], carry[2 * pr + 1][1]
        out0 = acc0 / acc0[:, HEAD_DIM:HEAD_DIM + 1]
        out1 = acc1 / acc1[:, 0:1]
        o_ref[:, pr * LANES:(pr + 1) * LANES] = jnp.where(low_q, out0, out1).astype(BF16)


def _attention(q, auxq, k, auxk, v, batch, seq):
    tq = math.gcd(seq, ATTN_Q_TILE)
    tk = math.gcd(tq, ATTN_K_TILE)
    nq = seq // tq
    T = batch * seq
    width = ATTN_PAIRS * LANES
    q_spec = pl.BlockSpec((tq, width), lambda b, hp, qi: (b * nq + qi, hp))
    kv_spec = pl.BlockSpec((seq, width), lambda b, hp, qi: (b, hp))
    return pl.pallas_call(
        functools.partial(_attn_kernel, tq=tq, tk=tk),
        grid=(batch, N_HEADS // (2 * ATTN_PAIRS), nq),
        in_specs=[q_spec, q_spec, kv_spec, kv_spec, kv_spec],
        out_specs=q_spec,
        out_shape=jax.ShapeDtypeStruct((T, FOX_WIDTH), BF16),
        compiler_params=_cparams(("arbitrary", "arbitrary", "arbitrary")),
        name="attn",
    )(q, auxq, k, auxk, v)


def _mix_kernel(x_ref, attn_ref, w2_ref, b2_ref, lng_ref, lnb_ref, ws_ref, bs_ref,
                wa_ref, wb_ref, wo_ref, bo_ref, g1_ref, b1_ref, wr_ref, br_ref,
                h_ref, hx_ref, idx_ref, gate_ref, rank_ref, cnt_ref, carry_ref, *, tm, alpha):
    i = pl.program_id(0)

    @pl.when(i == 0)
    def _():
        carry_ref[...] = jnp.zeros_like(carry_ref)

    x = x_ref[...]
    proj = _dot(x.astype(BF16), w2_ref[...]) + b2_ref[...]
    u = _gelu(proj[:, :GMLP_WIDTH])
    gv = _gelu(proj[:, GMLP_WIDTH:2 * GMLP_WIDTH])
    vln = _layer_norm(gv, lng_ref[...], lnb_ref[...]).astype(BF16)

    cr = lax.broadcasted_iota(I32, (GMLP_CHUNK, GMLP_CHUNK), 0)
    cc = lax.broadcasted_iota(I32, (GMLP_CHUNK, GMLP_CHUNK), 1)
    tril = cc <= cr
    lo_half = cc < HEAD_DIM
    zero_w = jnp.zeros((GMLP_CHUNK, GMLP_CHUNK), BF16)
    n_slab = GMLP_WIDTH // LANES
    lhs = []
    for s in range(n_slab):
        w0 = jnp.where(tril, ws_ref[2 * s], zero_w)
        w1 = jnp.where(tril, ws_ref[2 * s + 1], zero_w)
        lhs.append(jnp.concatenate([w0, w1], axis=1))
    bs = bs_ref[...]
    rows = []
    for c in range(tm // GMLP_CHUNK):
        cols = []
        for s in range(n_slab):
            vs = vln[c * GMLP_CHUNK:(c + 1) * GMLP_CHUNK, s * LANES:(s + 1) * LANES]
            rhs = jnp.concatenate([jnp.where(lo_half, vs, zero_w), jnp.where(lo_half, zero_w, vs)], axis=0)
            cols.append(_dot(lhs[s], rhs))
        rows.append(jnp.concatenate(cols, axis=1) + bs)
    sp = jnp.concatenate(rows, axis=0) if len(rows) > 1 else rows[0]
    sgu = (u * sp).astype(BF16)

    ga = jax.nn.sigmoid(proj[:, 2 * GMLP_WIDTH:2 * GMLP_WIDTH + D_MODEL])
    gb = jax.nn.sigmoid(proj[:, 2 * GMLP_WIDTH + D_MODEL:])
    merged = ga * _dot(attn_ref[...], wa_ref[...]) + gb * _dot(sgu, wb_ref[...])
    mix = _dot(merged.astype(BF16), wo_ref[...]) + bo_ref[...]
    h = _layer_norm(alpha * x + mix, g1_ref[...], b1_ref[...])
    _store_token_rows(h_ref, h)
    _store_token_rows(hx_ref, _pack_bf16_pairs(h))

    a_hi = h.astype(BF16)
    a_lo = (h - a_hi.astype(F32)).astype(BF16)
    wr = wr_ref[...]
    w_hi = wr.astype(BF16)
    w_lo = (wr - w_hi.astype(F32)).astype(BF16)
    w_cat = jnp.concatenate([w_hi, w_lo], axis=1)
    r_hi = _dot(a_hi, w_cat)
    r_lo = _dot(a_lo, w_cat)
    logits = (r_hi[:, :LANES] + r_hi[:, LANES:]) + (r_lo[:, :LANES] + r_lo[:, LANES:]) + br_ref[...]

    lane_i = lax.broadcasted_iota(I32, (tm, LANES), 1)
    lane_f = lane_i.astype(F32)
    vals, idxs = [], []
    l = logits
    for _ in range(TOP_K):
        m = jnp.max(l, axis=1, keepdims=True)
        ix = jnp.min(jnp.where(l == m, lane_f, float(LANES)), axis=1, keepdims=True)
        vals.append(m)
        idxs.append(ix)
        l = jnp.where(lane_f == ix, -jnp.inf, l)
    es = [jnp.exp(v - vals[0]) for v in vals]
    den = es[0] + es[1] + es[2] + es[3]

    onehot = jnp.zeros((tm, LANES), F32)
    idx_out = jnp.zeros((tm, LANES), F32)
    gate_out = jnp.zeros((tm, LANES), F32)
    for k in range(TOP_K):
        onehot = onehot + jnp.where(lane_f == idxs[k], 1.0, 0.0)
        idx_out = jnp.where(lane_i == k, idxs[k], idx_out)
        gate_out = jnp.where(lane_i == k, es[k] / den, gate_out)

    tr = lax.broadcasted_iota(I32, (tm, tm), 0)
    tc = lax.broadcasted_iota(I32, (tm, tm), 1)
    strict = jnp.where(tc < tr, 1.0, 0.0).astype(BF16)
    carry = carry_ref[0:1, :]
    before = _dot(strict, onehot.astype(BF16)) + carry
    rank_out = jnp.zeros((tm, LANES), F32)
    for k in range(TOP_K):
        rk = jnp.sum(jnp.where(lane_f == idxs[k], before, 0.0), axis=1, keepdims=True)
        rank_out = jnp.where(lane_i == k, rk, rank_out)
    new_carry = carry + jnp.sum(onehot, axis=0, keepdims=True)
    carry_ref[...] = jnp.broadcast_to(new_carry, carry_ref.shape)
    cnt_ref[...] = jnp.broadcast_to(new_carry, cnt_ref.shape).astype(I32)
    idx_ref[...] = idx_out.astype(I32)
    gate_ref[...] = gate_out
    rank_ref[...] = rank_out.astype(I32)


def _mix(x2, attn, w2, b2, lng, lnb, ws, bs_tile, wa, wb, wo, bo, g1, b1, wr, br, tm, alpha, tile_off):
    T = attn.shape[0]
    n2 = w2.shape[1]
    const = lambda *shape: pl.BlockSpec(shape, lambda i: (0,) * len(shape))
    return pl.pallas_call(
        functools.partial(_mix_kernel, tm=tm, alpha=alpha),
        grid=(T // tm,),
        in_specs=[
            pl.BlockSpec((tm, D_MODEL), lambda i: (i + tile_off, 0)),
            pl.BlockSpec((tm, FOX_WIDTH), lambda i: (i, 0)),
            const(D_MODEL, n2), const(1, n2),
            const(1, GMLP_WIDTH), const(1, GMLP_WIDTH),
            const(GMLP_WIDTH // HEAD_DIM, GMLP_CHUNK, GMLP_CHUNK), const(GMLP_CHUNK, GMLP_WIDTH),
            const(FOX_WIDTH, D_MODEL), const(GMLP_WIDTH, D_MODEL),
            const(D_MODEL, D_MODEL), const(1, D_MODEL),
            const(1, D_MODEL), const(1, D_MODEL),
            const(D_MODEL, LANES), const(1, LANES),
        ],
        out_specs=[
            pl.BlockSpec((tm * ROW_TILE, LANES), lambda i: (i, 0)),
            pl.BlockSpec((tm * PACK_TILE, LANES), lambda i: (i, 0)),
            pl.BlockSpec((tm, LANES), lambda i: (i, 0)),
            pl.BlockSpec((tm, LANES), lambda i: (i, 0)),
            pl.BlockSpec((tm, LANES), lambda i: (i, 0)),
            pl.BlockSpec((ROW_TILE, LANES), lambda i: (0, 0)),
        ],
        out_shape=[
            jax.ShapeDtypeStruct((T * ROW_TILE, LANES), F32),
            jax.ShapeDtypeStruct((T * PACK_TILE, LANES), U32),
            jax.ShapeDtypeStruct((T, LANES), I32),
            jax.ShapeDtypeStruct((T, LANES), F32),
            jax.ShapeDtypeStruct((T, LANES), I32),
            jax.ShapeDtypeStruct((ROW_TILE, LANES), I32),
        ],
        scratch_shapes=[pltpu.VMEM((ROW_TILE, LANES), F32)],
        compiler_params=_cparams(("arbitrary",)),
        name="mix",
    )(x2, attn, w2, b2, lng, lnb, ws, bs_tile, wa, wb, wo, bo, g1, b1, wr, br)


PLAN_EXPERT, PLAN_VALID, PLAN_SLOT, PLAN_NEXT, PLAN_USED = range(5)


def _lane_cumsum(x):
    lane = lax.broadcasted_iota(I32, x.shape, 1)
    shift = 1
    while shift < LANES:
        x = x + jnp.where(lane >= shift, pltpu.roll(x, shift, 1), 0.0)
        shift *= 2
    return x


def _route_kernel(cnt_ref, idx_ref, rank_ref, dest_ref, plan_ref, start_ref, *, tm, bm, nbp):
    @pl.when(pl.program_id(0) == 0)
    def _():
        lane = lax.broadcasted_iota(I32, (ROW_TILE, LANES), 1)
        counts = jnp.where(lane < N_EXPERTS, cnt_ref[...].astype(F32), 0.0)
        padded = jnp.floor((counts + (bm - 1)) / bm) * bm
        pad_end = _lane_cumsum(padded)
        pad_start = pad_end - padded
        start_ref[...] = pad_start
        has_rows = jnp.where(counts > 0, 1.0, 0.0)
        order = _lane_cumsum(has_rows) - 1.0
        slot = order - 2.0 * jnp.floor(order * 0.5)
        total = jnp.sum(jnp.where(lane == N_EXPERTS - 1, pad_end, 0.0), axis=1, keepdims=True)[0:1, :]
        n_used = total / bm

        col = lambda r: jnp.transpose(r)[:, 0:1]
        sub = lax.broadcasted_iota(I32, (LANES, 1), 0)
        is_expert = sub < N_EXPERTS
        lane_w = lax.broadcasted_iota(I32, (LANES, LANES), 1)
        sub_w = lax.broadcasted_iota(I32, (LANES, LANES), 0)
        later = (lane_w > sub_w) & (lane_w < N_EXPERTS) & (has_rows[0:1, :] > 0)
        succ = jnp.min(jnp.where(later, lane_w, N_EXPERTS).astype(F32), axis=1, keepdims=True)
        succ = jnp.where(succ == N_EXPERTS, sub.astype(F32), succ)

        blk = lax.broadcasted_iota(I32, (1, nbp), 1).astype(F32)
        blk_src = jnp.minimum(blk, n_used - 1.0)
        blk_row = blk_src * bm
        below = is_expert & (col(pad_end) <= blk_row)
        blk_e = jnp.minimum(jnp.sum(jnp.where(below, 1.0, 0.0), axis=0, keepdims=True), N_EXPERTS - 1.0)
        hit = sub.astype(F32) == blk_e
        take = lambda c: jnp.sum(jnp.where(hit, c, 0.0), axis=0, keepdims=True)
        blk_valid = jnp.clip(take(col(pad_start + counts)) - blk_row, 0.0, bm)
        rows = [None] * ROW_TILE
        rows[PLAN_EXPERT], rows[PLAN_VALID] = blk_e, blk_valid
        rows[PLAN_SLOT], rows[PLAN_NEXT] = take(col(slot)), take(succ)
        rows[PLAN_USED] = jnp.broadcast_to(n_used, (1, nbp))
        zero = jnp.zeros((1, nbp), F32)
        plan_ref[...] = jnp.concatenate([zero if r is None else r for r in rows], axis=0).astype(I32)

    pad_start = start_ref[0:1, :]
    lane_i = lax.broadcasted_iota(I32, (tm, LANES), 1)
    lane_f = lane_i.astype(F32)
    idx = idx_ref[...].astype(F32)
    rank = rank_ref[...].astype(F32)
    dest = jnp.zeros((tm, LANES), F32)
    for k in range(TOP_K):
        base = jnp.sum(jnp.where(lane_f == idx[:, k:k + 1], pad_start, 0.0), axis=1, keepdims=True)
        dest = jnp.where(lane_i == k, base + rank[:, k:k + 1], dest)
    dest_ref[...] = jnp.transpose(dest)[:ROW_TILE, :].astype(I32)


def _route(cnt_o, idx_o, rank_o, bm, n_blk):
    Tp = idx_o.shape[0]
    tm = math.gcd(Tp, ROUTE_TILE)
    nbp = -(-n_blk // LANES) * LANES
    return pl.pallas_call(
        functools.partial(_route_kernel, tm=tm, bm=bm, nbp=nbp),
        grid=(Tp // tm,),
        in_specs=[
            pl.BlockSpec((ROW_TILE, LANES), lambda i: (0, 0)),
            pl.BlockSpec((tm, LANES), lambda i: (i, 0)),
            pl.BlockSpec((tm, LANES), lambda i: (i, 0)),
        ],
        out_specs=[
            pl.BlockSpec((ROW_TILE, tm), lambda i: (0, i)),
            pl.BlockSpec((ROW_TILE, nbp), lambda i: (0, 0)),
        ],
        out_shape=[
            jax.ShapeDtypeStruct((ROW_TILE, Tp), I32),
            jax.ShapeDtypeStruct((ROW_TILE, nbp), I32),
        ],
        scratch_shapes=[pltpu.VMEM((ROW_TILE, LANES), F32)],
        compiler_params=_cparams(("arbitrary",)),
        name="route",
    )(cnt_o, idx_o, rank_o)


def _sc_window_indices(idx):
    return jnp.pad(idx.reshape(-1, SC_WINDOW), ((0, 0), (0, LANES - SC_WINDOW)))


def _sc_scatter_rows(dest_km, src3, n_dst):
    n_slot, n_tok = dest_km.shape
    n_win = n_tok // SC_WINDOW
    row_shape = src3.shape[1:]
    mesh = plsc.VectorSubcoreMesh(core_axis_name="core", subcore_axis_name="subcore",
                                  num_cores=SC_CORES, num_subcores=SC_SUBCORES)

    @pl.kernel(out_type=jax.ShapeDtypeStruct((n_dst,) + row_shape, src3.dtype), mesh=mesh, name="sc_scatter")
    def scatter(src_hbm, i_hbm, o_hbm):
        def body(x_vmem, i_vmem):
            for k in range(n_slot):
                pltpu.sync_copy(x_vmem, o_hbm.at[i_vmem.at[k, pl.ds(0, SC_WINDOW)]])

        @pl.when(lax.axis_index("core") == 0)
        def _():
            pltpu.emit_pipeline(
                body,
                grid=(n_win,),
                in_specs=[pl.BlockSpec((SC_WINDOW,) + row_shape, lambda i: (i, 0, 0)),
                          pl.BlockSpec((ROW_TILE, LANES), lambda i: (i, 0))],
                out_specs=[],
                core_axis_name="subcore",
                dimension_semantics=(pltpu.PARALLEL,),
            )(src_hbm, i_hbm)

    idx = dest_km.reshape(n_slot, n_win, SC_WINDOW).transpose(1, 0, 2)
    idx = jnp.pad(idx, ((0, 0), (0, ROW_TILE - n_slot), (0, LANES - SC_WINDOW)))
    return scatter(src3, idx.reshape(n_win * ROW_TILE, LANES))


def _sc_gather_rows(sidx, src3, both_cores):
    n = sidx.shape[0]
    row_shape = src3.shape[1:]
    mesh = plsc.VectorSubcoreMesh(core_axis_name="core", subcore_axis_name="subcore",
                                  num_cores=SC_CORES, num_subcores=SC_SUBCORES)

    @pl.kernel(out_type=jax.ShapeDtypeStruct((n,) + row_shape, src3.dtype), mesh=mesh, name="sc_gather")
    def gather(src_hbm, i_hbm, o_hbm):
        def body(i_vmem, o_vmem):
            pltpu.sync_copy(src_hbm.at[i_vmem.at[0, pl.ds(0, SC_WINDOW)]], o_vmem)

        def run(axes):
            pltpu.emit_pipeline(
                body,
                grid=(n // SC_WINDOW,),
                in_specs=[pl.BlockSpec((1, LANES), lambda i: (i, 0))],
                out_specs=[pl.BlockSpec((SC_WINDOW,) + row_shape, lambda i: (i, 0, 0))],
                core_axis_name=axes,
                dimension_semantics=(pltpu.PARALLEL,),
            )(i_hbm, o_hbm)

        if both_cores:
            run(("core", "subcore"))
        else:
            pl.when(lax.axis_index("core") == 0)(lambda: run("subcore"))

    return gather(src3, _sc_window_indices(sidx))


def _moe_kernel(plan_ref, x_hbm, wgu_hbm, bgu_ref, wdn_hbm, bdn_ref, y_ref,
                x_buf, wgu_f32, wdn_f32, wgu_bf, wdn_bf, x_sems, sems):
    i = pl.program_id(0)
    n_used = plan_ref[PLAN_USED, i]
    expert = plan_ref[PLAN_EXPERT, i]
    active = i < n_used
    fresh = (i == 0) | (expert != plan_ref[PLAN_EXPERT, jnp.maximum(i - 1, 0)])
    slot = plan_ref[PLAN_SLOT, i]
    successor = plan_ref[PLAN_NEXT, i]
    rows = x_buf.shape[1]
    depth = x_buf.shape[0]

    def x_copy(blk):
        s = blk % depth
        return pltpu.make_async_copy(x_hbm.at[pl.ds(pl.multiple_of(blk * rows, rows), rows)],
                                     x_buf.at[s], x_sems.at[s])

    def weight_copies(e, s):
        return (pltpu.make_async_copy(wgu_hbm.at[e], wgu_f32.at[s], sems.at[s, 0]),
                pltpu.make_async_copy(wdn_hbm.at[e], wdn_f32.at[s], sems.at[s, 1]))

    @pl.when(i == 0)
    def _():
        for ahead in range(depth - 1):
            @pl.when(ahead < n_used)
            def _():
                x_copy(ahead).start()

    @pl.when(i + (depth - 1) < n_used)
    def _():
        x_copy(i + (depth - 1)).start()

    @pl.when(i == 0)
    def _():
        for c in weight_copies(expert, slot):
            c.start()

    @pl.when(active & fresh)
    def _():
        @pl.when(successor != expert)
        def _():
            for c in weight_copies(successor, 1 - slot):
                c.start()

        for c in weight_copies(expert, slot):
            c.wait()
        wgu_bf[...] = wgu_f32[slot].astype(BF16)
        wdn_bf[...] = wdn_f32[slot].astype(BF16)

    bm = rows // PACK_TILE
    n_valid = plan_ref[PLAN_VALID, i]

    def expert_ffn(n):
        words = _load_token_rows(x_buf.at[i % depth], n, r=PACK_TILE)
        valid = lax.broadcasted_iota(I32, (n, 1), 0) < n_valid
        x = _unpack_bf16_pairs(jnp.where(valid, words, jnp.uint32(0))).astype(BF16)
        gu = _dot(x, wgu_bf[...]) + bgu_ref[...]
        gate = jnp.minimum(gu[:, :D_FF], SWIGLU_LIMIT)
        up = jnp.clip(gu[:, D_FF:], -SWIGLU_LIMIT, SWIGLU_LIMIT)
        hid = (up + 1.0) * (gate * jax.nn.sigmoid(SWIGLU_ALPHA * gate))
        _store_token_rows(y_ref, _pack_bf16_pairs(_dot(hid.astype(BF16), wdn_bf[...]) + bdn_ref[...]))

    @pl.when(active)
    def _():
        x_copy(i).wait()

    half = bm // 2

    @pl.when(active & (n_valid > half))
    def _():
        expert_ffn(bm)

    @pl.when(active & (n_valid <= half))
    def _():
        expert_ffn(half)
        y_ref[pl.ds(half * PACK_TILE, half * PACK_TILE), :] = jnp.zeros((half * PACK_TILE, LANES), U32)

    @pl.when(jnp.logical_not(active))
    def _():
        y_ref[...] = jnp.zeros_like(y_ref)


def _moe(plan, xpad, wgu, bgu, wdn, bdn, n_blk, bm):
    bias = lambda i, plan: (plan[PLAN_EXPERT, i], 0, 0)
    grid_spec = pltpu.PrefetchScalarGridSpec(
        num_scalar_prefetch=1,
        grid=(n_blk,),
        in_specs=[
            pl.BlockSpec(memory_space=pl.ANY),
            pl.BlockSpec(memory_space=pl.ANY),
            pl.BlockSpec((None, 1, 2 * D_FF), bias),
            pl.BlockSpec(memory_space=pl.ANY),
            pl.BlockSpec((None, 1, D_MODEL), bias),
        ],
        out_specs=pl.BlockSpec((bm * PACK_TILE, LANES), lambda i, plan: (i, 0)),
        scratch_shapes=[
            pltpu.VMEM((MOE_X_DEPTH, bm * PACK_TILE, LANES), U32),
            pltpu.VMEM((2, D_MODEL, 2 * D_FF), F32), pltpu.VMEM((2, D_FF, D_MODEL), F32),
            pltpu.VMEM((D_MODEL, 2 * D_FF), BF16), pltpu.VMEM((D_FF, D_MODEL), BF16),
            pltpu.SemaphoreType.DMA((MOE_X_DEPTH,)), pltpu.SemaphoreType.DMA((2, 2)),
        ],
    )
    return pl.pallas_call(
        _moe_kernel,
        grid_spec=grid_spec,
        out_shape=jax.ShapeDtypeStruct((n_blk * bm * PACK_TILE, LANES), U32),
        compiler_params=_cparams(("arbitrary",)),
        name="moe",
    )(plan, xpad, wgu, bgu, wdn, bdn)


def _final_kernel(h_ref, yg_ref, gate_ref, p_ref, wple_ref, wpg_ref, bpg_ref,
                  g2_ref, b2_ref, g3_ref, b3_ref, *rest, tm, alpha):
    o_ref = rest[-1]
    rc = FINAL_CHUNK
    for c in range(tm // rc):
        rows = slice(c * rc, (c + 1) * rc)
        gates = gate_ref[rows, :]
        z = alpha * _load_token_rows(h_ref, rc, c * rc)
        for k in range(TOP_K):
            y = _unpack_bf16_pairs(_load_token_rows(yg_ref.at[k], rc, c * rc, r=PACK_TILE))
            z = z + gates[:, k:k + 1] * y
        h2 = _layer_norm(z, g2_ref[...], b2_ref[...])
        emb = _dot(p_ref[rows, :].astype(BF16), wple_ref[...])
        pg = jax.nn.sigmoid(_dot(h2.astype(BF16), wpg_ref[...]) + bpg_ref[...])
        o_ref[rows, :] = _layer_norm(alpha * h2 + emb * pg, g3_ref[...], b3_ref[...])


def _final(h1, yg, gates, p2, wple, wpg, bpg, g2, b2, g3, b3, tm, alpha, tile_off, out_prev):
    T = h1.shape[0] // ROW_TILE
    const = lambda *shape: pl.BlockSpec(shape, lambda i: (0,) * len(shape))
    in_specs = [
        pl.BlockSpec((tm * ROW_TILE, LANES), lambda i: (i, 0)),
        pl.BlockSpec((TOP_K, tm * PACK_TILE, LANES), lambda i: (0, i, 0)),
        pl.BlockSpec((tm, LANES), lambda i: (i, 0)),
        pl.BlockSpec((tm, PLE_DIM), lambda i: (i + tile_off, 0)),
        const(PLE_DIM, D_MODEL), const(D_MODEL, D_MODEL), const(1, D_MODEL),
        const(1, D_MODEL), const(1, D_MODEL), const(1, D_MODEL), const(1, D_MODEL),
    ]
    args = [h1, yg, gates, p2, wple, wpg, bpg, g2, b2, g3, b3]
    aliases = {}
    if out_prev is not None:
        in_specs.append(pl.BlockSpec(memory_space=pl.ANY))
        aliases = {len(args): 0}
        args.append(out_prev)
    return pl.pallas_call(
        functools.partial(_final_kernel, tm=tm, alpha=alpha),
        grid=(T // tm,),
        in_specs=in_specs,
        out_specs=pl.BlockSpec((tm, D_MODEL), lambda i: (i + tile_off, 0)),
        out_shape=jax.ShapeDtypeStruct((p2.shape[0], D_MODEL), F32),
        input_output_aliases=aliases,
        compiler_params=_cparams(("arbitrary",)),
        name="final",
    )(*args)


def _layer(h2d, p2d, batch, seq, alpha, w_in, b_in, gmlp_ln_g, gmlp_ln_b, w_spatial, b_spatial,
           w_branch_a, w_branch_b, w_out, b_out, ln1_g, ln1_b, w_router, b_router,
           w_gate_up, b_gate_up, w_down, b_down, ln2_g, ln2_b, w_ple, w_ple_gate, b_ple_gate,
           ln3_g, ln3_b):
    T = batch * seq
    tm = math.gcd(T, ROW_TILE_A)
    bm = MOE_BLOCK
    off_f = 3 * FOX_WIDTH
    off_u = off_f + N_HEADS
    row = lambda v: v.reshape(1, -1).astype(F32)

    w1 = jnp.concatenate([w_in[:, :off_u], jnp.zeros((D_MODEL, LANES - N_HEADS), F32)], axis=1).astype(BF16)
    b1 = jnp.concatenate([b_in[:off_u], jnp.zeros((LANES - N_HEADS,), F32)]).reshape(1, -1)
    w2 = w_in[:, off_u:].astype(BF16)
    b2 = row(b_in[off_u:])
    bs_tile = jnp.repeat(b_spatial.T, HEAD_DIM, axis=1)
    wr = jnp.concatenate([w_router, jnp.zeros((D_MODEL, LANES - N_EXPERTS), F32)], axis=1)
    br = jnp.concatenate([b_router, jnp.full((LANES - N_EXPERTS,), -1e30, F32)]).reshape(1, -1)

    wsb, wab, wbb, wob = (w.astype(BF16) for w in (w_spatial, w_branch_a, w_branch_b, w_out))
    wpleb, wpgb = w_ple.astype(BF16), w_ple_gate.astype(BF16)
    bgu, bdn = b_gate_up.reshape(N_EXPERTS, 1, -1), b_down.reshape(N_EXPERTS, 1, -1)

    n_parts = N_PARTS if batch % N_PARTS == 0 else 1
    pb = batch // n_parts
    Tp = pb * seq
    tiles = Tp // tm
    n_assign = Tp * TOP_K
    n_blk = -(-n_assign // bm) + N_EXPERTS
    n_rows = n_blk * bm

    def front(part):
        q, k, v, f_pad = _qkvf(h2d, w1, b1, tm, Tp, part * tiles)
        auxq, auxk = _decay(f_pad, pb, seq)
        attn = _attention(q, auxq, k, auxk, v, pb, seq)
        h1, hx, idx_o, gate_o, rank_o, cnt_o = _mix(
            h2d, attn, w2, b2, row(gmlp_ln_g), row(gmlp_ln_b), wsb, bs_tile, wab, wbb, wob, row(b_out),
            row(ln1_g), row(ln1_b), wr, br, tm, alpha, part * tiles)
        dest8, plan = _route(cnt_o, idx_o, rank_o, bm, n_blk)
        dest_km = dest8[:TOP_K]
        xpad = _sc_scatter_rows(dest_km, hx.reshape(Tp, PACK_TILE, LANES), n_rows)
        return h1, gate_o, dest_km, plan, xpad

    def experts(part, state):
        h1, gate_o, dest_km, plan, xpad = state
        ypad = _moe(plan, xpad.reshape(n_rows * PACK_TILE, LANES), w_gate_up, bgu, w_down, bdn, n_blk, bm)
        yg = _sc_gather_rows(dest_km.reshape(-1), ypad.reshape(n_rows, PACK_TILE, LANES),
                             both_cores=part == n_parts - 1)
        return h1, gate_o, yg

    def back(part, state, out_prev):
        h1, gate_o, yg = state
        return _final(h1, yg.reshape(TOP_K, Tp * PACK_TILE, LANES), gate_o, p2d, wpleb, wpgb,
                      row(b_ple_gate), row(ln2_g), row(ln2_b), row(ln3_g), row(ln3_b), tm, alpha,
                      part * tiles, out_prev)

    fronts = [front(part) for part in range(n_parts)]
    mids = [experts(part, state) for part, state in enumerate(fronts)]
    out = None
    for part in range(n_parts):
        out = back(part, mids[part], out)
    return out


def kernel(x, p, w_in, b_in, gmlp_ln_g, gmlp_ln_b, w_spatial, b_spatial, w_branch_a, w_branch_b, w_out, b_out, ln1_g, ln1_b, w_router, b_router, w_gate_up, b_gate_up, w_down, b_down, ln2_g, ln2_b, w_ple, w_ple_gate, b_ple_gate, ln3_g, ln3_b):
    batch, seq, d = x.shape
    depth = w_in.shape[0]
    assert x.dtype == F32 and d == D_MODEL and seq % GMLP_CHUNK == 0
    assert w_in.shape[1:] == (D_MODEL, 3 * FOX_WIDTH + N_HEADS + 2 * GMLP_WIDTH + 2 * D_MODEL)
    assert w_gate_up.shape[1:] == (N_EXPERTS, D_MODEL, 2 * D_FF) and p.shape[-1] == PLE_DIM
    alpha = (2.0 * depth) ** 0.25
    h = x.reshape(batch * seq, d)
    for i in range(depth):
        h = _layer(h, p[i].reshape(batch * seq, PLE_DIM), batch, seq, alpha,
                   w_in[i], b_in[i], gmlp_ln_g[i], gmlp_ln_b[i], w_spatial[i], b_spatial[i],
                   w_branch_a[i], w_branch_b[i], w_out[i], b_out[i], ln1_g[i], ln1_b[i],
                   w_router[i], b_router[i], w_gate_up[i], b_gate_up[i], w_down[i], b_down[i],
                   ln2_g[i], ln2_b[i], w_ple[i], w_ple_gate[i], b_ple_gate[i], ln3_g[i], ln3_b[i])
    return h.reshape(batch, seq, d)
```

```python
import functools
import math

import jax
import jax.numpy as jnp
import numpy as np
from jax import lax
from jax.experimental import pallas as pl
from jax.experimental.pallas import tpu as pltpu
from jax.experimental.pallas import tpu_sc as plsc

F32 = jnp.float32
BF16 = jnp.bfloat16
I32 = jnp.int32
U32 = jnp.uint32

D_MODEL = 1024
N_HEADS = 8
HEAD_DIM = 64
FOX_WIDTH = N_HEADS * HEAD_DIM
GMLP_WIDTH = 512
GMLP_CHUNK = 128
N_EXPERTS = 32
TOP_K = 4
D_FF = 1024
PLE_DIM = 256
SWIGLU_LIMIT = 7.0
SWIGLU_ALPHA = 1.702
LN_EPS = 1e-5
LANES = 128
ROW_TILE = 8
PACK_TILE = D_MODEL // 2 // LANES
VMEM_LIMIT = 56 * 1024 * 1024

MOE_BLOCK = 512
MOE_TAIL_STEP = 128
MOE_X_DEPTH = 3
ATTN_Q_TILE = 1024
ATTN_K_TILE = 1024
ATTN_PAIRS = 2
LOG2E = math.log2(math.e)
ROW_TILE_A = 512
SC_CORES = 2
SC_SUBCORES = 16
SC_WINDOW = 32
N_PARTS = 2
ROUTE_TILE = 1024
FINAL_CHUNK = 256


def _cparams(sem):
    return pltpu.CompilerParams(dimension_semantics=sem, vmem_limit_bytes=VMEM_LIMIT)


def _gelu(x):
    c = math.sqrt(2.0 / math.pi)
    return 0.5 * x * (1.0 + jnp.tanh(c * (x + 0.044715 * (x * x * x))))


def _layer_norm(x, g, b):
    mu = jnp.mean(x, axis=-1, keepdims=True)
    xc = x - mu
    var = jnp.mean(xc * xc, axis=-1, keepdims=True)
    return xc * lax.rsqrt(var + LN_EPS) * g + b


def _split3(x):
    hi = x.astype(BF16)
    r = x - hi.astype(F32)
    mid = r.astype(BF16)
    lo = (r - mid.astype(F32)).astype(BF16)
    return hi, mid, lo


def _dot(a, b):
    return jnp.dot(a, b, preferred_element_type=F32)


def _load_token_rows(ref, n, first=0, r=ROW_TILE):
    return jnp.concatenate([ref[pl.ds(first * r + j, n, stride=r), :] for j in range(r)], axis=1)


def _store_token_rows(ref, val):
    n, r = val.shape[0], val.shape[1] // LANES
    for j in range(r):
        ref[pl.ds(j, n, stride=r), :] = val[:, j * LANES:(j + 1) * LANES]


def _pack_bf16_pairs(x):
    half = x.shape[1] // 2
    hi = pltpu.bitcast(x[:, :half].astype(BF16).astype(F32), U32)
    lo = pltpu.bitcast(x[:, half:].astype(BF16).astype(F32), U32)
    return hi | (lo >> 16)


def _unpack_bf16_pairs(w):
    a = pltpu.bitcast(w & jnp.uint32(0xFFFF0000), F32)
    b = pltpu.bitcast(w << 16, F32)
    return jnp.concatenate([a, b], axis=1)


def _qkvf_kernel(x_ref, w_ref, b_ref, q_ref, k_ref, v_ref, f_ref):
    x = x_ref[...].astype(BF16)
    proj = _dot(x, w_ref[...]) + b_ref[...]
    q_ref[...] = (proj[:, :FOX_WIDTH] * (HEAD_DIM ** -0.5 * LOG2E)).astype(BF16)
    k_ref[...] = proj[:, FOX_WIDTH:2 * FOX_WIDTH].astype(BF16)
    v_ref[...] = proj[:, 2 * FOX_WIDTH:3 * FOX_WIDTH].astype(BF16)
    f_ref[...] = proj[:, 3 * FOX_WIDTH:]


def _qkvf(x2, w, b, tm, T, tile_off):
    n_out = w.shape[1]
    return pl.pallas_call(
        _qkvf_kernel,
        grid=(T // tm,),
        in_specs=[
            pl.BlockSpec((tm, D_MODEL), lambda i: (i + tile_off, 0)),
            pl.BlockSpec((D_MODEL, n_out), lambda i: (0, 0)),
            pl.BlockSpec((1, n_out), lambda i: (0, 0)),
        ],
        out_specs=[
            pl.BlockSpec((tm, FOX_WIDTH), lambda i: (i, 0)),
            pl.BlockSpec((tm, FOX_WIDTH), lambda i: (i, 0)),
            pl.BlockSpec((tm, FOX_WIDTH), lambda i: (i, 0)),
            pl.BlockSpec((tm, LANES), lambda i: (i, 0)),
        ],
        out_shape=[
            jax.ShapeDtypeStruct((T, FOX_WIDTH), BF16),
            jax.ShapeDtypeStruct((T, FOX_WIDTH), BF16),
            jax.ShapeDtypeStruct((T, FOX_WIDTH), BF16),
            jax.ShapeDtypeStruct((T, LANES), F32),
        ],
        compiler_params=_cparams(("arbitrary",)),
        name="qkvf",
    )(x2, w, b)


def _decay_placement():
    pq = np.zeros((3 * LANES, FOX_WIDTH), np.float32)
    pk = np.zeros((3 * LANES, FOX_WIDTH), np.float32)
    cq = np.zeros((1, FOX_WIDTH), np.float32)
    ck = np.zeros((1, FOX_WIDTH), np.float32)
    for h in range(N_HEADS):
        base = (h // 2) * LANES + (HEAD_DIM if h % 2 == 0 else 0)
        for piece in range(3):
            pq[piece * LANES + h, base + piece] = 1.0
            pk[piece * LANES + h, base + 3 + piece] = -1.0
            cq[0, base + 3 + piece] = 1.0
            ck[0, base + piece] = 1.0
    return pq, pk, cq, ck


def _decay_kernel(f_ref, pq_ref, pk_ref, cq_ref, ck_ref, auxq_ref, auxk_ref, *, seq, blk):
    r = lax.broadcasted_iota(I32, (blk, blk), 0)
    c = lax.broadcasted_iota(I32, (blk, blk), 1)
    tri = jnp.where(c <= r, 1.0, 0.0).astype(BF16)
    carry = jnp.zeros((1, LANES), F32)
    for i in range(seq // blk):
        f = f_ref[i * blk:(i + 1) * blk, :]
        ls = jnp.minimum(f, 0.0) - jnp.log1p(jnp.exp(-jnp.abs(f)))
        hi, mid, lo = _split3(ls)
        cs = _dot(tri, hi) + _dot(tri, mid) + _dot(tri, lo) + carry
        carry = cs[blk - 1:blk, :]
        pieces = jnp.concatenate(_split3(cs * LOG2E), axis=1)
        auxq_ref[i * blk:(i + 1) * blk, :] = (_dot(pieces, pq_ref[...]) + cq_ref[...]).astype(BF16)
        auxk_ref[i * blk:(i + 1) * blk, :] = (_dot(pieces, pk_ref[...]) + ck_ref[...]).astype(BF16)


def _decay(f_pad, batch, seq):
    blk = 256 if seq % 256 == 0 else LANES
    pq, pk, cq, ck = _decay_placement()
    const = lambda *shape: pl.BlockSpec(shape, lambda b: (0,) * len(shape))
    return pl.pallas_call(
        functools.partial(_decay_kernel, seq=seq, blk=blk),
        grid=(batch,),
        in_specs=[
            pl.BlockSpec((seq, LANES), lambda b: (b, 0)),
            const(3 * LANES, FOX_WIDTH), const(3 * LANES, FOX_WIDTH), const(1, FOX_WIDTH), const(1, FOX_WIDTH),
        ],
        out_specs=[
            pl.BlockSpec((seq, FOX_WIDTH), lambda b: (b, 0)),
            pl.BlockSpec((seq, FOX_WIDTH), lambda b: (b, 0)),
        ],
        out_shape=[
            jax.ShapeDtypeStruct((batch * seq, FOX_WIDTH), BF16),
            jax.ShapeDtypeStruct((batch * seq, FOX_WIDTH), BF16),
        ],
        compiler_params=_cparams(("arbitrary",)),
        name="decay",
    )(f_pad, jnp.asarray(pq, BF16), jnp.asarray(pk, BF16), jnp.asarray(cq), jnp.asarray(ck))


def _attn_kernel(q_ref, auxq_ref, k_ref, auxk_ref, v_ref, o_ref, *, tq, tk):
    qi = pl.program_id(2)
    n_pairs = q_ref.shape[1] // LANES
    low_q = lax.broadcasted_iota(I32, (tq, LANES), 1) < HEAD_DIM
    qs = []
    for pr in range(n_pairs):
        q = q_ref[:, pr * LANES:(pr + 1) * LANES]
        aq = auxq_ref[:, pr * LANES:(pr + 1) * LANES]
        qs += [jnp.where(low_q, q, aq), jnp.where(low_q, aq, q)]

    def step(start, n, carry, masked):
        low_k = lax.broadcasted_iota(I32, (n, LANES), 1) < HEAD_DIM
        ones = jnp.ones((n, LANES), BF16)
        out = []
        for pr in range(n_pairs):
            lanes = slice(pr * LANES, (pr + 1) * LANES)
            kb = k_ref[pl.ds(start, n), lanes]
            ak = auxk_ref[pl.ds(start, n), lanes]
            vb = v_ref[pl.ds(start, n), lanes]
            ks = (jnp.where(low_k, kb, ak), jnp.where(low_k, ak, kb))
            vs = (jnp.where(low_k, vb, ones), jnp.where(low_k, ones, vb))
            for j in range(2):
                m, acc = carry[2 * pr + j]
                s = lax.dot_general(qs[2 * pr + j], ks[j], (((1,), (1,)), ((), ())),
                                    preferred_element_type=F32)
                if masked:
                    row = lax.broadcasted_iota(I32, (tq, n), 0)
                    col = lax.broadcasted_iota(I32, (tq, n), 1)
                    s = jnp.where(col + (start - qi * tq) <= row, s, -jnp.inf)
                m_new = jnp.maximum(m, jnp.max(s, axis=1, keepdims=True))
                p = jnp.exp2(s - m_new)
                acc = jnp.exp2(m - m_new) * acc + _dot(p.astype(BF16), vs[j])
                out.append((m_new, acc))
        return tuple(out)

    init = tuple((jnp.full((tq, 1), -jnp.inf, F32), jnp.zeros((tq, LANES), F32))
                 for _ in range(2 * n_pairs))
    carry = lax.fori_loop(0, qi, lambda t, c: step(pl.multiple_of(t * tq, tq), tq, c, False), init)
    for d in range(tq // tk):
        carry = step(pl.multiple_of(qi * tq + d * tk, tk), tk, carry, True)
    for pr in range(n_pairs):
        acc0, acc1 = carry[2 * pr][1], carry[2 * pr + 1][1]
        out0 = acc0 / acc0[:, HEAD_DIM:HEAD_DIM + 1]
        out1 = acc1 / acc1[:, 0:1]
        o_ref[:, pr * LANES:(pr + 1) * LANES] = jnp.where(low_q, out0, out1).astype(BF16)


def _attention(q, auxq, k, auxk, v, batch, seq):
    tq = math.gcd(seq, ATTN_Q_TILE)
    tk = math.gcd(tq, ATTN_K_TILE)
    nq = seq // tq
    T = batch * seq
    width = ATTN_PAIRS * LANES
    q_spec = pl.BlockSpec((tq, width), lambda b, hp, qi: (b * nq + qi, hp))
    kv_spec = pl.BlockSpec((seq, width), lambda b, hp, qi: (b, hp))
    return pl.pallas_call(
        functools.partial(_attn_kernel, tq=tq, tk=tk),
        grid=(batch, N_HEADS // (2 * ATTN_PAIRS), nq),
        in_specs=[q_spec, q_spec, kv_spec, kv_spec, kv_spec],
        out_specs=q_spec,
        out_shape=jax.ShapeDtypeStruct((T, FOX_WIDTH), BF16),
        compiler_params=_cparams(("arbitrary", "arbitrary", "arbitrary")),
        name="attn",
    )(q, auxq, k, auxk, v)


def _mix_kernel(x_ref, attn_ref, w2_ref, b2_ref, lng_ref, lnb_ref, ws_ref, bs_ref,
                wa_ref, wb_ref, wo_ref, bo_ref, g1_ref, b1_ref, wr_ref, br_ref,
                h_ref, hx_ref, idx_ref, gate_ref, rank_ref, cnt_ref, carry_ref, *, tm, alpha):
    i = pl.program_id(0)

    @pl.when(i == 0)
    def _():
        carry_ref[...] = jnp.zeros_like(carry_ref)

    x = x_ref[...]
    proj = _dot(x.astype(BF16), w2_ref[...]) + b2_ref[...]
    u = _gelu(proj[:, :GMLP_WIDTH])
    gv = _gelu(proj[:, GMLP_WIDTH:2 * GMLP_WIDTH])
    vln = _layer_norm(gv, lng_ref[...], lnb_ref[...]).astype(BF16)

    cr = lax.broadcasted_iota(I32, (GMLP_CHUNK, GMLP_CHUNK), 0)
    cc = lax.broadcasted_iota(I32, (GMLP_CHUNK, GMLP_CHUNK), 1)
    tril = cc <= cr
    lo_half = cc < HEAD_DIM
    zero_w = jnp.zeros((GMLP_CHUNK, GMLP_CHUNK), BF16)
    n_slab = GMLP_WIDTH // LANES
    lhs = []
    for s in range(n_slab):
        w0 = jnp.where(tril, ws_ref[2 * s], zero_w)
        w1 = jnp.where(tril, ws_ref[2 * s + 1], zero_w)
        lhs.append(jnp.concatenate([w0, w1], axis=1))
    bs = bs_ref[...]
    rows = []
    for c in range(tm // GMLP_CHUNK):
        cols = []
        for s in range(n_slab):
            vs = vln[c * GMLP_CHUNK:(c + 1) * GMLP_CHUNK, s * LANES:(s + 1) * LANES]
            rhs = jnp.concatenate([jnp.where(lo_half, vs, zero_w), jnp.where(lo_half, zero_w, vs)], axis=0)
            cols.append(_dot(lhs[s], rhs))
        rows.append(jnp.concatenate(cols, axis=1) + bs)
    sp = jnp.concatenate(rows, axis=0) if len(rows) > 1 else rows[0]
    sgu = (u * sp).astype(BF16)

    ga = jax.nn.sigmoid(proj[:, 2 * GMLP_WIDTH:2 * GMLP_WIDTH + D_MODEL])
    gb = jax.nn.sigmoid(proj[:, 2 * GMLP_WIDTH + D_MODEL:])
    merged = ga * _dot(attn_ref[...], wa_ref[...]) + gb * _dot(sgu, wb_ref[...])
    mix = _dot(merged.astype(BF16), wo_ref[...]) + bo_ref[...]
    h = _layer_norm(alpha * x + mix, g1_ref[...], b1_ref[...])
    _store_token_rows(h_ref, h)
    _store_token_rows(hx_ref, _pack_bf16_pairs(h))

    a_hi = h.astype(BF16)
    a_lo = (h - a_hi.astype(F32)).astype(BF16)
    wr = wr_ref[...]
    w_hi = wr.astype(BF16)
    w_lo = (wr - w_hi.astype(F32)).astype(BF16)
    w_cat = jnp.concatenate([w_hi, w_lo], axis=1)
    r_hi = _dot(a_hi, w_cat)
    r_lo = _dot(a_lo, w_cat)
    logits = (r_hi[:, :LANES] + r_hi[:, LANES:]) + (r_lo[:, :LANES] + r_lo[:, LANES:]) + br_ref[...]

    lane_i = lax.broadcasted_iota(I32, (tm, LANES), 1)
    lane_f = lane_i.astype(F32)
    vals, idxs = [], []
    l = logits
    for _ in range(TOP_K):
        m = jnp.max(l, axis=1, keepdims=True)
        ix = jnp.min(jnp.where(l == m, lane_f, float(LANES)), axis=1, keepdims=True)
        vals.append(m)
        idxs.append(ix)
        l = jnp.where(lane_f == ix, -jnp.inf, l)
    es = [jnp.exp(v - vals[0]) for v in vals]
    den = es[0] + es[1] + es[2] + es[3]

    onehot = jnp.zeros((tm, LANES), F32)
    idx_out = jnp.zeros((tm, LANES), F32)
    gate_out = jnp.zeros((tm, LANES), F32)
    for k in range(TOP_K):
        onehot = onehot + jnp.where(lane_f == idxs[k], 1.0, 0.0)
        idx_out = jnp.where(lane_i == k, idxs[k], idx_out)
        gate_out = jnp.where(lane_i == k, es[k] / den, gate_out)

    tr = lax.broadcasted_iota(I32, (tm, tm), 0)
    tc = lax.broadcasted_iota(I32, (tm, tm), 1)
    strict = jnp.where(tc < tr, 1.0, 0.0).astype(BF16)
    carry = carry_ref[0:1, :]
    before = _dot(strict, onehot.astype(BF16)) + carry
    rank_out = jnp.zeros((tm, LANES), F32)
    for k in range(TOP_K):
        rk = jnp.sum(jnp.where(lane_f == idxs[k], before, 0.0), axis=1, keepdims=True)
        rank_out = jnp.where(lane_i == k, rk, rank_out)
    new_carry = carry + jnp.sum(onehot, axis=0, keepdims=True)
    carry_ref[...] = jnp.broadcast_to(new_carry, carry_ref.shape)
    cnt_ref[...] = jnp.broadcast_to(new_carry, cnt_ref.shape).astype(I32)
    idx_ref[...] = idx_out.astype(I32)
    gate_ref[...] = gate_out
    rank_ref[...] = rank_out.astype(I32)


def _mix(x2, attn, w2, b2, lng, lnb, ws, bs_tile, wa, wb, wo, bo, g1, b1, wr, br, tm, alpha, tile_off):
    T = attn.shape[0]
    n2 = w2.shape[1]
    const = lambda *shape: pl.BlockSpec(shape, lambda i: (0,) * len(shape))
    return pl.pallas_call(
        functools.partial(_mix_kernel, tm=tm, alpha=alpha),
        grid=(T // tm,),
        in_specs=[
            pl.BlockSpec((tm, D_MODEL), lambda i: (i + tile_off, 0)),
            pl.BlockSpec((tm, FOX_WIDTH), lambda i: (i, 0)),
            const(D_MODEL, n2), const(1, n2),
            const(1, GMLP_WIDTH), const(1, GMLP_WIDTH),
            const(GMLP_WIDTH // HEAD_DIM, GMLP_CHUNK, GMLP_CHUNK), const(GMLP_CHUNK, GMLP_WIDTH),
            const(FOX_WIDTH, D_MODEL), const(GMLP_WIDTH, D_MODEL),
            const(D_MODEL, D_MODEL), const(1, D_MODEL),
            const(1, D_MODEL), const(1, D_MODEL),
            const(D_MODEL, LANES), const(1, LANES),
        ],
        out_specs=[
            pl.BlockSpec((tm * ROW_TILE, LANES), lambda i: (i, 0)),
            pl.BlockSpec((tm * PACK_TILE, LANES), lambda i: (i, 0)),
            pl.BlockSpec((tm, LANES), lambda i: (i, 0)),
            pl.BlockSpec((tm, LANES), lambda i: (i, 0)),
            pl.BlockSpec((tm, LANES), lambda i: (i, 0)),
            pl.BlockSpec((ROW_TILE, LANES), lambda i: (0, 0)),
        ],
        out_shape=[
            jax.ShapeDtypeStruct((T * ROW_TILE, LANES), F32),
            jax.ShapeDtypeStruct((T * PACK_TILE, LANES), U32),
            jax.ShapeDtypeStruct((T, LANES), I32),
            jax.ShapeDtypeStruct((T, LANES), F32),
            jax.ShapeDtypeStruct((T, LANES), I32),
            jax.ShapeDtypeStruct((ROW_TILE, LANES), I32),
        ],
        scratch_shapes=[pltpu.VMEM((ROW_TILE, LANES), F32)],
        compiler_params=_cparams(("arbitrary",)),
        name="mix",
    )(x2, attn, w2, b2, lng, lnb, ws, bs_tile, wa, wb, wo, bo, g1, b1, wr, br)


PLAN_EXPERT, PLAN_VALID, PLAN_SLOT, PLAN_NEXT, PLAN_USED = range(5)


def _lane_cumsum(x):
    lane = lax.broadcasted_iota(I32, x.shape, 1)
    shift = 1
    while shift < LANES:
        x = x + jnp.where(lane >= shift, pltpu.roll(x, shift, 1), 0.0)
        shift *= 2
    return x


def _route_kernel(cnt_ref, idx_ref, rank_ref, dest_ref, plan_ref, start_ref, *, tm, bm, nbp):
    @pl.when(pl.program_id(0) == 0)
    def _():
        lane = lax.broadcasted_iota(I32, (ROW_TILE, LANES), 1)
        counts = jnp.where(lane < N_EXPERTS, cnt_ref[...].astype(F32), 0.0)
        padded = jnp.floor((counts + (bm - 1)) / bm) * bm
        pad_end = _lane_cumsum(padded)
        pad_start = pad_end - padded
        start_ref[...] = pad_start
        has_rows = jnp.where(counts > 0, 1.0, 0.0)
        order = _lane_cumsum(has_rows) - 1.0
        slot = order - 2.0 * jnp.floor(order * 0.5)
        total = jnp.sum(jnp.where(lane == N_EXPERTS - 1, pad_end, 0.0), axis=1, keepdims=True)[0:1, :]
        n_used = total / bm

        col = lambda r: jnp.transpose(r)[:, 0:1]
        sub = lax.broadcasted_iota(I32, (LANES, 1), 0)
        is_expert = sub < N_EXPERTS
        lane_w = lax.broadcasted_iota(I32, (LANES, LANES), 1)
        sub_w = lax.broadcasted_iota(I32, (LANES, LANES), 0)
        later = (lane_w > sub_w) & (lane_w < N_EXPERTS) & (has_rows[0:1, :] > 0)
        succ = jnp.min(jnp.where(later, lane_w, N_EXPERTS).astype(F32), axis=1, keepdims=True)
        succ = jnp.where(succ == N_EXPERTS, sub.astype(F32), succ)

        blk = lax.broadcasted_iota(I32, (1, nbp), 1).astype(F32)
        blk_src = jnp.minimum(blk, n_used - 1.0)
        blk_row = blk_src * bm
        below = is_expert & (col(pad_end) <= blk_row)
        blk_e = jnp.minimum(jnp.sum(jnp.where(below, 1.0, 0.0), axis=0, keepdims=True), N_EXPERTS - 1.0)
        hit = sub.astype(F32) == blk_e
        take = lambda c: jnp.sum(jnp.where(hit, c, 0.0), axis=0, keepdims=True)
        blk_valid = jnp.clip(take(col(pad_start + counts)) - blk_row, 0.0, bm)
        rows = [None] * ROW_TILE
        rows[PLAN_EXPERT], rows[PLAN_VALID] = blk_e, blk_valid
        rows[PLAN_SLOT], rows[PLAN_NEXT] = take(col(slot)), take(succ)
        rows[PLAN_USED] = jnp.broadcast_to(n_used, (1, nbp))
        zero = jnp.zeros((1, nbp), F32)
        plan_ref[...] = jnp.concatenate([zero if r is None else r for r in rows], axis=0).astype(I32)

    pad_start = start_ref[0:1, :]
    lane_i = lax.broadcasted_iota(I32, (tm, LANES), 1)
    lane_f = lane_i.astype(F32)
    idx = idx_ref[...].astype(F32)
    rank = rank_ref[...].astype(F32)
    dest = jnp.zeros((tm, LANES), F32)
    for k in range(TOP_K):
        base = jnp.sum(jnp.where(lane_f == idx[:, k:k + 1], pad_start, 0.0), axis=1, keepdims=True)
        dest = jnp.where(lane_i == k, base + rank[:, k:k + 1], dest)
    dest_ref[...] = jnp.transpose(dest)[:ROW_TILE, :].astype(I32)


def _route(cnt_o, idx_o, rank_o, bm, n_blk):
    Tp = idx_o.shape[0]
    tm = math.gcd(Tp, ROUTE_TILE)
    nbp = -(-n_blk // LANES) * LANES
    return pl.pallas_call(
        functools.partial(_route_kernel, tm=tm, bm=bm, nbp=nbp),
        grid=(Tp // tm,),
        in_specs=[
            pl.BlockSpec((ROW_TILE, LANES), lambda i: (0, 0)),
            pl.BlockSpec((tm, LANES), lambda i: (i, 0)),
            pl.BlockSpec((tm, LANES), lambda i: (i, 0)),
        ],
        out_specs=[
            pl.BlockSpec((ROW_TILE, tm), lambda i: (0, i)),
            pl.BlockSpec((ROW_TILE, nbp), lambda i: (0, 0)),
        ],
        out_shape=[
            jax.ShapeDtypeStruct((ROW_TILE, Tp), I32),
            jax.ShapeDtypeStruct((ROW_TILE, nbp), I32),
        ],
        scratch_shapes=[pltpu.VMEM((ROW_TILE, LANES), F32)],
        compiler_params=_cparams(("arbitrary",)),
        name="route",
    )(cnt_o, idx_o, rank_o)


def _sc_window_indices(idx):
    return jnp.pad(idx.reshape(-1, SC_WINDOW), ((0, 0), (0, LANES - SC_WINDOW)))


def _sc_scatter_rows(dest_km, src3, n_dst):
    n_slot, n_tok = dest_km.shape
    n_win = n_tok // SC_WINDOW
    row_shape = src3.shape[1:]
    mesh = plsc.VectorSubcoreMesh(core_axis_name="core", subcore_axis_name="subcore",
                                  num_cores=SC_CORES, num_subcores=SC_SUBCORES)

    @pl.kernel(out_type=jax.ShapeDtypeStruct((n_dst,) + row_shape, src3.dtype), mesh=mesh, name="sc_scatter")
    def scatter(src_hbm, i_hbm, o_hbm):
        def body(x_vmem, i_vmem):
            for k in range(n_slot):
                pltpu.sync_copy(x_vmem, o_hbm.at[i_vmem.at[k, pl.ds(0, SC_WINDOW)]])

        @pl.when(lax.axis_index("core") == 0)
        def _():
            pltpu.emit_pipeline(
                body,
                grid=(n_win,),
                in_specs=[pl.BlockSpec((SC_WINDOW,) + row_shape, lambda i: (i, 0, 0)),
                          pl.BlockSpec((ROW_TILE, LANES), lambda i: (i, 0))],
                out_specs=[],
                core_axis_name="subcore",
                dimension_semantics=(pltpu.PARALLEL,),
            )(src_hbm, i_hbm)

    idx = dest_km.reshape(n_slot, n_win, SC_WINDOW).transpose(1, 0, 2)
    idx = jnp.pad(idx, ((0, 0), (0, ROW_TILE - n_slot), (0, LANES - SC_WINDOW)))
    return scatter(src3, idx.reshape(n_win * ROW_TILE, LANES))


def _sc_gather_rows(sidx, src3, both_cores):
    n = sidx.shape[0]
    row_shape = src3.shape[1:]
    mesh = plsc.VectorSubcoreMesh(core_axis_name="core", subcore_axis_name="subcore",
                                  num_cores=SC_CORES, num_subcores=SC_SUBCORES)

    @pl.kernel(out_type=jax.ShapeDtypeStruct((n,) + row_shape, src3.dtype), mesh=mesh, name="sc_gather")
    def gather(src_hbm, i_hbm, o_hbm):
        def body(i_vmem, o_vmem):
            pltpu.sync_copy(src_hbm.at[i_vmem.at[0, pl.ds(0, SC_WINDOW)]], o_vmem)

        def run(axes):
            pltpu.emit_pipeline(
                body,
                grid=(n // SC_WINDOW,),
                in_specs=[pl.BlockSpec((1, LANES), lambda i: (i, 0))],
                out_specs=[pl.BlockSpec((SC_WINDOW,) + row_shape, lambda i: (i, 0, 0))],
                core_axis_name=axes,
                dimension_semantics=(pltpu.PARALLEL,),
            )(i_hbm, o_hbm)

        if both_cores:
            run(("core", "subcore"))
        else:
            pl.when(lax.axis_index("core") == 0)(lambda: run("subcore"))

    return gather(src3, _sc_window_indices(sidx))


def _moe_kernel(plan_ref, x_hbm, wgu_hbm, bgu_ref, wdn_hbm, bdn_ref, y_ref,
                x_buf, wgu_f32, wdn_f32, wgu_bf, wdn_bf, x_sems, sems):
    i = pl.program_id(0)
    n_used = plan_ref[PLAN_USED, i]
    expert = plan_ref[PLAN_EXPERT, i]
    active = i < n_used
    fresh = (i == 0) | (expert != plan_ref[PLAN_EXPERT, jnp.maximum(i - 1, 0)])
    slot = plan_ref[PLAN_SLOT, i]
    successor = plan_ref[PLAN_NEXT, i]
    rows = x_buf.shape[1]
    depth = x_buf.shape[0]

    def x_copy(blk):
        s = blk % depth
        return pltpu.make_async_copy(x_hbm.at[pl.ds(pl.multiple_of(blk * rows, rows), rows)],
                                     x_buf.at[s], x_sems.at[s])

    def weight_copies(e, s):
        return (pltpu.make_async_copy(wgu_hbm.at[e], wgu_f32.at[s], sems.at[s, 0]),
                pltpu.make_async_copy(wdn_hbm.at[e], wdn_f32.at[s], sems.at[s, 1]))

    @pl.when(i == 0)
    def _():
        for ahead in range(depth - 1):
            @pl.when(ahead < n_used)
            def _():
                x_copy(ahead).start()

    @pl.when(i + (depth - 1) < n_used)
    def _():
        x_copy(i + (depth - 1)).start()

    @pl.when(i == 0)
    def _():
        for c in weight_copies(expert, slot):
            c.start()

    @pl.when(active & fresh)
    def _():
        @pl.when(successor != expert)
        def _():
            for c in weight_copies(successor, 1 - slot):
                c.start()

        for c in weight_copies(expert, slot):
            c.wait()
        wgu_bf[...] = wgu_f32[slot].astype(BF16)
        wdn_bf[...] = wdn_f32[slot].astype(BF16)

    bm = rows // PACK_TILE
    n_valid = plan_ref[PLAN_VALID, i]

    def expert_ffn(n):
        words = _load_token_rows(x_buf.at[i % depth], n, r=PACK_TILE)
        valid = lax.broadcasted_iota(I32, (n, 1), 0) < n_valid
        x = _unpack_bf16_pairs(jnp.where(valid, words, jnp.uint32(0))).astype(BF16)
        gu = _dot(x, wgu_bf[...]) + bgu_ref[...]
        gate = jnp.minimum(gu[:, :D_FF], SWIGLU_LIMIT)
        up = jnp.clip(gu[:, D_FF:], -SWIGLU_LIMIT, SWIGLU_LIMIT)
        hid = (up + 1.0) * (gate * jax.nn.sigmoid(SWIGLU_ALPHA * gate))
        _store_token_rows(y_ref, _pack_bf16_pairs(_dot(hid.astype(BF16), wdn_bf[...]) + bdn_ref[...]))

    @pl.when(active)
    def _():
        x_copy(i).wait()

    step = math.gcd(bm, MOE_TAIL_STEP)
    for n in range(step, bm + 1, step):
        @pl.when(active & (n_valid > n - step) & (n_valid <= n))
        def _(n=n):
            expert_ffn(n)
            if n < bm:
                rest = (bm - n) * PACK_TILE
                y_ref[pl.ds(n * PACK_TILE, rest), :] = jnp.zeros((rest, LANES), U32)

    @pl.when(jnp.logical_not(active))
    def _():
        y_ref[...] = jnp.zeros_like(y_ref)


def _moe(plan, xpad, wgu, bgu, wdn, bdn, n_blk, bm):
    bias = lambda i, plan: (plan[PLAN_EXPERT, i], 0, 0)
    grid_spec = pltpu.PrefetchScalarGridSpec(
        num_scalar_prefetch=1,
        grid=(n_blk,),
        in_specs=[
            pl.BlockSpec(memory_space=pl.ANY),
            pl.BlockSpec(memory_space=pl.ANY),
            pl.BlockSpec((None, 1, 2 * D_FF), bias),
            pl.BlockSpec(memory_space=pl.ANY),
            pl.BlockSpec((None, 1, D_MODEL), bias),
        ],
        out_specs=pl.BlockSpec((bm * PACK_TILE, LANES), lambda i, plan: (i, 0)),
        scratch_shapes=[
            pltpu.VMEM((MOE_X_DEPTH, bm * PACK_TILE, LANES), U32),
            pltpu.VMEM((2, D_MODEL, 2 * D_FF), F32), pltpu.VMEM((2, D_FF, D_MODEL), F32),
            pltpu.VMEM((D_MODEL, 2 * D_FF), BF16), pltpu.VMEM((D_FF, D_MODEL), BF16),
            pltpu.SemaphoreType.DMA((MOE_X_DEPTH,)), pltpu.SemaphoreType.DMA((2, 2)),
        ],
    )
    return pl.pallas_call(
        _moe_kernel,
        grid_spec=grid_spec,
        out_shape=jax.ShapeDtypeStruct((n_blk * bm * PACK_TILE, LANES), U32),
        compiler_params=_cparams(("arbitrary",)),
        name="moe",
    )(plan, xpad, wgu, bgu, wdn, bdn)


def _final_kernel(h_ref, yg_ref, gate_ref, p_ref, wple_ref, wpg_ref, bpg_ref,
                  g2_ref, b2_ref, g3_ref, b3_ref, *rest, tm, alpha):
    o_ref = rest[-1]
    rc = FINAL_CHUNK
    for c in range(tm // rc):
        rows = slice(c * rc, (c + 1) * rc)
        gates = gate_ref[rows, :]
        z = alpha * _load_token_rows(h_ref, rc, c * rc)
        for k in range(TOP_K):
            y = _unpack_bf16_pairs(_load_token_rows(yg_ref.at[k], rc, c * rc, r=PACK_TILE))
            z = z + gates[:, k:k + 1] * y
        h2 = _layer_norm(z, g2_ref[...], b2_ref[...])
        emb = _dot(p_ref[rows, :].astype(BF16), wple_ref[...])
        pg = jax.nn.sigmoid(_dot(h2.astype(BF16), wpg_ref[...]) + bpg_ref[...])
        o_ref[rows, :] = _layer_norm(alpha * h2 + emb * pg, g3_ref[...], b3_ref[...])


def _final(h1, yg, gates, p2, wple, wpg, bpg, g2, b2, g3, b3, tm, alpha, tile_off, out_prev):
    T = h1.shape[0] // ROW_TILE
    const = lambda *shape: pl.BlockSpec(shape, lambda i: (0,) * len(shape))
    in_specs = [
        pl.BlockSpec((tm * ROW_TILE, LANES), lambda i: (i, 0)),
        pl.BlockSpec((TOP_K, tm * PACK_TILE, LANES), lambda i: (0, i, 0)),
        pl.BlockSpec((tm, LANES), lambda i: (i, 0)),
        pl.BlockSpec((tm, PLE_DIM), lambda i: (i + tile_off, 0)),
        const(PLE_DIM, D_MODEL), const(D_MODEL, D_MODEL), const(1, D_MODEL),
        const(1, D_MODEL), const(1, D_MODEL), const(1, D_MODEL), const(1, D_MODEL),
    ]
    args = [h1, yg, gates, p2, wple, wpg, bpg, g2, b2, g3, b3]
    aliases = {}
    if out_prev is not None:
        in_specs.append(pl.BlockSpec(memory_space=pl.ANY))
        aliases = {len(args): 0}
        args.append(out_prev)
    return pl.pallas_call(
        functools.partial(_final_kernel, tm=tm, alpha=alpha),
        grid=(T // tm,),
        in_specs=in_specs,
        out_specs=pl.BlockSpec((tm, D_MODEL), lambda i: (i + tile_off, 0)),
        out_shape=jax.ShapeDtypeStruct((p2.shape[0], D_MODEL), F32),
        input_output_aliases=aliases,
        compiler_params=_cparams(("arbitrary",)),
        name="final",
    )(*args)


def _layer(h2d, p2d, batch, seq, alpha, w_in, b_in, gmlp_ln_g, gmlp_ln_b, w_spatial, b_spatial,
           w_branch_a, w_branch_b, w_out, b_out, ln1_g, ln1_b, w_router, b_router,
           w_gate_up, b_gate_up, w_down, b_down, ln2_g, ln2_b, w_ple, w_ple_gate, b_ple_gate,
           ln3_g, ln3_b):
    T = batch * seq
    tm = math.gcd(T, ROW_TILE_A)
    bm = MOE_BLOCK
    off_f = 3 * FOX_WIDTH
    off_u = off_f + N_HEADS
    row = lambda v: v.reshape(1, -1).astype(F32)

    w1 = jnp.concatenate([w_in[:, :off_u], jnp.zeros((D_MODEL, LANES - N_HEADS), F32)], axis=1).astype(BF16)
    b1 = jnp.concatenate([b_in[:off_u], jnp.zeros((LANES - N_HEADS,), F32)]).reshape(1, -1)
    w2 = w_in[:, off_u:].astype(BF16)
    b2 = row(b_in[off_u:])
    bs_tile = jnp.repeat(b_spatial.T, HEAD_DIM, axis=1)
    wr = jnp.concatenate([w_router, jnp.zeros((D_MODEL, LANES - N_EXPERTS), F32)], axis=1)
    br = jnp.concatenate([b_router, jnp.full((LANES - N_EXPERTS,), -1e30, F32)]).reshape(1, -1)

    wsb, wab, wbb, wob = (w.astype(BF16) for w in (w_spatial, w_branch_a, w_branch_b, w_out))
    wpleb, wpgb = w_ple.astype(BF16), w_ple_gate.astype(BF16)
    bgu, bdn = b_gate_up.reshape(N_EXPERTS, 1, -1), b_down.reshape(N_EXPERTS, 1, -1)

    n_parts = N_PARTS if batch % N_PARTS == 0 else 1
    pb = batch // n_parts
    Tp = pb * seq
    tiles = Tp // tm
    n_assign = Tp * TOP_K
    n_blk = -(-n_assign // bm) + N_EXPERTS
    n_rows = n_blk * bm

    def front(part):
        q, k, v, f_pad = _qkvf(h2d, w1, b1, tm, Tp, part * tiles)
        auxq, auxk = _decay(f_pad, pb, seq)
        attn = _attention(q, auxq, k, auxk, v, pb, seq)
        h1, hx, idx_o, gate_o, rank_o, cnt_o = _mix(
            h2d, attn, w2, b2, row(gmlp_ln_g), row(gmlp_ln_b), wsb, bs_tile, wab, wbb, wob, row(b_out),
            row(ln1_g), row(ln1_b), wr, br, tm, alpha, part * tiles)
        dest8, plan = _route(cnt_o, idx_o, rank_o, bm, n_blk)
        dest_km = dest8[:TOP_K]
        xpad = _sc_scatter_rows(dest_km, hx.reshape(Tp, PACK_TILE, LANES), n_rows)
        return h1, gate_o, dest_km, plan, xpad

    def experts(part, state):
        h1, gate_o, dest_km, plan, xpad = state
        ypad = _moe(plan, xpad.reshape(n_rows * PACK_TILE, LANES), w_gate_up, bgu, w_down, bdn, n_blk, bm)
        yg = _sc_gather_rows(dest_km.reshape(-1), ypad.reshape(n_rows, PACK_TILE, LANES),
                             both_cores=part == n_parts - 1)
        return h1, gate_o, yg

    def back(part, state, out_prev):
        h1, gate_o, yg = state
        return _final(h1, yg.reshape(TOP_K, Tp * PACK_TILE, LANES), gate_o, p2d, wpleb, wpgb,
                      row(b_ple_gate), row(ln2_g), row(ln2_b), row(ln3_g), row(ln3_b), tm, alpha,
                      part * tiles, out_prev)

    fronts = [front(part) for part in range(n_parts)]
    mids = [experts(part, state) for part, state in enumerate(fronts)]
    out = None
    for part in range(n_parts):
        out = back(part, mids[part], out)
    return out


def kernel(x, p, w_in, b_in, gmlp_ln_g, gmlp_ln_b, w_spatial, b_spatial, w_branch_a, w_branch_b, w_out, b_out, ln1_g, ln1_b, w_router, b_router, w_gate_up, b_gate_up, w_down, b_down, ln2_g, ln2_b, w_ple, w_ple_gate, b_ple_gate, ln3_g, ln3_b):
    batch, seq, d = x.shape
    depth = w_in.shape[0]
    assert x.dtype == F32 and d == D_MODEL and seq % GMLP_CHUNK == 0
    assert w_in.shape[1:] == (D_MODEL, 3 * FOX_WIDTH + N_HEADS + 2 * GMLP_WIDTH + 2 * D_MODEL)
    assert w_gate_up.shape[1:] == (N_EXPERTS, D_MODEL, 2 * D_FF) and p.shape[-1] == PLE_DIM
    alpha = (2.0 * depth) ** 0.25
    h = x.reshape(batch * seq, d)
    for i in range(depth):
        h = _layer(h, p[i].reshape(batch * seq, PLE_DIM), batch, seq, alpha,
                   w_in[i], b_in[i], gmlp_ln_g[i], gmlp_ln_b[i], w_spatial[i], b_spatial[i],
                   w_branch_a[i], w_branch_b[i], w_out[i], b_out[i], ln1_g[i], ln1_b[i],
                   w_router[i], b_router[i], w_gate_up[i], b_gate_up[i], w_down[i], b_down[i],
                   ln2_g[i], ln2_b[i], w_ple[i], w_ple_gate[i], b_ple_gate[i], ln3_g[i], ln3_b[i])
    return h.reshape(batch, seq, d)
```
